```python
import math
import jax, jax.numpy as jnp
from jax import lax
import numpy as np

D_MODEL = 1024
BATCH = 32
SEQ = 2048
DEPTH = 1

MIX_WIDTH = D_MODEL
POOL_WIDTH = MIX_WIDTH // 2
POOL_WINDOWS = (2, 4, 8, 16)
POOL_GROUPS = len(POOL_WINDOWS)
POOL_CH = POOL_WIDTH // POOL_GROUPS
SGU_WIDTH = MIX_WIDTH - POOL_WIDTH
SGU_HEADS = 4
SGU_HD = SGU_WIDTH // SGU_HEADS
CHUNK = 128
IN_COLS = POOL_WIDTH + 2 * SGU_WIDTH
D_FF = int(math.ceil((8 * D_MODEL / 3) / 256) * 256)
LN_EPS = 1e-5
DEEPNORM_ALPHA = float((2.0 * DEPTH) ** 0.25)
DEEPNORM_BETA = float((8.0 * DEPTH) ** -0.25)

kernel_name = "hybrid_pool_sgu_deepnorm_layer"


def layer_norm(x, g, b):
    xf = x.astype(jnp.float32)
    mu = jnp.mean(xf, axis=-1, keepdims=True)
    var = jnp.mean(jnp.square(xf - mu), axis=-1, keepdims=True)
    out = (xf - mu) * lax.rsqrt(var + LN_EPS)
    return (out * g.astype(jnp.float32) + b.astype(jnp.float32)).astype(x.dtype)


def causal_multiscale_pool(xp):
    S = xp.shape[1]
    xf = xp.astype(jnp.float32)
    cs0 = jnp.pad(jnp.cumsum(xf, axis=1), ((0, 0), (1, 0), (0, 0)))
    pos = jnp.arange(1, S + 1, dtype=jnp.int32)
    outs = []
    for g, w in enumerate(POOL_WINDOWS):
        sl = slice(g * POOL_CH, (g + 1) * POOL_CH)
        c = cs0[..., sl]
        lower = jnp.pad(c, ((0, 0), (w - 1, 0), (0, 0)))[:, :S]
        cnt = jnp.minimum(pos, w).astype(jnp.float32)[None, :, None]
        outs.append((c[:, 1:] - lower) / cnt - xf[..., sl])
    return jnp.stack(outs, axis=2)


def spatial_gating(z, ln_g, ln_b, w_s, b_s):
    B, S, _ = z.shape
    u, v = z[..., :SGU_WIDTH], z[..., SGU_WIDTH:]
    v = layer_norm(v, ln_g, ln_b)
    v = v.reshape(B, S // CHUNK, CHUNK, SGU_HEADS, SGU_HD)
    mask = jnp.tril(jnp.ones((CHUNK, CHUNK), dtype=w_s.dtype))
    ws = w_s * mask[None]
    mixed = jnp.einsum('hts,bnshd->bnthd', ws, v)
    mixed = mixed + jnp.transpose(b_s)[None, None, :, :, None]
    return u * mixed.reshape(B, S, SGU_WIDTH)


def swiglu_ffn(h, w_gate_up, w_down):
    gu = jnp.einsum('bsd,df->bsf', h, w_gate_up)
    gate, up = gu[..., :D_FF], gu[..., D_FF:]
    return jnp.einsum('bsf,fd->bsd', jax.nn.silu(gate) * up, w_down)


def _fwd_setup_inputs(seed: int = 0) -> dict:
    key = jax.random.key(seed)
    ks = jax.random.split(key, 16)
    f32 = jnp.float32
    def nrm(k, shape, scale):
        return jax.random.normal(k, shape, f32) * scale
    return {
        "x": jax.random.normal(ks[0], (BATCH, SEQ, D_MODEL), f32),
        "w_in": nrm(ks[1], (DEPTH, D_MODEL, IN_COLS), D_MODEL ** -0.5),
        "pool_w": nrm(ks[2], (DEPTH, POOL_GROUPS, POOL_CH, POOL_CH), POOL_CH ** -0.5),
        "pool_scale": 1.0 + nrm(ks[3], (DEPTH, POOL_WIDTH), 0.1),
        "sgu_ln_g": 1.0 + nrm(ks[4], (DEPTH, SGU_WIDTH), 0.01),
        "sgu_ln_b": nrm(ks[5], (DEPTH, SGU_WIDTH), 0.01),
        "sgu_w": nrm(ks[6], (DEPTH, SGU_HEADS, CHUNK, CHUNK), CHUNK ** -0.5),
        "sgu_b": 1.0 + nrm(ks[7], (DEPTH, SGU_HEADS, CHUNK), 0.01),
        "w_out": nrm(ks[8], (DEPTH, MIX_WIDTH, D_MODEL), MIX_WIDTH ** -0.5 * DEEPNORM_BETA),
        "ln1_g": 1.0 + nrm(ks[9], (DEPTH, D_MODEL), 0.01),
        "ln1_b": nrm(ks[10], (DEPTH, D_MODEL), 0.01),
        "w_gate_up": nrm(ks[11], (DEPTH, D_MODEL, 2 * D_FF), D_MODEL ** -0.5),
        "w_down": nrm(ks[12], (DEPTH, D_FF, D_MODEL), D_FF ** -0.5 * DEEPNORM_BETA),
        "ln2_g": 1.0 + nrm(ks[13], (DEPTH, D_MODEL), 0.01),
        "ln2_b": nrm(ks[14], (DEPTH, D_MODEL), 0.01),
    }


def _fwd_reference(x, w_in, pool_w, pool_scale, sgu_ln_g, sgu_ln_b, sgu_w, sgu_b,
              w_out, ln1_g, ln1_b, w_gate_up, w_down, ln2_g, ln2_b):
    B, S, _ = x.shape
    alpha = jnp.asarray(DEEPNORM_ALPHA, dtype=x.dtype)
    for l in range(DEPTH):
        proj = jnp.einsum('bsd,dc->bsc', x, w_in[l])
        xp = proj[..., :POOL_WIDTH]
        zg = jax.nn.gelu(proj[..., POOL_WIDTH:], approximate=False)
        pooled = causal_multiscale_pool(xp)
        pool_out = jnp.einsum('bsgc,gcd->bsgd', pooled, pool_w[l].astype(jnp.float32))
        pool_out = (pool_out.reshape(B, S, POOL_WIDTH) * pool_scale[l]).astype(x.dtype)
        sgu_out = spatial_gating(zg, sgu_ln_g[l], sgu_ln_b[l], sgu_w[l], sgu_b[l])
        mix = jnp.concatenate([pool_out, sgu_out], axis=-1)
        mix = jnp.einsum('bsc,cd->bsd', mix, w_out[l])
        h = layer_norm(alpha * x + mix, ln1_g[l], ln1_b[l])
        x = layer_norm(alpha * h + swiglu_ffn(h, w_gate_up[l], w_down[l]), ln2_g[l], ln2_b[l])
    return x


import jax as _jax
import jax.numpy as _jnp

TWIN_FORMAT = 'train_step'
FWD_PARAMS = ['x', 'w_in', 'pool_w', 'pool_scale', 'sgu_ln_g', 'sgu_ln_b', 'sgu_w', 'sgu_b', 'w_out', 'ln1_g', 'ln1_b', 'w_gate_up', 'w_down', 'ln2_g', 'ln2_b']
TWIN_WEIGHTS = ['w_in', 'pool_w', 'pool_scale', 'sgu_ln_g', 'sgu_ln_b', 'sgu_w', 'sgu_b', 'w_out', 'ln1_g', 'ln1_b', 'w_gate_up', 'w_down', 'ln2_g', 'ln2_b']
TWIN_DIFF_INPUT = 'x'
TWIN_INPUTS = ['x', 'w_in', 'pool_w', 'pool_scale', 'sgu_ln_g', 'sgu_ln_b', 'sgu_w', 'sgu_b', 'w_out', 'ln1_g', 'ln1_b', 'w_gate_up', 'w_down', 'ln2_g', 'ln2_b', 'loss_target', 'm_w_in', 'm_pool_w', 'm_pool_scale', 'm_sgu_ln_g', 'm_sgu_ln_b', 'm_sgu_w', 'm_sgu_b', 'm_w_out', 'm_ln1_g', 'm_ln1_b', 'm_w_gate_up', 'm_w_down', 'm_ln2_g', 'm_ln2_b', 'v_w_in', 'v_pool_w', 'v_pool_scale', 'v_sgu_ln_g', 'v_sgu_ln_b', 'v_sgu_w', 'v_sgu_b', 'v_w_out', 'v_ln1_g', 'v_ln1_b', 'v_w_gate_up', 'v_w_down', 'v_ln2_g', 'v_ln2_b']
TWIN_OUTPUTS = ['loss', 'grad_x', 'grad_w_in', 'grad_pool_w', 'grad_pool_scale', 'grad_sgu_ln_g', 'grad_sgu_ln_b', 'grad_sgu_w', 'grad_sgu_b', 'grad_w_out', 'grad_ln1_g', 'grad_ln1_b', 'grad_w_gate_up', 'grad_w_down', 'grad_ln2_g', 'grad_ln2_b', 'delta_w_in', 'delta_pool_w', 'delta_pool_scale', 'delta_sgu_ln_g', 'delta_sgu_ln_b', 'delta_sgu_w', 'delta_sgu_b', 'delta_w_out', 'delta_ln1_g', 'delta_ln1_b', 'delta_w_gate_up', 'delta_w_down', 'delta_ln2_g', 'delta_ln2_b', 'new_m_w_in', 'new_m_pool_w', 'new_m_pool_scale', 'new_m_sgu_ln_g', 'new_m_sgu_ln_b', 'new_m_sgu_w', 'new_m_sgu_b', 'new_m_w_out', 'new_m_ln1_g', 'new_m_ln1_b', 'new_m_w_gate_up', 'new_m_w_down', 'new_m_ln2_g', 'new_m_ln2_b', 'new_v_w_in', 'new_v_pool_w', 'new_v_pool_scale', 'new_v_sgu_ln_g', 'new_v_sgu_ln_b', 'new_v_sgu_w', 'new_v_sgu_b', 'new_v_w_out', 'new_v_ln1_g', 'new_v_ln1_b', 'new_v_w_gate_up', 'new_v_w_down', 'new_v_ln2_g', 'new_v_ln2_b']
TWIN_LEAF_KINDS = {'loss': 'loss', 'grad_x': 'grad_x', 'grad_w_in': 'grad_w', 'grad_pool_w': 'grad_w', 'grad_pool_scale': 'grad_w', 'grad_sgu_ln_g': 'grad_w', 'grad_sgu_ln_b': 'grad_w', 'grad_sgu_w': 'grad_w', 'grad_sgu_b': 'grad_w', 'grad_w_out': 'grad_w', 'grad_ln1_g': 'grad_w', 'grad_ln1_b': 'grad_w', 'grad_w_gate_up': 'grad_w', 'grad_w_down': 'grad_w', 'grad_ln2_g': 'grad_w', 'grad_ln2_b': 'grad_w', 'delta_w_in': 'delta_w', 'delta_pool_w': 'delta_w', 'delta_pool_scale': 'delta_w', 'delta_sgu_ln_g': 'delta_w', 'delta_sgu_ln_b': 'delta_w', 'delta_sgu_w': 'delta_w', 'delta_sgu_b': 'delta_w', 'delta_w_out': 'delta_w', 'delta_ln1_g': 'delta_w', 'delta_ln1_b': 'delta_w', 'delta_w_gate_up': 'delta_w', 'delta_w_down': 'delta_w', 'delta_ln2_g': 'delta_w', 'delta_ln2_b': 'delta_w', 'new_m_w_in': 'new_m', 'new_m_pool_w': 'new_m', 'new_m_pool_scale': 'new_m', 'new_m_sgu_ln_g': 'new_m', 'new_m_sgu_ln_b': 'new_m', 'new_m_sgu_w': 'new_m', 'new_m_sgu_b': 'new_m', 'new_m_w_out': 'new_m', 'new_m_ln1_g': 'new_m', 'new_m_ln1_b': 'new_m', 'new_m_w_gate_up': 'new_m', 'new_m_w_down': 'new_m', 'new_m_ln2_g': 'new_m', 'new_m_ln2_b': 'new_m', 'new_v_w_in': 'new_v', 'new_v_pool_w': 'new_v', 'new_v_pool_scale': 'new_v', 'new_v_sgu_ln_g': 'new_v', 'new_v_sgu_ln_b': 'new_v', 'new_v_sgu_w': 'new_v', 'new_v_sgu_b': 'new_v', 'new_v_w_out': 'new_v', 'new_v_ln1_g': 'new_v', 'new_v_ln1_b': 'new_v', 'new_v_w_gate_up': 'new_v', 'new_v_w_down': 'new_v', 'new_v_ln2_g': 'new_v', 'new_v_ln2_b': 'new_v'}


def _forward(args):
    return _fwd_reference(*[args[k] for k in FWD_PARAMS])


def _output_shape():
    out = _jax.eval_shape(lambda: _forward(_fwd_setup_inputs(0)))
    return out.shape, out.dtype

N_MICROBATCH = 1
ADAM_LR = 0.001
ADAM_B1 = 0.9
ADAM_B2 = 0.999
ADAM_EPS = 1e-08
ADAM_WD = 0.01
ADAM_STEP = 10
PER_EXAMPLE_BATCH_AXIS = {'x': 0, 'loss_target': 0}
SHARED_INPUTS = []
_WEIGHT_DTYPES = {'w_in': _jnp.float32, 'pool_w': _jnp.float32, 'pool_scale': _jnp.float32, 'sgu_ln_g': _jnp.float32, 'sgu_ln_b': _jnp.float32, 'sgu_w': _jnp.float32, 'sgu_b': _jnp.float32, 'w_out': _jnp.float32, 'ln1_g': _jnp.float32, 'ln1_b': _jnp.float32, 'w_gate_up': _jnp.float32, 'w_down': _jnp.float32, 'ln2_g': _jnp.float32, 'ln2_b': _jnp.float32}
MOMENT_SCALE = {'w_in': 9.152378e-02, 'pool_w': 1.060949e-01, 'pool_scale': 1.080154e-01, 'sgu_ln_g': 5.675021e-02, 'sgu_ln_b': 6.138779e-02, 'sgu_w': 5.645672e-02, 'sgu_b': 7.900780e-02, 'w_out': 1.735853e-01, 'ln1_g': 7.008495e-01, 'ln1_b': 3.596710e-01, 'w_gate_up': 4.377638e-02, 'w_down': 1.205132e-01, 'ln2_g': 6.380114e+01, 'ln2_b': 5.011419e+00}


def _to_microbatches(a, axis):
    t = _jnp.moveaxis(a, axis, 0)
    t = t.reshape((N_MICROBATCH, t.shape[0] // N_MICROBATCH) + t.shape[1:])
    return _jnp.moveaxis(t, 1, axis + 1)


def setup_inputs(seed: int = 0) -> dict:
    inp = _fwd_setup_inputs(seed)
    key = _jax.random.fold_in(_jax.random.key(seed), 7919)
    shape, _ = _output_shape()
    out = dict(inp)
    out["loss_target"] = _jax.random.normal(_jax.random.fold_in(key, 0), shape, _jnp.float32)
    for i, name in enumerate(TWIN_WEIGHTS):
        w = inp[name].astype(_jnp.float32)
        if MOMENT_SCALE is None:
            s = _jnp.sqrt(_jnp.mean(_jnp.square(w)) + 1e-30)
        else:
            s = MOMENT_SCALE[name]
        km, kv = _jax.random.split(_jax.random.fold_in(key, i + 1))
        out[name] = w
        out["m_" + name] = s * _jax.random.normal(km, w.shape, _jnp.float32)
        out["v_" + name] = (s * s) * _jax.random.uniform(kv, w.shape, _jnp.float32, 0.5, 1.5)
    if N_MICROBATCH > 1:
        for name, axis in PER_EXAMPLE_BATCH_AXIS.items():
            out[name] = _to_microbatches(out[name], axis)
    return {'x': out['x'], 'w_in': out['w_in'], 'pool_w': out['pool_w'], 'pool_scale': out['pool_scale'], 'sgu_ln_g': out['sgu_ln_g'], 'sgu_ln_b': out['sgu_ln_b'], 'sgu_w': out['sgu_w'], 'sgu_b': out['sgu_b'], 'w_out': out['w_out'], 'ln1_g': out['ln1_g'], 'ln1_b': out['ln1_b'], 'w_gate_up': out['w_gate_up'], 'w_down': out['w_down'], 'ln2_g': out['ln2_g'], 'ln2_b': out['ln2_b'], 'loss_target': out['loss_target'], 'm_w_in': out['m_w_in'], 'm_pool_w': out['m_pool_w'], 'm_pool_scale': out['m_pool_scale'], 'm_sgu_ln_g': out['m_sgu_ln_g'], 'm_sgu_ln_b': out['m_sgu_ln_b'], 'm_sgu_w': out['m_sgu_w'], 'm_sgu_b': out['m_sgu_b'], 'm_w_out': out['m_w_out'], 'm_ln1_g': out['m_ln1_g'], 'm_ln1_b': out['m_ln1_b'], 'm_w_gate_up': out['m_w_gate_up'], 'm_w_down': out['m_w_down'], 'm_ln2_g': out['m_ln2_g'], 'm_ln2_b': out['m_ln2_b'], 'v_w_in': out['v_w_in'], 'v_pool_w': out['v_pool_w'], 'v_pool_scale': out['v_pool_scale'], 'v_sgu_ln_g': out['v_sgu_ln_g'], 'v_sgu_ln_b': out['v_sgu_ln_b'], 'v_sgu_w': out['v_sgu_w'], 'v_sgu_b': out['v_sgu_b'], 'v_w_out': out['v_w_out'], 'v_ln1_g': out['v_ln1_g'], 'v_ln1_b': out['v_ln1_b'], 'v_w_gate_up': out['v_w_gate_up'], 'v_w_down': out['v_w_down'], 'v_ln2_g': out['v_ln2_g'], 'v_ln2_b': out['v_ln2_b']}


def _loss(weights, diff, rest, loss_target):
    with _jax.named_scope("forward"):
        args = {**rest, TWIN_DIFF_INPUT: diff, **{k: w.astype(_WEIGHT_DTYPES[k]) for k, w in weights.items()}}
        y = _forward(args)
    with _jax.named_scope("loss_head"):
        err = _jnp.square(y.astype(_jnp.float32) - loss_target)
        return 0.5 * _jnp.sum(_jnp.mean(err, axis=-1)) if err.ndim else 0.5 * err


def _adamw(w, g, m, v):
    m = ADAM_B1 * m + (1.0 - ADAM_B1) * g
    v = ADAM_B2 * v + (1.0 - ADAM_B2) * _jnp.square(g)
    m_hat = m / (1.0 - ADAM_B1 ** ADAM_STEP)
    v_hat = v / (1.0 - ADAM_B2 ** ADAM_STEP)
    delta = -ADAM_LR * (m_hat / (_jnp.sqrt(v_hat) + ADAM_EPS) + ADAM_WD * w)
    return delta, m, v


def reference(x, w_in, pool_w, pool_scale, sgu_ln_g, sgu_ln_b, sgu_w, sgu_b, w_out, ln1_g, ln1_b, w_gate_up, w_down, ln2_g, ln2_b, loss_target, m_w_in, m_pool_w, m_pool_scale, m_sgu_ln_g, m_sgu_ln_b, m_sgu_w, m_sgu_b, m_w_out, m_ln1_g, m_ln1_b, m_w_gate_up, m_w_down, m_ln2_g, m_ln2_b, v_w_in, v_pool_w, v_pool_scale, v_sgu_ln_g, v_sgu_ln_b, v_sgu_w, v_sgu_b, v_w_out, v_ln1_g, v_ln1_b, v_w_gate_up, v_w_down, v_ln2_g, v_ln2_b):
    given = dict(x=x, w_in=w_in, pool_w=pool_w, pool_scale=pool_scale, sgu_ln_g=sgu_ln_g, sgu_ln_b=sgu_ln_b, sgu_w=sgu_w, sgu_b=sgu_b, w_out=w_out, ln1_g=ln1_g, ln1_b=ln1_b, w_gate_up=w_gate_up, w_down=w_down, ln2_g=ln2_g, ln2_b=ln2_b, loss_target=loss_target, m_w_in=m_w_in, m_pool_w=m_pool_w, m_pool_scale=m_pool_scale, m_sgu_ln_g=m_sgu_ln_g, m_sgu_ln_b=m_sgu_ln_b, m_sgu_w=m_sgu_w, m_sgu_b=m_sgu_b, m_w_out=m_w_out, m_ln1_g=m_ln1_g, m_ln1_b=m_ln1_b, m_w_gate_up=m_w_gate_up, m_w_down=m_w_down, m_ln2_g=m_ln2_g, m_ln2_b=m_ln2_b, v_w_in=v_w_in, v_pool_w=v_pool_w, v_pool_scale=v_pool_scale, v_sgu_ln_g=v_sgu_ln_g, v_sgu_ln_b=v_sgu_ln_b, v_sgu_w=v_sgu_w, v_sgu_b=v_sgu_b, v_w_out=v_w_out, v_ln1_g=v_ln1_g, v_ln1_b=v_ln1_b, v_w_gate_up=v_w_gate_up, v_w_down=v_w_down, v_ln2_g=v_ln2_g, v_ln2_b=v_ln2_b)
    weights = {n: given[n] for n in TWIN_WEIGHTS}
    shared = {n: given[n] for n in SHARED_INPUTS}
    per_example = {n: given[n] for n in ['x']}
    grad_fn = _jax.value_and_grad(_loss, argnums=(0, 1))

    def one_microbatch(ex, loss_target):
        ex = dict(ex)
        diff = ex.pop(TWIN_DIFF_INPUT)
        return grad_fn(weights, diff, {**shared, **ex}, loss_target)

    if N_MICROBATCH == 1:
        loss, (grad_w, grad_x) = one_microbatch(per_example, given["loss_target"])
    else:
        def body(carry, xs):
            loss_sum, grad_sum = carry
            l_k, (gw_k, gx_k) = one_microbatch(xs[0], xs[1])
            with _jax.named_scope("update"):
                return (loss_sum + l_k, _jax.tree.map(_jnp.add, grad_sum, gw_k)), gx_k

        init = (_jnp.zeros((), _jnp.float32), _jax.tree.map(_jnp.zeros_like, weights))
        (loss, grad_w), grad_x = _jax.lax.scan(body, init, (per_example, given["loss_target"]))
    with _jax.named_scope("update"):
        delta_w, new_m, new_v = {}, {}, {}
        for n in TWIN_WEIGHTS:
            delta_w[n], new_m[n], new_v[n] = _adamw(weights[n], grad_w[n], given["m_" + n], given["v_" + n])
    return (loss, grad_x, *[grad_w[n] for n in TWIN_WEIGHTS], *[delta_w[n] for n in TWIN_WEIGHTS],
            *[new_m[n] for n in TWIN_WEIGHTS], *[new_v[n] for n in TWIN_WEIGHTS])
```

```python
import functools
import math

import jax
import jax.numpy as jnp
from jax import lax
from jax.experimental import pallas as pl
from jax.experimental.pallas import tpu as pltpu

f32 = jnp.float32
bf16 = jnp.bfloat16
MESH = pl.DeviceIdType.MESH

D_MODEL = 1024
POOL_WIDTH = 512
SGU_WIDTH = 512
IN_COLS = POOL_WIDTH + 2 * SGU_WIDTH
D_FF = 2816
POOL_WINDOWS = (2, 4, 8, 16)
GROUP = 128
N_GROUPS = 4
HALO = 16
LN_EPS = 1e-5
ALPHA = float(2.0 ** 0.25)
N_CHIPS = 4

ADAM_LR = 0.001
ADAM_B1 = 0.9
ADAM_B2 = 0.999
ADAM_EPS = 1e-08
ADAM_WD = 0.01
ADAM_STEP = 10

TOKEN_TILE = 512
FFN_BWD_TILE = 256
FF_CHUNK = D_FF // 2
V7X_VMEM_LIMIT = 56 * 1024 * 1024

SMALL_ROWS = 1088
SMALL_BLOCK = SMALL_ROWS // 8


def _params(**kw):
    return pltpu.CompilerParams(vmem_limit_bytes=V7X_VMEM_LIMIT, **kw)


def _mm(a, b):
    return jnp.dot(a, b, preferred_element_type=f32)


def _mm_nt(a, b):
    return lax.dot_general(a, b, (((1,), (1,)), ((), ())), preferred_element_type=f32)


def _mm_tn(a, b):
    return lax.dot_general(a, b, (((0,), (0,)), ((), ())), preferred_element_type=f32)


def _ln_fwd(r, g, b):
    mu = jnp.mean(r, axis=-1, keepdims=True)
    xc = r - mu
    var = jnp.mean(xc * xc, axis=-1, keepdims=True)
    rstd = lax.rsqrt(var + LN_EPS)
    xhat = xc * rstd
    return xhat * g + b, xhat, rstd


def _ln_bwd(dout, xhat, rstd, g):
    dxhat = dout * g
    m1 = jnp.mean(dxhat, axis=-1, keepdims=True)
    m2 = jnp.mean(dxhat * xhat, axis=-1, keepdims=True)
    return rstd * (dxhat - m1 - xhat * m2)


def _col_sum(a):
    return jnp.sum(a, axis=0, keepdims=True)


def _gelu_parts(z):
    cdf = 0.5 * (1.0 + lax.erf(z * (1.0 / math.sqrt(2.0))))
    pdf = jnp.exp(-0.5 * z * z) * (1.0 / math.sqrt(2.0 * math.pi))
    return cdf, pdf


def _inv_counts(seq_tile, rows):
    pos = seq_tile * rows + lax.broadcasted_iota(jnp.int32, (rows, GROUP), 0) + 1
    return [1.0 / jnp.minimum(pos, w).astype(f32) for w in POOL_WINDOWS]


def _window_sums(e, back):
    n = e.shape[0]

    def shifted(a, s):
        return pltpu.roll(a, s if back else n - s, 0)

    s2 = e + shifted(e, 1)
    s4 = s2[:, GROUP:] + shifted(s2[:, GROUP:], 2)
    s8 = s4[:, GROUP:] + shifted(s4[:, GROUP:], 4)
    s16 = s8[:, GROUP:] + shifted(s8[:, GROUP:], 8)
    return [s2[:, :GROUP], s4[:, :GROUP], s8[:, :GROUP], s16]


def _pooled_groups(xp, halo, inv):
    sums = _window_sums(jnp.concatenate([halo, xp], axis=0), back=True)
    return [sums[g][HALO:] * inv[g] - xp[:, g * GROUP:(g + 1) * GROUP] for g in range(N_GROUPS)]


def _tril_mask():
    r = lax.broadcasted_iota(jnp.int32, (GROUP, GROUP), 0)
    c = lax.broadcasted_iota(jnp.int32, (GROUP, GROUP), 1)
    return (r >= c).astype(f32)


def _gs(g):
    return slice(g * GROUP, (g + 1) * GROUP)


def _fwd_proj(x2, w_in_b, tile):
    tokens = x2.shape[0]

    def body(x_ref, w_ref, o_ref):
        o_ref[...] = _mm(x_ref[...].astype(bf16), w_ref[...])

    return pl.pallas_call(
        body, name="fwd_proj", grid=(tokens // tile,),
        in_specs=[pl.BlockSpec((tile, D_MODEL), lambda i: (i, 0)),
                  pl.BlockSpec((D_MODEL, IN_COLS), lambda i: (0, 0))],
        out_specs=pl.BlockSpec((tile, IN_COLS), lambda i: (i, 0)),
        out_shape=jax.ShapeDtypeStruct((tokens, IN_COLS), f32),
        compiler_params=_params(),
    )(x2, w_in_b)


def _small_specs():
    return [pl.BlockSpec((N_GROUPS, GROUP, GROUP), lambda i: (0, 0, 0)),
            pl.BlockSpec((1, POOL_WIDTH), lambda i: (0, 0)),
            pl.BlockSpec((1, SGU_WIDTH), lambda i: (0, 0)),
            pl.BlockSpec((1, SGU_WIDTH), lambda i: (0, 0)),
            pl.BlockSpec((N_GROUPS, GROUP, GROUP), lambda i: (0, 0, 0)),
            pl.BlockSpec((N_GROUPS, GROUP, GROUP), lambda i: (0, 0, 0))]


def _fwd_mix(proj, x2, small, w_out_b, ln1_g, ln1_b, tile, seq):
    tokens = x2.shape[0]
    tps = seq // tile
    hb = tile // HALO

    def body(proj_ref, halo_ref, x_ref, pw_ref, ps_ref, lg_ref, lb_ref, sw_ref, sb_ref, wout_ref, g1_ref, b1_ref,
             mix_ref, r1_ref, h_ref):
        seq_tile = pl.program_id(0) % tps
        xp = proj_ref[:, :POOL_WIDTH]
        halo = jnp.where(seq_tile == 0, 0.0, halo_ref[...])
        pooled = _pooled_groups(xp, halo, _inv_counts(seq_tile, tile))
        for g in range(N_GROUPS):
            po = _mm(pooled[g].astype(bf16), pw_ref[g].astype(bf16)) * ps_ref[:, _gs(g)]
            mix_ref[:, _gs(g)] = po.astype(bf16)

        cdf, _ = _gelu_parts(proj_ref[:, POOL_WIDTH:])
        zg = proj_ref[:, POOL_WIDTH:] * cdf
        u = zg[:, :SGU_WIDTH]
        vln, _, _ = _ln_fwd(zg[:, SGU_WIDTH:], lg_ref[...], lb_ref[...])
        vb = vln.astype(bf16)
        mask = _tril_mask()
        for h in range(N_GROUPS):
            wm = (sw_ref[h] * mask).astype(bf16)
            bias = sb_ref[h]
            for n in range(tile // GROUP):
                rows = slice(n * GROUP, (n + 1) * GROUP)
                mixed = _mm(wm, vb[rows, _gs(h)]) + bias
                mix_ref[rows, POOL_WIDTH + h * GROUP:POOL_WIDTH + (h + 1) * GROUP] = (u[rows, _gs(h)] * mixed).astype(bf16)

        r1 = ALPHA * x_ref[...] + _mm(mix_ref[...], wout_ref[...])
        r1_ref[...] = r1
        h1, _, _ = _ln_fwd(r1, g1_ref[...], b1_ref[...])
        h_ref[...] = h1.astype(bf16)

    row = lambda i: (i, 0)
    vec = pl.BlockSpec((1, D_MODEL), lambda i: (0, 0))
    return pl.pallas_call(
        body, name="fwd_mix", grid=(tokens // tile,),
        in_specs=[pl.BlockSpec((tile, IN_COLS), row),
                  pl.BlockSpec((HALO, POOL_WIDTH), lambda i: (jnp.maximum(i * hb - 1, 0), 0)),
                  pl.BlockSpec((tile, D_MODEL), row)] + _small_specs()
                 + [pl.BlockSpec((D_MODEL, D_MODEL), lambda i: (0, 0)), vec, vec],
        out_specs=[pl.BlockSpec((tile, D_MODEL), row)] * 3,
        out_shape=[jax.ShapeDtypeStruct((tokens, D_MODEL), bf16),
                   jax.ShapeDtypeStruct((tokens, D_MODEL), f32),
                   jax.ShapeDtypeStruct((tokens, D_MODEL), bf16)],
        compiler_params=_params(),
    )(proj, proj, x2, *small, w_out_b, ln1_g, ln1_b)


def _fwd_gate_up(h_b, w_gu_b, tile):
    tokens = h_b.shape[0]

    def body(h_ref, w_ref, gu_ref, a_ref):
        hb = h_ref[...]
        for c in range(D_FF // FF_CHUNK):
            gcols = slice(c * FF_CHUNK, (c + 1) * FF_CHUNK)
            ucols = slice(D_FF + c * FF_CHUNK, D_FF + (c + 1) * FF_CHUNK)
            gate = _mm(hb, w_ref[:, gcols])
            up = _mm(hb, w_ref[:, ucols])
            gu_ref[:, gcols] = gate.astype(bf16)
            gu_ref[:, ucols] = up.astype(bf16)
            a_ref[:, gcols] = (gate * jax.nn.sigmoid(gate) * up).astype(bf16)

    return pl.pallas_call(
        body, name="fwd_gate_up", grid=(tokens // tile,),
        in_specs=[pl.BlockSpec((tile, D_MODEL), lambda i: (i, 0)),
                  pl.BlockSpec((D_MODEL, 2 * D_FF), lambda i: (0, 0), pipeline_mode=pl.Buffered(1))],
        out_specs=[pl.BlockSpec((tile, 2 * D_FF), lambda i: (i, 0)),
                   pl.BlockSpec((tile, D_FF), lambda i: (i, 0))],
        out_shape=[jax.ShapeDtypeStruct((tokens, 2 * D_FF), bf16),
                   jax.ShapeDtypeStruct((tokens, D_FF), bf16)],
        compiler_params=_params(),
    )(h_b, w_gu_b)


def _fwd_down_loss(a_b, w_dn_b, r1, target, ln1_g, ln1_b, ln2_g, ln2_b, tile):
    tokens = a_b.shape[0]

    def body(a_ref, w_ref, r1_ref, t_ref, g1_ref, b1_ref, g2_ref, b2_ref, dr2_ref, st_ref):
        @pl.when(pl.program_id(0) == 0)
        def _():
            st_ref[...] = jnp.zeros_like(st_ref)

        h1, _, _ = _ln_fwd(r1_ref[...], g1_ref[...], b1_ref[...])
        r2 = ALPHA * h1 + _mm(a_ref[...], w_ref[...])
        y, xhat, rstd = _ln_fwd(r2, g2_ref[...], b2_ref[...])
        diff = y - t_ref[...]
        dy = diff * (1.0 / D_MODEL)
        st_ref[0:1, :] += _col_sum(dy * xhat)
        st_ref[1:2, :] += _col_sum(dy)
        st_ref[2:3, :] += _col_sum(diff * diff)
        dr2_ref[...] = _ln_bwd(dy, xhat, rstd, g2_ref[...])

    row = lambda i: (i, 0)
    vec = pl.BlockSpec((1, D_MODEL), lambda i: (0, 0))
    return pl.pallas_call(
        body, name="fwd_down_loss", grid=(tokens // tile,),
        in_specs=[pl.BlockSpec((tile, D_FF), row), pl.BlockSpec((D_FF, D_MODEL), lambda i: (0, 0)),
                  pl.BlockSpec((tile, D_MODEL), row), pl.BlockSpec((tile, D_MODEL), row), vec, vec, vec, vec],
        out_specs=[pl.BlockSpec((tile, D_MODEL), row), pl.BlockSpec((8, D_MODEL), lambda i: (0, 0))],
        out_shape=[jax.ShapeDtypeStruct((tokens, D_MODEL), f32), jax.ShapeDtypeStruct((8, D_MODEL), f32)],
        compiler_params=_params(),
    )(a_b, w_dn_b, r1, target, ln1_g, ln1_b, ln2_g, ln2_b)


def _bwd_gate_up(dr2, gu_b, w_dn_b, tile):
    tokens = dr2.shape[0]

    def body(d_ref, gu_ref, w_ref, dgu_ref):
        d = d_ref[...].astype(bf16)
        for c in range(D_FF // FF_CHUNK):
            gcols = slice(c * FF_CHUNK, (c + 1) * FF_CHUNK)
            ucols = slice(D_FF + c * FF_CHUNK, D_FF + (c + 1) * FF_CHUNK)
            da = _mm_nt(d, w_ref[gcols, :])
            gate = gu_ref[:, gcols].astype(f32)
            up = gu_ref[:, ucols].astype(f32)
            sg = jax.nn.sigmoid(gate)
            dgu_ref[:, gcols] = (da * up * (sg * (1.0 + gate * (1.0 - sg)))).astype(bf16)
            dgu_ref[:, ucols] = (da * (gate * sg)).astype(bf16)

    return pl.pallas_call(
        body, name="bwd_gate_up", grid=(tokens // tile,),
        in_specs=[pl.BlockSpec((tile, D_MODEL), lambda i: (i, 0)),
                  pl.BlockSpec((tile, 2 * D_FF), lambda i: (i, 0)),
                  pl.BlockSpec((D_FF, D_MODEL), lambda i: (0, 0))],
        out_specs=pl.BlockSpec((tile, 2 * D_FF), lambda i: (i, 0)),
        out_shape=jax.ShapeDtypeStruct((tokens, 2 * D_FF), bf16),
        compiler_params=_params(),
    )(dr2, gu_b, w_dn_b)


def _bwd_ffn_in(dgu_b, w_gu_b, dr2, r1, ln1_g, ln1_b, tile):
    tokens = dr2.shape[0]

    def body(dgu_ref, w_ref, d_ref, r1_ref, g1_ref, b1_ref, dr1_ref, st_ref):
        @pl.when(pl.program_id(0) == 0)
        def _():
            st_ref[...] = jnp.zeros_like(st_ref)

        dh = ALPHA * d_ref[...] + _mm_nt(dgu_ref[...], w_ref[...])
        _, xhat, rstd = _ln_fwd(r1_ref[...], g1_ref[...], b1_ref[...])
        st_ref[0:1, :] += _col_sum(dh * xhat)
        st_ref[1:2, :] += _col_sum(dh)
        dr1_ref[...] = _ln_bwd(dh, xhat, rstd, g1_ref[...])

    row = lambda i: (i, 0)
    vec = pl.BlockSpec((1, D_MODEL), lambda i: (0, 0))
    return pl.pallas_call(
        body, name="bwd_ffn_in", grid=(tokens // tile,),
        in_specs=[pl.BlockSpec((tile, 2 * D_FF), row),
                  pl.BlockSpec((D_MODEL, 2 * D_FF), lambda i: (0, 0), pipeline_mode=pl.Buffered(1)),
                  pl.BlockSpec((tile, D_MODEL), row), pl.BlockSpec((tile, D_MODEL), row), vec, vec],
        out_specs=[pl.BlockSpec((tile, D_MODEL), row), pl.BlockSpec((8, D_MODEL), lambda i: (0, 0))],
        out_shape=[jax.ShapeDtypeStruct((tokens, D_MODEL), f32), jax.ShapeDtypeStruct((8, D_MODEL), f32)],
        compiler_params=_params(),
    )(dgu_b, w_gu_b, dr2, r1, ln1_g, ln1_b)


def _bwd_mix(dr1, proj, small, w_out_b, tile, seq):
    tokens = dr1.shape[0]
    tps = seq // tile
    hb = tile // HALO
    steps = tokens // tile

    def body(dr1_ref, proj_ref, halo_ref, wout_ref, pw_ref, ps_ref, lg_ref, lb_ref, sw_ref, sb_ref,
             dpp_ref, gpw_ref, gsw_ref, gsb_ref, vec_ref, du_ref, dvln_ref):
        step = pl.program_id(0)
        seq_tile = step % tps

        @pl.when(step == 0)
        def _():
            gpw_ref[...] = jnp.zeros_like(gpw_ref)
            gsw_ref[...] = jnp.zeros_like(gsw_ref)
            gsb_ref[...] = jnp.zeros_like(gsb_ref)
            vec_ref[...] = jnp.zeros_like(vec_ref)

        dmix = _mm_nt(dr1_ref[...].astype(bf16), wout_ref[...])

        xp = proj_ref[:, :POOL_WIDTH]
        halo = jnp.where(seq_tile == 0, 0.0, halo_ref[...])
        pooled = _pooled_groups(xp, halo, _inv_counts(seq_tile, tile))
        for g in range(N_GROUPS):
            pb = pooled[g].astype(bf16)
            pwb = pw_ref[g].astype(bf16)
            dpo = dmix[:, _gs(g)]
            vec_ref[0:1, _gs(g)] += _col_sum(dpo * _mm(pb, pwb))
            dpo_b = (dpo * ps_ref[:, _gs(g)]).astype(bf16)
            gpw_ref[g] += _mm_tn(pb, dpo_b)
            dpp_ref[:, _gs(g)] = _mm_nt(dpo_b, pwb)

        pre = proj_ref[:, POOL_WIDTH:]
        cdf, pdf = _gelu_parts(pre)
        zg = pre * cdf
        u = zg[:, :SGU_WIDTH]
        vln, vhat, rstd = _ln_fwd(zg[:, SGU_WIDTH:], lg_ref[...], lb_ref[...])
        vb = vln.astype(bf16)
        mask = _tril_mask()
        for h in range(N_GROUPS):
            wm = (sw_ref[h] * mask).astype(bf16)
            bias = sb_ref[h]
            gsw = jnp.zeros((GROUP, GROUP), f32)
            gsb = jnp.zeros((GROUP, GROUP), f32)
            for n in range(tile // GROUP):
                rows = slice(n * GROUP, (n + 1) * GROUP)
                v_nh = vb[rows, _gs(h)]
                d = dmix[rows, POOL_WIDTH + h * GROUP:POOL_WIDTH + (h + 1) * GROUP]
                du_ref[rows, _gs(h)] = d * (_mm(wm, v_nh) + bias)
                dmixed = d * u[rows, _gs(h)]
                gsb += dmixed
                dmixed_b = dmixed.astype(bf16)
                gsw += _mm_nt(dmixed_b, v_nh)
                dvln_ref[rows, _gs(h)] = _mm_tn(wm, dmixed_b)
            gsw_ref[h] += gsw * mask
            gsb_ref[h] += gsb

        dvln = dvln_ref[...]
        vec_ref[1:2, :] += _col_sum(dvln * vhat)
        vec_ref[2:3, :] += _col_sum(dvln)
        dgelu = cdf + pre * pdf
        dpp_ref[:, POOL_WIDTH:POOL_WIDTH + SGU_WIDTH] = du_ref[...] * dgelu[:, :SGU_WIDTH]
        dpp_ref[:, POOL_WIDTH + SGU_WIDTH:] = _ln_bwd(dvln, vhat, rstd, lg_ref[...]) * dgelu[:, SGU_WIDTH:]

        @pl.when(step == steps - 1)
        def _():
            for h in range(N_GROUPS):
                gsb_ref[h] = jnp.broadcast_to(jnp.sum(gsb_ref[h], axis=1, keepdims=True), (GROUP, GROUP))

    row = lambda i: (i, 0)
    sq = jax.ShapeDtypeStruct((N_GROUPS, GROUP, GROUP), f32)
    sq_spec = pl.BlockSpec((N_GROUPS, GROUP, GROUP), lambda i: (0, 0, 0))
    return pl.pallas_call(
        body, name="bwd_mix", grid=(steps,),
        in_specs=[pl.BlockSpec((tile, D_MODEL), row), pl.BlockSpec((tile, IN_COLS), row),
                  pl.BlockSpec((HALO, POOL_WIDTH), lambda i: (jnp.maximum(i * hb - 1, 0), 0)),
                  pl.BlockSpec((D_MODEL, D_MODEL), lambda i: (0, 0))] + _small_specs(),
        out_specs=[pl.BlockSpec((tile, IN_COLS), row), sq_spec, sq_spec, sq_spec,
                   pl.BlockSpec((8, POOL_WIDTH), lambda i: (0, 0))],
        out_shape=[jax.ShapeDtypeStruct((tokens, IN_COLS), f32), sq, sq, sq,
                   jax.ShapeDtypeStruct((8, POOL_WIDTH), f32)],
        scratch_shapes=[pltpu.VMEM((tile, SGU_WIDTH), f32), pltpu.VMEM((tile, SGU_WIDTH), f32)],
        compiler_params=_params(),
    )(dr1, proj, proj, w_out_b, *small)


def _bwd_in(dpp, dr1, w_in_b, tile, seq):
    tokens = dr1.shape[0]
    tps = seq // tile
    hb = tile // HALO
    last_halo = tokens // HALO - 1

    def body(dpp_ref, nxt_ref, dr1_ref, w_ref, dx_ref, dproj_ref):
        seq_tile = pl.program_id(0) % tps
        inv = _inv_counts(seq_tile, tile)
        dpl = dpp_ref[:, :POOL_WIDTH]
        nxt = jnp.where(seq_tile == tps - 1, 0.0, nxt_ref[...])
        scaled = jnp.concatenate([dpl[:, _gs(g)] * inv[g] for g in range(N_GROUPS)], axis=1)
        scaled_nxt = jnp.concatenate([nxt[:, _gs(g)] * (1.0 / POOL_WINDOWS[g]) for g in range(N_GROUPS)], axis=1)
        sums = _window_sums(jnp.concatenate([scaled, scaled_nxt], axis=0), back=False)
        for g in range(N_GROUPS):
            dproj_ref[:, _gs(g)] = (sums[g][:tile] - dpl[:, _gs(g)]).astype(bf16)
        dproj_ref[:, POOL_WIDTH:] = dpp_ref[:, POOL_WIDTH:].astype(bf16)
        dx_ref[...] = ALPHA * dr1_ref[...] + _mm_nt(dproj_ref[...], w_ref[...])

    row = lambda i: (i, 0)
    return pl.pallas_call(
        body, name="bwd_in", grid=(tokens // tile,),
        in_specs=[pl.BlockSpec((tile, IN_COLS), row),
                  pl.BlockSpec((HALO, POOL_WIDTH), lambda i: (jnp.minimum((i + 1) * hb, last_halo), 0)),
                  pl.BlockSpec((tile, D_MODEL), row),
                  pl.BlockSpec((D_MODEL, IN_COLS), lambda i: (0, 0))],
        out_specs=[pl.BlockSpec((tile, D_MODEL), row), pl.BlockSpec((tile, IN_COLS), row)],
        out_shape=[jax.ShapeDtypeStruct((tokens, D_MODEL), f32), jax.ShapeDtypeStruct((tokens, IN_COLS), bf16)],
        compiler_params=_params(),
    )(dpp, dpp, dr1, w_in_b)


def _wgrad(a, b, col_tile, tile, name):
    tokens, m = a.shape
    n = b.shape[1]

    def body(a_ref, b_ref, o_ref):
        @pl.when(pl.program_id(1) == 0)
        def _():
            o_ref[...] = jnp.zeros_like(o_ref)

        o_ref[...] += _mm_tn(a_ref[...].astype(bf16), b_ref[...].astype(bf16))

    return pl.pallas_call(
        body, name=name, grid=(n // col_tile, tokens // tile),
        in_specs=[pl.BlockSpec((tile, m), lambda j, k: (k, 0)),
                  pl.BlockSpec((tile, col_tile), lambda j, k: (k, j))],
        out_specs=pl.BlockSpec((m, col_tile), lambda j, k: (0, j)),
        out_shape=jax.ShapeDtypeStruct((m, n), f32),
        compiler_params=_params(),
    )(a, b)


class _Cut:
    def __init__(self, rows, cols, by_cols):
        self.rows, self.cols, self.by_cols = rows, cols, by_cols
        if by_cols:
            self.block_shape = (rows // 2, cols // N_CHIPS)
            self.shard_shape = (rows, cols // N_CHIPS)
        else:
            self.block_shape = (rows // (2 * N_CHIPS), cols)
            self.shard_shape = (rows // N_CHIPS, cols)

    def block(self, ref, chip, half):
        br, bc = self.block_shape
        if self.by_cols:
            return ref.at[pl.ds(pl.multiple_of(half * br, 16), br), pl.ds(pl.multiple_of(chip * bc, 128), bc)]
        return ref.at[pl.ds(pl.multiple_of((2 * chip + half) * br, 8), br), :]

    def shard(self, ref, chip):
        sr, sc = self.shard_shape
        if self.by_cols:
            return ref.at[:, pl.ds(pl.multiple_of(chip * sc, 128), sc)]
        return ref.at[pl.ds(pl.multiple_of(chip * sr, 16), sr), :]

    def half_of_shard(self, ref, half):
        br = self.block_shape[0]
        return ref.at[pl.ds(pl.multiple_of(half * br, 8), br), :]

    def block_index(self, chip, half):
        return (half, chip) if self.by_cols else (2 * chip + half, 0)


CUTS = {
    "w_in": _Cut(D_MODEL, IN_COLS, True),
    "w_out": _Cut(D_MODEL, D_MODEL, False),
    "w_gate_up": _Cut(D_MODEL, 2 * D_FF, True),
    "w_down": _Cut(D_FF, D_MODEL, False),
    "small": _Cut(SMALL_ROWS, GROUP, False),
}
BIG = ("w_in", "w_out", "w_gate_up", "w_down")
ALL = BIG + ("small",)
ANY = pl.BlockSpec(memory_space=pl.ANY)


def _place():
    x, y, c = lax.axis_index("x"), lax.axis_index("y"), lax.axis_index("c")
    others = [(1 - x, y), (x, 1 - y), (1 - x, 1 - y)]
    return x, y, c, 2 * x + y, others, [2 * ox + oy for ox, oy in others]


def _remote(src, dst, send_sems, recv_sems, k, to):
    return pltpu.make_async_remote_copy(src_ref=src, dst_ref=dst, send_sem=send_sems.at[k], recv_sem=recv_sems.at[k],
                                        device_id=to, device_id_type=MESH)


def _gather_weights(shards):
    n = len(BIG)

    def body(*refs):
        shard_refs, full_refs = refs[:n], refs[n:2 * n]
        send_sems, recv_sems, local_sems = refs[2 * n:]
        x, y, c, me, others, other_ids = _place()
        sibling = (x, y, 1 - c)
        own = [pltpu.make_async_copy(shard_refs[w], CUTS[name].shard(full_refs[w], me), local_sems.at[w])
               for w, name in enumerate(BIG)]
        for cp in own:
            cp.start()
        sent = []
        for w, name in enumerate(BIG):
            cut = CUTS[name]
            for k, chip in enumerate(others):
                cp = _remote(cut.half_of_shard(shard_refs[w], c), cut.block(full_refs[w], me, c),
                             send_sems, recv_sems, 3 * w + k, (*chip, c))
                cp.start()
                sent.append(cp)
        for w, name in enumerate(BIG):
            cut = CUTS[name]
            for k, chip in enumerate(others):
                landed = cut.block(full_refs[w], other_ids[k], c)
                _remote(landed, landed, send_sems, recv_sems, 3 * w + k, (*chip, c)).wait_recv()
                cp = _remote(landed, landed, send_sems, recv_sems, 3 * n + 3 * w + k, sibling)
                cp.start()
                sent.append(cp)
        for w, name in enumerate(BIG):
            for k in range(3):
                passed = CUTS[name].block(full_refs[w], other_ids[k], 1 - c)
                _remote(passed, passed, send_sems, recv_sems, 3 * n + 3 * w + k, sibling).wait_recv()
        for cp in sent:
            cp.wait_send()
        for cp in own:
            cp.wait()

    return pl.pallas_call(
        body, name="gather_weights",
        in_specs=[ANY] * n, out_specs=[ANY] * n,
        out_shape=[jax.ShapeDtypeStruct((CUTS[name].rows, CUTS[name].cols), bf16) for name in BIG],
        scratch_shapes=[pltpu.SemaphoreType.DMA((6 * n,)), pltpu.SemaphoreType.DMA((6 * n,)),
                        pltpu.SemaphoreType.DMA((n,))],
    )(*[shards[name] for name in BIG])


def _swap_halves(grads):
    n = len(ALL)

    def body(*refs):
        g_refs, got_refs = refs[:n], refs[n:2 * n]
        send_sems, recv_sems = refs[2 * n:]
        x, y, c, _, _, _ = _place()
        copies = [_remote(CUTS[name].block(g_refs[a], j, 1 - c), got_refs[a].at[j], send_sems, recv_sems,
                          N_CHIPS * a + j, (x, y, 1 - c))
                  for a, name in enumerate(ALL) for j in range(N_CHIPS)]
        for cp in copies:
            cp.start()
        for cp in copies:
            cp.wait()

    return pl.pallas_call(
        body, name="reduce_swap_halves",
        in_specs=[ANY] * n, out_specs=[ANY] * n,
        out_shape=[jax.ShapeDtypeStruct((N_CHIPS, *CUTS[name].block_shape), f32) for name in ALL],
        scratch_shapes=[pltpu.SemaphoreType.DMA((N_CHIPS * n,)), pltpu.SemaphoreType.DMA((N_CHIPS * n,))],
    )(*[grads[name] for name in ALL])


def _add_halves(name, g, got, core):
    cut = CUTS[name]
    br, bc = cut.block_shape

    def body(core_ref, g_ref, got_ref, o_ref):
        o_ref[...] = g_ref[...] + got_ref[...]

    return pl.pallas_call(
        body, name="reduce_add_halves_" + name,
        grid_spec=pltpu.PrefetchScalarGridSpec(
            num_scalar_prefetch=1, grid=(N_CHIPS,),
            in_specs=[pl.BlockSpec((br, bc), lambda j, core_ref: cut.block_index(j, core_ref[0])),
                      pl.BlockSpec((None, br, bc), lambda j, core_ref: (j, 0, 0))],
            out_specs=pl.BlockSpec((None, br, bc), lambda j, core_ref: (j, 0, 0))),
        out_shape=jax.ShapeDtypeStruct((N_CHIPS, br, bc), f32),
        compiler_params=_params(),
    )(core, g, got)


def _swap_chips(partials):
    n = len(ALL)

    def body(*refs):
        p_refs, got_refs = refs[:n], refs[n:2 * n]
        send_sems, recv_sems = refs[2 * n:]
        _, _, c, _, others, other_ids = _place()
        copies = [_remote(p_refs[a].at[other_ids[k]], got_refs[a].at[k], send_sems, recv_sems, 3 * a + k, (*chip, c))
                  for a in range(n) for k, chip in enumerate(others)]
        for cp in copies:
            cp.start()
        for cp in copies:
            cp.wait()

    return pl.pallas_call(
        body, name="reduce_swap_chips",
        in_specs=[ANY] * n, out_specs=[ANY] * n,
        out_shape=[jax.ShapeDtypeStruct((3, *CUTS[name].block_shape), f32) for name in ALL],
        scratch_shapes=[pltpu.SemaphoreType.DMA((3 * n,)), pltpu.SemaphoreType.DMA((3 * n,))],
    )(*[partials[name] for name in ALL])


def _add_chips(name, partial, got, chip):
    br, bc = CUTS[name].block_shape
    rt = br // 2 if br % 16 == 0 else br

    def body(chip_ref, p_ref, got_ref, o_ref):
        o_ref[...] = ((p_ref[...] + got_ref[0]) + got_ref[1]) + got_ref[2]

    return pl.pallas_call(
        body, name="reduce_add_chips_" + name,
        grid_spec=pltpu.PrefetchScalarGridSpec(
            num_scalar_prefetch=1, grid=(br // rt,),
            in_specs=[pl.BlockSpec((None, rt, bc), lambda i, chip_ref: (chip_ref[0], i, 0)),
                      pl.BlockSpec((3, rt, bc), lambda i, chip_ref: (0, i, 0))],
            out_specs=pl.BlockSpec((rt, bc), lambda i, chip_ref: (i, 0))),
        out_shape=jax.ShapeDtypeStruct((br, bc), f32),
        compiler_params=_params(),
    )(chip, partial, got)


def _share_reduced(reduced):
    n = len(BIG)

    def body(*refs):
        f_refs, out_refs = refs[:n + 1], refs[n + 1:2 * n + 2]
        send_sems, recv_sems, local_sems = refs[2 * n + 2:]
        x, y, c, me, others, other_ids = _place()
        sibling = (x, y, 1 - c)
        small = CUTS["small"]
        own = [pltpu.make_async_copy(f_refs[a], CUTS[name].half_of_shard(out_refs[a], c), local_sems.at[a])
               for a, name in enumerate(BIG)]
        own.append(pltpu.make_async_copy(f_refs[n], small.block(out_refs[n], me, c), local_sems.at[n]))
        for cp in own:
            cp.start()
        swaps = []
        for a, name in enumerate(BIG):
            dst = CUTS[name].half_of_shard(out_refs[a], c)
            swaps.append(_remote(f_refs[a], dst, send_sems, recv_sems, a, sibling))
        swaps.append(_remote(f_refs[n], small.block(out_refs[n], me, c), send_sems, recv_sems, n, sibling))
        for cp in swaps:
            cp.start()
        sent = []
        for k, chip in enumerate(others):
            cp = _remote(f_refs[n], small.block(out_refs[n], me, c), send_sems, recv_sems, n + 1 + k, (*chip, c))
            cp.start()
            sent.append(cp)
        for k, chip in enumerate(others):
            landed = small.block(out_refs[n], other_ids[k], c)
            _remote(landed, landed, send_sems, recv_sems, n + 1 + k, (*chip, c)).wait_recv()
            cp = _remote(landed, landed, send_sems, recv_sems, n + 4 + k, sibling)
            cp.start()
            sent.append(cp)
        for a, name in enumerate(BIG):
            theirs = CUTS[name].half_of_shard(out_refs[a], 1 - c)
            _remote(theirs, theirs, send_sems, recv_sems, a, sibling).wait_recv()
        theirs = small.block(out_refs[n], me, 1 - c)
        _remote(theirs, theirs, send_sems, recv_sems, n, sibling).wait_recv()
        for k in range(3):
            passed = small.block(out_refs[n], other_ids[k], 1 - c)
            _remote(passed, passed, send_sems, recv_sems, n + 4 + k, sibling).wait_recv()
        for cp in swaps + sent:
            cp.wait_send()
        for cp in own:
            cp.wait()

    n_sems = n + 7
    return pl.pallas_call(
        body, name="reduce_share",
        in_specs=[ANY] * (n + 1), out_specs=[ANY] * (n + 1),
        out_shape=[jax.ShapeDtypeStruct(CUTS[name].shard_shape, f32) for name in BIG]
                  + [jax.ShapeDtypeStruct((SMALL_ROWS, GROUP), f32)],
        scratch_shapes=[pltpu.SemaphoreType.DMA((n_sems,)), pltpu.SemaphoreType.DMA((n_sems,)),
                        pltpu.SemaphoreType.DMA((n + 1,))],
    )(*[reduced[name] for name in ALL])


def _adamw(name, w, g, m, v):
    rows, cols = w.shape
    rt = rows // 4

    def body(w_ref, g_ref, m_ref, v_ref, d_ref, nm_ref, nv_ref):
        g = g_ref[...]
        nm = ADAM_B1 * m_ref[...] + (1.0 - ADAM_B1) * g
        nv = ADAM_B2 * v_ref[...] + (1.0 - ADAM_B2) * jnp.square(g)
        m_hat = nm / (1.0 - ADAM_B1 ** ADAM_STEP)
        v_hat = nv / (1.0 - ADAM_B2 ** ADAM_STEP)
        d_ref[...] = -ADAM_LR * (m_hat / (jnp.sqrt(v_hat) + ADAM_EPS) + ADAM_WD * w_ref[...])
        nm_ref[...] = nm
        nv_ref[...] = nv

    spec = pl.BlockSpec((rt, cols), lambda i: (i, 0))
    shape = jax.ShapeDtypeStruct((rows, cols), f32)
    return pl.pallas_call(
        body, name="adamw_" + name, grid=(rows // rt,),
        in_specs=[spec] * 4, out_specs=[spec] * 3, out_shape=[shape] * 3,
        compiler_params=_params(),
    )(w, g, m, v)


SMALL_NAMES = ("pool_w", "pool_scale", "sgu_ln_g", "sgu_ln_b", "sgu_w", "sgu_b", "ln1_g", "ln1_b", "ln2_g", "ln2_b")
WEIGHT_ORDER = ("w_in", "pool_w", "pool_scale", "sgu_ln_g", "sgu_ln_b", "sgu_w", "sgu_b", "w_out", "ln1_g", "ln1_b",
                "w_gate_up", "w_down", "ln2_g", "ln2_b")


def _pack_small(parts):
    rows = [parts[name].reshape(-1, GROUP) for name in SMALL_NAMES]
    used = sum(r.shape[0] for r in rows)
    return jnp.concatenate(rows + [jnp.zeros((SMALL_ROWS - used, GROUP), f32)], axis=0)


def _unpack_small(packed, shapes):
    out, at = {}, 0
    for name in SMALL_NAMES:
        rows = math.prod(shapes[name]) // GROUP
        out[name] = packed[at:at + rows].reshape(shapes[name])
        at += rows
    return out


def kernel(x, w_in, pool_w, pool_scale, sgu_ln_g, sgu_ln_b, sgu_w, sgu_b, w_out, ln1_g, ln1_b, w_gate_up, w_down, ln2_g, ln2_b, loss_target, m_w_in, m_pool_w, m_pool_scale, m_sgu_ln_g, m_sgu_ln_b, m_sgu_w, m_sgu_b, m_w_out, m_ln1_g, m_ln1_b, m_w_gate_up, m_w_down, m_ln2_g, m_ln2_b, v_w_in, v_pool_w, v_pool_scale, v_sgu_ln_g, v_sgu_ln_b, v_sgu_w, v_sgu_b, v_w_out, v_ln1_g, v_ln1_b, v_w_gate_up, v_w_down, v_ln2_g, v_ln2_b):
    given = dict(locals())
    batch, seq, _ = x.shape
    tokens = batch * seq
    tile = min(TOKEN_TILE, seq)
    ffn_bwd_tile = min(FFN_BWD_TILE, seq)
    shapes = {name: given[name].shape for name in WEIGHT_ORDER}

    x2 = x.reshape(tokens, D_MODEL)
    target = loss_target.reshape(tokens, D_MODEL)
    small = (pool_w[0], pool_scale[0][None], sgu_ln_g[0][None], sgu_ln_b[0][None], sgu_w[0],
             jnp.broadcast_to(sgu_b[0][:, :, None], (N_GROUPS, GROUP, GROUP)))
    g1, b1, g2, b2 = ln1_g[0][None], ln1_b[0][None], ln2_g[0][None], ln2_b[0][None]

    full = dict(zip(BIG, _gather_weights({name: given[name][0].astype(bf16) for name in BIG})))

    proj = _fwd_proj(x2, full["w_in"], tile)
    mix_b, r1, h_b = _fwd_mix(proj, x2, small, full["w_out"], g1, b1, tile, seq)
    gu_b, a_b = _fwd_gate_up(h_b, full["w_gate_up"], tile)
    dr2, stats2 = _fwd_down_loss(a_b, full["w_down"], r1, target, g1, b1, g2, b2, tile)

    dgu_b = _bwd_gate_up(dr2, gu_b, full["w_down"], ffn_bwd_tile)
    dr1, stats1 = _bwd_ffn_in(dgu_b, full["w_gate_up"], dr2, r1, g1, b1, tile)
    dpp, g_pool_w, g_sgu_w, g_sgu_b, vecs = _bwd_mix(dr1, proj, small, full["w_out"], tile, seq)
    grad_x, dproj_b = _bwd_in(dpp, dr1, full["w_in"], tile, seq)

    grads = {
        "w_down": _wgrad(a_b, dr2, D_MODEL // 2, tile, "wgrad_down"),
        "w_gate_up": _wgrad(h_b, dgu_b, 2 * D_FF // 4, tile, "wgrad_gate_up"),
        "w_out": _wgrad(mix_b, dr1, D_MODEL, tile, "wgrad_out"),
        "w_in": _wgrad(x2, dproj_b, IN_COLS // 2, tile, "wgrad_in"),
        "small": _pack_small({
            "pool_w": g_pool_w, "pool_scale": vecs[0], "sgu_ln_g": vecs[1], "sgu_ln_b": vecs[2],
            "sgu_w": g_sgu_w, "sgu_b": g_sgu_b[:, :, 0], "ln1_g": stats1[0], "ln1_b": stats1[1],
            "ln2_g": stats2[0], "ln2_b": stats2[1]}),
    }

    core = lax.axis_index("c").astype(jnp.int32).reshape(1)
    chip = (2 * lax.axis_index("x") + lax.axis_index("y")).astype(jnp.int32).reshape(1)
    got = dict(zip(ALL, _swap_halves(grads)))
    partials = {name: _add_halves(name, grads[name], got[name], core) for name in ALL}
    got = dict(zip(ALL, _swap_chips(partials)))
    reduced = {name: _add_chips(name, partials[name], got[name], chip) for name in ALL}
    shared = dict(zip(ALL, _share_reduced(reduced)))

    grad, delta, new_m, new_v = {}, {}, {}, {}
    for name in BIG:
        grad[name] = shared[name][None]
        d, nm, nv = _adamw(name, given[name][0], shared[name], given["m_" + name][0], given["v_" + name][0])
        delta[name], new_m[name], new_v[name] = d[None], nm[None], nv[None]
    packed = [_pack_small({name: given[pre + name] for name in SMALL_NAMES}) for pre in ("", "m_", "v_")]
    d, nm, nv = _adamw("small", packed[0], shared["small"], packed[1], packed[2])
    for out, packed in ((grad, shared["small"]), (delta, d), (new_m, nm), (new_v, nv)):
        out.update(_unpack_small(packed, shapes))

    loss = lax.psum(jnp.sum(stats2[2]) * (0.5 / D_MODEL), ("x", "y", "c"))
    return (loss, grad_x.reshape(x.shape), *[grad[name] for name in WEIGHT_ORDER],
            *[delta[name] for name in WEIGHT_ORDER], *[new_m[name] for name in WEIGHT_ORDER],
            *[new_v[name] for name in WEIGHT_ORDER])
```

```python
import math

import jax
import jax.numpy as jnp
from jax import lax
from jax.experimental import pallas as pl
from jax.experimental.pallas import tpu as pltpu

f32 = jnp.float32
bf16 = jnp.bfloat16
MESH = pl.DeviceIdType.MESH

D_MODEL = 1024
POOL_WIDTH = 512
SGU_WIDTH = 512
IN_COLS = POOL_WIDTH + 2 * SGU_WIDTH
D_FF = 2816
POOL_WINDOWS = (2, 4, 8, 16)
GROUP = 128
N_GROUPS = 4
HALO = 16
LN_EPS = 1e-5
ALPHA = float(2.0 ** 0.25)
N_CHIPS = 4

ADAM_LR = 0.001
ADAM_B1 = 0.9
ADAM_B2 = 0.999
ADAM_EPS = 1e-08
ADAM_WD = 0.01
ADAM_STEP = 10

TOKEN_TILE = 512
FFN_BWD_TILE = 256
FF_CHUNK = D_FF // 2
V7X_VMEM_LIMIT = 56 * 1024 * 1024

SMALL_ROWS = 1088
LOSS_ROWS = D_MODEL // GROUP


def _params(**kw):
    return pltpu.CompilerParams(vmem_limit_bytes=V7X_VMEM_LIMIT, **kw)


def _mm(a, b):
    return jnp.dot(a, b, preferred_element_type=f32)


def _mm_nt(a, b):
    return lax.dot_general(a, b, (((1,), (1,)), ((), ())), preferred_element_type=f32)


def _mm_tn(a, b):
    return lax.dot_general(a, b, (((0,), (0,)), ((), ())), preferred_element_type=f32)


def _ln_fwd(r, g, b):
    mu = jnp.mean(r, axis=-1, keepdims=True)
    xc = r - mu
    var = jnp.mean(xc * xc, axis=-1, keepdims=True)
    rstd = lax.rsqrt(var + LN_EPS)
    xhat = xc * rstd
    return xhat * g + b, xhat, rstd


def _ln_bwd(dout, xhat, rstd, g):
    dxhat = dout * g
    m1 = jnp.mean(dxhat, axis=-1, keepdims=True)
    m2 = jnp.mean(dxhat * xhat, axis=-1, keepdims=True)
    return rstd * (dxhat - m1 - xhat * m2)


def _col_sum(a):
    return jnp.sum(a, axis=0, keepdims=True)


def _gelu_parts(z):
    cdf = 0.5 * (1.0 + lax.erf(z * (1.0 / math.sqrt(2.0))))
    pdf = jnp.exp(-0.5 * z * z) * (1.0 / math.sqrt(2.0 * math.pi))
    return cdf, pdf


def _inv_counts(seq_tile, rows):
    pos = seq_tile * rows + lax.broadcasted_iota(jnp.int32, (rows, GROUP), 0) + 1
    return [1.0 / jnp.minimum(pos, w).astype(f32) for w in POOL_WINDOWS]


def _window_sums(e, back):
    n = e.shape[0]

    def shifted(a, s):
        return pltpu.roll(a, s if back else n - s, 0)

    s2 = e + shifted(e, 1)
    s4 = s2[:, GROUP:] + shifted(s2[:, GROUP:], 2)
    s8 = s4[:, GROUP:] + shifted(s4[:, GROUP:], 4)
    s16 = s8[:, GROUP:] + shifted(s8[:, GROUP:], 8)
    return [s2[:, :GROUP], s4[:, :GROUP], s8[:, :GROUP], s16]


def _pooled_groups(xp, halo, inv):
    sums = _window_sums(jnp.concatenate([halo, xp], axis=0), back=True)
    return [sums[g][HALO:] * inv[g] - xp[:, g * GROUP:(g + 1) * GROUP] for g in range(N_GROUPS)]


def _tril_mask():
    r = lax.broadcasted_iota(jnp.int32, (GROUP, GROUP), 0)
    c = lax.broadcasted_iota(jnp.int32, (GROUP, GROUP), 1)
    return (r >= c).astype(f32)


def _gs(g):
    return slice(g * GROUP, (g + 1) * GROUP)


class _Cut:
    def __init__(self, rows, cols, by_cols):
        self.rows, self.cols, self.by_cols = rows, cols, by_cols
        if by_cols:
            self.block_shape = (rows // 2, cols // N_CHIPS)
            self.shard_shape = (rows, cols // N_CHIPS)
        else:
            self.block_shape = (rows // (2 * N_CHIPS), cols)
            self.shard_shape = (rows // N_CHIPS, cols)

    def block(self, ref, chip, half):
        br, bc = self.block_shape
        if self.by_cols:
            return ref.at[pl.ds(pl.multiple_of(half * br, 16), br), pl.ds(pl.multiple_of(chip * bc, 128), bc)]
        return ref.at[pl.ds(pl.multiple_of((2 * chip + half) * br, 8), br), :]

    def shard(self, ref, chip):
        sr, sc = self.shard_shape
        if self.by_cols:
            return ref.at[:, pl.ds(pl.multiple_of(chip * sc, 128), sc)]
        return ref.at[pl.ds(pl.multiple_of(chip * sr, 16), sr), :]

    def half_of_shard(self, ref, half):
        br = self.block_shape[0]
        return ref.at[pl.ds(pl.multiple_of(half * br, 8), br), :]

    def block_index(self, chip, half):
        return (half, chip) if self.by_cols else (2 * chip + half, 0)


CUTS = {
    "w_in": _Cut(D_MODEL, IN_COLS, True),
    "w_out": _Cut(D_MODEL, D_MODEL, False),
    "w_gate_up": _Cut(D_MODEL, 2 * D_FF, True),
    "w_down": _Cut(D_FF, D_MODEL, False),
    "small": _Cut(SMALL_ROWS, GROUP, False),
}
BIG = ("w_in", "w_out", "w_gate_up", "w_down")
ALL = BIG + ("small",)
ANY = pl.BlockSpec(memory_space=pl.ANY)


def _wire_dtype(name):
    return bf16 if name in BIG else f32


def _place():
    x, y, c = lax.axis_index("x"), lax.axis_index("y"), lax.axis_index("c")
    others = [(1 - x, y), (x, 1 - y), (1 - x, 1 - y)]
    return x, y, c, 2 * x + y, others, [2 * ox + oy for ox, oy in others]


def _remote(src, dst, send_sems, recv_sems, k, to):
    return pltpu.make_async_remote_copy(src_ref=src, dst_ref=dst, send_sem=send_sems.at[k], recv_sem=recv_sems.at[k],
                                        device_id=to, device_id_type=MESH)


class _Gather:
    def __init__(self, names, shard_refs, full_refs, scratch):
        self.names, self.shards, self.full = names, shard_refs, full_refs
        self.send_sems, self.recv_sems, self.local_sems = scratch[:3]
        self.stages = scratch[3:]

    @staticmethod
    def scratch_shapes(names):
        n = len(names)
        return ([pltpu.SemaphoreType.DMA((6 * n,)), pltpu.SemaphoreType.DMA((6 * n,)),
                 pltpu.SemaphoreType.DMA((2 * n,))] + [pltpu.VMEM(CUTS[name].shard_shape, bf16) for name in names])

    @staticmethod
    def full_shapes(names):
        return [jax.ShapeDtypeStruct((CUTS[name].rows, CUTS[name].cols), bf16) for name in names]

    def _sends(self):
        _, _, c, me, others, _ = _place()
        return [_remote(CUTS[name].half_of_shard(self.shards[w], c), CUTS[name].block(self.full[w], me, c),
                        self.send_sems, self.recv_sems, 3 * w + k, (*chip, c))
                for w, name in enumerate(self.names) for k, chip in enumerate(others)]

    def _relays(self, half):
        x, y, c, _, _, other_ids = _place()
        n = len(self.names)
        return [_remote(CUTS[name].block(self.full[w], other_ids[k], half),
                        CUTS[name].block(self.full[w], other_ids[k], half),
                        self.send_sems, self.recv_sems, 3 * n + 3 * w + k, (x, y, 1 - c))
                for w, name in enumerate(self.names) for k in range(3)]

    def _stores(self):
        me = _place()[3]
        return [pltpu.make_async_copy(self.stages[w], CUTS[name].shard(self.full[w], me), self.local_sems.at[2 * w + 1])
                for w, name in enumerate(self.names)]

    def start(self):
        loads = [pltpu.make_async_copy(self.shards[w], self.stages[w], self.local_sems.at[2 * w])
                 for w in range(len(self.names))]
        for cp in loads:
            cp.start()
        for cp in self._sends():
            cp.start()
        for load, store in zip(loads, self._stores()):
            load.wait()
            store.start()

    def relay(self):
        _, _, c, _, others, other_ids = _place()
        relays = self._relays(c)
        for w, name in enumerate(self.names):
            for k, chip in enumerate(others):
                landed = CUTS[name].block(self.full[w], other_ids[k], c)
                _remote(landed, landed, self.send_sems, self.recv_sems, 3 * w + k, (*chip, c)).wait_recv()
                relays[3 * w + k].start()

    def finish(self):
        c = _place()[2]
        for cp in self._relays(1 - c):
            cp.wait_recv()
        for cp in self._sends() + self._relays(c):
            cp.wait_send()
        for cp in self._stores():
            cp.wait()


def _gather_weights(shards):
    names = tuple(shards)
    n = len(names)

    def body(*refs):
        gather = _Gather(names, refs[:n], refs[n:2 * n], refs[2 * n:])
        gather.start()
        gather.relay()
        gather.finish()

    return pl.pallas_call(
        body, name="gather_weights",
        in_specs=[ANY] * n, out_specs=[ANY] * n, out_shape=_Gather.full_shapes(names),
        scratch_shapes=_Gather.scratch_shapes(names),
        compiler_params=_params(),
    )(*shards.values())


def _carry_gather(names, steps, shard_refs, full_refs, scratch):
    gather = _Gather(names, shard_refs, full_refs, scratch)

    def before():
        @pl.when(pl.program_id(0) == 0)
        def _():
            gather.start()

    def after():
        @pl.when(pl.program_id(0) == steps - 1)
        def _():
            gather.relay()
            gather.finish()

    return before, after


def _fwd_proj(x2, w_in_b, carried, tile):
    tokens = x2.shape[0]
    steps = tokens // tile
    names = tuple(carried)
    n = len(names)

    def body(x_ref, w_ref, *rest):
        before, after = _carry_gather(names, steps, rest[:n], rest[n + 1:2 * n + 1], rest[2 * n + 1:])
        before()
        rest[n][...] = _mm(x_ref[...].astype(bf16), w_ref[...])
        after()

    return pl.pallas_call(
        body, name="fwd_proj", grid=(steps,),
        in_specs=[pl.BlockSpec((tile, D_MODEL), lambda i: (i, 0)),
                  pl.BlockSpec((D_MODEL, IN_COLS), lambda i: (0, 0))] + [ANY] * n,
        out_specs=[pl.BlockSpec((tile, IN_COLS), lambda i: (i, 0))] + [ANY] * n,
        out_shape=[jax.ShapeDtypeStruct((tokens, IN_COLS), f32)] + _Gather.full_shapes(names),
        scratch_shapes=_Gather.scratch_shapes(names),
        compiler_params=_params(),
    )(x2, w_in_b, *carried.values())


def _small_specs():
    return [pl.BlockSpec((N_GROUPS, GROUP, GROUP), lambda i: (0, 0, 0)),
            pl.BlockSpec((1, POOL_WIDTH), lambda i: (0, 0)),
            pl.BlockSpec((1, SGU_WIDTH), lambda i: (0, 0)),
            pl.BlockSpec((1, SGU_WIDTH), lambda i: (0, 0)),
            pl.BlockSpec((N_GROUPS, GROUP, GROUP), lambda i: (0, 0, 0)),
            pl.BlockSpec((N_GROUPS, GROUP, GROUP), lambda i: (0, 0, 0))]


def _fwd_mix(proj, x2, small, w_out_b, ln1_g, ln1_b, carried, tile, seq):
    tokens = x2.shape[0]
    steps = tokens // tile
    tps = seq // tile
    hb = tile // HALO
    names = tuple(carried)
    n = len(names)

    def body(proj_ref, halo_ref, x_ref, pw_ref, ps_ref, lg_ref, lb_ref, sw_ref, sb_ref, wout_ref, g1_ref, b1_ref,
             *rest):
        mix_ref, r1_ref, h_ref = rest[n:n + 3]
        before, after = _carry_gather(names, steps, rest[:n], rest[n + 3:2 * n + 3], rest[2 * n + 3:])
        before()
        seq_tile = pl.program_id(0) % tps
        xp = proj_ref[:, :POOL_WIDTH]
        halo = jnp.where(seq_tile == 0, 0.0, halo_ref[...])
        pooled = _pooled_groups(xp, halo, _inv_counts(seq_tile, tile))
        for g in range(N_GROUPS):
            po = _mm(pooled[g].astype(bf16), pw_ref[g].astype(bf16)) * ps_ref[:, _gs(g)]
            mix_ref[:, _gs(g)] = po.astype(bf16)

        cdf, _ = _gelu_parts(proj_ref[:, POOL_WIDTH:])
        zg = proj_ref[:, POOL_WIDTH:] * cdf
        u = zg[:, :SGU_WIDTH]
        vln, _, _ = _ln_fwd(zg[:, SGU_WIDTH:], lg_ref[...], lb_ref[...])
        vb = vln.astype(bf16)
        mask = _tril_mask()
        for h in range(N_GROUPS):
            wm = (sw_ref[h] * mask).astype(bf16)
            bias = sb_ref[h]
            for c in range(tile // GROUP):
                rows = slice(c * GROUP, (c + 1) * GROUP)
                mixed = _mm(wm, vb[rows, _gs(h)]) + bias
                mix_ref[rows, POOL_WIDTH + h * GROUP:POOL_WIDTH + (h + 1) * GROUP] = (u[rows, _gs(h)] * mixed).astype(bf16)

        r1 = ALPHA * x_ref[...] + _mm(mix_ref[...], wout_ref[...])
        r1_ref[...] = r1
        h1, _, _ = _ln_fwd(r1, g1_ref[...], b1_ref[...])
        h_ref[...] = h1.astype(bf16)
        after()

    row = lambda i: (i, 0)
    vec = pl.BlockSpec((1, D_MODEL), lambda i: (0, 0))
    return pl.pallas_call(
        body, name="fwd_mix", grid=(steps,),
        in_specs=[pl.BlockSpec((tile, IN_COLS), row),
                  pl.BlockSpec((HALO, POOL_WIDTH), lambda i: (jnp.maximum(i * hb - 1, 0), 0)),
                  pl.BlockSpec((tile, D_MODEL), row)] + _small_specs()
                 + [pl.BlockSpec((D_MODEL, D_MODEL), lambda i: (0, 0)), vec, vec] + [ANY] * n,
        out_specs=[pl.BlockSpec((tile, D_MODEL), row)] * 3 + [ANY] * n,
        out_shape=[jax.ShapeDtypeStruct((tokens, D_MODEL), bf16),
                   jax.ShapeDtypeStruct((tokens, D_MODEL), f32),
                   jax.ShapeDtypeStruct((tokens, D_MODEL), bf16)] + _Gather.full_shapes(names),
        scratch_shapes=_Gather.scratch_shapes(names),
        compiler_params=_params(),
    )(proj, proj, x2, *small, w_out_b, ln1_g, ln1_b, *carried.values())


def _fwd_gate_up(h_b, w_gu_b, tile):
    tokens = h_b.shape[0]

    def body(h_ref, w_ref, gu_ref, a_ref):
        hb = h_ref[...]
        for c in range(D_FF // FF_CHUNK):
            gcols = slice(c * FF_CHUNK, (c + 1) * FF_CHUNK)
            ucols = slice(D_FF + c * FF_CHUNK, D_FF + (c + 1) * FF_CHUNK)
            gate = _mm(hb, w_ref[:, gcols])
            up = _mm(hb, w_ref[:, ucols])
            gu_ref[:, gcols] = gate.astype(bf16)
            gu_ref[:, ucols] = up.astype(bf16)
            a_ref[:, gcols] = (gate * jax.nn.sigmoid(gate) * up).astype(bf16)

    return pl.pallas_call(
        body, name="fwd_gate_up", grid=(tokens // tile,),
        in_specs=[pl.BlockSpec((tile, D_MODEL), lambda i: (i, 0)),
                  pl.BlockSpec((D_MODEL, 2 * D_FF), lambda i: (0, 0), pipeline_mode=pl.Buffered(1))],
        out_specs=[pl.BlockSpec((tile, 2 * D_FF), lambda i: (i, 0)),
                   pl.BlockSpec((tile, D_FF), lambda i: (i, 0))],
        out_shape=[jax.ShapeDtypeStruct((tokens, 2 * D_FF), bf16),
                   jax.ShapeDtypeStruct((tokens, D_FF), bf16)],
        compiler_params=_params(),
    )(h_b, w_gu_b)


def _fwd_down_loss(a_b, w_dn_b, r1, target, ln1_g, ln1_b, ln2_g, ln2_b, tile):
    tokens = a_b.shape[0]

    def body(a_ref, w_ref, r1_ref, t_ref, g1_ref, b1_ref, g2_ref, b2_ref, dr2_ref, st_ref):
        @pl.when(pl.program_id(0) == 0)
        def _():
            st_ref[...] = jnp.zeros_like(st_ref)

        h1, _, _ = _ln_fwd(r1_ref[...], g1_ref[...], b1_ref[...])
        r2 = ALPHA * h1 + _mm(a_ref[...], w_ref[...])
        y, xhat, rstd = _ln_fwd(r2, g2_ref[...], b2_ref[...])
        diff = y - t_ref[...]
        dy = diff * (1.0 / D_MODEL)
        st_ref[0:1, :] += _col_sum(dy * xhat)
        st_ref[1:2, :] += _col_sum(dy)
        st_ref[2:3, :] += _col_sum(diff * diff)
        dr2_ref[...] = _ln_bwd(dy, xhat, rstd, g2_ref[...])

    row = lambda i: (i, 0)
    vec = pl.BlockSpec((1, D_MODEL), lambda i: (0, 0))
    return pl.pallas_call(
        body, name="fwd_down_loss", grid=(tokens // tile,),
        in_specs=[pl.BlockSpec((tile, D_FF), row), pl.BlockSpec((D_FF, D_MODEL), lambda i: (0, 0)),
                  pl.BlockSpec((tile, D_MODEL), row), pl.BlockSpec((tile, D_MODEL), row), vec, vec, vec, vec],
        out_specs=[pl.BlockSpec((tile, D_MODEL), row), pl.BlockSpec((8, D_MODEL), lambda i: (0, 0))],
        out_shape=[jax.ShapeDtypeStruct((tokens, D_MODEL), f32), jax.ShapeDtypeStruct((8, D_MODEL), f32)],
        compiler_params=_params(),
    )(a_b, w_dn_b, r1, target, ln1_g, ln1_b, ln2_g, ln2_b)


def _bwd_gate_up(dr2, gu_b, w_dn_b, tile):
    tokens = dr2.shape[0]

    def body(d_ref, gu_ref, w_ref, dgu_ref):
        d = d_ref[...].astype(bf16)
        for c in range(D_FF // FF_CHUNK):
            gcols = slice(c * FF_CHUNK, (c + 1) * FF_CHUNK)
            ucols = slice(D_FF + c * FF_CHUNK, D_FF + (c + 1) * FF_CHUNK)
            da = _mm_nt(d, w_ref[gcols, :])
            gate = gu_ref[:, gcols].astype(f32)
            up = gu_ref[:, ucols].astype(f32)
            sg = jax.nn.sigmoid(gate)
            dgu_ref[:, gcols] = (da * up * (sg * (1.0 + gate * (1.0 - sg)))).astype(bf16)
            dgu_ref[:, ucols] = (da * (gate * sg)).astype(bf16)

    return pl.pallas_call(
        body, name="bwd_gate_up", grid=(tokens // tile,),
        in_specs=[pl.BlockSpec((tile, D_MODEL), lambda i: (i, 0)),
                  pl.BlockSpec((tile, 2 * D_FF), lambda i: (i, 0)),
                  pl.BlockSpec((D_FF, D_MODEL), lambda i: (0, 0))],
        out_specs=pl.BlockSpec((tile, 2 * D_FF), lambda i: (i, 0)),
        out_shape=jax.ShapeDtypeStruct((tokens, 2 * D_FF), bf16),
        compiler_params=_params(),
    )(dr2, gu_b, w_dn_b)


def _bwd_ffn_in(dgu_b, w_gu_b, dr2, r1, ln1_g, ln1_b, tile):
    tokens = dr2.shape[0]

    def body(dgu_ref, w_ref, d_ref, r1_ref, g1_ref, b1_ref, dr1_ref, st_ref):
        @pl.when(pl.program_id(0) == 0)
        def _():
            st_ref[...] = jnp.zeros_like(st_ref)

        dh = ALPHA * d_ref[...] + _mm_nt(dgu_ref[...], w_ref[...])
        _, xhat, rstd = _ln_fwd(r1_ref[...], g1_ref[...], b1_ref[...])
        st_ref[0:1, :] += _col_sum(dh * xhat)
        st_ref[1:2, :] += _col_sum(dh)
        dr1_ref[...] = _ln_bwd(dh, xhat, rstd, g1_ref[...])

    row = lambda i: (i, 0)
    vec = pl.BlockSpec((1, D_MODEL), lambda i: (0, 0))
    return pl.pallas_call(
        body, name="bwd_ffn_in", grid=(tokens // tile,),
        in_specs=[pl.BlockSpec((tile, 2 * D_FF), row),
                  pl.BlockSpec((D_MODEL, 2 * D_FF), lambda i: (0, 0), pipeline_mode=pl.Buffered(1)),
                  pl.BlockSpec((tile, D_MODEL), row), pl.BlockSpec((tile, D_MODEL), row), vec, vec],
        out_specs=[pl.BlockSpec((tile, D_MODEL), row), pl.BlockSpec((8, D_MODEL), lambda i: (0, 0))],
        out_shape=[jax.ShapeDtypeStruct((tokens, D_MODEL), f32), jax.ShapeDtypeStruct((8, D_MODEL), f32)],
        compiler_params=_params(),
    )(dgu_b, w_gu_b, dr2, r1, ln1_g, ln1_b)


def _bwd_mix(dr1, proj, small, w_out_b, tile, seq):
    tokens = dr1.shape[0]
    tps = seq // tile
    hb = tile // HALO
    steps = tokens // tile

    def body(dr1_ref, proj_ref, halo_ref, wout_ref, pw_ref, ps_ref, lg_ref, lb_ref, sw_ref, sb_ref,
             dpp_ref, gpw_ref, gsw_ref, gsb_ref, vec_ref, du_ref, dvln_ref):
        step = pl.program_id(0)
        seq_tile = step % tps

        @pl.when(step == 0)
        def _():
            gpw_ref[...] = jnp.zeros_like(gpw_ref)
            gsw_ref[...] = jnp.zeros_like(gsw_ref)
            gsb_ref[...] = jnp.zeros_like(gsb_ref)
            vec_ref[...] = jnp.zeros_like(vec_ref)

        dmix = _mm_nt(dr1_ref[...].astype(bf16), wout_ref[...])

        xp = proj_ref[:, :POOL_WIDTH]
        halo = jnp.where(seq_tile == 0, 0.0, halo_ref[...])
        pooled = _pooled_groups(xp, halo, _inv_counts(seq_tile, tile))
        for g in range(N_GROUPS):
            pb = pooled[g].astype(bf16)
            pwb = pw_ref[g].astype(bf16)
            dpo = dmix[:, _gs(g)]
            vec_ref[0:1, _gs(g)] += _col_sum(dpo * _mm(pb, pwb))
            dpo_b = (dpo * ps_ref[:, _gs(g)]).astype(bf16)
            gpw_ref[g] += _mm_tn(pb, dpo_b)
            dpp_ref[:, _gs(g)] = _mm_nt(dpo_b, pwb)

        pre = proj_ref[:, POOL_WIDTH:]
        cdf, pdf = _gelu_parts(pre)
        zg = pre * cdf
        u = zg[:, :SGU_WIDTH]
        vln, vhat, rstd = _ln_fwd(zg[:, SGU_WIDTH:], lg_ref[...], lb_ref[...])
        vb = vln.astype(bf16)
        mask = _tril_mask()
        for h in range(N_GROUPS):
            wm = (sw_ref[h] * mask).astype(bf16)
            bias = sb_ref[h]
            gsw = jnp.zeros((GROUP, GROUP), f32)
            gsb = jnp.zeros((GROUP, GROUP), f32)
            for c in range(tile // GROUP):
                rows = slice(c * GROUP, (c + 1) * GROUP)
                v_ch = vb[rows, _gs(h)]
                d = dmix[rows, POOL_WIDTH + h * GROUP:POOL_WIDTH + (h + 1) * GROUP]
                du_ref[rows, _gs(h)] = d * (_mm(wm, v_ch) + bias)
                dmixed = d * u[rows, _gs(h)]
                gsb += dmixed
                dmixed_b = dmixed.astype(bf16)
                gsw += _mm_nt(dmixed_b, v_ch)
                dvln_ref[rows, _gs(h)] = _mm_tn(wm, dmixed_b)
            gsw_ref[h] += gsw * mask
            gsb_ref[h] += gsb

        dvln = dvln_ref[...]
        vec_ref[1:2, :] += _col_sum(dvln * vhat)
        vec_ref[2:3, :] += _col_sum(dvln)
        dgelu = cdf + pre * pdf
        dpp_ref[:, POOL_WIDTH:POOL_WIDTH + SGU_WIDTH] = du_ref[...] * dgelu[:, :SGU_WIDTH]
        dpp_ref[:, POOL_WIDTH + SGU_WIDTH:] = _ln_bwd(dvln, vhat, rstd, lg_ref[...]) * dgelu[:, SGU_WIDTH:]

        @pl.when(step == steps - 1)
        def _():
            for h in range(N_GROUPS):
                gsb_ref[h] = jnp.broadcast_to(jnp.sum(gsb_ref[h], axis=1, keepdims=True), (GROUP, GROUP))

    row = lambda i: (i, 0)
    sq = jax.ShapeDtypeStruct((N_GROUPS, GROUP, GROUP), f32)
    sq_spec = pl.BlockSpec((N_GROUPS, GROUP, GROUP), lambda i: (0, 0, 0))
    return pl.pallas_call(
        body, name="bwd_mix", grid=(steps,),
        in_specs=[pl.BlockSpec((tile, D_MODEL), row), pl.BlockSpec((tile, IN_COLS), row),
                  pl.BlockSpec((HALO, POOL_WIDTH), lambda i: (jnp.maximum(i * hb - 1, 0), 0)),
                  pl.BlockSpec((D_MODEL, D_MODEL), lambda i: (0, 0))] + _small_specs(),
        out_specs=[pl.BlockSpec((tile, IN_COLS), row), sq_spec, sq_spec, sq_spec,
                   pl.BlockSpec((8, POOL_WIDTH), lambda i: (0, 0))],
        out_shape=[jax.ShapeDtypeStruct((tokens, IN_COLS), f32), sq, sq, sq,
                   jax.ShapeDtypeStruct((8, POOL_WIDTH), f32)],
        scratch_shapes=[pltpu.VMEM((tile, SGU_WIDTH), f32), pltpu.VMEM((tile, SGU_WIDTH), f32)],
        compiler_params=_params(),
    )(dr1, proj, proj, w_out_b, *small)


def _bwd_in(dpp, dr1, w_in_b, tile, seq):
    tokens = dr1.shape[0]
    tps = seq // tile
    hb = tile // HALO
    last_halo = tokens // HALO - 1

    def body(dpp_ref, nxt_ref, dr1_ref, w_ref, dx_ref, dproj_ref):
        seq_tile = pl.program_id(0) % tps
        inv = _inv_counts(seq_tile, tile)
        dpl = dpp_ref[:, :POOL_WIDTH]
        nxt = jnp.where(seq_tile == tps - 1, 0.0, nxt_ref[...])
        scaled = jnp.concatenate([dpl[:, _gs(g)] * inv[g] for g in range(N_GROUPS)], axis=1)
        scaled_nxt = jnp.concatenate([nxt[:, _gs(g)] * (1.0 / POOL_WINDOWS[g]) for g in range(N_GROUPS)], axis=1)
        sums = _window_sums(jnp.concatenate([scaled, scaled_nxt], axis=0), back=False)
        for g in range(N_GROUPS):
            dproj_ref[:, _gs(g)] = (sums[g][:tile] - dpl[:, _gs(g)]).astype(bf16)
        dproj_ref[:, POOL_WIDTH:] = dpp_ref[:, POOL_WIDTH:].astype(bf16)
        dx_ref[...] = ALPHA * dr1_ref[...] + _mm_nt(dproj_ref[...], w_ref[...])

    row = lambda i: (i, 0)
    return pl.pallas_call(
        body, name="bwd_in", grid=(tokens // tile,),
        in_specs=[pl.BlockSpec((tile, IN_COLS), row),
                  pl.BlockSpec((HALO, POOL_WIDTH), lambda i: (jnp.minimum((i + 1) * hb, last_halo), 0)),
                  pl.BlockSpec((tile, D_MODEL), row),
                  pl.BlockSpec((D_MODEL, IN_COLS), lambda i: (0, 0))],
        out_specs=[pl.BlockSpec((tile, D_MODEL), row), pl.BlockSpec((tile, IN_COLS), row)],
        out_shape=[jax.ShapeDtypeStruct((tokens, D_MODEL), f32), jax.ShapeDtypeStruct((tokens, IN_COLS), bf16)],
        compiler_params=_params(),
    )(dpp, dpp, dr1, w_in_b)


def _wgrad(a, b, col_tile, tile, name):
    tokens, m = a.shape
    n = b.shape[1]

    def body(a_ref, b_ref, o_ref):
        @pl.when(pl.program_id(1) == 0)
        def _():
            o_ref[...] = jnp.zeros_like(o_ref)

        o_ref[...] += _mm_tn(a_ref[...].astype(bf16), b_ref[...].astype(bf16))

    return pl.pallas_call(
        body, name=name, grid=(n // col_tile, tokens // tile),
        in_specs=[pl.BlockSpec((tile, m), lambda j, k: (k, 0)),
                  pl.BlockSpec((tile, col_tile), lambda j, k: (k, j))],
        out_specs=pl.BlockSpec((m, col_tile), lambda j, k: (0, j)),
        out_shape=jax.ShapeDtypeStruct((m, n), f32),
        compiler_params=_params(),
    )(a, b)


def _swap_halves(grads):
    n = len(ALL)

    def body(*refs):
        g_refs, got_refs = refs[:n], refs[n:2 * n]
        send_sems, recv_sems = refs[2 * n:]
        x, y, c, _, _, _ = _place()
        copies = [_remote(CUTS[name].block(g_refs[a], j, 1 - c), got_refs[a].at[j], send_sems, recv_sems,
                          N_CHIPS * a + j, (x, y, 1 - c))
                  for a, name in enumerate(ALL) for j in range(N_CHIPS)]
        for cp in copies:
            cp.start()
        for cp in copies:
            cp.wait()

    return pl.pallas_call(
        body, name="reduce_swap_halves",
        in_specs=[ANY] * n, out_specs=[ANY] * n,
        out_shape=[jax.ShapeDtypeStruct((N_CHIPS, *CUTS[name].block_shape), f32) for name in ALL],
        scratch_shapes=[pltpu.SemaphoreType.DMA((N_CHIPS * n,)), pltpu.SemaphoreType.DMA((N_CHIPS * n,))],
    )(*[grads[name] for name in ALL])


def _add_halves(name, g, got, core):
    cut = CUTS[name]
    br, bc = cut.block_shape
    wire = _wire_dtype(name)

    def body(core_ref, g_ref, got_ref, o_ref, wire_ref):
        s = g_ref[...] + got_ref[...]
        o_ref[...] = s
        wire_ref[...] = s.astype(wire)

    blocks = pl.BlockSpec((None, br, bc), lambda j, core_ref: (j, 0, 0))
    return pl.pallas_call(
        body, name="reduce_add_halves_" + name,
        grid_spec=pltpu.PrefetchScalarGridSpec(
            num_scalar_prefetch=1, grid=(N_CHIPS,),
            in_specs=[pl.BlockSpec((br, bc), lambda j, core_ref: cut.block_index(j, core_ref[0])), blocks],
            out_specs=[blocks, blocks]),
        out_shape=[jax.ShapeDtypeStruct((N_CHIPS, br, bc), f32), jax.ShapeDtypeStruct((N_CHIPS, br, bc), wire)],
        compiler_params=_params(),
    )(core, g, got)


def _swap_chips(partials):
    n = len(ALL)

    def body(*refs):
        p_refs, got_refs = refs[:n], refs[n:2 * n]
        send_sems, recv_sems = refs[2 * n:]
        _, _, c, _, others, other_ids = _place()
        copies = [_remote(p_refs[a].at[other_ids[k]], got_refs[a].at[k], send_sems, recv_sems, 3 * a + k, (*chip, c))
                  for a in range(n) for k, chip in enumerate(others)]
        for cp in copies:
            cp.start()
        for cp in copies:
            cp.wait()

    return pl.pallas_call(
        body, name="reduce_swap_chips",
        in_specs=[ANY] * n, out_specs=[ANY] * n,
        out_shape=[jax.ShapeDtypeStruct((3, *CUTS[name].block_shape), _wire_dtype(name)) for name in ALL],
        scratch_shapes=[pltpu.SemaphoreType.DMA((3 * n,)), pltpu.SemaphoreType.DMA((3 * n,))],
    )(*[partials[name] for name in ALL])


def _add_chips(name, partial, got, chip):
    br, bc = CUTS[name].block_shape
    rt = br // 2 if br % 32 == 0 else br

    def body(chip_ref, p_ref, got_ref, o_ref):
        o_ref[...] = ((p_ref[...] + got_ref[0].astype(f32)) + got_ref[1].astype(f32)) + got_ref[2].astype(f32)

    return pl.pallas_call(
        body, name="reduce_add_chips_" + name,
        grid_spec=pltpu.PrefetchScalarGridSpec(
            num_scalar_prefetch=1, grid=(br // rt,),
            in_specs=[pl.BlockSpec((None, rt, bc), lambda i, chip_ref: (chip_ref[0], i, 0)),
                      pl.BlockSpec((3, rt, bc), lambda i, chip_ref: (0, i, 0))],
            out_specs=pl.BlockSpec((rt, bc), lambda i, chip_ref: (i, 0))),
        out_shape=jax.ShapeDtypeStruct((br, bc), f32),
        compiler_params=_params(),
    )(chip, partial, got)


def _share_reduced(reduced):
    n = len(BIG)

    def body(*refs):
        f_refs, out_refs = refs[:n + 1], refs[n + 1:2 * n + 2]
        send_sems, recv_sems, local_sems = refs[2 * n + 2:2 * n + 5]
        stages = refs[2 * n + 5:]
        x, y, c, me, others, other_ids = _place()
        sibling = (x, y, 1 - c)
        small = CUTS["small"]
        mine = [CUTS[name].half_of_shard(out_refs[a], c) for a, name in enumerate(BIG)]
        mine.append(small.block(out_refs[n], me, c))
        loads = [pltpu.make_async_copy(f_refs[a], stages[a], local_sems.at[2 * a]) for a in range(n + 1)]
        stores = [pltpu.make_async_copy(stages[a], mine[a], local_sems.at[2 * a + 1]) for a in range(n + 1)]
        for cp in loads:
            cp.start()
        swaps = [_remote(f_refs[a], mine[a], send_sems, recv_sems, a, sibling) for a in range(n + 1)]
        for cp in swaps:
            cp.start()
        sent = []
        for k, chip in enumerate(others):
            cp = _remote(f_refs[n], mine[n], send_sems, recv_sems, n + 1 + k, (*chip, c))
            cp.start()
            sent.append(cp)
        for load, store in zip(loads, stores):
            load.wait()
            store.start()
        for k, chip in enumerate(others):
            landed = small.block(out_refs[n], other_ids[k], c)
            _remote(landed, landed, send_sems, recv_sems, n + 1 + k, (*chip, c)).wait_recv()
            cp = _remote(landed, landed, send_sems, recv_sems, n + 4 + k, sibling)
            cp.start()
            sent.append(cp)
        for a, name in enumerate(BIG):
            theirs = CUTS[name].half_of_shard(out_refs[a], 1 - c)
            _remote(theirs, theirs, send_sems, recv_sems, a, sibling).wait_recv()
        theirs = small.block(out_refs[n], me, 1 - c)
        _remote(theirs, theirs, send_sems, recv_sems, n, sibling).wait_recv()
        for k in range(3):
            passed = small.block(out_refs[n], other_ids[k], 1 - c)
            _remote(passed, passed, send_sems, recv_sems, n + 4 + k, sibling).wait_recv()
        for cp in swaps + sent:
            cp.wait_send()
        for cp in stores:
            cp.wait()

    n_sems = n + 7
    return pl.pallas_call(
        body, name="reduce_share",
        in_specs=[ANY] * (n + 1), out_specs=[ANY] * (n + 1),
        out_shape=[jax.ShapeDtypeStruct(CUTS[name].shard_shape, f32) for name in BIG]
                  + [jax.ShapeDtypeStruct((SMALL_ROWS, GROUP), f32)],
        scratch_shapes=[pltpu.SemaphoreType.DMA((n_sems,)), pltpu.SemaphoreType.DMA((n_sems,)),
                        pltpu.SemaphoreType.DMA((2 * (n + 1),))]
                       + [pltpu.VMEM(CUTS[name].block_shape, f32) for name in ALL],
        compiler_params=_params(),
    )(*[reduced[name] for name in ALL])


def _adamw(name, w, g, m, v):
    rows, cols = w.shape
    rt = rows // 4

    def body(w_ref, g_ref, m_ref, v_ref, d_ref, nm_ref, nv_ref):
        g = g_ref[...]
        nm = ADAM_B1 * m_ref[...] + (1.0 - ADAM_B1) * g
        nv = ADAM_B2 * v_ref[...] + (1.0 - ADAM_B2) * jnp.square(g)
        m_hat = nm / (1.0 - ADAM_B1 ** ADAM_STEP)
        v_hat = nv / (1.0 - ADAM_B2 ** ADAM_STEP)
        d_ref[...] = -ADAM_LR * (m_hat / (jnp.sqrt(v_hat) + ADAM_EPS) + ADAM_WD * w_ref[...])
        nm_ref[...] = nm
        nv_ref[...] = nv

    spec = pl.BlockSpec((rt, cols), lambda i: (i, 0))
    shape = jax.ShapeDtypeStruct((rows, cols), f32)
    return pl.pallas_call(
        body, name="adamw_" + name, grid=(rows // rt,),
        in_specs=[spec] * 4, out_specs=[spec] * 3, out_shape=[shape] * 3,
        compiler_params=_params(),
    )(w, g, m, v)


SMALL_NAMES = ("pool_w", "pool_scale", "sgu_ln_g", "sgu_ln_b", "sgu_w", "sgu_b", "ln1_g", "ln1_b", "ln2_g", "ln2_b")
WEIGHT_ORDER = ("w_in", "pool_w", "pool_scale", "sgu_ln_g", "sgu_ln_b", "sgu_w", "sgu_b", "w_out", "ln1_g", "ln1_b",
                "w_gate_up", "w_down", "ln2_g", "ln2_b")


def _pack_small(parts, extra=None):
    rows = [parts[name].reshape(-1, GROUP) for name in SMALL_NAMES]
    if extra is not None:
        rows.append(extra.reshape(-1, GROUP))
    used = sum(r.shape[0] for r in rows)
    return jnp.concatenate(rows + [jnp.zeros((SMALL_ROWS - used, GROUP), f32)], axis=0)


def _unpack_small(packed, shapes):
    out, at = {}, 0
    for name in SMALL_NAMES:
        rows = math.prod(shapes[name]) // GROUP
        out[name] = packed[at:at + rows].reshape(shapes[name])
        at += rows
    return out, packed[at:]


def kernel(x, w_in, pool_w, pool_scale, sgu_ln_g, sgu_ln_b, sgu_w, sgu_b, w_out, ln1_g, ln1_b, w_gate_up, w_down, ln2_g, ln2_b, loss_target, m_w_in, m_pool_w, m_pool_scale, m_sgu_ln_g, m_sgu_ln_b, m_sgu_w, m_sgu_b, m_w_out, m_ln1_g, m_ln1_b, m_w_gate_up, m_w_down, m_ln2_g, m_ln2_b, v_w_in, v_pool_w, v_pool_scale, v_sgu_ln_g, v_sgu_ln_b, v_sgu_w, v_sgu_b, v_w_out, v_ln1_g, v_ln1_b, v_w_gate_up, v_w_down, v_ln2_g, v_ln2_b):
    given = dict(locals())
    batch, seq, _ = x.shape
    tokens = batch * seq
    tile = min(TOKEN_TILE, seq)
    ffn_bwd_tile = min(FFN_BWD_TILE, seq)
    shapes = {name: given[name].shape for name in WEIGHT_ORDER}

    x2 = x.reshape(tokens, D_MODEL)
    target = loss_target.reshape(tokens, D_MODEL)
    small = (pool_w[0], pool_scale[0][None], sgu_ln_g[0][None], sgu_ln_b[0][None], sgu_w[0],
             jnp.broadcast_to(sgu_b[0][:, :, None], (N_GROUPS, GROUP, GROUP)))
    g1, b1, g2, b2 = ln1_g[0][None], ln1_b[0][None], ln2_g[0][None], ln2_b[0][None]
    shard_b = {name: given[name][0].astype(bf16) for name in BIG}

    w_in_b, w_out_b = _gather_weights({name: shard_b[name] for name in ("w_in", "w_out")})
    proj, w_dn_b = _fwd_proj(x2, w_in_b, {"w_down": shard_b["w_down"]}, tile)
    mix_b, r1, h_b, w_gu_b = _fwd_mix(proj, x2, small, w_out_b, g1, b1, {"w_gate_up": shard_b["w_gate_up"]}, tile, seq)
    gu_b, a_b = _fwd_gate_up(h_b, w_gu_b, tile)
    dr2, stats2 = _fwd_down_loss(a_b, w_dn_b, r1, target, g1, b1, g2, b2, tile)

    dgu_b = _bwd_gate_up(dr2, gu_b, w_dn_b, ffn_bwd_tile)
    dr1, stats1 = _bwd_ffn_in(dgu_b, w_gu_b, dr2, r1, g1, b1, tile)
    dpp, g_pool_w, g_sgu_w, g_sgu_b, vecs = _bwd_mix(dr1, proj, small, w_out_b, tile, seq)
    grad_x, dproj_b = _bwd_in(dpp, dr1, w_in_b, tile, seq)

    grads = {
        "w_down": _wgrad(a_b, dr2, D_MODEL // 2, tile, "wgrad_down"),
        "w_gate_up": _wgrad(h_b, dgu_b, 2 * D_FF // 4, tile, "wgrad_gate_up"),
        "w_out": _wgrad(mix_b, dr1, D_MODEL, tile, "wgrad_out"),
        "w_in": _wgrad(x2, dproj_b, IN_COLS // 2, tile, "wgrad_in"),
        "small": _pack_small({
            "pool_w": g_pool_w, "pool_scale": vecs[0], "sgu_ln_g": vecs[1], "sgu_ln_b": vecs[2],
            "sgu_w": g_sgu_w, "sgu_b": g_sgu_b[:, :, 0], "ln1_g": stats1[0], "ln1_b": stats1[1],
            "ln2_g": stats2[0], "ln2_b": stats2[1]}, extra=stats2[2]),
    }

    core = lax.axis_index("c").astype(jnp.int32).reshape(1)
    chip = (2 * lax.axis_index("x") + lax.axis_index("y")).astype(jnp.int32).reshape(1)
    got = dict(zip(ALL, _swap_halves(grads)))
    sums = {name: _add_halves(name, grads[name], got[name], core) for name in ALL}
    got = dict(zip(ALL, _swap_chips({name: sums[name][1] for name in ALL})))
    reduced = {name: _add_chips(name, sums[name][0], got[name], chip) for name in ALL}
    shared = dict(zip(ALL, _share_reduced(reduced)))

    grad, delta, new_m, new_v = {}, {}, {}, {}
    for name in BIG:
        grad[name] = shared[name][None]
        d, nm, nv = _adamw(name, given[name][0], shared[name], given["m_" + name][0], given["v_" + name][0])
        delta[name], new_m[name], new_v[name] = d[None], nm[None], nv[None]
    packed = [_pack_small({name: given[pre + name] for name in SMALL_NAMES}) for pre in ("", "m_", "v_")]
    d, nm, nv = _adamw("small", packed[0], shared["small"], packed[1], packed[2])
    for out, rows in ((grad, shared["small"]), (delta, d), (new_m, nm), (new_v, nv)):
        out.update(_unpack_small(rows, shapes)[0])

    sq_err = _unpack_small(shared["small"], shapes)[1][:LOSS_ROWS]
    loss = jnp.sum(sq_err) * (0.5 / D_MODEL)
    return (loss, grad_x.reshape(x.shape), *[grad[name] for name in WEIGHT_ORDER],
            *[delta[name] for name in WEIGHT_ORDER], *[new_m[name] for name in WEIGHT_ORDER],
            *[new_v[name] for name in WEIGHT_ORDER])
```

```python
import math

import jax
import jax.numpy as jnp
from jax import lax
from jax.experimental import pallas as pl
from jax.experimental.pallas import tpu as pltpu

f32 = jnp.float32
bf16 = jnp.bfloat16
MESH = pl.DeviceIdType.MESH

D_MODEL = 1024
POOL_WIDTH = 512
SGU_WIDTH = 512
IN_COLS = POOL_WIDTH + 2 * SGU_WIDTH
D_FF = 2816
POOL_WINDOWS = (2, 4, 8, 16)
GROUP = 128
N_GROUPS = 4
HALO = 16
LN_EPS = 1e-5
ALPHA = float(2.0 ** 0.25)
N_CHIPS = 4

ADAM_LR = 0.001
ADAM_B1 = 0.9
ADAM_B2 = 0.999
ADAM_EPS = 1e-08
ADAM_WD = 0.01
ADAM_STEP = 10

TOKEN_TILE = 512
FFN_BWD_TILE = 256
FF_CHUNK = D_FF // 2
V7X_VMEM_LIMIT = 56 * 1024 * 1024

SQUARE_ROWS = N_GROUPS * GROUP
VEC_ROWS = 64
LOSS_ROWS = D_MODEL // GROUP


def _params(**kw):
    return pltpu.CompilerParams(vmem_limit_bytes=V7X_VMEM_LIMIT, **kw)


def _mm(a, b):
    return jnp.dot(a, b, preferred_element_type=f32)


def _mm_nt(a, b):
    return lax.dot_general(a, b, (((1,), (1,)), ((), ())), preferred_element_type=f32)


def _mm_tn(a, b):
    return lax.dot_general(a, b, (((0,), (0,)), ((), ())), preferred_element_type=f32)


def _ln_fwd(r, g, b):
    mu = jnp.mean(r, axis=-1, keepdims=True)
    xc = r - mu
    var = jnp.mean(xc * xc, axis=-1, keepdims=True)
    rstd = lax.rsqrt(var + LN_EPS)
    xhat = xc * rstd
    return xhat * g + b, xhat, rstd


def _ln_bwd(dout, xhat, rstd, g):
    dxhat = dout * g
    m1 = jnp.mean(dxhat, axis=-1, keepdims=True)
    m2 = jnp.mean(dxhat * xhat, axis=-1, keepdims=True)
    return rstd * (dxhat - m1 - xhat * m2)


def _col_sum(a):
    return jnp.sum(a, axis=0, keepdims=True)


def _gelu_parts(z):
    cdf = 0.5 * (1.0 + lax.erf(z * (1.0 / math.sqrt(2.0))))
    pdf = jnp.exp(-0.5 * z * z) * (1.0 / math.sqrt(2.0 * math.pi))
    return cdf, pdf


def _inv_counts(seq_tile, rows):
    pos = seq_tile * rows + lax.broadcasted_iota(jnp.int32, (rows, GROUP), 0) + 1
    return [1.0 / jnp.minimum(pos, w).astype(f32) for w in POOL_WINDOWS]


def _window_sums(e, back):
    n = e.shape[0]

    def shifted(a, s):
        return pltpu.roll(a, s if back else n - s, 0)

    s2 = e + shifted(e, 1)
    s4 = s2[:, GROUP:] + shifted(s2[:, GROUP:], 2)
    s8 = s4[:, GROUP:] + shifted(s4[:, GROUP:], 4)
    s16 = s8[:, GROUP:] + shifted(s8[:, GROUP:], 8)
    return [s2[:, :GROUP], s4[:, :GROUP], s8[:, :GROUP], s16]


def _pooled_groups(xp, halo, inv):
    sums = _window_sums(jnp.concatenate([halo, xp], axis=0), back=True)
    return [sums[g][HALO:] * inv[g] - xp[:, g * GROUP:(g + 1) * GROUP] for g in range(N_GROUPS)]


def _tril_mask():
    r = lax.broadcasted_iota(jnp.int32, (GROUP, GROUP), 0)
    c = lax.broadcasted_iota(jnp.int32, (GROUP, GROUP), 1)
    return (r >= c).astype(f32)


def _gs(g):
    return slice(g * GROUP, (g + 1) * GROUP)


class _Cut:
    def __init__(self, rows, cols, by_cols):
        self.rows, self.cols, self.by_cols = rows, cols, by_cols
        if by_cols:
            self.block_shape = (rows // 2, cols // N_CHIPS)
            self.shard_shape = (rows, cols // N_CHIPS)
        else:
            self.block_shape = (rows // (2 * N_CHIPS), cols)
            self.shard_shape = (rows // N_CHIPS, cols)

    def block(self, ref, chip, half):
        br, bc = self.block_shape
        if self.by_cols:
            return ref.at[pl.ds(pl.multiple_of(half * br, 16), br), pl.ds(pl.multiple_of(chip * bc, 128), bc)]
        return ref.at[pl.ds(pl.multiple_of((2 * chip + half) * br, 8), br), :]

    def shard(self, ref, chip):
        sr, sc = self.shard_shape
        if self.by_cols:
            return ref.at[:, pl.ds(pl.multiple_of(chip * sc, 128), sc)]
        return ref.at[pl.ds(pl.multiple_of(chip * sr, 16), sr), :]

    def half_of_shard(self, ref, half):
        br = self.block_shape[0]
        return ref.at[pl.ds(pl.multiple_of(half * br, 8), br), :]

    def block_index(self, chip, half):
        return (half, chip) if self.by_cols else (2 * chip + half, 0)


CUTS = {
    "w_in": _Cut(D_MODEL, IN_COLS, True),
    "w_out": _Cut(D_MODEL, D_MODEL, False),
    "w_gate_up": _Cut(D_MODEL, 2 * D_FF, True),
    "w_down": _Cut(D_FF, D_MODEL, False),
    "pool_w": _Cut(SQUARE_ROWS, GROUP, False),
    "sgu_w": _Cut(SQUARE_ROWS, GROUP, False),
    "vecs": _Cut(VEC_ROWS, GROUP, False),
}
BIG = ("w_in", "w_out", "w_gate_up", "w_down")
SMALL = ("pool_w", "sgu_w", "vecs")
ANY = pl.BlockSpec(memory_space=pl.ANY)


def _wire_dtype(name):
    return bf16 if name in BIG else f32


def _place():
    x, y, c = lax.axis_index("x"), lax.axis_index("y"), lax.axis_index("c")
    others = [(1 - x, y), (x, 1 - y), (1 - x, 1 - y)]
    return x, y, c, 2 * x + y, others, [2 * ox + oy for ox, oy in others]


def _remote(src, dst, send_sems, recv_sems, k, to):
    return pltpu.make_async_remote_copy(src_ref=src, dst_ref=dst, send_sem=send_sems.at[k], recv_sem=recv_sems.at[k],
                                        device_id=to, device_id_type=MESH)


class _GatherJob:
    def __init__(self, shards):
        self.names = tuple(shards)
        self.arrays = tuple(shards.values())
        n = len(self.names)
        self.out_shapes = [jax.ShapeDtypeStruct((CUTS[name].rows, CUTS[name].cols), bf16) for name in self.names]
        self.scratch_shapes = ([pltpu.SemaphoreType.DMA((6 * n,)), pltpu.SemaphoreType.DMA((6 * n,)),
                                pltpu.SemaphoreType.DMA((2 * n,))]
                               + [pltpu.VMEM(CUTS[name].shard_shape, bf16) for name in self.names])

    def bind(self, shard_refs, full_refs, scratch):
        self.shards, self.full = shard_refs, full_refs
        self.send_sems, self.recv_sems, self.local_sems = scratch[:3]
        self.stages = scratch[3:]
        return self

    def _sends(self):
        _, _, c, me, others, _ = _place()
        return [_remote(CUTS[name].half_of_shard(self.shards[w], c), CUTS[name].block(self.full[w], me, c),
                        self.send_sems, self.recv_sems, 3 * w + k, (*chip, c))
                for w, name in enumerate(self.names) for k, chip in enumerate(others)]

    def _relays(self, half):
        x, y, c, _, _, other_ids = _place()
        n = len(self.names)
        return [_remote(CUTS[name].block(self.full[w], other_ids[k], half),
                        CUTS[name].block(self.full[w], other_ids[k], half),
                        self.send_sems, self.recv_sems, 3 * n + 3 * w + k, (x, y, 1 - c))
                for w, name in enumerate(self.names) for k in range(3)]

    def _stores(self):
        me = _place()[3]
        return [pltpu.make_async_copy(self.stages[w], CUTS[name].shard(self.full[w], me), self.local_sems.at[2 * w + 1])
                for w, name in enumerate(self.names)]

    def start(self):
        loads = [pltpu.make_async_copy(self.shards[w], self.stages[w], self.local_sems.at[2 * w])
                 for w in range(len(self.names))]
        for cp in loads:
            cp.start()
        for cp in self._sends():
            cp.start()
        for load, store in zip(loads, self._stores()):
            load.wait()
            store.start()

    def finish(self):
        _, _, c, _, others, other_ids = _place()
        relays = self._relays(c)
        for w, name in enumerate(self.names):
            for k, chip in enumerate(others):
                landed = CUTS[name].block(self.full[w], other_ids[k], c)
                _remote(landed, landed, self.send_sems, self.recv_sems, 3 * w + k, (*chip, c)).wait_recv()
                relays[3 * w + k].start()
        for cp in self._relays(1 - c):
            cp.wait_recv()
        for cp in self._sends() + relays:
            cp.wait_send()
        for cp in self._stores():
            cp.wait()


class _SwapHalvesJob:
    def __init__(self, grads):
        self.names = tuple(grads)
        self.arrays = tuple(grads.values())
        n = len(self.names)
        self.out_shapes = [jax.ShapeDtypeStruct((N_CHIPS, *CUTS[name].block_shape), f32) for name in self.names]
        self.scratch_shapes = [pltpu.SemaphoreType.DMA((N_CHIPS * n,)), pltpu.SemaphoreType.DMA((N_CHIPS * n,))]

    def bind(self, g_refs, got_refs, scratch):
        self.g_refs, self.got_refs = g_refs, got_refs
        self.send_sems, self.recv_sems = scratch
        return self

    def _copies(self):
        x, y, c, _, _, _ = _place()
        return [_remote(CUTS[name].block(self.g_refs[a], j, 1 - c), self.got_refs[a].at[j], self.send_sems,
                        self.recv_sems, N_CHIPS * a + j, (x, y, 1 - c))
                for a, name in enumerate(self.names) for j in range(N_CHIPS)]

    def start(self):
        for cp in self._copies():
            cp.start()

    def finish(self):
        for cp in self._copies():
            cp.wait()


class _SwapChipsJob:
    def __init__(self, partials):
        self.names = tuple(partials)
        self.arrays = tuple(partials.values())
        n = len(self.names)
        self.out_shapes = [jax.ShapeDtypeStruct((3, *CUTS[name].block_shape), _wire_dtype(name)) for name in self.names]
        self.scratch_shapes = [pltpu.SemaphoreType.DMA((3 * n,)), pltpu.SemaphoreType.DMA((3 * n,))]

    def bind(self, p_refs, got_refs, scratch):
        self.p_refs, self.got_refs = p_refs, got_refs
        self.send_sems, self.recv_sems = scratch
        return self

    def _copies(self):
        _, _, c, _, others, other_ids = _place()
        return [_remote(self.p_refs[a].at[other_ids[k]], self.got_refs[a].at[k], self.send_sems, self.recv_sems,
                        3 * a + k, (*chip, c))
                for a in range(len(self.names)) for k, chip in enumerate(others)]

    def start(self):
        for cp in self._copies():
            cp.start()

    def finish(self):
        for cp in self._copies():
            cp.wait()


class _ShareJob:
    def __init__(self, reduced):
        self.names = tuple(reduced)
        self.arrays = tuple(reduced.values())
        self.big = [a for a, name in enumerate(self.names) if name in BIG]
        self.small = [a for a, name in enumerate(self.names) if name in SMALL]
        self.out_shapes = [jax.ShapeDtypeStruct(CUTS[name].shard_shape if name in BIG
                                                else (CUTS[name].rows, CUTS[name].cols), f32) for name in self.names]
        n_sems = len(self.big) + 7 * len(self.small)
        self.scratch_shapes = ([pltpu.SemaphoreType.DMA((n_sems,)), pltpu.SemaphoreType.DMA((n_sems,)),
                                pltpu.SemaphoreType.DMA((2 * len(self.names),))]
                               + [pltpu.VMEM(CUTS[name].block_shape, f32) for name in self.names])

    def bind(self, f_refs, out_refs, scratch):
        self.f_refs, self.out_refs = f_refs, out_refs
        self.send_sems, self.recv_sems, self.local_sems = scratch[:3]
        self.stages = scratch[3:]
        return self

    def _sem(self, a, which=0):
        if a in self.big:
            return self.big.index(a)
        return len(self.big) + 7 * self.small.index(a) + which

    def _mine(self, a, half):
        me = _place()[3]
        cut = CUTS[self.names[a]]
        return cut.half_of_shard(self.out_refs[a], half) if a in self.big else cut.block(self.out_refs[a], me, half)

    def _to_sibling(self):
        x, y, c, _, _, _ = _place()
        return [_remote(self.f_refs[a], self._mine(a, c), self.send_sems, self.recv_sems, self._sem(a), (x, y, 1 - c))
                for a in range(len(self.names))]

    def _to_chips(self):
        _, _, c, _, others, _ = _place()
        return [_remote(self.f_refs[a], self._mine(a, c), self.send_sems, self.recv_sems, self._sem(a, 1 + k), (*chip, c))
                for a in self.small for k, chip in enumerate(others)]

    def _passes(self, half):
        x, y, c, _, _, other_ids = _place()
        out = []
        for a in self.small:
            for k in range(3):
                blk = CUTS[self.names[a]].block(self.out_refs[a], other_ids[k], half)
                out.append(_remote(blk, blk, self.send_sems, self.recv_sems, self._sem(a, 4 + k), (x, y, 1 - c)))
        return out

    def _stores(self):
        c = _place()[2]
        return [pltpu.make_async_copy(self.stages[a], self._mine(a, c), self.local_sems.at[2 * a + 1])
                for a in range(len(self.names))]

    def start(self):
        loads = [pltpu.make_async_copy(self.f_refs[a], self.stages[a], self.local_sems.at[2 * a])
                 for a in range(len(self.names))]
        for cp in loads:
            cp.start()
        for cp in self._to_sibling() + self._to_chips():
            cp.start()
        for load, store in zip(loads, self._stores()):
            load.wait()
            store.start()

    def finish(self):
        x, y, c, _, others, other_ids = _place()
        passes = self._passes(c)
        for s, a in enumerate(self.small):
            for k, chip in enumerate(others):
                landed = CUTS[self.names[a]].block(self.out_refs[a], other_ids[k], c)
                _remote(landed, landed, self.send_sems, self.recv_sems, self._sem(a, 1 + k), (*chip, c)).wait_recv()
                passes[3 * s + k].start()
        for a in range(len(self.names)):
            theirs = self._mine(a, 1 - c)
            _remote(theirs, theirs, self.send_sems, self.recv_sems, self._sem(a), (x, y, 1 - c)).wait_recv()
        for cp in self._passes(1 - c):
            cp.wait_recv()
        for cp in self._to_sibling() + self._to_chips() + passes:
            cp.wait_send()
        for cp in self._stores():
            cp.wait()


def _call(body, *, name, grid, in_specs, out_specs, out_shape, args, scratch_shapes=(), jobs=()):
    n_in, n_out, n_scr = len(in_specs), len(out_specs), len(scratch_shapes)
    j_in = [len(j.arrays) for j in jobs]
    j_out = [len(j.out_shapes) for j in jobs]
    j_scr = [len(j.scratch_shapes) for j in jobs]

    def wrapped(*refs):
        refs = list(refs)

        def take(k):
            head = refs[:k]
            del refs[:k]
            return head

        ins, jins = take(n_in), [take(k) for k in j_in]
        outs, jouts = take(n_out), [take(k) for k in j_out]
        scr, jscr = take(n_scr), [take(k) for k in j_scr]
        bound = [j.bind(a, b, c) for j, a, b, c in zip(jobs, jins, jouts, jscr)]
        if not grid:
            for b in bound:
                b.start()
            body(*ins, *outs, *scr)
            for b in bound:
                b.finish()
            return
        if not bound:
            body(*ins, *outs, *scr)
            return
        first = _all([pl.program_id(d) == 0 for d in range(len(grid))])
        last = _all([pl.program_id(d) == grid[d] - 1 for d in range(len(grid))])

        @pl.when(first)
        def _():
            for b in bound:
                b.start()

        body(*ins, *outs, *scr)

        @pl.when(last)
        def _():
            for b in bound:
                b.finish()

    kw = dict(grid=grid) if grid else {}
    results = pl.pallas_call(
        wrapped, name=name,
        in_specs=list(in_specs) + [ANY] * sum(j_in), out_specs=list(out_specs) + [ANY] * sum(j_out),
        out_shape=list(out_shape) + [s for j in jobs for s in j.out_shapes],
        scratch_shapes=list(scratch_shapes) + [s for j in jobs for s in j.scratch_shapes],
        compiler_params=_params(), **kw,
    )(*args, *[a for j in jobs for a in j.arrays])
    results = list(results)
    own, rest = results[:n_out], results[n_out:]
    per_job = []
    for j, k in zip(jobs, j_out):
        per_job.append(dict(zip(j.names, rest[:k])))
        rest = rest[k:]
    return own, per_job


def _all(conds):
    out = conds[0]
    for c in conds[1:]:
        out = jnp.logical_and(out, c)
    return out


def _alone(job, name):
    return _call(lambda: None, name=name, grid=None, in_specs=[], out_specs=[], out_shape=[], args=[], jobs=[job])[1][0]


def _fwd_proj(x2, w_in_b, tile, jobs):
    tokens = x2.shape[0]

    def body(x_ref, w_ref, o_ref):
        o_ref[...] = _mm(x_ref[...].astype(bf16), w_ref[...])

    return _call(
        body, name="fwd_proj", grid=(tokens // tile,),
        in_specs=[pl.BlockSpec((tile, D_MODEL), lambda i: (i, 0)),
                  pl.BlockSpec((D_MODEL, IN_COLS), lambda i: (0, 0))],
        out_specs=[pl.BlockSpec((tile, IN_COLS), lambda i: (i, 0))],
        out_shape=[jax.ShapeDtypeStruct((tokens, IN_COLS), f32)],
        args=(x2, w_in_b), jobs=jobs)


def _small_specs():
    return [pl.BlockSpec((N_GROUPS, GROUP, GROUP), lambda i: (0, 0, 0)),
            pl.BlockSpec((1, POOL_WIDTH), lambda i: (0, 0)),
            pl.BlockSpec((1, SGU_WIDTH), lambda i: (0, 0)),
            pl.BlockSpec((1, SGU_WIDTH), lambda i: (0, 0)),
            pl.BlockSpec((N_GROUPS, GROUP, GROUP), lambda i: (0, 0, 0)),
            pl.BlockSpec((N_GROUPS, GROUP, GROUP), lambda i: (0, 0, 0))]


def _fwd_mix(proj, x2, small, w_out_b, ln1_g, ln1_b, tile, seq, jobs):
    tokens = x2.shape[0]
    tps = seq // tile
    hb = tile // HALO

    def body(proj_ref, halo_ref, x_ref, pw_ref, ps_ref, lg_ref, lb_ref, sw_ref, sb_ref, wout_ref, g1_ref, b1_ref,
             mix_ref, r1_ref, h_ref):
        seq_tile = pl.program_id(0) % tps
        xp = proj_ref[:, :POOL_WIDTH]
        halo = jnp.where(seq_tile == 0, 0.0, halo_ref[...])
        pooled = _pooled_groups(xp, halo, _inv_counts(seq_tile, tile))
        for g in range(N_GROUPS):
            po = _mm(pooled[g].astype(bf16), pw_ref[g].astype(bf16)) * ps_ref[:, _gs(g)]
            mix_ref[:, _gs(g)] = po.astype(bf16)

        cdf, _ = _gelu_parts(proj_ref[:, POOL_WIDTH:])
        zg = proj_ref[:, POOL_WIDTH:] * cdf
        u = zg[:, :SGU_WIDTH]
        vln, _, _ = _ln_fwd(zg[:, SGU_WIDTH:], lg_ref[...], lb_ref[...])
        vb = vln.astype(bf16)
        mask = _tril_mask()
        for h in range(N_GROUPS):
            wm = (sw_ref[h] * mask).astype(bf16)
            bias = sb_ref[h]
            for c in range(tile // GROUP):
                rows = slice(c * GROUP, (c + 1) * GROUP)
                mixed = _mm(wm, vb[rows, _gs(h)]) + bias
                mix_ref[rows, POOL_WIDTH + h * GROUP:POOL_WIDTH + (h + 1) * GROUP] = (u[rows, _gs(h)] * mixed).astype(bf16)

        r1 = ALPHA * x_ref[...] + _mm(mix_ref[...], wout_ref[...])
        r1_ref[...] = r1
        h1, _, _ = _ln_fwd(r1, g1_ref[...], b1_ref[...])
        h_ref[...] = h1.astype(bf16)

    row = lambda i: (i, 0)
    vec = pl.BlockSpec((1, D_MODEL), lambda i: (0, 0))
    return _call(
        body, name="fwd_mix", grid=(tokens // tile,),
        in_specs=[pl.BlockSpec((tile, IN_COLS), row),
                  pl.BlockSpec((HALO, POOL_WIDTH), lambda i: (jnp.maximum(i * hb - 1, 0), 0)),
                  pl.BlockSpec((tile, D_MODEL), row)] + _small_specs()
                 + [pl.BlockSpec((D_MODEL, D_MODEL), lambda i: (0, 0)), vec, vec],
        out_specs=[pl.BlockSpec((tile, D_MODEL), row)] * 3,
        out_shape=[jax.ShapeDtypeStruct((tokens, D_MODEL), bf16),
                   jax.ShapeDtypeStruct((tokens, D_MODEL), f32),
                   jax.ShapeDtypeStruct((tokens, D_MODEL), bf16)],
        args=(proj, proj, x2, *small, w_out_b, ln1_g, ln1_b), jobs=jobs)


def _fwd_gate_up(h_b, w_gu_b, tile, jobs):
    tokens = h_b.shape[0]

    def body(h_ref, w_ref, gu_ref, a_ref):
        hb = h_ref[...]
        for c in range(D_FF // FF_CHUNK):
            gcols = slice(c * FF_CHUNK, (c + 1) * FF_CHUNK)
            ucols = slice(D_FF + c * FF_CHUNK, D_FF + (c + 1) * FF_CHUNK)
            gate = _mm(hb, w_ref[:, gcols])
            up = _mm(hb, w_ref[:, ucols])
            gu_ref[:, gcols] = gate.astype(bf16)
            gu_ref[:, ucols] = up.astype(bf16)
            a_ref[:, gcols] = (gate * jax.nn.sigmoid(gate) * up).astype(bf16)

    return _call(
        body, name="fwd_gate_up", grid=(tokens // tile,),
        in_specs=[pl.BlockSpec((tile, D_MODEL), lambda i: (i, 0)),
                  pl.BlockSpec((D_MODEL, 2 * D_FF), lambda i: (0, 0), pipeline_mode=pl.Buffered(1))],
        out_specs=[pl.BlockSpec((tile, 2 * D_FF), lambda i: (i, 0)),
                   pl.BlockSpec((tile, D_FF), lambda i: (i, 0))],
        out_shape=[jax.ShapeDtypeStruct((tokens, 2 * D_FF), bf16),
                   jax.ShapeDtypeStruct((tokens, D_FF), bf16)],
        args=(h_b, w_gu_b), jobs=jobs)


def _fwd_down_loss(a_b, w_dn_b, r1, target, ln1_g, ln1_b, ln2_g, ln2_b, tile):
    tokens = a_b.shape[0]

    def body(a_ref, w_ref, r1_ref, t_ref, g1_ref, b1_ref, g2_ref, b2_ref, dr2_ref, st_ref):
        @pl.when(pl.program_id(0) == 0)
        def _():
            st_ref[...] = jnp.zeros_like(st_ref)

        h1, _, _ = _ln_fwd(r1_ref[...], g1_ref[...], b1_ref[...])
        r2 = ALPHA * h1 + _mm(a_ref[...], w_ref[...])
        y, xhat, rstd = _ln_fwd(r2, g2_ref[...], b2_ref[...])
        diff = y - t_ref[...]
        dy = diff * (1.0 / D_MODEL)
        st_ref[0:1, :] += _col_sum(dy * xhat)
        st_ref[1:2, :] += _col_sum(dy)
        st_ref[2:3, :] += _col_sum(diff * diff)
        dr2_ref[...] = _ln_bwd(dy, xhat, rstd, g2_ref[...])

    row = lambda i: (i, 0)
    vec = pl.BlockSpec((1, D_MODEL), lambda i: (0, 0))
    return _call(
        body, name="fwd_down_loss", grid=(tokens // tile,),
        in_specs=[pl.BlockSpec((tile, D_FF), row), pl.BlockSpec((D_FF, D_MODEL), lambda i: (0, 0)),
                  pl.BlockSpec((tile, D_MODEL), row), pl.BlockSpec((tile, D_MODEL), row), vec, vec, vec, vec],
        out_specs=[pl.BlockSpec((tile, D_MODEL), row), pl.BlockSpec((8, D_MODEL), lambda i: (0, 0))],
        out_shape=[jax.ShapeDtypeStruct((tokens, D_MODEL), f32), jax.ShapeDtypeStruct((8, D_MODEL), f32)],
        args=(a_b, w_dn_b, r1, target, ln1_g, ln1_b, ln2_g, ln2_b))[0]


def _bwd_gate_up(dr2, gu_b, w_dn_b, tile, jobs):
    tokens = dr2.shape[0]

    def body(d_ref, gu_ref, w_ref, dgu_ref):
        d = d_ref[...].astype(bf16)
        for c in range(D_FF // FF_CHUNK):
            gcols = slice(c * FF_CHUNK, (c + 1) * FF_CHUNK)
            ucols = slice(D_FF + c * FF_CHUNK, D_FF + (c + 1) * FF_CHUNK)
            da = _mm_nt(d, w_ref[gcols, :])
            gate = gu_ref[:, gcols].astype(f32)
            up = gu_ref[:, ucols].astype(f32)
            sg = jax.nn.sigmoid(gate)
            dgu_ref[:, gcols] = (da * up * (sg * (1.0 + gate * (1.0 - sg)))).astype(bf16)
            dgu_ref[:, ucols] = (da * (gate * sg)).astype(bf16)

    return _call(
        body, name="bwd_gate_up", grid=(tokens // tile,),
        in_specs=[pl.BlockSpec((tile, D_MODEL), lambda i: (i, 0)),
                  pl.BlockSpec((tile, 2 * D_FF), lambda i: (i, 0)),
                  pl.BlockSpec((D_FF, D_MODEL), lambda i: (0, 0))],
        out_specs=[pl.BlockSpec((tile, 2 * D_FF), lambda i: (i, 0))],
        out_shape=[jax.ShapeDtypeStruct((tokens, 2 * D_FF), bf16)],
        args=(dr2, gu_b, w_dn_b), jobs=jobs)


def _bwd_ffn_in(dgu_b, w_gu_b, dr2, r1, ln1_g, ln1_b, tile, jobs):
    tokens = dr2.shape[0]

    def body(dgu_ref, w_ref, d_ref, r1_ref, g1_ref, b1_ref, dr1_ref, st_ref):
        @pl.when(pl.program_id(0) == 0)
        def _():
            st_ref[...] = jnp.zeros_like(st_ref)

        dh = ALPHA * d_ref[...] + _mm_nt(dgu_ref[...], w_ref[...])
        _, xhat, rstd = _ln_fwd(r1_ref[...], g1_ref[...], b1_ref[...])
        st_ref[0:1, :] += _col_sum(dh * xhat)
        st_ref[1:2, :] += _col_sum(dh)
        dr1_ref[...] = _ln_bwd(dh, xhat, rstd, g1_ref[...])

    row = lambda i: (i, 0)
    vec = pl.BlockSpec((1, D_MODEL), lambda i: (0, 0))
    return _call(
        body, name="bwd_ffn_in", grid=(tokens // tile,),
        in_specs=[pl.BlockSpec((tile, 2 * D_FF), row),
                  pl.BlockSpec((D_MODEL, 2 * D_FF), lambda i: (0, 0), pipeline_mode=pl.Buffered(1)),
                  pl.BlockSpec((tile, D_MODEL), row), pl.BlockSpec((tile, D_MODEL), row), vec, vec],
        out_specs=[pl.BlockSpec((tile, D_MODEL), row), pl.BlockSpec((8, D_MODEL), lambda i: (0, 0))],
        out_shape=[jax.ShapeDtypeStruct((tokens, D_MODEL), f32), jax.ShapeDtypeStruct((8, D_MODEL), f32)],
        args=(dgu_b, w_gu_b, dr2, r1, ln1_g, ln1_b), jobs=jobs)


def _bwd_mix(dr1, proj, small, w_out_b, tile, seq, jobs):
    tokens = dr1.shape[0]
    tps = seq // tile
    hb = tile // HALO
    steps = tokens // tile

    def body(dr1_ref, proj_ref, halo_ref, wout_ref, pw_ref, ps_ref, lg_ref, lb_ref, sw_ref, sb_ref,
             dpp_ref, gpw_ref, gsw_ref, gsb_ref, vec_ref, du_ref, dvln_ref):
        step = pl.program_id(0)
        seq_tile = step % tps

        @pl.when(step == 0)
        def _():
            gpw_ref[...] = jnp.zeros_like(gpw_ref)
            gsw_ref[...] = jnp.zeros_like(gsw_ref)
            gsb_ref[...] = jnp.zeros_like(gsb_ref)
            vec_ref[...] = jnp.zeros_like(vec_ref)

        dmix = _mm_nt(dr1_ref[...].astype(bf16), wout_ref[...])

        xp = proj_ref[:, :POOL_WIDTH]
        halo = jnp.where(seq_tile == 0, 0.0, halo_ref[...])
        pooled = _pooled_groups(xp, halo, _inv_counts(seq_tile, tile))
        for g in range(N_GROUPS):
            pb = pooled[g].astype(bf16)
            pwb = pw_ref[g].astype(bf16)
            dpo = dmix[:, _gs(g)]
            vec_ref[0:1, _gs(g)] += _col_sum(dpo * _mm(pb, pwb))
            dpo_b = (dpo * ps_ref[:, _gs(g)]).astype(bf16)
            gpw_ref[g] += _mm_tn(pb, dpo_b)
            dpp_ref[:, _gs(g)] = _mm_nt(dpo_b, pwb)

        pre = proj_ref[:, POOL_WIDTH:]
        cdf, pdf = _gelu_parts(pre)
        zg = pre * cdf
        u = zg[:, :SGU_WIDTH]
        vln, vhat, rstd = _ln_fwd(zg[:, SGU_WIDTH:], lg_ref[...], lb_ref[...])
        vb = vln.astype(bf16)
        mask = _tril_mask()
        for h in range(N_GROUPS):
            wm = (sw_ref[h] * mask).astype(bf16)
            bias = sb_ref[h]
            gsw = jnp.zeros((GROUP, GROUP), f32)
            gsb = jnp.zeros((GROUP, GROUP), f32)
            for c in range(tile // GROUP):
                rows = slice(c * GROUP, (c + 1) * GROUP)
                v_ch = vb[rows, _gs(h)]
                d = dmix[rows, POOL_WIDTH + h * GROUP:POOL_WIDTH + (h + 1) * GROUP]
                du_ref[rows, _gs(h)] = d * (_mm(wm, v_ch) + bias)
                dmixed = d * u[rows, _gs(h)]
                gsb += dmixed
                dmixed_b = dmixed.astype(bf16)
                gsw += _mm_nt(dmixed_b, v_ch)
                dvln_ref[rows, _gs(h)] = _mm_tn(wm, dmixed_b)
            gsw_ref[h] += gsw * mask
            gsb_ref[h] += gsb

        dvln = dvln_ref[...]
        vec_ref[1:2, :] += _col_sum(dvln * vhat)
        vec_ref[2:3, :] += _col_sum(dvln)
        dgelu = cdf + pre * pdf
        dpp_ref[:, POOL_WIDTH:POOL_WIDTH + SGU_WIDTH] = du_ref[...] * dgelu[:, :SGU_WIDTH]
        dpp_ref[:, POOL_WIDTH + SGU_WIDTH:] = _ln_bwd(dvln, vhat, rstd, lg_ref[...]) * dgelu[:, SGU_WIDTH:]

        @pl.when(step == steps - 1)
        def _():
            for h in range(N_GROUPS):
                gsb_ref[h] = jnp.broadcast_to(jnp.sum(gsb_ref[h], axis=1, keepdims=True), (GROUP, GROUP))

    row = lambda i: (i, 0)
    sq = jax.ShapeDtypeStruct((N_GROUPS, GROUP, GROUP), f32)
    sq_spec = pl.BlockSpec((N_GROUPS, GROUP, GROUP), lambda i: (0, 0, 0))
    return _call(
        body, name="bwd_mix", grid=(steps,),
        in_specs=[pl.BlockSpec((tile, D_MODEL), row), pl.BlockSpec((tile, IN_COLS), row),
                  pl.BlockSpec((HALO, POOL_WIDTH), lambda i: (jnp.maximum(i * hb - 1, 0), 0)),
                  pl.BlockSpec((D_MODEL, D_MODEL), lambda i: (0, 0))] + _small_specs(),
        out_specs=[pl.BlockSpec((tile, IN_COLS), row), sq_spec, sq_spec, sq_spec,
                   pl.BlockSpec((8, POOL_WIDTH), lambda i: (0, 0))],
        out_shape=[jax.ShapeDtypeStruct((tokens, IN_COLS), f32), sq, sq, sq,
                   jax.ShapeDtypeStruct((8, POOL_WIDTH), f32)],
        scratch_shapes=[pltpu.VMEM((tile, SGU_WIDTH), f32), pltpu.VMEM((tile, SGU_WIDTH), f32)],
        args=(dr1, proj, proj, w_out_b, *small), jobs=jobs)


def _bwd_in(dpp, dr1, w_in_b, tile, seq):
    tokens = dr1.shape[0]
    tps = seq // tile
    hb = tile // HALO
    last_halo = tokens // HALO - 1

    def body(dpp_ref, nxt_ref, dr1_ref, w_ref, dx_ref, dproj_ref):
        seq_tile = pl.program_id(0) % tps
        inv = _inv_counts(seq_tile, tile)
        dpl = dpp_ref[:, :POOL_WIDTH]
        nxt = jnp.where(seq_tile == tps - 1, 0.0, nxt_ref[...])
        scaled = jnp.concatenate([dpl[:, _gs(g)] * inv[g] for g in range(N_GROUPS)], axis=1)
        scaled_nxt = jnp.concatenate([nxt[:, _gs(g)] * (1.0 / POOL_WINDOWS[g]) for g in range(N_GROUPS)], axis=1)
        sums = _window_sums(jnp.concatenate([scaled, scaled_nxt], axis=0), back=False)
        for g in range(N_GROUPS):
            dproj_ref[:, _gs(g)] = (sums[g][:tile] - dpl[:, _gs(g)]).astype(bf16)
        dproj_ref[:, POOL_WIDTH:] = dpp_ref[:, POOL_WIDTH:].astype(bf16)
        dx_ref[...] = ALPHA * dr1_ref[...] + _mm_nt(dproj_ref[...], w_ref[...])

    row = lambda i: (i, 0)
    return _call(
        body, name="bwd_in", grid=(tokens // tile,),
        in_specs=[pl.BlockSpec((tile, IN_COLS), row),
                  pl.BlockSpec((HALO, POOL_WIDTH), lambda i: (jnp.minimum((i + 1) * hb, last_halo), 0)),
                  pl.BlockSpec((tile, D_MODEL), row),
                  pl.BlockSpec((D_MODEL, IN_COLS), lambda i: (0, 0))],
        out_specs=[pl.BlockSpec((tile, D_MODEL), row), pl.BlockSpec((tile, IN_COLS), row)],
        out_shape=[jax.ShapeDtypeStruct((tokens, D_MODEL), f32), jax.ShapeDtypeStruct((tokens, IN_COLS), bf16)],
        args=(dpp, dpp, dr1, w_in_b))[0]


def _wgrad(a, b, col_tile, tile, name, jobs=()):
    tokens, m = a.shape
    n = b.shape[1]

    def body(a_ref, b_ref, o_ref):
        @pl.when(pl.program_id(1) == 0)
        def _():
            o_ref[...] = jnp.zeros_like(o_ref)

        o_ref[...] += _mm_tn(a_ref[...].astype(bf16), b_ref[...].astype(bf16))

    (out,), got = _call(
        body, name=name, grid=(n // col_tile, tokens // tile),
        in_specs=[pl.BlockSpec((tile, m), lambda j, k: (k, 0)),
                  pl.BlockSpec((tile, col_tile), lambda j, k: (k, j))],
        out_specs=[pl.BlockSpec((m, col_tile), lambda j, k: (0, j))],
        out_shape=[jax.ShapeDtypeStruct((m, n), f32)],
        args=(a, b), jobs=jobs)
    return out, got


def _add_halves(name, g, got, core):
    cut = CUTS[name]
    br, bc = cut.block_shape
    wire = _wire_dtype(name)

    def body(core_ref, g_ref, got_ref, o_ref, wire_ref):
        s = g_ref[...] + got_ref[...]
        o_ref[...] = s
        wire_ref[...] = s.astype(wire)

    blocks = pl.BlockSpec((None, br, bc), lambda j, core_ref: (j, 0, 0))
    return pl.pallas_call(
        body, name="reduce_add_halves_" + name,
        grid_spec=pltpu.PrefetchScalarGridSpec(
            num_scalar_prefetch=1, grid=(N_CHIPS,),
            in_specs=[pl.BlockSpec((br, bc), lambda j, core_ref: cut.block_index(j, core_ref[0])), blocks],
            out_specs=[blocks, blocks]),
        out_shape=[jax.ShapeDtypeStruct((N_CHIPS, br, bc), f32), jax.ShapeDtypeStruct((N_CHIPS, br, bc), wire)],
        compiler_params=_params(),
    )(core, g, got)


def _add_chips(name, partial, got, chip):
    br, bc = CUTS[name].block_shape
    rt = br // 2 if br % 32 == 0 else br

    def body(chip_ref, p_ref, got_ref, o_ref):
        o_ref[...] = ((p_ref[...] + got_ref[0].astype(f32)) + got_ref[1].astype(f32)) + got_ref[2].astype(f32)

    return pl.pallas_call(
        body, name="reduce_add_chips_" + name,
        grid_spec=pltpu.PrefetchScalarGridSpec(
            num_scalar_prefetch=1, grid=(br // rt,),
            in_specs=[pl.BlockSpec((None, rt, bc), lambda i, chip_ref: (chip_ref[0], i, 0)),
                      pl.BlockSpec((3, rt, bc), lambda i, chip_ref: (0, i, 0))],
            out_specs=pl.BlockSpec((rt, bc), lambda i, chip_ref: (i, 0))),
        out_shape=jax.ShapeDtypeStruct((br, bc), f32),
        compiler_params=_params(),
    )(chip, partial, got)


def _adamw(name, w, g, m, v):
    rows, cols = w.shape
    rt = rows // 4

    def body(w_ref, g_ref, m_ref, v_ref, d_ref, nm_ref, nv_ref):
        g = g_ref[...]
        nm = ADAM_B1 * m_ref[...] + (1.0 - ADAM_B1) * g
        nv = ADAM_B2 * v_ref[...] + (1.0 - ADAM_B2) * jnp.square(g)
        m_hat = nm / (1.0 - ADAM_B1 ** ADAM_STEP)
        v_hat = nv / (1.0 - ADAM_B2 ** ADAM_STEP)
        d_ref[...] = -ADAM_LR * (m_hat / (jnp.sqrt(v_hat) + ADAM_EPS) + ADAM_WD * w_ref[...])
        nm_ref[...] = nm
        nv_ref[...] = nv

    spec = pl.BlockSpec((rt, cols), lambda i: (i, 0))
    shape = jax.ShapeDtypeStruct((rows, cols), f32)
    return pl.pallas_call(
        body, name="adamw_" + name, grid=(rows // rt,),
        in_specs=[spec] * 4, out_specs=[spec] * 3, out_shape=[shape] * 3,
        compiler_params=_params(),
    )(w, g, m, v)


VEC_NAMES = ("pool_scale", "sgu_ln_g", "sgu_ln_b", "sgu_b", "ln1_g", "ln1_b", "ln2_g", "ln2_b")
WEIGHT_ORDER = ("w_in", "pool_w", "pool_scale", "sgu_ln_g", "sgu_ln_b", "sgu_w", "sgu_b", "w_out", "ln1_g", "ln1_b",
                "w_gate_up", "w_down", "ln2_g", "ln2_b")


def _pack_vecs(parts, extra=None):
    rows = [parts[name].reshape(-1, GROUP) for name in VEC_NAMES]
    if extra is not None:
        rows.append(extra.reshape(-1, GROUP))
    used = sum(r.shape[0] for r in rows)
    return jnp.concatenate(rows + [jnp.zeros((VEC_ROWS - used, GROUP), f32)], axis=0)


def _unpack_vecs(packed, shapes):
    out, at = {}, 0
    for name in VEC_NAMES:
        rows = math.prod(shapes[name]) // GROUP
        out[name] = packed[at:at + rows].reshape(shapes[name])
        at += rows
    return out, packed[at:]


def kernel(x, w_in, pool_w, pool_scale, sgu_ln_g, sgu_ln_b, sgu_w, sgu_b, w_out, ln1_g, ln1_b, w_gate_up, w_down, ln2_g, ln2_b, loss_target, m_w_in, m_pool_w, m_pool_scale, m_sgu_ln_g, m_sgu_ln_b, m_sgu_w, m_sgu_b, m_w_out, m_ln1_g, m_ln1_b, m_w_gate_up, m_w_down, m_ln2_g, m_ln2_b, v_w_in, v_pool_w, v_pool_scale, v_sgu_ln_g, v_sgu_ln_b, v_sgu_w, v_sgu_b, v_w_out, v_ln1_g, v_ln1_b, v_w_gate_up, v_w_down, v_ln2_g, v_ln2_b):
    given = dict(locals())
    batch, seq, _ = x.shape
    tokens = batch * seq
    tile = min(TOKEN_TILE, seq)
    ffn_bwd_tile = min(FFN_BWD_TILE, seq)
    shapes = {name: given[name].shape for name in WEIGHT_ORDER}

    x2 = x.reshape(tokens, D_MODEL)
    target = loss_target.reshape(tokens, D_MODEL)
    small = (pool_w[0], pool_scale[0][None], sgu_ln_g[0][None], sgu_ln_b[0][None], sgu_w[0],
             jnp.broadcast_to(sgu_b[0][:, :, None], (N_GROUPS, GROUP, GROUP)))
    g1, b1, g2, b2 = ln1_g[0][None], ln1_b[0][None], ln2_g[0][None], ln2_b[0][None]
    shard_b = {name: given[name][0].astype(bf16) for name in BIG}
    core = lax.axis_index("c").astype(jnp.int32).reshape(1)
    chip = (2 * lax.axis_index("x") + lax.axis_index("y")).astype(jnp.int32).reshape(1)

    def gather(name):
        return [_GatherJob({name: shard_b[name]})]

    def halves_summed(name, grad, got):
        return _add_halves(name, grad, got, core)

    def chips_summed(name, sums, got):
        return _add_chips(name, sums[0], got, chip)

    w_in_b = _alone(gather("w_in")[0], "gather_w_in")["w_in"]
    (proj,), (got,) = _fwd_proj(x2, w_in_b, tile, gather("w_out"))
    w_out_b = got["w_out"]
    (mix_b, r1, h_b), (got,) = _fwd_mix(proj, x2, small, w_out_b, g1, b1, tile, seq, gather("w_gate_up"))
    w_gu_b = got["w_gate_up"]
    (gu_b, a_b), (got,) = _fwd_gate_up(h_b, w_gu_b, tile, gather("w_down"))
    w_dn_b = got["w_down"]
    dr2, stats2 = _fwd_down_loss(a_b, w_dn_b, r1, target, g1, b1, g2, b2, tile)

    reduced = {}
    g_down, _ = _wgrad(a_b, dr2, D_MODEL // 2, tile, "wgrad_down")
    (dgu_b,), (got,) = _bwd_gate_up(dr2, gu_b, w_dn_b, ffn_bwd_tile, [_SwapHalvesJob({"w_down": g_down})])
    sums_down = halves_summed("w_down", g_down, got["w_down"])
    g_gu, (got,) = _wgrad(h_b, dgu_b, 2 * D_FF // 4, tile, "wgrad_gate_up", [_SwapChipsJob({"w_down": sums_down[1]})])
    reduced["w_down"] = chips_summed("w_down", sums_down, got["w_down"])
    (dr1, stats1), (got,) = _bwd_ffn_in(dgu_b, w_gu_b, dr2, r1, g1, b1, tile, [_SwapHalvesJob({"w_gate_up": g_gu})])
    sums_gu = halves_summed("w_gate_up", g_gu, got["w_gate_up"])
    g_out, _ = _wgrad(mix_b, dr1, D_MODEL, tile, "wgrad_out")
    (dpp, g_pool_w, g_sgu_w, g_sgu_b, vecs), (got_gu, got_out) = _bwd_mix(
        dr1, proj, small, w_out_b, tile, seq,
        [_SwapChipsJob({"w_gate_up": sums_gu[1]}), _SwapHalvesJob({"w_out": g_out})])
    reduced["w_gate_up"] = chips_summed("w_gate_up", sums_gu, got_gu["w_gate_up"])
    sums_out = halves_summed("w_out", g_out, got_out["w_out"])
    grad_x, dproj_b = _bwd_in(dpp, dr1, w_in_b, tile, seq)
    g_in, (got,) = _wgrad(x2, dproj_b, IN_COLS // 2, tile, "wgrad_in", [_SwapChipsJob({"w_out": sums_out[1]})])
    reduced["w_out"] = chips_summed("w_out", sums_out, got["w_out"])

    last = {
        "w_in": g_in,
        "pool_w": g_pool_w.reshape(SQUARE_ROWS, GROUP),
        "sgu_w": g_sgu_w.reshape(SQUARE_ROWS, GROUP),
        "vecs": _pack_vecs({"pool_scale": vecs[0], "sgu_ln_g": vecs[1], "sgu_ln_b": vecs[2], "sgu_b": g_sgu_b[:, :, 0],
                            "ln1_g": stats1[0], "ln1_b": stats1[1], "ln2_g": stats2[0], "ln2_b": stats2[1]},
                           extra=stats2[2]),
    }
    got = _alone(_SwapHalvesJob(last), "reduce_swap_halves")
    sums = {name: halves_summed(name, last[name], got[name]) for name in last}
    got = _alone(_SwapChipsJob({name: sums[name][1] for name in last}), "reduce_swap_chips")
    reduced.update({name: chips_summed(name, sums[name], got[name]) for name in last})
    shared = _alone(_ShareJob({name: reduced[name] for name in BIG + SMALL}), "reduce_share")

    grad, delta, new_m, new_v = {}, {}, {}, {}
    for name in BIG:
        grad[name] = shared[name][None]
        d, nm, nv = _adamw(name, given[name][0], shared[name], given["m_" + name][0], given["v_" + name][0])
        delta[name], new_m[name], new_v[name] = d[None], nm[None], nv[None]
    for name in ("pool_w", "sgu_w"):
        flat = [given[pre + name].reshape(SQUARE_ROWS, GROUP) for pre in ("", "m_", "v_")]
        d, nm, nv = _adamw(name, flat[0], shared[name], flat[1], flat[2])
        for out, rows in ((grad, shared[name]), (delta, d), (new_m, nm), (new_v, nv)):
            out[name] = rows.reshape(shapes[name])
    packed = [_pack_vecs({name: given[pre + name] for name in VEC_NAMES}) for pre in ("", "m_", "v_")]
    d, nm, nv = _adamw("vecs", packed[0], shared["vecs"], packed[1], packed[2])
    for out, rows in ((grad, shared["vecs"]), (delta, d), (new_m, nm), (new_v, nv)):
        out.update(_unpack_vecs(rows, shapes)[0])

    sq_err = _unpack_vecs(shared["vecs"], shapes)[1][:LOSS_ROWS]
    loss = jnp.sum(sq_err) * (0.5 / D_MODEL)
    return (loss, grad_x.reshape(x.shape), *[grad[name] for name in WEIGHT_ORDER],
            *[delta[name] for name in WEIGHT_ORDER], *[new_m[name] for name in WEIGHT_ORDER],
            *[new_v[name] for name in WEIGHT_ORDER])
```

```python
import math

import jax
import jax.numpy as jnp
from jax import lax
from jax.experimental import pallas as pl
from jax.experimental.pallas import tpu as pltpu

f32 = jnp.float32
bf16 = jnp.bfloat16
MESH = pl.DeviceIdType.MESH

D_MODEL = 1024
POOL_WIDTH = 512
SGU_WIDTH = 512
IN_COLS = POOL_WIDTH + 2 * SGU_WIDTH
D_FF = 2816
POOL_WINDOWS = (2, 4, 8, 16)
GROUP = 128
N_GROUPS = 4
HALO = 16
LN_EPS = 1e-5
ALPHA = float(2.0 ** 0.25)
N_CHIPS = 4

ADAM_LR = 0.001
ADAM_B1 = 0.9
ADAM_B2 = 0.999
ADAM_EPS = 1e-08
ADAM_WD = 0.01
ADAM_STEP = 10

TOKEN_TILE = 512
FFN_BWD_TILE = 256
FF_CHUNK = 256
ROW_SUB = 256
WGRAD_TILE = 1024
V7X_VMEM_LIMIT = 56 * 1024 * 1024

SQUARE_ROWS = N_GROUPS * GROUP
VEC_ROWS = 64
LOSS_ROWS = D_MODEL // GROUP


def _params(**kw):
    return pltpu.CompilerParams(vmem_limit_bytes=V7X_VMEM_LIMIT, **kw)


def _mm(a, b):
    return jnp.dot(a, b, preferred_element_type=f32)


def _mm_nt(a, b):
    return lax.dot_general(a, b, (((1,), (1,)), ((), ())), preferred_element_type=f32)


def _mm_tn(a, b):
    return lax.dot_general(a, b, (((0,), (0,)), ((), ())), preferred_element_type=f32)


def _ln_fwd(r, g, b):
    mu = jnp.mean(r, axis=-1, keepdims=True)
    xc = r - mu
    var = jnp.mean(xc * xc, axis=-1, keepdims=True)
    rstd = lax.rsqrt(var + LN_EPS)
    xhat = xc * rstd
    return xhat * g + b, xhat, rstd


def _ln_bwd(dout, xhat, rstd, g):
    dxhat = dout * g
    m1 = jnp.mean(dxhat, axis=-1, keepdims=True)
    m2 = jnp.mean(dxhat * xhat, axis=-1, keepdims=True)
    return rstd * (dxhat - m1 - xhat * m2)


def _col_sum(a):
    return jnp.sum(a, axis=0, keepdims=True)


def _gelu_parts(z):
    cdf = 0.5 * (1.0 + lax.erf(z * (1.0 / math.sqrt(2.0))))
    pdf = jnp.exp(-0.5 * z * z) * (1.0 / math.sqrt(2.0 * math.pi))
    return cdf, pdf


def _inv_counts(seq_tile, rows):
    pos = seq_tile * rows + lax.broadcasted_iota(jnp.int32, (rows, GROUP), 0) + 1
    return [1.0 / jnp.minimum(pos, w).astype(f32) for w in POOL_WINDOWS]


def _window_sums(e, back):
    n = e.shape[0]

    def shifted(a, s):
        return pltpu.roll(a, s if back else n - s, 0)

    s2 = e + shifted(e, 1)
    s4 = s2[:, GROUP:] + shifted(s2[:, GROUP:], 2)
    s8 = s4[:, GROUP:] + shifted(s4[:, GROUP:], 4)
    s16 = s8[:, GROUP:] + shifted(s8[:, GROUP:], 8)
    return [s2[:, :GROUP], s4[:, :GROUP], s8[:, :GROUP], s16]


def _pooled_groups(xp, halo, inv):
    sums = _window_sums(jnp.concatenate([halo, xp], axis=0), back=True)
    return [sums[g][HALO:] * inv[g] - xp[:, g * GROUP:(g + 1) * GROUP] for g in range(N_GROUPS)]


def _tril_mask():
    r = lax.broadcasted_iota(jnp.int32, (GROUP, GROUP), 0)
    c = lax.broadcasted_iota(jnp.int32, (GROUP, GROUP), 1)
    return (r >= c).astype(f32)


def _gs(g):
    return slice(g * GROUP, (g + 1) * GROUP)


class _Cut:
    def __init__(self, rows, cols, by_cols):
        self.rows, self.cols, self.by_cols = rows, cols, by_cols
        if by_cols:
            self.block_shape = (rows // 2, cols // N_CHIPS)
            self.shard_shape = (rows, cols // N_CHIPS)
        else:
            self.block_shape = (rows // (2 * N_CHIPS), cols)
            self.shard_shape = (rows // N_CHIPS, cols)

    def block(self, ref, chip, half):
        br, bc = self.block_shape
        if self.by_cols:
            return ref.at[pl.ds(pl.multiple_of(half * br, 16), br), pl.ds(pl.multiple_of(chip * bc, 128), bc)]
        return ref.at[pl.ds(pl.multiple_of((2 * chip + half) * br, 8), br), :]

    def shard(self, ref, chip):
        sr, sc = self.shard_shape
        if self.by_cols:
            return ref.at[:, pl.ds(pl.multiple_of(chip * sc, 128), sc)]
        return ref.at[pl.ds(pl.multiple_of(chip * sr, 16), sr), :]

    def half_of_shard(self, ref, half):
        br = self.block_shape[0]
        return ref.at[pl.ds(pl.multiple_of(half * br, 8), br), :]

    def block_index(self, chip, half):
        return (half, chip) if self.by_cols else (2 * chip + half, 0)


CUTS = {
    "w_in": _Cut(D_MODEL, IN_COLS, True),
    "w_out": _Cut(D_MODEL, D_MODEL, False),
    "w_gate_up": _Cut(D_MODEL, 2 * D_FF, True),
    "w_down": _Cut(D_FF, D_MODEL, False),
    "pool_w": _Cut(SQUARE_ROWS, GROUP, False),
    "sgu_w": _Cut(SQUARE_ROWS, GROUP, False),
    "vecs": _Cut(VEC_ROWS, GROUP, False),
}
BIG = ("w_in", "w_out", "w_gate_up", "w_down")
SMALL = ("pool_w", "sgu_w", "vecs")
ANY = pl.BlockSpec(memory_space=pl.ANY)


def _wire_dtype(name):
    return bf16 if name in BIG else f32


def _place():
    x, y, c = lax.axis_index("x"), lax.axis_index("y"), lax.axis_index("c")
    others = [(1 - x, y), (x, 1 - y), (1 - x, 1 - y)]
    return x, y, c, 2 * x + y, others, [2 * ox + oy for ox, oy in others]


def _remote(src, dst, send_sems, recv_sems, k, to):
    return pltpu.make_async_remote_copy(src_ref=src, dst_ref=dst, send_sem=send_sems.at[k], recv_sem=recv_sems.at[k],
                                        device_id=to, device_id_type=MESH)


class _GatherJob:
    def __init__(self, shards):
        self.names = tuple(shards)
        self.arrays = tuple(shards.values())
        n = len(self.names)
        self.out_shapes = [jax.ShapeDtypeStruct((CUTS[name].rows, CUTS[name].cols), bf16) for name in self.names]
        self.scratch_shapes = ([pltpu.SemaphoreType.DMA((6 * n,)), pltpu.SemaphoreType.DMA((6 * n,)),
                                pltpu.SemaphoreType.DMA((2 * n,))]
                               + [pltpu.VMEM(CUTS[name].shard_shape, bf16) for name in self.names])

    def bind(self, shard_refs, full_refs, scratch):
        self.shards, self.full = shard_refs, full_refs
        self.send_sems, self.recv_sems, self.local_sems = scratch[:3]
        self.stages = scratch[3:]
        return self

    def _sends(self):
        _, _, c, me, others, _ = _place()
        return [_remote(CUTS[name].half_of_shard(self.shards[w], c), CUTS[name].block(self.full[w], me, c),
                        self.send_sems, self.recv_sems, 3 * w + k, (*chip, c))
                for w, name in enumerate(self.names) for k, chip in enumerate(others)]

    def _relays(self, half):
        x, y, c, _, _, other_ids = _place()
        n = len(self.names)
        return [_remote(CUTS[name].block(self.full[w], other_ids[k], half),
                        CUTS[name].block(self.full[w], other_ids[k], half),
                        self.send_sems, self.recv_sems, 3 * n + 3 * w + k, (x, y, 1 - c))
                for w, name in enumerate(self.names) for k in range(3)]

    def _stores(self):
        me = _place()[3]
        return [pltpu.make_async_copy(self.stages[w], CUTS[name].shard(self.full[w], me), self.local_sems.at[2 * w + 1])
                for w, name in enumerate(self.names)]

    def start(self):
        loads = [pltpu.make_async_copy(self.shards[w], self.stages[w], self.local_sems.at[2 * w])
                 for w in range(len(self.names))]
        for cp in loads:
            cp.start()
        for cp in self._sends():
            cp.start()
        for load, store in zip(loads, self._stores()):
            load.wait()
            store.start()

    def finish(self):
        _, _, c, _, others, other_ids = _place()
        relays = self._relays(c)
        for w, name in enumerate(self.names):
            for k, chip in enumerate(others):
                landed = CUTS[name].block(self.full[w], other_ids[k], c)
                _remote(landed, landed, self.send_sems, self.recv_sems, 3 * w + k, (*chip, c)).wait_recv()
                relays[3 * w + k].start()
        for cp in self._relays(1 - c):
            cp.wait_recv()
        for cp in self._sends() + relays:
            cp.wait_send()
        for cp in self._stores():
            cp.wait()


class _SwapHalvesJob:
    def __init__(self, grads):
        self.names = tuple(grads)
        self.arrays = tuple(grads.values())
        n = len(self.names)
        self.out_shapes = [jax.ShapeDtypeStruct((N_CHIPS, *CUTS[name].block_shape), f32) for name in self.names]
        self.scratch_shapes = [pltpu.SemaphoreType.DMA((N_CHIPS * n,)), pltpu.SemaphoreType.DMA((N_CHIPS * n,))]

    def bind(self, g_refs, got_refs, scratch):
        self.g_refs, self.got_refs = g_refs, got_refs
        self.send_sems, self.recv_sems = scratch
        return self

    def _copies(self):
        x, y, c, _, _, _ = _place()
        return [_remote(CUTS[name].block(self.g_refs[a], j, 1 - c), self.got_refs[a].at[j], self.send_sems,
                        self.recv_sems, N_CHIPS * a + j, (x, y, 1 - c))
                for a, name in enumerate(self.names) for j in range(N_CHIPS)]

    def start(self):
        for cp in self._copies():
            cp.start()

    def finish(self):
        for cp in self._copies():
            cp.wait()


class _SwapChipsJob:
    def __init__(self, partials):
        self.names = tuple(partials)
        self.arrays = tuple(partials.values())
        n = len(self.names)
        self.out_shapes = [jax.ShapeDtypeStruct((3, *CUTS[name].block_shape), _wire_dtype(name)) for name in self.names]
        self.scratch_shapes = [pltpu.SemaphoreType.DMA((3 * n,)), pltpu.SemaphoreType.DMA((3 * n,))]

    def bind(self, p_refs, got_refs, scratch):
        self.p_refs, self.got_refs = p_refs, got_refs
        self.send_sems, self.recv_sems = scratch
        return self

    def _copies(self):
        _, _, c, _, others, other_ids = _place()
        return [_remote(self.p_refs[a].at[other_ids[k]], self.got_refs[a].at[k], self.send_sems, self.recv_sems,
                        3 * a + k, (*chip, c))
                for a in range(len(self.names)) for k, chip in enumerate(others)]

    def start(self):
        for cp in self._copies():
            cp.start()

    def finish(self):
        for cp in self._copies():
            cp.wait()


class _ShareJob:
    def __init__(self, reduced):
        self.names = tuple(reduced)
        self.arrays = tuple(reduced.values())
        self.big = [a for a, name in enumerate(self.names) if name in BIG]
        self.small = [a for a, name in enumerate(self.names) if name in SMALL]
        self.out_shapes = [jax.ShapeDtypeStruct(CUTS[name].shard_shape if name in BIG
                                                else (CUTS[name].rows, CUTS[name].cols), f32) for name in self.names]
        n_sems = len(self.big) + 7 * len(self.small)
        self.scratch_shapes = ([pltpu.SemaphoreType.DMA((n_sems,)), pltpu.SemaphoreType.DMA((n_sems,)),
                                pltpu.SemaphoreType.DMA((2 * len(self.names),))]
                               + [pltpu.VMEM(CUTS[name].block_shape, f32) for name in self.names])

    def bind(self, f_refs, out_refs, scratch):
        self.f_refs, self.out_refs = f_refs, out_refs
        self.send_sems, self.recv_sems, self.local_sems = scratch[:3]
        self.stages = scratch[3:]
        return self

    def _sem(self, a, which=0):
        if a in self.big:
            return self.big.index(a)
        return len(self.big) + 7 * self.small.index(a) + which

    def _mine(self, a, half):
        me = _place()[3]
        cut = CUTS[self.names[a]]
        return cut.half_of_shard(self.out_refs[a], half) if a in self.big else cut.block(self.out_refs[a], me, half)

    def _to_sibling(self):
        x, y, c, _, _, _ = _place()
        return [_remote(self.f_refs[a], self._mine(a, c), self.send_sems, self.recv_sems, self._sem(a), (x, y, 1 - c))
                for a in range(len(self.names))]

    def _to_chips(self):
        _, _, c, _, others, _ = _place()
        return [_remote(self.f_refs[a], self._mine(a, c), self.send_sems, self.recv_sems, self._sem(a, 1 + k), (*chip, c))
                for a in self.small for k, chip in enumerate(others)]

    def _passes(self, half):
        x, y, c, _, _, other_ids = _place()
        out = []
        for a in self.small:
            for k in range(3):
                blk = CUTS[self.names[a]].block(self.out_refs[a], other_ids[k], half)
                out.append(_remote(blk, blk, self.send_sems, self.recv_sems, self._sem(a, 4 + k), (x, y, 1 - c)))
        return out

    def _stores(self):
        c = _place()[2]
        return [pltpu.make_async_copy(self.stages[a], self._mine(a, c), self.local_sems.at[2 * a + 1])
                for a in range(len(self.names))]

    def start(self):
        loads = [pltpu.make_async_copy(self.f_refs[a], self.stages[a], self.local_sems.at[2 * a])
                 for a in range(len(self.names))]
        for cp in loads:
            cp.start()
        for cp in self._to_sibling() + self._to_chips():
            cp.start()
        for load, store in zip(loads, self._stores()):
            load.wait()
            store.start()

    def finish(self):
        x, y, c, _, others, other_ids = _place()
        passes = self._passes(c)
        for s, a in enumerate(self.small):
            for k, chip in enumerate(others):
                landed = CUTS[self.names[a]].block(self.out_refs[a], other_ids[k], c)
                _remote(landed, landed, self.send_sems, self.recv_sems, self._sem(a, 1 + k), (*chip, c)).wait_recv()
                passes[3 * s + k].start()
        for a in range(len(self.names)):
            theirs = self._mine(a, 1 - c)
            _remote(theirs, theirs, self.send_sems, self.recv_sems, self._sem(a), (x, y, 1 - c)).wait_recv()
        for cp in self._passes(1 - c):
            cp.wait_recv()
        for cp in self._to_sibling() + self._to_chips() + passes:
            cp.wait_send()
        for cp in self._stores():
            cp.wait()


def _call(body, *, name, grid, in_specs, out_specs, out_shape, args, scratch_shapes=(), jobs=()):
    n_in, n_out, n_scr = len(in_specs), len(out_specs), len(scratch_shapes)
    j_in = [len(j.arrays) for j in jobs]
    j_out = [len(j.out_shapes) for j in jobs]
    j_scr = [len(j.scratch_shapes) for j in jobs]

    def wrapped(*refs):
        refs = list(refs)

        def take(k):
            head = refs[:k]
            del refs[:k]
            return head

        ins, jins = take(n_in), [take(k) for k in j_in]
        outs, jouts = take(n_out), [take(k) for k in j_out]
        scr, jscr = take(n_scr), [take(k) for k in j_scr]
        bound = [j.bind(a, b, c) for j, a, b, c in zip(jobs, jins, jouts, jscr)]
        if not grid:
            for b in bound:
                b.start()
            body(*ins, *outs, *scr)
            for b in bound:
                b.finish()
            return
        if not bound:
            body(*ins, *outs, *scr)
            return
        first = _all([pl.program_id(d) == 0 for d in range(len(grid))])
        last = _all([pl.program_id(d) == grid[d] - 1 for d in range(len(grid))])

        @pl.when(first)
        def _():
            for b in bound:
                b.start()

        body(*ins, *outs, *scr)

        @pl.when(last)
        def _():
            for b in bound:
                b.finish()

    kw = dict(grid=grid) if grid else {}
    results = pl.pallas_call(
        wrapped, name=name,
        in_specs=list(in_specs) + [ANY] * sum(j_in), out_specs=list(out_specs) + [ANY] * sum(j_out),
        out_shape=list(out_shape) + [s for j in jobs for s in j.out_shapes],
        scratch_shapes=list(scratch_shapes) + [s for j in jobs for s in j.scratch_shapes],
        compiler_params=_params(), **kw,
    )(*args, *[a for j in jobs for a in j.arrays])
    results = list(results)
    own, rest = results[:n_out], results[n_out:]
    per_job = []
    for j, k in zip(jobs, j_out):
        per_job.append(dict(zip(j.names, rest[:k])))
        rest = rest[k:]
    return own, per_job


def _all(conds):
    out = conds[0]
    for c in conds[1:]:
        out = jnp.logical_and(out, c)
    return out


def _alone(job, name):
    return _call(lambda: None, name=name, grid=None, in_specs=[], out_specs=[], out_shape=[], args=[], jobs=[job])[1][0]


def _fwd_proj(x2, w_in_b, tile, jobs):
    tokens = x2.shape[0]

    def body(x_ref, w_ref, o_ref):
        o_ref[...] = _mm(x_ref[...].astype(bf16), w_ref[...])

    return _call(
        body, name="fwd_proj", grid=(tokens // tile,),
        in_specs=[pl.BlockSpec((tile, D_MODEL), lambda i: (i, 0)),
                  pl.BlockSpec((D_MODEL, IN_COLS), lambda i: (0, 0))],
        out_specs=[pl.BlockSpec((tile, IN_COLS), lambda i: (i, 0))],
        out_shape=[jax.ShapeDtypeStruct((tokens, IN_COLS), f32)],
        args=(x2, w_in_b), jobs=jobs)


def _small_specs():
    return [pl.BlockSpec((N_GROUPS, GROUP, GROUP), lambda i: (0, 0, 0)),
            pl.BlockSpec((1, POOL_WIDTH), lambda i: (0, 0)),
            pl.BlockSpec((1, SGU_WIDTH), lambda i: (0, 0)),
            pl.BlockSpec((1, SGU_WIDTH), lambda i: (0, 0)),
            pl.BlockSpec((N_GROUPS, GROUP, GROUP), lambda i: (0, 0, 0)),
            pl.BlockSpec((N_GROUPS, GROUP, GROUP), lambda i: (0, 0, 0))]


def _fwd_mix(proj, x2, small, w_out_b, ln1_g, ln1_b, tile, seq, jobs):
    tokens = x2.shape[0]
    tps = seq // tile
    hb = tile // HALO

    def body(proj_ref, halo_ref, x_ref, pw_ref, ps_ref, lg_ref, lb_ref, sw_ref, sb_ref, wout_ref, g1_ref, b1_ref,
             mix_ref, r1_ref, h_ref):
        seq_tile = pl.program_id(0) % tps
        xp = proj_ref[:, :POOL_WIDTH]
        halo = jnp.where(seq_tile == 0, 0.0, halo_ref[...])
        pooled = _pooled_groups(xp, halo, _inv_counts(seq_tile, tile))
        for g in range(N_GROUPS):
            po = _mm(pooled[g].astype(bf16), pw_ref[g].astype(bf16)) * ps_ref[:, _gs(g)]
            mix_ref[:, _gs(g)] = po.astype(bf16)

        cdf, _ = _gelu_parts(proj_ref[:, POOL_WIDTH:])
        zg = proj_ref[:, POOL_WIDTH:] * cdf
        u = zg[:, :SGU_WIDTH]
        vln, _, _ = _ln_fwd(zg[:, SGU_WIDTH:], lg_ref[...], lb_ref[...])
        vb = vln.astype(bf16)
        mask = _tril_mask()
        for h in range(N_GROUPS):
            wm = (sw_ref[h] * mask).astype(bf16)
            bias = sb_ref[h]
            for c in range(tile // GROUP):
                rows = slice(c * GROUP, (c + 1) * GROUP)
                mixed = _mm(wm, vb[rows, _gs(h)]) + bias
                mix_ref[rows, POOL_WIDTH + h * GROUP:POOL_WIDTH + (h + 1) * GROUP] = (u[rows, _gs(h)] * mixed).astype(bf16)

        r1 = ALPHA * x_ref[...] + _mm(mix_ref[...], wout_ref[...])
        r1_ref[...] = r1
        h1, _, _ = _ln_fwd(r1, g1_ref[...], b1_ref[...])
        h_ref[...] = h1.astype(bf16)

    row = lambda i: (i, 0)
    vec = pl.BlockSpec((1, D_MODEL), lambda i: (0, 0))
    return _call(
        body, name="fwd_mix", grid=(tokens // tile,),
        in_specs=[pl.BlockSpec((tile, IN_COLS), row),
                  pl.BlockSpec((HALO, POOL_WIDTH), lambda i: (jnp.maximum(i * hb - 1, 0), 0)),
                  pl.BlockSpec((tile, D_MODEL), row)] + _small_specs()
                 + [pl.BlockSpec((D_MODEL, D_MODEL), lambda i: (0, 0)), vec, vec],
        out_specs=[pl.BlockSpec((tile, D_MODEL), row)] * 3,
        out_shape=[jax.ShapeDtypeStruct((tokens, D_MODEL), bf16),
                   jax.ShapeDtypeStruct((tokens, D_MODEL), f32),
                   jax.ShapeDtypeStruct((tokens, D_MODEL), bf16)],
        args=(proj, proj, x2, *small, w_out_b, ln1_g, ln1_b), jobs=jobs)


def _fwd_gate_up(h_b, w_gu_b, tile, jobs):
    tokens = h_b.shape[0]

    def body(h_ref, w_ref, gu_ref, a_ref):
        hb = h_ref[...]
        for c in range(D_FF // FF_CHUNK):
            gcols = slice(c * FF_CHUNK, (c + 1) * FF_CHUNK)
            ucols = slice(D_FF + c * FF_CHUNK, D_FF + (c + 1) * FF_CHUNK)
            gate = _mm(hb, w_ref[:, gcols])
            up = _mm(hb, w_ref[:, ucols])
            gu_ref[:, gcols] = gate.astype(bf16)
            gu_ref[:, ucols] = up.astype(bf16)
            a_ref[:, gcols] = (gate * jax.nn.sigmoid(gate) * up).astype(bf16)

    return _call(
        body, name="fwd_gate_up", grid=(tokens // tile,),
        in_specs=[pl.BlockSpec((tile, D_MODEL), lambda i: (i, 0)),
                  pl.BlockSpec((D_MODEL, 2 * D_FF), lambda i: (0, 0), pipeline_mode=pl.Buffered(1))],
        out_specs=[pl.BlockSpec((tile, 2 * D_FF), lambda i: (i, 0)),
                   pl.BlockSpec((tile, D_FF), lambda i: (i, 0))],
        out_shape=[jax.ShapeDtypeStruct((tokens, 2 * D_FF), bf16),
                   jax.ShapeDtypeStruct((tokens, D_FF), bf16)],
        args=(h_b, w_gu_b), jobs=jobs)


def _fwd_down_loss(a_b, w_dn_b, r1, target, ln1_g, ln1_b, ln2_g, ln2_b, tile):
    tokens = a_b.shape[0]

    def body(a_ref, w_ref, r1_ref, t_ref, g1_ref, b1_ref, g2_ref, b2_ref, dr2_ref, st_ref):
        @pl.when(pl.program_id(0) == 0)
        def _():
            st_ref[...] = jnp.zeros_like(st_ref)

        sub = min(ROW_SUB, tile)
        for s in range(tile // sub):
            rows = slice(s * sub, (s + 1) * sub)
            h1, _, _ = _ln_fwd(r1_ref[rows, :], g1_ref[...], b1_ref[...])
            r2 = ALPHA * h1 + _mm(a_ref[rows, :], w_ref[...])
            y, xhat, rstd = _ln_fwd(r2, g2_ref[...], b2_ref[...])
            diff = y - t_ref[rows, :]
            dy = diff * (1.0 / D_MODEL)
            st_ref[0:1, :] += _col_sum(dy * xhat)
            st_ref[1:2, :] += _col_sum(dy)
            st_ref[2:3, :] += _col_sum(diff * diff)
            dr2_ref[rows, :] = _ln_bwd(dy, xhat, rstd, g2_ref[...])

    row = lambda i: (i, 0)
    vec = pl.BlockSpec((1, D_MODEL), lambda i: (0, 0))
    return _call(
        body, name="fwd_down_loss", grid=(tokens // tile,),
        in_specs=[pl.BlockSpec((tile, D_FF), row), pl.BlockSpec((D_FF, D_MODEL), lambda i: (0, 0)),
                  pl.BlockSpec((tile, D_MODEL), row), pl.BlockSpec((tile, D_MODEL), row), vec, vec, vec, vec],
        out_specs=[pl.BlockSpec((tile, D_MODEL), row), pl.BlockSpec((8, D_MODEL), lambda i: (0, 0))],
        out_shape=[jax.ShapeDtypeStruct((tokens, D_MODEL), f32), jax.ShapeDtypeStruct((8, D_MODEL), f32)],
        args=(a_b, w_dn_b, r1, target, ln1_g, ln1_b, ln2_g, ln2_b))[0]


def _bwd_gate_up(dr2, gu_b, w_dn_b, tile, jobs):
    tokens = dr2.shape[0]

    def body(d_ref, gu_ref, w_ref, dgu_ref):
        d = d_ref[...].astype(bf16)
        for c in range(D_FF // FF_CHUNK):
            gcols = slice(c * FF_CHUNK, (c + 1) * FF_CHUNK)
            ucols = slice(D_FF + c * FF_CHUNK, D_FF + (c + 1) * FF_CHUNK)
            da = _mm_nt(d, w_ref[gcols, :])
            gate = gu_ref[:, gcols].astype(f32)
            up = gu_ref[:, ucols].astype(f32)
            sg = jax.nn.sigmoid(gate)
            silu = gate * sg
            dgu_ref[:, gcols] = (da * up * (sg + silu * (1.0 - sg))).astype(bf16)
            dgu_ref[:, ucols] = (da * silu).astype(bf16)

    return _call(
        body, name="bwd_gate_up", grid=(tokens // tile,),
        in_specs=[pl.BlockSpec((tile, D_MODEL), lambda i: (i, 0)),
                  pl.BlockSpec((tile, 2 * D_FF), lambda i: (i, 0)),
                  pl.BlockSpec((D_FF, D_MODEL), lambda i: (0, 0))],
        out_specs=[pl.BlockSpec((tile, 2 * D_FF), lambda i: (i, 0))],
        out_shape=[jax.ShapeDtypeStruct((tokens, 2 * D_FF), bf16)],
        args=(dr2, gu_b, w_dn_b), jobs=jobs)


def _bwd_ffn_in(dgu_b, w_gu_b, dr2, r1, ln1_g, ln1_b, tile, jobs):
    tokens = dr2.shape[0]

    def body(dgu_ref, w_ref, d_ref, r1_ref, g1_ref, b1_ref, dr1_ref, st_ref):
        @pl.when(pl.program_id(0) == 0)
        def _():
            st_ref[...] = jnp.zeros_like(st_ref)

        dh = ALPHA * d_ref[...] + _mm_nt(dgu_ref[...], w_ref[...])
        _, xhat, rstd = _ln_fwd(r1_ref[...], g1_ref[...], b1_ref[...])
        st_ref[0:1, :] += _col_sum(dh * xhat)
        st_ref[1:2, :] += _col_sum(dh)
        dr1_ref[...] = _ln_bwd(dh, xhat, rstd, g1_ref[...])

    row = lambda i: (i, 0)
    vec = pl.BlockSpec((1, D_MODEL), lambda i: (0, 0))
    return _call(
        body, name="bwd_ffn_in", grid=(tokens // tile,),
        in_specs=[pl.BlockSpec((tile, 2 * D_FF), row),
                  pl.BlockSpec((D_MODEL, 2 * D_FF), lambda i: (0, 0), pipeline_mode=pl.Buffered(1)),
                  pl.BlockSpec((tile, D_MODEL), row), pl.BlockSpec((tile, D_MODEL), row), vec, vec],
        out_specs=[pl.BlockSpec((tile, D_MODEL), row), pl.BlockSpec((8, D_MODEL), lambda i: (0, 0))],
        out_shape=[jax.ShapeDtypeStruct((tokens, D_MODEL), f32), jax.ShapeDtypeStruct((8, D_MODEL), f32)],
        args=(dgu_b, w_gu_b, dr2, r1, ln1_g, ln1_b), jobs=jobs)


def _bwd_mix(dr1, proj, small, w_out_b, tile, seq, jobs):
    tokens = dr1.shape[0]
    tps = seq // tile
    hb = tile // HALO
    steps = tokens // tile

    def body(dr1_ref, proj_ref, halo_ref, wout_ref, pw_ref, ps_ref, lg_ref, lb_ref, sw_ref, sb_ref,
             dpp_ref, gpw_ref, gsw_ref, gsb_ref, vec_ref, du_ref, dvln_ref):
        step = pl.program_id(0)
        seq_tile = step % tps

        @pl.when(step == 0)
        def _():
            gpw_ref[...] = jnp.zeros_like(gpw_ref)
            gsw_ref[...] = jnp.zeros_like(gsw_ref)
            gsb_ref[...] = jnp.zeros_like(gsb_ref)
            vec_ref[...] = jnp.zeros_like(vec_ref)

        dmix = _mm_nt(dr1_ref[...].astype(bf16), wout_ref[...])

        xp = proj_ref[:, :POOL_WIDTH]
        halo = jnp.where(seq_tile == 0, 0.0, halo_ref[...])
        pooled = _pooled_groups(xp, halo, _inv_counts(seq_tile, tile))
        for g in range(N_GROUPS):
            pb = pooled[g].astype(bf16)
            pwb = pw_ref[g].astype(bf16)
            dpo = dmix[:, _gs(g)]
            vec_ref[0:1, _gs(g)] += _col_sum(dpo * _mm(pb, pwb))
            dpo_b = (dpo * ps_ref[:, _gs(g)]).astype(bf16)
            gpw_ref[g] += _mm_tn(pb, dpo_b)
            dpp_ref[:, _gs(g)] = _mm_nt(dpo_b, pwb)

        pre = proj_ref[:, POOL_WIDTH:]
        cdf, pdf = _gelu_parts(pre)
        zg = pre * cdf
        u = zg[:, :SGU_WIDTH]
        vln, vhat, rstd = _ln_fwd(zg[:, SGU_WIDTH:], lg_ref[...], lb_ref[...])
        vb = vln.astype(bf16)
        mask = _tril_mask()
        for h in range(N_GROUPS):
            wm = (sw_ref[h] * mask).astype(bf16)
            bias = sb_ref[h]
            gsw = jnp.zeros((GROUP, GROUP), f32)
            gsb = jnp.zeros((GROUP, GROUP), f32)
            for c in range(tile // GROUP):
                rows = slice(c * GROUP, (c + 1) * GROUP)
                v_ch = vb[rows, _gs(h)]
                d = dmix[rows, POOL_WIDTH + h * GROUP:POOL_WIDTH + (h + 1) * GROUP]
                du_ref[rows, _gs(h)] = d * (_mm(wm, v_ch) + bias)
                dmixed = d * u[rows, _gs(h)]
                gsb += dmixed
                dmixed_b = dmixed.astype(bf16)
                gsw += _mm_nt(dmixed_b, v_ch)
                dvln_ref[rows, _gs(h)] = _mm_tn(wm, dmixed_b)
            gsw_ref[h] += gsw * mask
            gsb_ref[h] += gsb

        dvln = dvln_ref[...]
        vec_ref[1:2, :] += _col_sum(dvln * vhat)
        vec_ref[2:3, :] += _col_sum(dvln)
        dgelu = cdf + pre * pdf
        dpp_ref[:, POOL_WIDTH:POOL_WIDTH + SGU_WIDTH] = du_ref[...] * dgelu[:, :SGU_WIDTH]
        dpp_ref[:, POOL_WIDTH + SGU_WIDTH:] = _ln_bwd(dvln, vhat, rstd, lg_ref[...]) * dgelu[:, SGU_WIDTH:]

        @pl.when(step == steps - 1)
        def _():
            for h in range(N_GROUPS):
                gsb_ref[h] = jnp.broadcast_to(jnp.sum(gsb_ref[h], axis=1, keepdims=True), (GROUP, GROUP))

    row = lambda i: (i, 0)
    sq = jax.ShapeDtypeStruct((N_GROUPS, GROUP, GROUP), f32)
    sq_spec = pl.BlockSpec((N_GROUPS, GROUP, GROUP), lambda i: (0, 0, 0))
    return _call(
        body, name="bwd_mix", grid=(steps,),
        in_specs=[pl.BlockSpec((tile, D_MODEL), row), pl.BlockSpec((tile, IN_COLS), row),
                  pl.BlockSpec((HALO, POOL_WIDTH), lambda i: (jnp.maximum(i * hb - 1, 0), 0)),
                  pl.BlockSpec((D_MODEL, D_MODEL), lambda i: (0, 0))] + _small_specs(),
        out_specs=[pl.BlockSpec((tile, IN_COLS), row), sq_spec, sq_spec, sq_spec,
                   pl.BlockSpec((8, POOL_WIDTH), lambda i: (0, 0))],
        out_shape=[jax.ShapeDtypeStruct((tokens, IN_COLS), f32), sq, sq, sq,
                   jax.ShapeDtypeStruct((8, POOL_WIDTH), f32)],
        scratch_shapes=[pltpu.VMEM((tile, SGU_WIDTH), f32), pltpu.VMEM((tile, SGU_WIDTH), f32)],
        args=(dr1, proj, proj, w_out_b, *small), jobs=jobs)


def _bwd_in(dpp, dr1, w_in_b, tile, seq):
    tokens = dr1.shape[0]
    tps = seq // tile
    hb = tile // HALO
    last_halo = tokens // HALO - 1

    def body(dpp_ref, nxt_ref, dr1_ref, w_ref, dx_ref, dproj_ref):
        seq_tile = pl.program_id(0) % tps
        inv = _inv_counts(seq_tile, tile)
        dpl = dpp_ref[:, :POOL_WIDTH]
        nxt = jnp.where(seq_tile == tps - 1, 0.0, nxt_ref[...])
        scaled = jnp.concatenate([dpl[:, _gs(g)] * inv[g] for g in range(N_GROUPS)], axis=1)
        scaled_nxt = jnp.concatenate([nxt[:, _gs(g)] * (1.0 / POOL_WINDOWS[g]) for g in range(N_GROUPS)], axis=1)
        sums = _window_sums(jnp.concatenate([scaled, scaled_nxt], axis=0), back=False)
        for g in range(N_GROUPS):
            dproj_ref[:, _gs(g)] = (sums[g][:tile] - dpl[:, _gs(g)]).astype(bf16)
        dproj_ref[:, POOL_WIDTH:] = dpp_ref[:, POOL_WIDTH:].astype(bf16)
        dx_ref[...] = ALPHA * dr1_ref[...] + _mm_nt(dproj_ref[...], w_ref[...])

    row = lambda i: (i, 0)
    return _call(
        body, name="bwd_in", grid=(tokens // tile,),
        in_specs=[pl.BlockSpec((tile, IN_COLS), row),
                  pl.BlockSpec((HALO, POOL_WIDTH), lambda i: (jnp.minimum((i + 1) * hb, last_halo), 0)),
                  pl.BlockSpec((tile, D_MODEL), row),
                  pl.BlockSpec((D_MODEL, IN_COLS), lambda i: (0, 0))],
        out_specs=[pl.BlockSpec((tile, D_MODEL), row), pl.BlockSpec((tile, IN_COLS), row)],
        out_shape=[jax.ShapeDtypeStruct((tokens, D_MODEL), f32), jax.ShapeDtypeStruct((tokens, IN_COLS), bf16)],
        args=(dpp, dpp, dr1, w_in_b))[0]


def _wgrad(a, b, col_tile, tile, name, jobs=()):
    tokens, m = a.shape
    n = b.shape[1]

    def body(a_ref, b_ref, o_ref):
        @pl.when(pl.program_id(1) == 0)
        def _():
            o_ref[...] = jnp.zeros_like(o_ref)

        o_ref[...] += _mm_tn(a_ref[...].astype(bf16), b_ref[...].astype(bf16))

    (out,), got = _call(
        body, name=name, grid=(n // col_tile, tokens // tile),
        in_specs=[pl.BlockSpec((tile, m), lambda j, k: (k, 0)),
                  pl.BlockSpec((tile, col_tile), lambda j, k: (k, j))],
        out_specs=[pl.BlockSpec((m, col_tile), lambda j, k: (0, j))],
        out_shape=[jax.ShapeDtypeStruct((m, n), f32)],
        args=(a, b), jobs=jobs)
    return out, got


def _add_halves(name, g, got, core):
    cut = CUTS[name]
    br, bc = cut.block_shape
    wire = _wire_dtype(name)

    def body(core_ref, g_ref, got_ref, o_ref, wire_ref):
        s = g_ref[...] + got_ref[...]
        o_ref[...] = s
        wire_ref[...] = s.astype(wire)

    blocks = pl.BlockSpec((None, br, bc), lambda j, core_ref: (j, 0, 0))
    return pl.pallas_call(
        body, name="reduce_add_halves_" + name,
        grid_spec=pltpu.PrefetchScalarGridSpec(
            num_scalar_prefetch=1, grid=(N_CHIPS,),
            in_specs=[pl.BlockSpec((br, bc), lambda j, core_ref: cut.block_index(j, core_ref[0])), blocks],
            out_specs=[blocks, blocks]),
        out_shape=[jax.ShapeDtypeStruct((N_CHIPS, br, bc), f32), jax.ShapeDtypeStruct((N_CHIPS, br, bc), wire)],
        compiler_params=_params(),
    )(core, g, got)


def _add_chips(name, partial, got, chip):
    br, bc = CUTS[name].block_shape
    rt = br // 2 if br % 32 == 0 else br

    def body(chip_ref, p_ref, got_ref, o_ref):
        o_ref[...] = ((p_ref[...] + got_ref[0].astype(f32)) + got_ref[1].astype(f32)) + got_ref[2].astype(f32)

    return pl.pallas_call(
        body, name="reduce_add_chips_" + name,
        grid_spec=pltpu.PrefetchScalarGridSpec(
            num_scalar_prefetch=1, grid=(br // rt,),
            in_specs=[pl.BlockSpec((None, rt, bc), lambda i, chip_ref: (chip_ref[0], i, 0)),
                      pl.BlockSpec((3, rt, bc), lambda i, chip_ref: (0, i, 0))],
            out_specs=pl.BlockSpec((rt, bc), lambda i, chip_ref: (i, 0))),
        out_shape=jax.ShapeDtypeStruct((br, bc), f32),
        compiler_params=_params(),
    )(chip, partial, got)


def _adamw(name, w, g, m, v):
    rows, cols = w.shape
    rt = rows // 4

    def body(w_ref, g_ref, m_ref, v_ref, d_ref, nm_ref, nv_ref):
        g = g_ref[...]
        nm = ADAM_B1 * m_ref[...] + (1.0 - ADAM_B1) * g
        nv = ADAM_B2 * v_ref[...] + (1.0 - ADAM_B2) * jnp.square(g)
        m_hat = nm / (1.0 - ADAM_B1 ** ADAM_STEP)
        v_hat = nv / (1.0 - ADAM_B2 ** ADAM_STEP)
        d_ref[...] = -ADAM_LR * (m_hat / (jnp.sqrt(v_hat) + ADAM_EPS) + ADAM_WD * w_ref[...])
        nm_ref[...] = nm
        nv_ref[...] = nv

    spec = pl.BlockSpec((rt, cols), lambda i: (i, 0))
    shape = jax.ShapeDtypeStruct((rows, cols), f32)
    return pl.pallas_call(
        body, name="adamw_" + name, grid=(rows // rt,),
        in_specs=[spec] * 4, out_specs=[spec] * 3, out_shape=[shape] * 3,
        compiler_params=_params(),
    )(w, g, m, v)


VEC_NAMES = ("pool_scale", "sgu_ln_g", "sgu_ln_b", "sgu_b", "ln1_g", "ln1_b", "ln2_g", "ln2_b")
WEIGHT_ORDER = ("w_in", "pool_w", "pool_scale", "sgu_ln_g", "sgu_ln_b", "sgu_w", "sgu_b", "w_out", "ln1_g", "ln1_b",
                "w_gate_up", "w_down", "ln2_g", "ln2_b")


def _pack_vecs(parts, extra=None):
    rows = [parts[name].reshape(-1, GROUP) for name in VEC_NAMES]
    if extra is not None:
        rows.append(extra.reshape(-1, GROUP))
    used = sum(r.shape[0] for r in rows)
    return jnp.concatenate(rows + [jnp.zeros((VEC_ROWS - used, GROUP), f32)], axis=0)


def _unpack_vecs(packed, shapes):
    out, at = {}, 0
    for name in VEC_NAMES:
        rows = math.prod(shapes[name]) // GROUP
        out[name] = packed[at:at + rows].reshape(shapes[name])
        at += rows
    return out, packed[at:]


def kernel(x, w_in, pool_w, pool_scale, sgu_ln_g, sgu_ln_b, sgu_w, sgu_b, w_out, ln1_g, ln1_b, w_gate_up, w_down, ln2_g, ln2_b, loss_target, m_w_in, m_pool_w, m_pool_scale, m_sgu_ln_g, m_sgu_ln_b, m_sgu_w, m_sgu_b, m_w_out, m_ln1_g, m_ln1_b, m_w_gate_up, m_w_down, m_ln2_g, m_ln2_b, v_w_in, v_pool_w, v_pool_scale, v_sgu_ln_g, v_sgu_ln_b, v_sgu_w, v_sgu_b, v_w_out, v_ln1_g, v_ln1_b, v_w_gate_up, v_w_down, v_ln2_g, v_ln2_b):
    given = dict(locals())
    batch, seq, _ = x.shape
    tokens = batch * seq
    tile = min(TOKEN_TILE, seq)
    ffn_bwd_tile = min(FFN_BWD_TILE, seq)
    wtile = min(WGRAD_TILE, tokens)
    shapes = {name: given[name].shape for name in WEIGHT_ORDER}

    x2 = x.reshape(tokens, D_MODEL)
    target = loss_target.reshape(tokens, D_MODEL)
    small = (pool_w[0], pool_scale[0][None], sgu_ln_g[0][None], sgu_ln_b[0][None], sgu_w[0],
             jnp.broadcast_to(sgu_b[0][:, :, None], (N_GROUPS, GROUP, GROUP)))
    g1, b1, g2, b2 = ln1_g[0][None], ln1_b[0][None], ln2_g[0][None], ln2_b[0][None]
    shard_b = {name: given[name][0].astype(bf16) for name in BIG}
    core = lax.axis_index("c").astype(jnp.int32).reshape(1)
    chip = (2 * lax.axis_index("x") + lax.axis_index("y")).astype(jnp.int32).reshape(1)

    def gather(name):
        return [_GatherJob({name: shard_b[name]})]

    def halves_summed(name, grad, got):
        return _add_halves(name, grad, got, core)

    def chips_summed(name, sums, got):
        return _add_chips(name, sums[0], got, chip)

    w_in_b = _alone(gather("w_in")[0], "gather_w_in")["w_in"]
    (proj,), (got,) = _fwd_proj(x2, w_in_b, tile, gather("w_out"))
    w_out_b = got["w_out"]
    (mix_b, r1, h_b), (got,) = _fwd_mix(proj, x2, small, w_out_b, g1, b1, tile, seq, gather("w_gate_up"))
    w_gu_b = got["w_gate_up"]
    (gu_b, a_b), (got,) = _fwd_gate_up(h_b, w_gu_b, tile, gather("w_down"))
    w_dn_b = got["w_down"]
    dr2, stats2 = _fwd_down_loss(a_b, w_dn_b, r1, target, g1, b1, g2, b2, tile)

    reduced = {}
    g_down, _ = _wgrad(a_b, dr2, D_MODEL // 2, wtile, "wgrad_down")
    (dgu_b,), (got,) = _bwd_gate_up(dr2, gu_b, w_dn_b, ffn_bwd_tile, [_SwapHalvesJob({"w_down": g_down})])
    sums_down = halves_summed("w_down", g_down, got["w_down"])
    g_gu, (got,) = _wgrad(h_b, dgu_b, 2 * D_FF // 4, wtile, "wgrad_gate_up", [_SwapChipsJob({"w_down": sums_down[1]})])
    reduced["w_down"] = chips_summed("w_down", sums_down, got["w_down"])
    (dr1, stats1), (got,) = _bwd_ffn_in(dgu_b, w_gu_b, dr2, r1, g1, b1, tile, [_SwapHalvesJob({"w_gate_up": g_gu})])
    sums_gu = halves_summed("w_gate_up", g_gu, got["w_gate_up"])
    g_out, _ = _wgrad(mix_b, dr1, D_MODEL, wtile, "wgrad_out")
    (dpp, g_pool_w, g_sgu_w, g_sgu_b, vecs), (got_gu, got_out) = _bwd_mix(
        dr1, proj, small, w_out_b, tile, seq,
        [_SwapChipsJob({"w_gate_up": sums_gu[1]}), _SwapHalvesJob({"w_out": g_out})])
    reduced["w_gate_up"] = chips_summed("w_gate_up", sums_gu, got_gu["w_gate_up"])
    sums_out = halves_summed("w_out", g_out, got_out["w_out"])
    grad_x, dproj_b = _bwd_in(dpp, dr1, w_in_b, tile, seq)
    g_in, (got,) = _wgrad(x2, dproj_b, IN_COLS // 2, wtile, "wgrad_in", [_SwapChipsJob({"w_out": sums_out[1]})])
    reduced["w_out"] = chips_summed("w_out", sums_out, got["w_out"])

    last = {
        "w_in": g_in,
        "pool_w": g_pool_w.reshape(SQUARE_ROWS, GROUP),
        "sgu_w": g_sgu_w.reshape(SQUARE_ROWS, GROUP),
        "vecs": _pack_vecs({"pool_scale": vecs[0], "sgu_ln_g": vecs[1], "sgu_ln_b": vecs[2], "sgu_b": g_sgu_b[:, :, 0],
                            "ln1_g": stats1[0], "ln1_b": stats1[1], "ln2_g": stats2[0], "ln2_b": stats2[1]},
                           extra=stats2[2]),
    }
    got = _alone(_SwapHalvesJob(last), "reduce_swap_halves")
    sums = {name: halves_summed(name, last[name], got[name]) for name in last}
    got = _alone(_SwapChipsJob({name: sums[name][1] for name in last}), "reduce_swap_chips")
    reduced.update({name: chips_summed(name, sums[name], got[name]) for name in last})
    shared = _alone(_ShareJob({name: reduced[name] for name in BIG + SMALL}), "reduce_share")

    grad, delta, new_m, new_v = {}, {}, {}, {}
    for name in BIG:
        grad[name] = shared[name][None]
        d, nm, nv = _adamw(name, given[name][0], shared[name], given["m_" + name][0], given["v_" + name][0])
        delta[name], new_m[name], new_v[name] = d[None], nm[None], nv[None]
    for name in ("pool_w", "sgu_w"):
        flat = [given[pre + name].reshape(SQUARE_ROWS, GROUP) for pre in ("", "m_", "v_")]
        d, nm, nv = _adamw(name, flat[0], shared[name], flat[1], flat[2])
        for out, rows in ((grad, shared[name]), (delta, d), (new_m, nm), (new_v, nv)):
            out[name] = rows.reshape(shapes[name])
    packed = [_pack_vecs({name: given[pre + name] for name in VEC_NAMES}) for pre in ("", "m_", "v_")]
    d, nm, nv = _adamw("vecs", packed[0], shared["vecs"], packed[1], packed[2])
    for out, rows in ((grad, shared["vecs"]), (delta, d), (new_m, nm), (new_v, nv)):
        out.update(_unpack_vecs(rows, shapes)[0])

    sq_err = _unpack_vecs(shared["vecs"], shapes)[1][:LOSS_ROWS]
    loss = jnp.sum(sq_err) * (0.5 / D_MODEL)
    return (loss, grad_x.reshape(x.shape), *[grad[name] for name in WEIGHT_ORDER],
            *[delta[name] for name in WEIGHT_ORDER], *[new_m[name] for name in WEIGHT_ORDER],
            *[new_v[name] for name in WEIGHT_ORDER])
```

```python
import math

import jax
import jax.numpy as jnp
from jax import lax
from jax.experimental import pallas as pl
from jax.experimental.pallas import tpu as pltpu

f32 = jnp.float32
bf16 = jnp.bfloat16
MESH = pl.DeviceIdType.MESH

D_MODEL = 1024
POOL_WIDTH = 512
SGU_WIDTH = 512
IN_COLS = POOL_WIDTH + 2 * SGU_WIDTH
D_FF = 2816
POOL_WINDOWS = (2, 4, 8, 16)
GROUP = 128
N_GROUPS = 4
HALO = 16
LN_EPS = 1e-5
ALPHA = float(2.0 ** 0.25)
N_CHIPS = 4

ADAM_LR = 0.001
ADAM_B1 = 0.9
ADAM_B2 = 0.999
ADAM_EPS = 1e-08
ADAM_WD = 0.01
ADAM_STEP = 10

TOKEN_TILE = 512
FFN_BWD_TILE = 256
FF_CHUNK = 256
ROW_SUB = 256
WGRAD_TILE = 1024
V7X_VMEM_LIMIT = 56 * 1024 * 1024

SQUARE_ROWS = N_GROUPS * GROUP
VEC_ROWS = 64
LOSS_ROWS = D_MODEL // GROUP


def _params(**kw):
    return pltpu.CompilerParams(vmem_limit_bytes=V7X_VMEM_LIMIT, **kw)


def _mm(a, b):
    return jnp.dot(a, b, preferred_element_type=f32)


def _mm_nt(a, b):
    return lax.dot_general(a, b, (((1,), (1,)), ((), ())), preferred_element_type=f32)


def _mm_tn(a, b):
    return lax.dot_general(a, b, (((0,), (0,)), ((), ())), preferred_element_type=f32)


def _ln_fwd(r, g, b):
    mu = jnp.mean(r, axis=-1, keepdims=True)
    xc = r - mu
    var = jnp.mean(xc * xc, axis=-1, keepdims=True)
    rstd = lax.rsqrt(var + LN_EPS)
    xhat = xc * rstd
    return xhat * g + b, xhat, rstd


def _ln_bwd(dout, xhat, rstd, g):
    dxhat = dout * g
    m1 = jnp.mean(dxhat, axis=-1, keepdims=True)
    m2 = jnp.mean(dxhat * xhat, axis=-1, keepdims=True)
    return rstd * (dxhat - m1 - xhat * m2)


def _col_sum(a):
    return jnp.sum(a, axis=0, keepdims=True)


def _gelu_parts(z):
    cdf = 0.5 * (1.0 + lax.erf(z * (1.0 / math.sqrt(2.0))))
    pdf = jnp.exp(-0.5 * z * z) * (1.0 / math.sqrt(2.0 * math.pi))
    return cdf, pdf


def _inv_counts(seq_tile, rows):
    pos = seq_tile * rows + lax.broadcasted_iota(jnp.int32, (rows, GROUP), 0) + 1
    return [1.0 / jnp.minimum(pos, w).astype(f32) for w in POOL_WINDOWS]


def _window_sums(e, back):
    n = e.shape[0]

    def shifted(a, s):
        return pltpu.roll(a, s if back else n - s, 0)

    s2 = e + shifted(e, 1)
    s4 = s2[:, GROUP:] + shifted(s2[:, GROUP:], 2)
    s8 = s4[:, GROUP:] + shifted(s4[:, GROUP:], 4)
    s16 = s8[:, GROUP:] + shifted(s8[:, GROUP:], 8)
    return [s2[:, :GROUP], s4[:, :GROUP], s8[:, :GROUP], s16]


def _pooled_groups(xp, halo, inv):
    sums = _window_sums(jnp.concatenate([halo, xp], axis=0), back=True)
    return [sums[g][HALO:] * inv[g] - xp[:, g * GROUP:(g + 1) * GROUP] for g in range(N_GROUPS)]


def _tril_mask():
    r = lax.broadcasted_iota(jnp.int32, (GROUP, GROUP), 0)
    c = lax.broadcasted_iota(jnp.int32, (GROUP, GROUP), 1)
    return (r >= c).astype(f32)


def _gs(g):
    return slice(g * GROUP, (g + 1) * GROUP)


class _Cut:
    def __init__(self, rows, cols, by_cols):
        self.rows, self.cols, self.by_cols = rows, cols, by_cols
        if by_cols:
            self.block_shape = (rows // 2, cols // N_CHIPS)
            self.shard_shape = (rows, cols // N_CHIPS)
        else:
            self.block_shape = (rows // (2 * N_CHIPS), cols)
            self.shard_shape = (rows // N_CHIPS, cols)

    def block(self, ref, chip, half):
        br, bc = self.block_shape
        if self.by_cols:
            return ref.at[pl.ds(pl.multiple_of(half * br, 16), br), pl.ds(pl.multiple_of(chip * bc, 128), bc)]
        return ref.at[pl.ds(pl.multiple_of((2 * chip + half) * br, 8), br), :]

    def shard(self, ref, chip):
        sr, sc = self.shard_shape
        if self.by_cols:
            return ref.at[:, pl.ds(pl.multiple_of(chip * sc, 128), sc)]
        return ref.at[pl.ds(pl.multiple_of(chip * sr, 16), sr), :]

    def half_of_shard(self, ref, half):
        br = self.block_shape[0]
        return ref.at[pl.ds(pl.multiple_of(half * br, 8), br), :]

    def block_index(self, chip, half):
        return (half, chip) if self.by_cols else (2 * chip + half, 0)


CUTS = {
    "w_in": _Cut(D_MODEL, IN_COLS, True),
    "w_out": _Cut(D_MODEL, D_MODEL, False),
    "w_gate_up": _Cut(D_MODEL, 2 * D_FF, True),
    "w_down": _Cut(D_FF, D_MODEL, False),
    "pool_w": _Cut(SQUARE_ROWS, GROUP, False),
    "sgu_w": _Cut(SQUARE_ROWS, GROUP, False),
    "vecs": _Cut(VEC_ROWS, GROUP, False),
}
BIG = ("w_in", "w_out", "w_gate_up", "w_down")
SMALL = ("pool_w", "sgu_w", "vecs")
ANY = pl.BlockSpec(memory_space=pl.ANY)


def _wire_dtype(name):
    return bf16 if name in BIG else f32


def _place():
    x, y, c = lax.axis_index("x"), lax.axis_index("y"), lax.axis_index("c")
    others = [(1 - x, y), (x, 1 - y), (1 - x, 1 - y)]
    return x, y, c, 2 * x + y, others, [2 * ox + oy for ox, oy in others]


def _remote(src, dst, send_sems, recv_sems, k, to):
    return pltpu.make_async_remote_copy(src_ref=src, dst_ref=dst, send_sem=send_sems.at[k], recv_sem=recv_sems.at[k],
                                        device_id=to, device_id_type=MESH)


class _GatherJob:
    def __init__(self, shards):
        self.names = tuple(shards)
        self.arrays = tuple(shards.values())
        n = len(self.names)
        self.out_shapes = [jax.ShapeDtypeStruct((CUTS[name].rows, CUTS[name].cols), bf16) for name in self.names]
        self.scratch_shapes = ([pltpu.SemaphoreType.DMA((6 * n,)), pltpu.SemaphoreType.DMA((6 * n,)),
                                pltpu.SemaphoreType.DMA((2 * n,))]
                               + [pltpu.VMEM(CUTS[name].shard_shape, bf16) for name in self.names])

    def bind(self, shard_refs, full_refs, scratch):
        self.shards, self.full = shard_refs, full_refs
        self.send_sems, self.recv_sems, self.local_sems = scratch[:3]
        self.stages = scratch[3:]
        return self

    def _sends(self):
        _, _, c, me, others, _ = _place()
        return [_remote(CUTS[name].half_of_shard(self.shards[w], c), CUTS[name].block(self.full[w], me, c),
                        self.send_sems, self.recv_sems, 3 * w + k, (*chip, c))
                for w, name in enumerate(self.names) for k, chip in enumerate(others)]

    def _relays(self, half):
        x, y, c, _, _, other_ids = _place()
        n = len(self.names)
        return [_remote(CUTS[name].block(self.full[w], other_ids[k], half),
                        CUTS[name].block(self.full[w], other_ids[k], half),
                        self.send_sems, self.recv_sems, 3 * n + 3 * w + k, (x, y, 1 - c))
                for w, name in enumerate(self.names) for k in range(3)]

    def _stores(self):
        me = _place()[3]
        return [pltpu.make_async_copy(self.stages[w], CUTS[name].shard(self.full[w], me), self.local_sems.at[2 * w + 1])
                for w, name in enumerate(self.names)]

    def start(self):
        loads = [pltpu.make_async_copy(self.shards[w], self.stages[w], self.local_sems.at[2 * w])
                 for w in range(len(self.names))]
        for cp in loads:
            cp.start()
        for cp in self._sends():
            cp.start()
        for load, store in zip(loads, self._stores()):
            load.wait()
            store.start()

    def finish(self):
        _, _, c, _, others, other_ids = _place()
        relays = self._relays(c)
        for w, name in enumerate(self.names):
            for k, chip in enumerate(others):
                landed = CUTS[name].block(self.full[w], other_ids[k], c)
                _remote(landed, landed, self.send_sems, self.recv_sems, 3 * w + k, (*chip, c)).wait_recv()
                relays[3 * w + k].start()
        for cp in self._relays(1 - c):
            cp.wait_recv()
        for cp in self._sends() + relays:
            cp.wait_send()
        for cp in self._stores():
            cp.wait()


class _SwapHalvesJob:
    def __init__(self, grads):
        self.names = tuple(grads)
        self.arrays = tuple(grads.values())
        n = len(self.names)
        self.out_shapes = [jax.ShapeDtypeStruct((N_CHIPS, *CUTS[name].block_shape), f32) for name in self.names]
        self.scratch_shapes = [pltpu.SemaphoreType.DMA((N_CHIPS * n,)), pltpu.SemaphoreType.DMA((N_CHIPS * n,))]

    def bind(self, g_refs, got_refs, scratch):
        self.g_refs, self.got_refs = g_refs, got_refs
        self.send_sems, self.recv_sems = scratch
        return self

    def _copies(self):
        x, y, c, _, _, _ = _place()
        return [_remote(CUTS[name].block(self.g_refs[a], j, 1 - c), self.got_refs[a].at[j], self.send_sems,
                        self.recv_sems, N_CHIPS * a + j, (x, y, 1 - c))
                for a, name in enumerate(self.names) for j in range(N_CHIPS)]

    def start(self):
        for cp in self._copies():
            cp.start()

    def finish(self):
        for cp in self._copies():
            cp.wait()


class _SwapChipsJob:
    def __init__(self, partials):
        self.names = tuple(partials)
        self.arrays = tuple(partials.values())
        n = len(self.names)
        self.out_shapes = [jax.ShapeDtypeStruct((3, *CUTS[name].block_shape), _wire_dtype(name)) for name in self.names]
        self.scratch_shapes = [pltpu.SemaphoreType.DMA((3 * n,)), pltpu.SemaphoreType.DMA((3 * n,))]

    def bind(self, p_refs, got_refs, scratch):
        self.p_refs, self.got_refs = p_refs, got_refs
        self.send_sems, self.recv_sems = scratch
        return self

    def _copies(self):
        _, _, c, _, others, other_ids = _place()
        return [_remote(self.p_refs[a].at[other_ids[k]], self.got_refs[a].at[k], self.send_sems, self.recv_sems,
                        3 * a + k, (*chip, c))
                for a in range(len(self.names)) for k, chip in enumerate(others)]

    def start(self):
        for cp in self._copies():
            cp.start()

    def finish(self):
        for cp in self._copies():
            cp.wait()


class _ShareJob:
    def __init__(self, reduced):
        self.names = tuple(reduced)
        self.arrays = tuple(reduced.values())
        self.big = [a for a, name in enumerate(self.names) if name in BIG]
        self.small = [a for a, name in enumerate(self.names) if name in SMALL]
        self.out_shapes = [jax.ShapeDtypeStruct(CUTS[name].shard_shape if name in BIG
                                                else (CUTS[name].rows, CUTS[name].cols), f32) for name in self.names]
        n_sems = len(self.big) + 7 * len(self.small)
        self.scratch_shapes = ([pltpu.SemaphoreType.DMA((n_sems,)), pltpu.SemaphoreType.DMA((n_sems,)),
                                pltpu.SemaphoreType.DMA((2 * len(self.names),))]
                               + [pltpu.VMEM(CUTS[name].block_shape, f32) for name in self.names])

    def bind(self, f_refs, out_refs, scratch):
        self.f_refs, self.out_refs = f_refs, out_refs
        self.send_sems, self.recv_sems, self.local_sems = scratch[:3]
        self.stages = scratch[3:]
        return self

    def _sem(self, a, which=0):
        if a in self.big:
            return self.big.index(a)
        return len(self.big) + 7 * self.small.index(a) + which

    def _mine(self, a, half):
        me = _place()[3]
        cut = CUTS[self.names[a]]
        return cut.half_of_shard(self.out_refs[a], half) if a in self.big else cut.block(self.out_refs[a], me, half)

    def _to_sibling(self):
        x, y, c, _, _, _ = _place()
        return [_remote(self.f_refs[a], self._mine(a, c), self.send_sems, self.recv_sems, self._sem(a), (x, y, 1 - c))
                for a in range(len(self.names))]

    def _to_chips(self):
        _, _, c, _, others, _ = _place()
        return [_remote(self.f_refs[a], self._mine(a, c), self.send_sems, self.recv_sems, self._sem(a, 1 + k), (*chip, c))
                for a in self.small for k, chip in enumerate(others)]

    def _passes(self, half):
        x, y, c, _, _, other_ids = _place()
        out = []
        for a in self.small:
            for k in range(3):
                blk = CUTS[self.names[a]].block(self.out_refs[a], other_ids[k], half)
                out.append(_remote(blk, blk, self.send_sems, self.recv_sems, self._sem(a, 4 + k), (x, y, 1 - c)))
        return out

    def _stores(self):
        c = _place()[2]
        return [pltpu.make_async_copy(self.stages[a], self._mine(a, c), self.local_sems.at[2 * a + 1])
                for a in range(len(self.names))]

    def start(self):
        loads = [pltpu.make_async_copy(self.f_refs[a], self.stages[a], self.local_sems.at[2 * a])
                 for a in range(len(self.names))]
        for cp in loads:
            cp.start()
        for cp in self._to_sibling() + self._to_chips():
            cp.start()
        for load, store in zip(loads, self._stores()):
            load.wait()
            store.start()

    def finish(self):
        x, y, c, _, others, other_ids = _place()
        passes = self._passes(c)
        for s, a in enumerate(self.small):
            for k, chip in enumerate(others):
                landed = CUTS[self.names[a]].block(self.out_refs[a], other_ids[k], c)
                _remote(landed, landed, self.send_sems, self.recv_sems, self._sem(a, 1 + k), (*chip, c)).wait_recv()
                passes[3 * s + k].start()
        for a in range(len(self.names)):
            theirs = self._mine(a, 1 - c)
            _remote(theirs, theirs, self.send_sems, self.recv_sems, self._sem(a), (x, y, 1 - c)).wait_recv()
        for cp in self._passes(1 - c):
            cp.wait_recv()
        for cp in self._to_sibling() + self._to_chips() + passes:
            cp.wait_send()
        for cp in self._stores():
            cp.wait()


def _call(body, *, name, grid, in_specs, out_specs, out_shape, args, scratch_shapes=(), jobs=()):
    n_in, n_out, n_scr = len(in_specs), len(out_specs), len(scratch_shapes)
    j_in = [len(j.arrays) for j in jobs]
    j_out = [len(j.out_shapes) for j in jobs]
    j_scr = [len(j.scratch_shapes) for j in jobs]

    def wrapped(*refs):
        refs = list(refs)

        def take(k):
            head = refs[:k]
            del refs[:k]
            return head

        ins, jins = take(n_in), [take(k) for k in j_in]
        outs, jouts = take(n_out), [take(k) for k in j_out]
        scr, jscr = take(n_scr), [take(k) for k in j_scr]
        bound = [j.bind(a, b, c) for j, a, b, c in zip(jobs, jins, jouts, jscr)]
        if not grid:
            for b in bound:
                b.start()
            body(*ins, *outs, *scr)
            for b in bound:
                b.finish()
            return
        if not bound:
            body(*ins, *outs, *scr)
            return
        first = _all([pl.program_id(d) == 0 for d in range(len(grid))])
        last = _all([pl.program_id(d) == grid[d] - 1 for d in range(len(grid))])

        @pl.when(first)
        def _():
            for b in bound:
                b.start()

        body(*ins, *outs, *scr)

        @pl.when(last)
        def _():
            for b in bound:
                b.finish()

    kw = dict(grid=grid) if grid else {}
    results = pl.pallas_call(
        wrapped, name=name,
        in_specs=list(in_specs) + [ANY] * sum(j_in), out_specs=list(out_specs) + [ANY] * sum(j_out),
        out_shape=list(out_shape) + [s for j in jobs for s in j.out_shapes],
        scratch_shapes=list(scratch_shapes) + [s for j in jobs for s in j.scratch_shapes],
        compiler_params=_params(), **kw,
    )(*args, *[a for j in jobs for a in j.arrays])
    results = list(results)
    own, rest = results[:n_out], results[n_out:]
    per_job = []
    for j, k in zip(jobs, j_out):
        per_job.append(dict(zip(j.names, rest[:k])))
        rest = rest[k:]
    return own, per_job


def _all(conds):
    out = conds[0]
    for c in conds[1:]:
        out = jnp.logical_and(out, c)
    return out


def _alone(job, name):
    return _call(lambda: None, name=name, grid=None, in_specs=[], out_specs=[], out_shape=[], args=[], jobs=[job])[1][0]


def _fwd_proj(x2, w_in_b, tile, jobs):
    tokens = x2.shape[0]

    def body(x_ref, w_ref, o_ref, xb_ref):
        xb_ref[...] = x_ref[...].astype(bf16)
        o_ref[...] = _mm(xb_ref[...], w_ref[...])

    return _call(
        body, name="fwd_proj", grid=(tokens // tile,),
        in_specs=[pl.BlockSpec((tile, D_MODEL), lambda i: (i, 0)),
                  pl.BlockSpec((D_MODEL, IN_COLS), lambda i: (0, 0))],
        out_specs=[pl.BlockSpec((tile, IN_COLS), lambda i: (i, 0)), pl.BlockSpec((tile, D_MODEL), lambda i: (i, 0))],
        out_shape=[jax.ShapeDtypeStruct((tokens, IN_COLS), f32), jax.ShapeDtypeStruct((tokens, D_MODEL), bf16)],
        args=(x2, w_in_b), jobs=jobs)


def _small_specs():
    return [pl.BlockSpec((N_GROUPS, GROUP, GROUP), lambda i: (0, 0, 0)),
            pl.BlockSpec((1, POOL_WIDTH), lambda i: (0, 0)),
            pl.BlockSpec((1, SGU_WIDTH), lambda i: (0, 0)),
            pl.BlockSpec((1, SGU_WIDTH), lambda i: (0, 0)),
            pl.BlockSpec((N_GROUPS, GROUP, GROUP), lambda i: (0, 0, 0)),
            pl.BlockSpec((N_GROUPS, GROUP, GROUP), lambda i: (0, 0, 0))]


def _fwd_mix(proj, x2, small, w_out_b, ln1_g, ln1_b, tile, seq, jobs):
    tokens = x2.shape[0]
    tps = seq // tile
    hb = tile // HALO

    def body(proj_ref, halo_ref, x_ref, pw_ref, ps_ref, lg_ref, lb_ref, sw_ref, sb_ref, wout_ref, g1_ref, b1_ref,
             mix_ref, r1_ref, h_ref):
        seq_tile = pl.program_id(0) % tps
        xp = proj_ref[:, :POOL_WIDTH]
        halo = jnp.where(seq_tile == 0, 0.0, halo_ref[...])
        pooled = _pooled_groups(xp, halo, _inv_counts(seq_tile, tile))
        for g in range(N_GROUPS):
            po = _mm(pooled[g].astype(bf16), pw_ref[g].astype(bf16)) * ps_ref[:, _gs(g)]
            mix_ref[:, _gs(g)] = po.astype(bf16)

        cdf, _ = _gelu_parts(proj_ref[:, POOL_WIDTH:])
        zg = proj_ref[:, POOL_WIDTH:] * cdf
        u = zg[:, :SGU_WIDTH]
        vln, _, _ = _ln_fwd(zg[:, SGU_WIDTH:], lg_ref[...], lb_ref[...])
        vb = vln.astype(bf16)
        mask = _tril_mask()
        for h in range(N_GROUPS):
            wm = (sw_ref[h] * mask).astype(bf16)
            bias = sb_ref[h]
            for c in range(tile // GROUP):
                rows = slice(c * GROUP, (c + 1) * GROUP)
                mixed = _mm(wm, vb[rows, _gs(h)]) + bias
                mix_ref[rows, POOL_WIDTH + h * GROUP:POOL_WIDTH + (h + 1) * GROUP] = (u[rows, _gs(h)] * mixed).astype(bf16)

        r1 = ALPHA * x_ref[...] + _mm(mix_ref[...], wout_ref[...])
        r1_ref[...] = r1
        h1, _, _ = _ln_fwd(r1, g1_ref[...], b1_ref[...])
        h_ref[...] = h1.astype(bf16)

    row = lambda i: (i, 0)
    vec = pl.BlockSpec((1, D_MODEL), lambda i: (0, 0))
    return _call(
        body, name="fwd_mix", grid=(tokens // tile,),
        in_specs=[pl.BlockSpec((tile, IN_COLS), row),
                  pl.BlockSpec((HALO, POOL_WIDTH), lambda i: (jnp.maximum(i * hb - 1, 0), 0)),
                  pl.BlockSpec((tile, D_MODEL), row)] + _small_specs()
                 + [pl.BlockSpec((D_MODEL, D_MODEL), lambda i: (0, 0)), vec, vec],
        out_specs=[pl.BlockSpec((tile, D_MODEL), row)] * 3,
        out_shape=[jax.ShapeDtypeStruct((tokens, D_MODEL), bf16),
                   jax.ShapeDtypeStruct((tokens, D_MODEL), f32),
                   jax.ShapeDtypeStruct((tokens, D_MODEL), bf16)],
        args=(proj, proj, x2, *small, w_out_b, ln1_g, ln1_b), jobs=jobs)


def _fwd_gate_up(h_b, w_gu_b, tile, jobs):
    tokens = h_b.shape[0]

    def body(h_ref, w_ref, gu_ref, a_ref):
        hb = h_ref[...]
        for c in range(D_FF // FF_CHUNK):
            gcols = slice(c * FF_CHUNK, (c + 1) * FF_CHUNK)
            ucols = slice(D_FF + c * FF_CHUNK, D_FF + (c + 1) * FF_CHUNK)
            gate = _mm(hb, w_ref[:, gcols])
            up = _mm(hb, w_ref[:, ucols])
            gu_ref[:, gcols] = gate.astype(bf16)
            gu_ref[:, ucols] = up.astype(bf16)
            a_ref[:, gcols] = (gate * jax.nn.sigmoid(gate) * up).astype(bf16)

    return _call(
        body, name="fwd_gate_up", grid=(tokens // tile,),
        in_specs=[pl.BlockSpec((tile, D_MODEL), lambda i: (i, 0)),
                  pl.BlockSpec((D_MODEL, 2 * D_FF), lambda i: (0, 0), pipeline_mode=pl.Buffered(1))],
        out_specs=[pl.BlockSpec((tile, 2 * D_FF), lambda i: (i, 0)),
                   pl.BlockSpec((tile, D_FF), lambda i: (i, 0))],
        out_shape=[jax.ShapeDtypeStruct((tokens, 2 * D_FF), bf16),
                   jax.ShapeDtypeStruct((tokens, D_FF), bf16)],
        args=(h_b, w_gu_b), jobs=jobs)


def _fwd_down_loss(a_b, w_dn_b, r1, target, ln1_g, ln1_b, ln2_g, ln2_b, tile):
    tokens = a_b.shape[0]

    def body(a_ref, w_ref, r1_ref, t_ref, g1_ref, b1_ref, g2_ref, b2_ref, dr2_ref, dr2b_ref, st_ref):
        @pl.when(pl.program_id(0) == 0)
        def _():
            st_ref[...] = jnp.zeros_like(st_ref)

        sub = min(ROW_SUB, tile)
        for s in range(tile // sub):
            rows = slice(s * sub, (s + 1) * sub)
            h1, _, _ = _ln_fwd(r1_ref[rows, :], g1_ref[...], b1_ref[...])
            r2 = ALPHA * h1 + _mm(a_ref[rows, :], w_ref[...])
            y, xhat, rstd = _ln_fwd(r2, g2_ref[...], b2_ref[...])
            diff = y - t_ref[rows, :]
            dy = diff * (1.0 / D_MODEL)
            st_ref[0:1, :] += _col_sum(dy * xhat)
            st_ref[1:2, :] += _col_sum(dy)
            st_ref[2:3, :] += _col_sum(diff * diff)
            dr2 = _ln_bwd(dy, xhat, rstd, g2_ref[...])
            dr2_ref[rows, :] = dr2
            dr2b_ref[rows, :] = dr2.astype(bf16)

    row = lambda i: (i, 0)
    vec = pl.BlockSpec((1, D_MODEL), lambda i: (0, 0))
    return _call(
        body, name="fwd_down_loss", grid=(tokens // tile,),
        in_specs=[pl.BlockSpec((tile, D_FF), row), pl.BlockSpec((D_FF, D_MODEL), lambda i: (0, 0)),
                  pl.BlockSpec((tile, D_MODEL), row), pl.BlockSpec((tile, D_MODEL), row), vec, vec, vec, vec],
        out_specs=[pl.BlockSpec((tile, D_MODEL), row), pl.BlockSpec((tile, D_MODEL), row),
                   pl.BlockSpec((8, D_MODEL), lambda i: (0, 0))],
        out_shape=[jax.ShapeDtypeStruct((tokens, D_MODEL), f32), jax.ShapeDtypeStruct((tokens, D_MODEL), bf16),
                   jax.ShapeDtypeStruct((8, D_MODEL), f32)],
        args=(a_b, w_dn_b, r1, target, ln1_g, ln1_b, ln2_g, ln2_b))[0]


def _bwd_gate_up(dr2, gu_b, w_dn_b, tile, jobs):
    tokens = dr2.shape[0]

    def body(d_ref, gu_ref, w_ref, dgu_ref):
        d = d_ref[...].astype(bf16)
        for c in range(D_FF // FF_CHUNK):
            gcols = slice(c * FF_CHUNK, (c + 1) * FF_CHUNK)
            ucols = slice(D_FF + c * FF_CHUNK, D_FF + (c + 1) * FF_CHUNK)
            da = _mm_nt(d, w_ref[gcols, :])
            gate = gu_ref[:, gcols].astype(f32)
            up = gu_ref[:, ucols].astype(f32)
            sg = jax.nn.sigmoid(gate)
            silu = gate * sg
            dgu_ref[:, gcols] = (da * up * (sg + silu * (1.0 - sg))).astype(bf16)
            dgu_ref[:, ucols] = (da * silu).astype(bf16)

    return _call(
        body, name="bwd_gate_up", grid=(tokens // tile,),
        in_specs=[pl.BlockSpec((tile, D_MODEL), lambda i: (i, 0)),
                  pl.BlockSpec((tile, 2 * D_FF), lambda i: (i, 0)),
                  pl.BlockSpec((D_FF, D_MODEL), lambda i: (0, 0))],
        out_specs=[pl.BlockSpec((tile, 2 * D_FF), lambda i: (i, 0))],
        out_shape=[jax.ShapeDtypeStruct((tokens, 2 * D_FF), bf16)],
        args=(dr2, gu_b, w_dn_b), jobs=jobs)


def _bwd_ffn_in(dgu_b, w_gu_b, dr2, r1, ln1_g, ln1_b, tile, jobs):
    tokens = dr2.shape[0]

    def body(dgu_ref, w_ref, d_ref, r1_ref, g1_ref, b1_ref, dr1_ref, dr1b_ref, st_ref):
        @pl.when(pl.program_id(0) == 0)
        def _():
            st_ref[...] = jnp.zeros_like(st_ref)

        dh = ALPHA * d_ref[...] + _mm_nt(dgu_ref[...], w_ref[...])
        _, xhat, rstd = _ln_fwd(r1_ref[...], g1_ref[...], b1_ref[...])
        st_ref[0:1, :] += _col_sum(dh * xhat)
        st_ref[1:2, :] += _col_sum(dh)
        dr1 = _ln_bwd(dh, xhat, rstd, g1_ref[...])
        dr1_ref[...] = dr1
        dr1b_ref[...] = dr1.astype(bf16)

    row = lambda i: (i, 0)
    vec = pl.BlockSpec((1, D_MODEL), lambda i: (0, 0))
    return _call(
        body, name="bwd_ffn_in", grid=(tokens // tile,),
        in_specs=[pl.BlockSpec((tile, 2 * D_FF), row),
                  pl.BlockSpec((D_MODEL, 2 * D_FF), lambda i: (0, 0), pipeline_mode=pl.Buffered(1)),
                  pl.BlockSpec((tile, D_MODEL), row), pl.BlockSpec((tile, D_MODEL), row), vec, vec],
        out_specs=[pl.BlockSpec((tile, D_MODEL), row), pl.BlockSpec((tile, D_MODEL), row),
                   pl.BlockSpec((8, D_MODEL), lambda i: (0, 0))],
        out_shape=[jax.ShapeDtypeStruct((tokens, D_MODEL), f32), jax.ShapeDtypeStruct((tokens, D_MODEL), bf16),
                   jax.ShapeDtypeStruct((8, D_MODEL), f32)],
        args=(dgu_b, w_gu_b, dr2, r1, ln1_g, ln1_b), jobs=jobs)


def _bwd_mix(dr1, proj, small, w_out_b, tile, seq, jobs):
    tokens = dr1.shape[0]
    tps = seq // tile
    hb = tile // HALO
    steps = tokens // tile

    def body(dr1_ref, proj_ref, halo_ref, wout_ref, pw_ref, ps_ref, lg_ref, lb_ref, sw_ref, sb_ref,
             dpool_ref, dpre_ref, gpw_ref, gsw_ref, gsb_ref, vec_ref, du_ref, dvln_ref):
        step = pl.program_id(0)
        seq_tile = step % tps

        @pl.when(step == 0)
        def _():
            gpw_ref[...] = jnp.zeros_like(gpw_ref)
            gsw_ref[...] = jnp.zeros_like(gsw_ref)
            gsb_ref[...] = jnp.zeros_like(gsb_ref)
            vec_ref[...] = jnp.zeros_like(vec_ref)

        dmix = _mm_nt(dr1_ref[...].astype(bf16), wout_ref[...])

        xp = proj_ref[:, :POOL_WIDTH]
        halo = jnp.where(seq_tile == 0, 0.0, halo_ref[...])
        pooled = _pooled_groups(xp, halo, _inv_counts(seq_tile, tile))
        for g in range(N_GROUPS):
            pb = pooled[g].astype(bf16)
            pwb = pw_ref[g].astype(bf16)
            dpo = dmix[:, _gs(g)]
            vec_ref[0:1, _gs(g)] += _col_sum(dpo * _mm(pb, pwb))
            dpo_b = (dpo * ps_ref[:, _gs(g)]).astype(bf16)
            gpw_ref[g] += _mm_tn(pb, dpo_b)
            dpool_ref[:, _gs(g)] = _mm_nt(dpo_b, pwb)

        pre = proj_ref[:, POOL_WIDTH:]
        cdf, pdf = _gelu_parts(pre)
        zg = pre * cdf
        u = zg[:, :SGU_WIDTH]
        vln, vhat, rstd = _ln_fwd(zg[:, SGU_WIDTH:], lg_ref[...], lb_ref[...])
        vb = vln.astype(bf16)
        mask = _tril_mask()
        for h in range(N_GROUPS):
            wm = (sw_ref[h] * mask).astype(bf16)
            bias = sb_ref[h]
            gsw = jnp.zeros((GROUP, GROUP), f32)
            gsb = jnp.zeros((GROUP, GROUP), f32)
            for c in range(tile // GROUP):
                rows = slice(c * GROUP, (c + 1) * GROUP)
                v_ch = vb[rows, _gs(h)]
                d = dmix[rows, POOL_WIDTH + h * GROUP:POOL_WIDTH + (h + 1) * GROUP]
                du_ref[rows, _gs(h)] = d * (_mm(wm, v_ch) + bias)
                dmixed = d * u[rows, _gs(h)]
                gsb += dmixed
                dmixed_b = dmixed.astype(bf16)
                gsw += _mm_nt(dmixed_b, v_ch)
                dvln_ref[rows, _gs(h)] = _mm_tn(wm, dmixed_b)
            gsw_ref[h] += gsw * mask
            gsb_ref[h] += gsb

        dvln = dvln_ref[...]
        vec_ref[1:2, :] += _col_sum(dvln * vhat)
        vec_ref[2:3, :] += _col_sum(dvln)
        dgelu = cdf + pre * pdf
        dpre_ref[:, :SGU_WIDTH] = (du_ref[...] * dgelu[:, :SGU_WIDTH]).astype(bf16)
        dpre_ref[:, SGU_WIDTH:] = (_ln_bwd(dvln, vhat, rstd, lg_ref[...]) * dgelu[:, SGU_WIDTH:]).astype(bf16)

        @pl.when(step == steps - 1)
        def _():
            for h in range(N_GROUPS):
                gsb_ref[h] = jnp.broadcast_to(jnp.sum(gsb_ref[h], axis=1, keepdims=True), (GROUP, GROUP))

    row = lambda i: (i, 0)
    sq = jax.ShapeDtypeStruct((N_GROUPS, GROUP, GROUP), f32)
    sq_spec = pl.BlockSpec((N_GROUPS, GROUP, GROUP), lambda i: (0, 0, 0))
    return _call(
        body, name="bwd_mix", grid=(steps,),
        in_specs=[pl.BlockSpec((tile, D_MODEL), row), pl.BlockSpec((tile, IN_COLS), row),
                  pl.BlockSpec((HALO, POOL_WIDTH), lambda i: (jnp.maximum(i * hb - 1, 0), 0)),
                  pl.BlockSpec((D_MODEL, D_MODEL), lambda i: (0, 0))] + _small_specs(),
        out_specs=[pl.BlockSpec((tile, POOL_WIDTH), row), pl.BlockSpec((tile, 2 * SGU_WIDTH), row),
                   sq_spec, sq_spec, sq_spec, pl.BlockSpec((8, POOL_WIDTH), lambda i: (0, 0))],
        out_shape=[jax.ShapeDtypeStruct((tokens, POOL_WIDTH), f32), jax.ShapeDtypeStruct((tokens, 2 * SGU_WIDTH), bf16),
                   sq, sq, sq, jax.ShapeDtypeStruct((8, POOL_WIDTH), f32)],
        scratch_shapes=[pltpu.VMEM((tile, SGU_WIDTH), f32), pltpu.VMEM((tile, SGU_WIDTH), f32)],
        args=(dr1, proj, proj, w_out_b, *small), jobs=jobs)


def _bwd_in(dpool, dpre_b, dr1, w_in_b, tile, seq):
    tokens = dr1.shape[0]
    tps = seq // tile
    hb = tile // HALO
    last_halo = tokens // HALO - 1

    def body(dpool_ref, nxt_ref, dpre_ref, dr1_ref, w_ref, dx_ref, dproj_ref):
        seq_tile = pl.program_id(0) % tps
        inv = _inv_counts(seq_tile, tile)
        dpl = dpool_ref[...]
        nxt = jnp.where(seq_tile == tps - 1, 0.0, nxt_ref[...])
        scaled = jnp.concatenate([dpl[:, _gs(g)] * inv[g] for g in range(N_GROUPS)], axis=1)
        scaled_nxt = jnp.concatenate([nxt[:, _gs(g)] * (1.0 / POOL_WINDOWS[g]) for g in range(N_GROUPS)], axis=1)
        sums = _window_sums(jnp.concatenate([scaled, scaled_nxt], axis=0), back=False)
        for g in range(N_GROUPS):
            dproj_ref[:, _gs(g)] = (sums[g][:tile] - dpl[:, _gs(g)]).astype(bf16)
        dproj_ref[:, POOL_WIDTH:] = dpre_ref[...]
        dx_ref[...] = ALPHA * dr1_ref[...] + _mm_nt(dproj_ref[...], w_ref[...])

    row = lambda i: (i, 0)
    return _call(
        body, name="bwd_in", grid=(tokens // tile,),
        in_specs=[pl.BlockSpec((tile, POOL_WIDTH), row),
                  pl.BlockSpec((HALO, POOL_WIDTH), lambda i: (jnp.minimum((i + 1) * hb, last_halo), 0)),
                  pl.BlockSpec((tile, 2 * SGU_WIDTH), row),
                  pl.BlockSpec((tile, D_MODEL), row),
                  pl.BlockSpec((D_MODEL, IN_COLS), lambda i: (0, 0))],
        out_specs=[pl.BlockSpec((tile, D_MODEL), row), pl.BlockSpec((tile, IN_COLS), row)],
        out_shape=[jax.ShapeDtypeStruct((tokens, D_MODEL), f32), jax.ShapeDtypeStruct((tokens, IN_COLS), bf16)],
        args=(dpool, dpool, dpre_b, dr1, w_in_b))[0]


def _wgrad(a, b, col_tile, tile, name, jobs=()):
    tokens, m = a.shape
    n = b.shape[1]

    def body(a_ref, b_ref, o_ref):
        @pl.when(pl.program_id(1) == 0)
        def _():
            o_ref[...] = jnp.zeros_like(o_ref)

        o_ref[...] += _mm_tn(a_ref[...].astype(bf16), b_ref[...].astype(bf16))

    (out,), got = _call(
        body, name=name, grid=(n // col_tile, tokens // tile),
        in_specs=[pl.BlockSpec((tile, m), lambda j, k: (k, 0)),
                  pl.BlockSpec((tile, col_tile), lambda j, k: (k, j))],
        out_specs=[pl.BlockSpec((m, col_tile), lambda j, k: (0, j))],
        out_shape=[jax.ShapeDtypeStruct((m, n), f32)],
        args=(a, b), jobs=jobs)
    return out, got


def _add_halves(name, g, got, core):
    cut = CUTS[name]
    br, bc = cut.block_shape
    wire = _wire_dtype(name)

    def body(core_ref, g_ref, got_ref, o_ref, wire_ref):
        s = g_ref[...] + got_ref[...]
        o_ref[...] = s
        wire_ref[...] = s.astype(wire)

    blocks = pl.BlockSpec((None, br, bc), lambda j, core_ref: (j, 0, 0))
    return pl.pallas_call(
        body, name="reduce_add_halves_" + name,
        grid_spec=pltpu.PrefetchScalarGridSpec(
            num_scalar_prefetch=1, grid=(N_CHIPS,),
            in_specs=[pl.BlockSpec((br, bc), lambda j, core_ref: cut.block_index(j, core_ref[0])), blocks],
            out_specs=[blocks, blocks]),
        out_shape=[jax.ShapeDtypeStruct((N_CHIPS, br, bc), f32), jax.ShapeDtypeStruct((N_CHIPS, br, bc), wire)],
        compiler_params=_params(),
    )(core, g, got)


def _add_chips(name, partial, got, chip):
    br, bc = CUTS[name].block_shape
    rt = br // 2 if br % 32 == 0 else br

    def body(chip_ref, p_ref, got_ref, o_ref):
        o_ref[...] = ((p_ref[...] + got_ref[0].astype(f32)) + got_ref[1].astype(f32)) + got_ref[2].astype(f32)

    return pl.pallas_call(
        body, name="reduce_add_chips_" + name,
        grid_spec=pltpu.PrefetchScalarGridSpec(
            num_scalar_prefetch=1, grid=(br // rt,),
            in_specs=[pl.BlockSpec((None, rt, bc), lambda i, chip_ref: (chip_ref[0], i, 0)),
                      pl.BlockSpec((3, rt, bc), lambda i, chip_ref: (0, i, 0))],
            out_specs=pl.BlockSpec((rt, bc), lambda i, chip_ref: (i, 0))),
        out_shape=jax.ShapeDtypeStruct((br, bc), f32),
        compiler_params=_params(),
    )(chip, partial, got)


def _adamw(name, w, g, m, v):
    rows, cols = w.shape
    rt = rows // 4

    def body(w_ref, g_ref, m_ref, v_ref, d_ref, nm_ref, nv_ref):
        g = g_ref[...]
        nm = ADAM_B1 * m_ref[...] + (1.0 - ADAM_B1) * g
        nv = ADAM_B2 * v_ref[...] + (1.0 - ADAM_B2) * jnp.square(g)
        m_hat = nm / (1.0 - ADAM_B1 ** ADAM_STEP)
        v_hat = nv / (1.0 - ADAM_B2 ** ADAM_STEP)
        d_ref[...] = -ADAM_LR * (m_hat / (jnp.sqrt(v_hat) + ADAM_EPS) + ADAM_WD * w_ref[...])
        nm_ref[...] = nm
        nv_ref[...] = nv

    spec = pl.BlockSpec((rt, cols), lambda i: (i, 0))
    shape = jax.ShapeDtypeStruct((rows, cols), f32)
    return pl.pallas_call(
        body, name="adamw_" + name, grid=(rows // rt,),
        in_specs=[spec] * 4, out_specs=[spec] * 3, out_shape=[shape] * 3,
        compiler_params=_params(),
    )(w, g, m, v)


VEC_NAMES = ("pool_scale", "sgu_ln_g", "sgu_ln_b", "sgu_b", "ln1_g", "ln1_b", "ln2_g", "ln2_b")
WEIGHT_ORDER = ("w_in", "pool_w", "pool_scale", "sgu_ln_g", "sgu_ln_b", "sgu_w", "sgu_b", "w_out", "ln1_g", "ln1_b",
                "w_gate_up", "w_down", "ln2_g", "ln2_b")


def _pack_vecs(parts, extra=None):
    rows = [parts[name].reshape(-1, GROUP) for name in VEC_NAMES]
    if extra is not None:
        rows.append(extra.reshape(-1, GROUP))
    used = sum(r.shape[0] for r in rows)
    return jnp.concatenate(rows + [jnp.zeros((VEC_ROWS - used, GROUP), f32)], axis=0)


def _unpack_vecs(packed, shapes):
    out, at = {}, 0
    for name in VEC_NAMES:
        rows = math.prod(shapes[name]) // GROUP
        out[name] = packed[at:at + rows].reshape(shapes[name])
        at += rows
    return out, packed[at:]


def kernel(x, w_in, pool_w, pool_scale, sgu_ln_g, sgu_ln_b, sgu_w, sgu_b, w_out, ln1_g, ln1_b, w_gate_up, w_down, ln2_g, ln2_b, loss_target, m_w_in, m_pool_w, m_pool_scale, m_sgu_ln_g, m_sgu_ln_b, m_sgu_w, m_sgu_b, m_w_out, m_ln1_g, m_ln1_b, m_w_gate_up, m_w_down, m_ln2_g, m_ln2_b, v_w_in, v_pool_w, v_pool_scale, v_sgu_ln_g, v_sgu_ln_b, v_sgu_w, v_sgu_b, v_w_out, v_ln1_g, v_ln1_b, v_w_gate_up, v_w_down, v_ln2_g, v_ln2_b):
    given = dict(locals())
    batch, seq, _ = x.shape
    tokens = batch * seq
    tile = min(TOKEN_TILE, seq)
    ffn_bwd_tile = min(FFN_BWD_TILE, seq)
    wtile = min(WGRAD_TILE, tokens)
    shapes = {name: given[name].shape for name in WEIGHT_ORDER}

    x2 = x.reshape(tokens, D_MODEL)
    target = loss_target.reshape(tokens, D_MODEL)
    small = (pool_w[0], pool_scale[0][None], sgu_ln_g[0][None], sgu_ln_b[0][None], sgu_w[0],
             jnp.broadcast_to(sgu_b[0][:, :, None], (N_GROUPS, GROUP, GROUP)))
    g1, b1, g2, b2 = ln1_g[0][None], ln1_b[0][None], ln2_g[0][None], ln2_b[0][None]
    shard_b = {name: given[name][0].astype(bf16) for name in BIG}
    core = lax.axis_index("c").astype(jnp.int32).reshape(1)
    chip = (2 * lax.axis_index("x") + lax.axis_index("y")).astype(jnp.int32).reshape(1)

    def gather(name):
        return [_GatherJob({name: shard_b[name]})]

    def halves_summed(name, grad, got):
        return _add_halves(name, grad, got, core)

    def chips_summed(name, sums, got):
        return _add_chips(name, sums[0], got, chip)

    w_in_b = _alone(gather("w_in")[0], "gather_w_in")["w_in"]
    (proj, x_b), (got,) = _fwd_proj(x2, w_in_b, tile, gather("w_out"))
    w_out_b = got["w_out"]
    (mix_b, r1, h_b), (got,) = _fwd_mix(proj, x2, small, w_out_b, g1, b1, tile, seq, gather("w_gate_up"))
    w_gu_b = got["w_gate_up"]
    (gu_b, a_b), (got,) = _fwd_gate_up(h_b, w_gu_b, tile, gather("w_down"))
    w_dn_b = got["w_down"]
    dr2, dr2_b, stats2 = _fwd_down_loss(a_b, w_dn_b, r1, target, g1, b1, g2, b2, tile)

    reduced = {}
    g_down, _ = _wgrad(a_b, dr2_b, D_MODEL, wtile, "wgrad_down")
    (dgu_b,), (got,) = _bwd_gate_up(dr2_b, gu_b, w_dn_b, ffn_bwd_tile, [_SwapHalvesJob({"w_down": g_down})])
    sums_down = halves_summed("w_down", g_down, got["w_down"])
    g_gu, (got,) = _wgrad(h_b, dgu_b, D_FF, wtile, "wgrad_gate_up", [_SwapChipsJob({"w_down": sums_down[1]})])
    reduced["w_down"] = chips_summed("w_down", sums_down, got["w_down"])
    (dr1, dr1_b, stats1), (got,) = _bwd_ffn_in(dgu_b, w_gu_b, dr2, r1, g1, b1, tile,
                                               [_SwapHalvesJob({"w_gate_up": g_gu})])
    sums_gu = halves_summed("w_gate_up", g_gu, got["w_gate_up"])
    g_out, _ = _wgrad(mix_b, dr1_b, D_MODEL, wtile, "wgrad_out")
    (dpool, dpre_b, g_pool_w, g_sgu_w, g_sgu_b, vecs), (got_gu, got_out) = _bwd_mix(
        dr1_b, proj, small, w_out_b, tile, seq,
        [_SwapChipsJob({"w_gate_up": sums_gu[1]}), _SwapHalvesJob({"w_out": g_out})])
    reduced["w_gate_up"] = chips_summed("w_gate_up", sums_gu, got_gu["w_gate_up"])
    sums_out = halves_summed("w_out", g_out, got_out["w_out"])
    grad_x, dproj_b = _bwd_in(dpool, dpre_b, dr1, w_in_b, tile, seq)
    g_in, (got,) = _wgrad(x_b, dproj_b, IN_COLS, wtile, "wgrad_in", [_SwapChipsJob({"w_out": sums_out[1]})])
    reduced["w_out"] = chips_summed("w_out", sums_out, got["w_out"])

    last = {
        "w_in": g_in,
        "pool_w": g_pool_w.reshape(SQUARE_ROWS, GROUP),
        "sgu_w": g_sgu_w.reshape(SQUARE_ROWS, GROUP),
        "vecs": _pack_vecs({"pool_scale": vecs[0], "sgu_ln_g": vecs[1], "sgu_ln_b": vecs[2], "sgu_b": g_sgu_b[:, :, 0],
                            "ln1_g": stats1[0], "ln1_b": stats1[1], "ln2_g": stats2[0], "ln2_b": stats2[1]},
                           extra=stats2[2]),
    }
    got = _alone(_SwapHalvesJob(last), "reduce_swap_halves")
    sums = {name: halves_summed(name, last[name], got[name]) for name in last}
    got = _alone(_SwapChipsJob({name: sums[name][1] for name in last}), "reduce_swap_chips")
    reduced.update({name: chips_summed(name, sums[name], got[name]) for name in last})
    shared = _alone(_ShareJob({name: reduced[name] for name in BIG + SMALL}), "reduce_share")

    grad, delta, new_m, new_v = {}, {}, {}, {}
    for name in BIG:
        grad[name] = shared[name][None]
        d, nm, nv = _adamw(name, given[name][0], shared[name], given["m_" + name][0], given["v_" + name][0])
        delta[name], new_m[name], new_v[name] = d[None], nm[None], nv[None]
    for name in ("pool_w", "sgu_w"):
        flat = [given[pre + name].reshape(SQUARE_ROWS, GROUP) for pre in ("", "m_", "v_")]
        d, nm, nv = _adamw(name, flat[0], shared[name], flat[1], flat[2])
        for out, rows in ((grad, shared[name]), (delta, d), (new_m, nm), (new_v, nv)):
            out[name] = rows.reshape(shapes[name])
    packed = [_pack_vecs({name: given[pre + name] for name in VEC_NAMES}) for pre in ("", "m_", "v_")]
    d, nm, nv = _adamw("vecs", packed[0], shared["vecs"], packed[1], packed[2])
    for out, rows in ((grad, shared["vecs"]), (delta, d), (new_m, nm), (new_v, nv)):
        out.update(_unpack_vecs(rows, shapes)[0])

    sq_err = _unpack_vecs(shared["vecs"], shapes)[1][:LOSS_ROWS]
    loss = jnp.sum(sq_err) * (0.5 / D_MODEL)
    return (loss, grad_x.reshape(x.shape), *[grad[name] for name in WEIGHT_ORDER],
            *[delta[name] for name in WEIGHT_ORDER], *[new_m[name] for name in WEIGHT_ORDER],
            *[new_v[name] for name in WEIGHT_ORDER])
```

```python
import math

import jax
import jax.numpy as jnp
from jax import lax
from jax.experimental import pallas as pl
from jax.experimental.pallas import tpu as pltpu

f32 = jnp.float32
bf16 = jnp.bfloat16
MESH = pl.DeviceIdType.MESH

D_MODEL = 1024
POOL_WIDTH = 512
SGU_WIDTH = 512
IN_COLS = POOL_WIDTH + 2 * SGU_WIDTH
D_FF = 2816
POOL_WINDOWS = (2, 4, 8, 16)
GROUP = 128
N_GROUPS = 4
HALO = 16
LN_EPS = 1e-5
ALPHA = float(2.0 ** 0.25)
N_CHIPS = 4

ADAM_LR = 0.001
ADAM_B1 = 0.9
ADAM_B2 = 0.999
ADAM_EPS = 1e-08
ADAM_WD = 0.01
ADAM_STEP = 10

TOKEN_TILE = 512
FFN_BWD_TILE = 512
FF_CHUNK = 256
ROW_SUB = 256
WGRAD_TILE = 1024
V7X_VMEM_LIMIT = 56 * 1024 * 1024

SQUARE_ROWS = N_GROUPS * GROUP
VEC_ROWS = 64
LOSS_ROWS = D_MODEL // GROUP


def _params(**kw):
    return pltpu.CompilerParams(vmem_limit_bytes=V7X_VMEM_LIMIT, **kw)


def _mm(a, b):
    return jnp.dot(a, b, preferred_element_type=f32)


def _mm_nt(a, b):
    return lax.dot_general(a, b, (((1,), (1,)), ((), ())), preferred_element_type=f32)


def _mm_tn(a, b):
    return lax.dot_general(a, b, (((0,), (0,)), ((), ())), preferred_element_type=f32)


def _ln_fwd(r, g, b):
    mu = jnp.mean(r, axis=-1, keepdims=True)
    xc = r - mu
    var = jnp.mean(xc * xc, axis=-1, keepdims=True)
    rstd = lax.rsqrt(var + LN_EPS)
    xhat = xc * rstd
    return xhat * g + b, xhat, rstd


def _ln_bwd(dout, xhat, rstd, g):
    dxhat = dout * g
    m1 = jnp.mean(dxhat, axis=-1, keepdims=True)
    m2 = jnp.mean(dxhat * xhat, axis=-1, keepdims=True)
    return rstd * (dxhat - m1 - xhat * m2)


def _col_sum(a):
    return jnp.sum(a, axis=0, keepdims=True)


def _gelu_parts(z):
    cdf = 0.5 * (1.0 + lax.erf(z * (1.0 / math.sqrt(2.0))))
    pdf = jnp.exp(-0.5 * z * z) * (1.0 / math.sqrt(2.0 * math.pi))
    return cdf, pdf


def _inv_counts(seq_tile, rows):
    pos = seq_tile * rows + lax.broadcasted_iota(jnp.int32, (rows, GROUP), 0) + 1
    return [1.0 / jnp.minimum(pos, w).astype(f32) for w in POOL_WINDOWS]


def _window_sums(e, back):
    n = e.shape[0]

    def shifted(a, s):
        return pltpu.roll(a, s if back else n - s, 0)

    s2 = e + shifted(e, 1)
    s4 = s2[:, GROUP:] + shifted(s2[:, GROUP:], 2)
    s8 = s4[:, GROUP:] + shifted(s4[:, GROUP:], 4)
    s16 = s8[:, GROUP:] + shifted(s8[:, GROUP:], 8)
    return [s2[:, :GROUP], s4[:, :GROUP], s8[:, :GROUP], s16]


def _pooled_groups(xp, halo, inv):
    sums = _window_sums(jnp.concatenate([halo, xp], axis=0), back=True)
    return [sums[g][HALO:] * inv[g] - xp[:, g * GROUP:(g + 1) * GROUP] for g in range(N_GROUPS)]


def _tril_mask():
    r = lax.broadcasted_iota(jnp.int32, (GROUP, GROUP), 0)
    c = lax.broadcasted_iota(jnp.int32, (GROUP, GROUP), 1)
    return (r >= c).astype(f32)


def _gs(g):
    return slice(g * GROUP, (g + 1) * GROUP)


class _Cut:
    def __init__(self, rows, cols, by_cols):
        self.rows, self.cols, self.by_cols = rows, cols, by_cols
        if by_cols:
            self.block_shape = (rows // 2, cols // N_CHIPS)
            self.shard_shape = (rows, cols // N_CHIPS)
        else:
            self.block_shape = (rows // (2 * N_CHIPS), cols)
            self.shard_shape = (rows // N_CHIPS, cols)

    def block(self, ref, chip, half):
        br, bc = self.block_shape
        if self.by_cols:
            return ref.at[pl.ds(pl.multiple_of(half * br, 16), br), pl.ds(pl.multiple_of(chip * bc, 128), bc)]
        return ref.at[pl.ds(pl.multiple_of((2 * chip + half) * br, 8), br), :]

    def shard(self, ref, chip):
        sr, sc = self.shard_shape
        if self.by_cols:
            return ref.at[:, pl.ds(pl.multiple_of(chip * sc, 128), sc)]
        return ref.at[pl.ds(pl.multiple_of(chip * sr, 16), sr), :]

    def half_of_shard(self, ref, half):
        br = self.block_shape[0]
        return ref.at[pl.ds(pl.multiple_of(half * br, 8), br), :]

    def block_index(self, chip, half):
        return (half, chip) if self.by_cols else (2 * chip + half, 0)


CUTS = {
    "w_in": _Cut(D_MODEL, IN_COLS, True),
    "w_out": _Cut(D_MODEL, D_MODEL, False),
    "w_gate_up": _Cut(D_MODEL, 2 * D_FF, True),
    "w_down": _Cut(D_FF, D_MODEL, False),
    "pool_w": _Cut(SQUARE_ROWS, GROUP, False),
    "sgu_w": _Cut(SQUARE_ROWS, GROUP, False),
    "vecs": _Cut(VEC_ROWS, GROUP, False),
}
BIG = ("w_in", "w_out", "w_gate_up", "w_down")
SMALL = ("pool_w", "sgu_w", "vecs")
ANY = pl.BlockSpec(memory_space=pl.ANY)


def _wire_dtype(name):
    return bf16 if name in BIG else f32


def _place():
    x, y, c = lax.axis_index("x"), lax.axis_index("y"), lax.axis_index("c")
    others = [(1 - x, y), (x, 1 - y), (1 - x, 1 - y)]
    return x, y, c, 2 * x + y, others, [2 * ox + oy for ox, oy in others]


def _remote(src, dst, send_sems, recv_sems, k, to):
    return pltpu.make_async_remote_copy(src_ref=src, dst_ref=dst, send_sem=send_sems.at[k], recv_sem=recv_sems.at[k],
                                        device_id=to, device_id_type=MESH)


class _GatherJob:
    def __init__(self, shards):
        self.names = tuple(shards)
        self.arrays = tuple(shards.values())
        n = len(self.names)
        self.out_shapes = [jax.ShapeDtypeStruct((CUTS[name].rows, CUTS[name].cols), bf16) for name in self.names]
        self.scratch_shapes = ([pltpu.SemaphoreType.DMA((6 * n,)), pltpu.SemaphoreType.DMA((6 * n,)),
                                pltpu.SemaphoreType.DMA((2 * n,))]
                               + [pltpu.VMEM(CUTS[name].shard_shape, bf16) for name in self.names])

    def bind(self, shard_refs, full_refs, scratch):
        self.shards, self.full = shard_refs, full_refs
        self.send_sems, self.recv_sems, self.local_sems = scratch[:3]
        self.stages = scratch[3:]
        return self

    def _sends(self):
        _, _, c, me, others, _ = _place()
        return [_remote(CUTS[name].half_of_shard(self.shards[w], c), CUTS[name].block(self.full[w], me, c),
                        self.send_sems, self.recv_sems, 3 * w + k, (*chip, c))
                for w, name in enumerate(self.names) for k, chip in enumerate(others)]

    def _relays(self, half):
        x, y, c, _, _, other_ids = _place()
        n = len(self.names)
        return [_remote(CUTS[name].block(self.full[w], other_ids[k], half),
                        CUTS[name].block(self.full[w], other_ids[k], half),
                        self.send_sems, self.recv_sems, 3 * n + 3 * w + k, (x, y, 1 - c))
                for w, name in enumerate(self.names) for k in range(3)]

    def _stores(self):
        me = _place()[3]
        return [pltpu.make_async_copy(self.stages[w], CUTS[name].shard(self.full[w], me), self.local_sems.at[2 * w + 1])
                for w, name in enumerate(self.names)]

    def start(self):
        loads = [pltpu.make_async_copy(self.shards[w], self.stages[w], self.local_sems.at[2 * w])
                 for w in range(len(self.names))]
        for cp in loads:
            cp.start()
        for cp in self._sends():
            cp.start()
        for load, store in zip(loads, self._stores()):
            load.wait()
            store.start()

    def finish(self):
        _, _, c, _, others, other_ids = _place()
        relays = self._relays(c)
        for w, name in enumerate(self.names):
            for k, chip in enumerate(others):
                landed = CUTS[name].block(self.full[w], other_ids[k], c)
                _remote(landed, landed, self.send_sems, self.recv_sems, 3 * w + k, (*chip, c)).wait_recv()
                relays[3 * w + k].start()
        for cp in self._relays(1 - c):
            cp.wait_recv()
        for cp in self._sends() + relays:
            cp.wait_send()
        for cp in self._stores():
            cp.wait()


class _SwapHalvesJob:
    def __init__(self, grads):
        self.names = tuple(grads)
        self.arrays = tuple(grads.values())
        n = len(self.names)
        self.out_shapes = [jax.ShapeDtypeStruct((N_CHIPS, *CUTS[name].block_shape), f32) for name in self.names]
        self.scratch_shapes = [pltpu.SemaphoreType.DMA((N_CHIPS * n,)), pltpu.SemaphoreType.DMA((N_CHIPS * n,))]

    def bind(self, g_refs, got_refs, scratch):
        self.g_refs, self.got_refs = g_refs, got_refs
        self.send_sems, self.recv_sems = scratch
        return self

    def _copies(self):
        x, y, c, _, _, _ = _place()
        return [_remote(CUTS[name].block(self.g_refs[a], j, 1 - c), self.got_refs[a].at[j], self.send_sems,
                        self.recv_sems, N_CHIPS * a + j, (x, y, 1 - c))
                for a, name in enumerate(self.names) for j in range(N_CHIPS)]

    def start(self):
        for cp in self._copies():
            cp.start()

    def finish(self):
        for cp in self._copies():
            cp.wait()


class _SwapChipsJob:
    def __init__(self, partials):
        self.names = tuple(partials)
        self.arrays = tuple(partials.values())
        n = len(self.names)
        self.out_shapes = [jax.ShapeDtypeStruct((3, *CUTS[name].block_shape), _wire_dtype(name)) for name in self.names]
        self.scratch_shapes = [pltpu.SemaphoreType.DMA((3 * n,)), pltpu.SemaphoreType.DMA((3 * n,))]

    def bind(self, p_refs, got_refs, scratch):
        self.p_refs, self.got_refs = p_refs, got_refs
        self.send_sems, self.recv_sems = scratch
        return self

    def _copies(self):
        _, _, c, _, others, other_ids = _place()
        return [_remote(self.p_refs[a].at[other_ids[k]], self.got_refs[a].at[k], self.send_sems, self.recv_sems,
                        3 * a + k, (*chip, c))
                for a in range(len(self.names)) for k, chip in enumerate(others)]

    def start(self):
        for cp in self._copies():
            cp.start()

    def finish(self):
        for cp in self._copies():
            cp.wait()


class _ShareJob:
    def __init__(self, reduced):
        self.names = tuple(reduced)
        self.arrays = tuple(reduced.values())
        self.big = [a for a, name in enumerate(self.names) if name in BIG]
        self.small = [a for a, name in enumerate(self.names) if name in SMALL]
        self.out_shapes = [jax.ShapeDtypeStruct(CUTS[name].shard_shape if name in BIG
                                                else (CUTS[name].rows, CUTS[name].cols), f32) for name in self.names]
        n_sems = len(self.big) + 7 * len(self.small)
        self.scratch_shapes = ([pltpu.SemaphoreType.DMA((n_sems,)), pltpu.SemaphoreType.DMA((n_sems,)),
                                pltpu.SemaphoreType.DMA((2 * len(self.names),))]
                               + [pltpu.VMEM(CUTS[name].block_shape, f32) for name in self.names])

    def bind(self, f_refs, out_refs, scratch):
        self.f_refs, self.out_refs = f_refs, out_refs
        self.send_sems, self.recv_sems, self.local_sems = scratch[:3]
        self.stages = scratch[3:]
        return self

    def _sem(self, a, which=0):
        if a in self.big:
            return self.big.index(a)
        return len(self.big) + 7 * self.small.index(a) + which

    def _mine(self, a, half):
        me = _place()[3]
        cut = CUTS[self.names[a]]
        return cut.half_of_shard(self.out_refs[a], half) if a in self.big else cut.block(self.out_refs[a], me, half)

    def _to_sibling(self):
        x, y, c, _, _, _ = _place()
        return [_remote(self.f_refs[a], self._mine(a, c), self.send_sems, self.recv_sems, self._sem(a), (x, y, 1 - c))
                for a in range(len(self.names))]

    def _to_chips(self):
        _, _, c, _, others, _ = _place()
        return [_remote(self.f_refs[a], self._mine(a, c), self.send_sems, self.recv_sems, self._sem(a, 1 + k), (*chip, c))
                for a in self.small for k, chip in enumerate(others)]

    def _passes(self, half):
        x, y, c, _, _, other_ids = _place()
        out = []
        for a in self.small:
            for k in range(3):
                blk = CUTS[self.names[a]].block(self.out_refs[a], other_ids[k], half)
                out.append(_remote(blk, blk, self.send_sems, self.recv_sems, self._sem(a, 4 + k), (x, y, 1 - c)))
        return out

    def _stores(self):
        c = _place()[2]
        return [pltpu.make_async_copy(self.stages[a], self._mine(a, c), self.local_sems.at[2 * a + 1])
                for a in range(len(self.names))]

    def start(self):
        loads = [pltpu.make_async_copy(self.f_refs[a], self.stages[a], self.local_sems.at[2 * a])
                 for a in range(len(self.names))]
        for cp in loads:
            cp.start()
        for cp in self._to_sibling() + self._to_chips():
            cp.start()
        for load, store in zip(loads, self._stores()):
            load.wait()
            store.start()

    def finish(self):
        x, y, c, _, others, other_ids = _place()
        passes = self._passes(c)
        for s, a in enumerate(self.small):
            for k, chip in enumerate(others):
                landed = CUTS[self.names[a]].block(self.out_refs[a], other_ids[k], c)
                _remote(landed, landed, self.send_sems, self.recv_sems, self._sem(a, 1 + k), (*chip, c)).wait_recv()
                passes[3 * s + k].start()
        for a in range(len(self.names)):
            theirs = self._mine(a, 1 - c)
            _remote(theirs, theirs, self.send_sems, self.recv_sems, self._sem(a), (x, y, 1 - c)).wait_recv()
        for cp in self._passes(1 - c):
            cp.wait_recv()
        for cp in self._to_sibling() + self._to_chips() + passes:
            cp.wait_send()
        for cp in self._stores():
            cp.wait()


def _call(body, *, name, grid, in_specs, out_specs, out_shape, args, scratch_shapes=(), jobs=()):
    n_in, n_out, n_scr = len(in_specs), len(out_specs), len(scratch_shapes)
    j_in = [len(j.arrays) for j in jobs]
    j_out = [len(j.out_shapes) for j in jobs]
    j_scr = [len(j.scratch_shapes) for j in jobs]

    def wrapped(*refs):
        refs = list(refs)

        def take(k):
            head = refs[:k]
            del refs[:k]
            return head

        ins, jins = take(n_in), [take(k) for k in j_in]
        outs, jouts = take(n_out), [take(k) for k in j_out]
        scr, jscr = take(n_scr), [take(k) for k in j_scr]
        bound = [j.bind(a, b, c) for j, a, b, c in zip(jobs, jins, jouts, jscr)]
        if not grid:
            for b in bound:
                b.start()
            body(*ins, *outs, *scr)
            for b in bound:
                b.finish()
            return
        if not bound:
            body(*ins, *outs, *scr)
            return
        first = _all([pl.program_id(d) == 0 for d in range(len(grid))])
        last = _all([pl.program_id(d) == grid[d] - 1 for d in range(len(grid))])

        @pl.when(first)
        def _():
            for b in bound:
                b.start()

        body(*ins, *outs, *scr)

        @pl.when(last)
        def _():
            for b in bound:
                b.finish()

    kw = dict(grid=grid) if grid else {}
    results = pl.pallas_call(
        wrapped, name=name,
        in_specs=list(in_specs) + [ANY] * sum(j_in), out_specs=list(out_specs) + [ANY] * sum(j_out),
        out_shape=list(out_shape) + [s for j in jobs for s in j.out_shapes],
        scratch_shapes=list(scratch_shapes) + [s for j in jobs for s in j.scratch_shapes],
        compiler_params=_params(), **kw,
    )(*args, *[a for j in jobs for a in j.arrays])
    results = list(results)
    own, rest = results[:n_out], results[n_out:]
    per_job = []
    for j, k in zip(jobs, j_out):
        per_job.append(dict(zip(j.names, rest[:k])))
        rest = rest[k:]
    return own, per_job


def _all(conds):
    out = conds[0]
    for c in conds[1:]:
        out = jnp.logical_and(out, c)
    return out


def _alone(job, name):
    return _call(lambda: None, name=name, grid=None, in_specs=[], out_specs=[], out_shape=[], args=[], jobs=[job])[1][0]


def _fwd_proj(x2, w_in_b, tile, jobs):
    tokens = x2.shape[0]

    def body(x_ref, w_ref, o_ref, xb_ref):
        xb_ref[...] = x_ref[...].astype(bf16)
        o_ref[...] = _mm(xb_ref[...], w_ref[...])

    return _call(
        body, name="fwd_proj", grid=(tokens // tile,),
        in_specs=[pl.BlockSpec((tile, D_MODEL), lambda i: (i, 0)),
                  pl.BlockSpec((D_MODEL, IN_COLS), lambda i: (0, 0))],
        out_specs=[pl.BlockSpec((tile, IN_COLS), lambda i: (i, 0)), pl.BlockSpec((tile, D_MODEL), lambda i: (i, 0))],
        out_shape=[jax.ShapeDtypeStruct((tokens, IN_COLS), f32), jax.ShapeDtypeStruct((tokens, D_MODEL), bf16)],
        args=(x2, w_in_b), jobs=jobs)


def _small_specs():
    return [pl.BlockSpec((N_GROUPS, GROUP, GROUP), lambda i: (0, 0, 0)),
            pl.BlockSpec((1, POOL_WIDTH), lambda i: (0, 0)),
            pl.BlockSpec((1, SGU_WIDTH), lambda i: (0, 0)),
            pl.BlockSpec((1, SGU_WIDTH), lambda i: (0, 0)),
            pl.BlockSpec((N_GROUPS, GROUP, GROUP), lambda i: (0, 0, 0)),
            pl.BlockSpec((N_GROUPS, GROUP, GROUP), lambda i: (0, 0, 0))]


def _fwd_mix(proj, x2, small, w_out_b, ln1_g, ln1_b, tile, seq, jobs):
    tokens = x2.shape[0]
    tps = seq // tile
    hb = tile // HALO

    def body(proj_ref, halo_ref, x_ref, pw_ref, ps_ref, lg_ref, lb_ref, sw_ref, sb_ref, wout_ref, g1_ref, b1_ref,
             mix_ref, r1_ref, h_ref):
        seq_tile = pl.program_id(0) % tps
        xp = proj_ref[:, :POOL_WIDTH]
        halo = jnp.where(seq_tile == 0, 0.0, halo_ref[...])
        pooled = _pooled_groups(xp, halo, _inv_counts(seq_tile, tile))
        for g in range(N_GROUPS):
            po = _mm(pooled[g].astype(bf16), pw_ref[g].astype(bf16)) * ps_ref[:, _gs(g)]
            mix_ref[:, _gs(g)] = po.astype(bf16)

        cdf, _ = _gelu_parts(proj_ref[:, POOL_WIDTH:])
        zg = proj_ref[:, POOL_WIDTH:] * cdf
        u = zg[:, :SGU_WIDTH]
        vln, _, _ = _ln_fwd(zg[:, SGU_WIDTH:], lg_ref[...], lb_ref[...])
        vb = vln.astype(bf16)
        mask = _tril_mask()
        for h in range(N_GROUPS):
            wm = (sw_ref[h] * mask).astype(bf16)
            bias = sb_ref[h]
            for c in range(tile // GROUP):
                rows = slice(c * GROUP, (c + 1) * GROUP)
                mixed = _mm(wm, vb[rows, _gs(h)]) + bias
                mix_ref[rows, POOL_WIDTH + h * GROUP:POOL_WIDTH + (h + 1) * GROUP] = (u[rows, _gs(h)] * mixed).astype(bf16)

        r1 = ALPHA * x_ref[...] + _mm(mix_ref[...], wout_ref[...])
        r1_ref[...] = r1
        h1, _, _ = _ln_fwd(r1, g1_ref[...], b1_ref[...])
        h_ref[...] = h1.astype(bf16)

    row = lambda i: (i, 0)
    vec = pl.BlockSpec((1, D_MODEL), lambda i: (0, 0))
    return _call(
        body, name="fwd_mix", grid=(tokens // tile,),
        in_specs=[pl.BlockSpec((tile, IN_COLS), row),
                  pl.BlockSpec((HALO, POOL_WIDTH), lambda i: (jnp.maximum(i * hb - 1, 0), 0)),
                  pl.BlockSpec((tile, D_MODEL), row)] + _small_specs()
                 + [pl.BlockSpec((D_MODEL, D_MODEL), lambda i: (0, 0)), vec, vec],
        out_specs=[pl.BlockSpec((tile, D_MODEL), row)] * 3,
        out_shape=[jax.ShapeDtypeStruct((tokens, D_MODEL), bf16),
                   jax.ShapeDtypeStruct((tokens, D_MODEL), f32),
                   jax.ShapeDtypeStruct((tokens, D_MODEL), bf16)],
        args=(proj, proj, x2, *small, w_out_b, ln1_g, ln1_b), jobs=jobs)


def _fwd_gate_up(h_b, w_gu_b, tile, jobs):
    tokens = h_b.shape[0]

    def body(h_ref, w_ref, gu_ref, a_ref):
        hb = h_ref[...]
        for c in range(D_FF // FF_CHUNK):
            gcols = slice(c * FF_CHUNK, (c + 1) * FF_CHUNK)
            ucols = slice(D_FF + c * FF_CHUNK, D_FF + (c + 1) * FF_CHUNK)
            gate = _mm(hb, w_ref[:, gcols])
            up = _mm(hb, w_ref[:, ucols])
            sg = jax.nn.sigmoid(gate)
            silu = gate * sg
            gu_ref[:, gcols] = (up * (sg + silu * (1.0 - sg))).astype(bf16)
            gu_ref[:, ucols] = silu.astype(bf16)
            a_ref[:, gcols] = (silu * up).astype(bf16)

    return _call(
        body, name="fwd_gate_up", grid=(tokens // tile,),
        in_specs=[pl.BlockSpec((tile, D_MODEL), lambda i: (i, 0)),
                  pl.BlockSpec((D_MODEL, 2 * D_FF), lambda i: (0, 0), pipeline_mode=pl.Buffered(1))],
        out_specs=[pl.BlockSpec((tile, 2 * D_FF), lambda i: (i, 0)),
                   pl.BlockSpec((tile, D_FF), lambda i: (i, 0))],
        out_shape=[jax.ShapeDtypeStruct((tokens, 2 * D_FF), bf16),
                   jax.ShapeDtypeStruct((tokens, D_FF), bf16)],
        args=(h_b, w_gu_b), jobs=jobs)


def _fwd_down_loss(a_b, w_dn_b, r1, target, ln1_g, ln1_b, ln2_g, ln2_b, tile):
    tokens = a_b.shape[0]

    def body(a_ref, w_ref, r1_ref, t_ref, g1_ref, b1_ref, g2_ref, b2_ref, dr2_ref, dr2b_ref, st_ref):
        @pl.when(pl.program_id(0) == 0)
        def _():
            st_ref[...] = jnp.zeros_like(st_ref)

        sub = min(ROW_SUB, tile)
        for s in range(tile // sub):
            rows = slice(s * sub, (s + 1) * sub)
            h1, _, _ = _ln_fwd(r1_ref[rows, :], g1_ref[...], b1_ref[...])
            r2 = ALPHA * h1 + _mm(a_ref[rows, :], w_ref[...])
            y, xhat, rstd = _ln_fwd(r2, g2_ref[...], b2_ref[...])
            diff = y - t_ref[rows, :]
            dy = diff * (1.0 / D_MODEL)
            st_ref[0:1, :] += _col_sum(dy * xhat)
            st_ref[1:2, :] += _col_sum(dy)
            st_ref[2:3, :] += _col_sum(diff * diff)
            dr2 = _ln_bwd(dy, xhat, rstd, g2_ref[...])
            dr2_ref[rows, :] = dr2
            dr2b_ref[rows, :] = dr2.astype(bf16)

    row = lambda i: (i, 0)
    vec = pl.BlockSpec((1, D_MODEL), lambda i: (0, 0))
    return _call(
        body, name="fwd_down_loss", grid=(tokens // tile,),
        in_specs=[pl.BlockSpec((tile, D_FF), row), pl.BlockSpec((D_FF, D_MODEL), lambda i: (0, 0)),
                  pl.BlockSpec((tile, D_MODEL), row), pl.BlockSpec((tile, D_MODEL), row), vec, vec, vec, vec],
        out_specs=[pl.BlockSpec((tile, D_MODEL), row), pl.BlockSpec((tile, D_MODEL), row),
                   pl.BlockSpec((8, D_MODEL), lambda i: (0, 0))],
        out_shape=[jax.ShapeDtypeStruct((tokens, D_MODEL), f32), jax.ShapeDtypeStruct((tokens, D_MODEL), bf16),
                   jax.ShapeDtypeStruct((8, D_MODEL), f32)],
        args=(a_b, w_dn_b, r1, target, ln1_g, ln1_b, ln2_g, ln2_b))[0]


def _bwd_gate_up(dr2, gu_b, w_dn_b, tile, jobs):
    tokens = dr2.shape[0]

    def body(d_ref, gu_ref, w_ref, dgu_ref):
        d = d_ref[...].astype(bf16)
        for c in range(D_FF // FF_CHUNK):
            gcols = slice(c * FF_CHUNK, (c + 1) * FF_CHUNK)
            ucols = slice(D_FF + c * FF_CHUNK, D_FF + (c + 1) * FF_CHUNK)
            da = _mm_nt(d, w_ref[gcols, :])
            dgu_ref[:, gcols] = (da * gu_ref[:, gcols].astype(f32)).astype(bf16)
            dgu_ref[:, ucols] = (da * gu_ref[:, ucols].astype(f32)).astype(bf16)

    return _call(
        body, name="bwd_gate_up", grid=(tokens // tile,),
        in_specs=[pl.BlockSpec((tile, D_MODEL), lambda i: (i, 0)),
                  pl.BlockSpec((tile, 2 * D_FF), lambda i: (i, 0)),
                  pl.BlockSpec((D_FF, D_MODEL), lambda i: (0, 0))],
        out_specs=[pl.BlockSpec((tile, 2 * D_FF), lambda i: (i, 0))],
        out_shape=[jax.ShapeDtypeStruct((tokens, 2 * D_FF), bf16)],
        args=(dr2, gu_b, w_dn_b), jobs=jobs)


def _bwd_ffn_in(dgu_b, w_gu_b, dr2, r1, ln1_g, ln1_b, tile, jobs):
    tokens = dr2.shape[0]

    def body(dgu_ref, w_ref, d_ref, r1_ref, g1_ref, b1_ref, dr1_ref, dr1b_ref, st_ref):
        @pl.when(pl.program_id(0) == 0)
        def _():
            st_ref[...] = jnp.zeros_like(st_ref)

        dh = ALPHA * d_ref[...] + _mm_nt(dgu_ref[...], w_ref[...])
        _, xhat, rstd = _ln_fwd(r1_ref[...], g1_ref[...], b1_ref[...])
        st_ref[0:1, :] += _col_sum(dh * xhat)
        st_ref[1:2, :] += _col_sum(dh)
        dr1 = _ln_bwd(dh, xhat, rstd, g1_ref[...])
        dr1_ref[...] = dr1
        dr1b_ref[...] = dr1.astype(bf16)

    row = lambda i: (i, 0)
    vec = pl.BlockSpec((1, D_MODEL), lambda i: (0, 0))
    return _call(
        body, name="bwd_ffn_in", grid=(tokens // tile,),
        in_specs=[pl.BlockSpec((tile, 2 * D_FF), row),
                  pl.BlockSpec((D_MODEL, 2 * D_FF), lambda i: (0, 0), pipeline_mode=pl.Buffered(1)),
                  pl.BlockSpec((tile, D_MODEL), row), pl.BlockSpec((tile, D_MODEL), row), vec, vec],
        out_specs=[pl.BlockSpec((tile, D_MODEL), row), pl.BlockSpec((tile, D_MODEL), row),
                   pl.BlockSpec((8, D_MODEL), lambda i: (0, 0))],
        out_shape=[jax.ShapeDtypeStruct((tokens, D_MODEL), f32), jax.ShapeDtypeStruct((tokens, D_MODEL), bf16),
                   jax.ShapeDtypeStruct((8, D_MODEL), f32)],
        args=(dgu_b, w_gu_b, dr2, r1, ln1_g, ln1_b), jobs=jobs)


def _bwd_mix(dr1, proj, small, w_out_b, tile, seq, jobs):
    tokens = dr1.shape[0]
    tps = seq // tile
    hb = tile // HALO
    steps = tokens // tile

    def body(dr1_ref, proj_ref, halo_ref, wout_ref, pw_ref, ps_ref, lg_ref, lb_ref, sw_ref, sb_ref,
             dpool_ref, dpre_ref, gpw_ref, gsw_ref, gsb_ref, vec_ref, du_ref, dvln_ref):
        step = pl.program_id(0)
        seq_tile = step % tps

        @pl.when(step == 0)
        def _():
            gpw_ref[...] = jnp.zeros_like(gpw_ref)
            gsw_ref[...] = jnp.zeros_like(gsw_ref)
            gsb_ref[...] = jnp.zeros_like(gsb_ref)
            vec_ref[...] = jnp.zeros_like(vec_ref)

        dmix = _mm_nt(dr1_ref[...].astype(bf16), wout_ref[...])

        xp = proj_ref[:, :POOL_WIDTH]
        halo = jnp.where(seq_tile == 0, 0.0, halo_ref[...])
        pooled = _pooled_groups(xp, halo, _inv_counts(seq_tile, tile))
        for g in range(N_GROUPS):
            pb = pooled[g].astype(bf16)
            pwb = pw_ref[g].astype(bf16)
            dpo = dmix[:, _gs(g)]
            vec_ref[0:1, _gs(g)] += _col_sum(dpo * _mm(pb, pwb))
            dpo_b = (dpo * ps_ref[:, _gs(g)]).astype(bf16)
            gpw_ref[g] += _mm_tn(pb, dpo_b)
            dpool_ref[:, _gs(g)] = _mm_nt(dpo_b, pwb)

        pre = proj_ref[:, POOL_WIDTH:]
        cdf, pdf = _gelu_parts(pre)
        zg = pre * cdf
        u = zg[:, :SGU_WIDTH]
        vln, vhat, rstd = _ln_fwd(zg[:, SGU_WIDTH:], lg_ref[...], lb_ref[...])
        vb = vln.astype(bf16)
        mask = _tril_mask()
        for h in range(N_GROUPS):
            wm = (sw_ref[h] * mask).astype(bf16)
            bias = sb_ref[h]
            gsw = jnp.zeros((GROUP, GROUP), f32)
            gsb = jnp.zeros((GROUP, GROUP), f32)
            for c in range(tile // GROUP):
                rows = slice(c * GROUP, (c + 1) * GROUP)
                v_ch = vb[rows, _gs(h)]
                d = dmix[rows, POOL_WIDTH + h * GROUP:POOL_WIDTH + (h + 1) * GROUP]
                du_ref[rows, _gs(h)] = d * (_mm(wm, v_ch) + bias)
                dmixed = d * u[rows, _gs(h)]
                gsb += dmixed
                dmixed_b = dmixed.astype(bf16)
                gsw += _mm_nt(dmixed_b, v_ch)
                dvln_ref[rows, _gs(h)] = _mm_tn(wm, dmixed_b)
            gsw_ref[h] += gsw * mask
            gsb_ref[h] += gsb

        dvln = dvln_ref[...]
        vec_ref[1:2, :] += _col_sum(dvln * vhat)
        vec_ref[2:3, :] += _col_sum(dvln)
        dgelu = cdf + pre * pdf
        dpre_ref[:, :SGU_WIDTH] = (du_ref[...] * dgelu[:, :SGU_WIDTH]).astype(bf16)
        dpre_ref[:, SGU_WIDTH:] = (_ln_bwd(dvln, vhat, rstd, lg_ref[...]) * dgelu[:, SGU_WIDTH:]).astype(bf16)

        @pl.when(step == steps - 1)
        def _():
            for h in range(N_GROUPS):
                gsb_ref[h] = jnp.broadcast_to(jnp.sum(gsb_ref[h], axis=1, keepdims=True), (GROUP, GROUP))

    row = lambda i: (i, 0)
    sq = jax.ShapeDtypeStruct((N_GROUPS, GROUP, GROUP), f32)
    sq_spec = pl.BlockSpec((N_GROUPS, GROUP, GROUP), lambda i: (0, 0, 0))
    return _call(
        body, name="bwd_mix", grid=(steps,),
        in_specs=[pl.BlockSpec((tile, D_MODEL), row), pl.BlockSpec((tile, IN_COLS), row),
                  pl.BlockSpec((HALO, POOL_WIDTH), lambda i: (jnp.maximum(i * hb - 1, 0), 0)),
                  pl.BlockSpec((D_MODEL, D_MODEL), lambda i: (0, 0))] + _small_specs(),
        out_specs=[pl.BlockSpec((tile, POOL_WIDTH), row), pl.BlockSpec((tile, 2 * SGU_WIDTH), row),
                   sq_spec, sq_spec, sq_spec, pl.BlockSpec((8, POOL_WIDTH), lambda i: (0, 0))],
        out_shape=[jax.ShapeDtypeStruct((tokens, POOL_WIDTH), f32), jax.ShapeDtypeStruct((tokens, 2 * SGU_WIDTH), bf16),
                   sq, sq, sq, jax.ShapeDtypeStruct((8, POOL_WIDTH), f32)],
        scratch_shapes=[pltpu.VMEM((tile, SGU_WIDTH), f32), pltpu.VMEM((tile, SGU_WIDTH), f32)],
        args=(dr1, proj, proj, w_out_b, *small), jobs=jobs)


def _bwd_in(dpool, dpre_b, dr1, w_in_b, tile, seq):
    tokens = dr1.shape[0]
    tps = seq // tile
    hb = tile // HALO
    last_halo = tokens // HALO - 1

    def body(dpool_ref, nxt_ref, dpre_ref, dr1_ref, w_ref, dx_ref, dproj_ref):
        seq_tile = pl.program_id(0) % tps
        inv = _inv_counts(seq_tile, tile)
        dpl = dpool_ref[...]
        nxt = jnp.where(seq_tile == tps - 1, 0.0, nxt_ref[...])
        scaled = jnp.concatenate([dpl[:, _gs(g)] * inv[g] for g in range(N_GROUPS)], axis=1)
        scaled_nxt = jnp.concatenate([nxt[:, _gs(g)] * (1.0 / POOL_WINDOWS[g]) for g in range(N_GROUPS)], axis=1)
        sums = _window_sums(jnp.concatenate([scaled, scaled_nxt], axis=0), back=False)
        for g in range(N_GROUPS):
            dproj_ref[:, _gs(g)] = (sums[g][:tile] - dpl[:, _gs(g)]).astype(bf16)
        dproj_ref[:, POOL_WIDTH:] = dpre_ref[...]
        dx_ref[...] = ALPHA * dr1_ref[...] + _mm_nt(dproj_ref[...], w_ref[...])

    row = lambda i: (i, 0)
    return _call(
        body, name="bwd_in", grid=(tokens // tile,),
        in_specs=[pl.BlockSpec((tile, POOL_WIDTH), row),
                  pl.BlockSpec((HALO, POOL_WIDTH), lambda i: (jnp.minimum((i + 1) * hb, last_halo), 0)),
                  pl.BlockSpec((tile, 2 * SGU_WIDTH), row),
                  pl.BlockSpec((tile, D_MODEL), row),
                  pl.BlockSpec((D_MODEL, IN_COLS), lambda i: (0, 0))],
        out_specs=[pl.BlockSpec((tile, D_MODEL), row), pl.BlockSpec((tile, IN_COLS), row)],
        out_shape=[jax.ShapeDtypeStruct((tokens, D_MODEL), f32), jax.ShapeDtypeStruct((tokens, IN_COLS), bf16)],
        args=(dpool, dpool, dpre_b, dr1, w_in_b))[0]


def _wgrad(a, b, col_tile, tile, name, jobs=()):
    tokens, m = a.shape
    n = b.shape[1]

    def body(a_ref, b_ref, o_ref):
        @pl.when(pl.program_id(1) == 0)
        def _():
            o_ref[...] = jnp.zeros_like(o_ref)

        o_ref[...] += _mm_tn(a_ref[...].astype(bf16), b_ref[...].astype(bf16))

    (out,), got = _call(
        body, name=name, grid=(n // col_tile, tokens // tile),
        in_specs=[pl.BlockSpec((tile, m), lambda j, k: (k, 0)),
                  pl.BlockSpec((tile, col_tile), lambda j, k: (k, j))],
        out_specs=[pl.BlockSpec((m, col_tile), lambda j, k: (0, j))],
        out_shape=[jax.ShapeDtypeStruct((m, n), f32)],
        args=(a, b), jobs=jobs)
    return out, got


def _add_halves(name, g, got, core):
    cut = CUTS[name]
    br, bc = cut.block_shape
    wire = _wire_dtype(name)

    def body(core_ref, g_ref, got_ref, o_ref, wire_ref):
        s = g_ref[...] + got_ref[...]
        o_ref[...] = s
        wire_ref[...] = s.astype(wire)

    blocks = pl.BlockSpec((None, br, bc), lambda j, core_ref: (j, 0, 0))
    return pl.pallas_call(
        body, name="reduce_add_halves_" + name,
        grid_spec=pltpu.PrefetchScalarGridSpec(
            num_scalar_prefetch=1, grid=(N_CHIPS,),
            in_specs=[pl.BlockSpec((br, bc), lambda j, core_ref: cut.block_index(j, core_ref[0])), blocks],
            out_specs=[blocks, blocks]),
        out_shape=[jax.ShapeDtypeStruct((N_CHIPS, br, bc), f32), jax.ShapeDtypeStruct((N_CHIPS, br, bc), wire)],
        compiler_params=_params(),
    )(core, g, got)


def _add_chips(name, partial, got, chip):
    br, bc = CUTS[name].block_shape
    rt = br // 2 if br % 32 == 0 else br

    def body(chip_ref, p_ref, got_ref, o_ref):
        o_ref[...] = ((p_ref[...] + got_ref[0].astype(f32)) + got_ref[1].astype(f32)) + got_ref[2].astype(f32)

    return pl.pallas_call(
        body, name="reduce_add_chips_" + name,
        grid_spec=pltpu.PrefetchScalarGridSpec(
            num_scalar_prefetch=1, grid=(br // rt,),
            in_specs=[pl.BlockSpec((None, rt, bc), lambda i, chip_ref: (chip_ref[0], i, 0)),
                      pl.BlockSpec((3, rt, bc), lambda i, chip_ref: (0, i, 0))],
            out_specs=pl.BlockSpec((rt, bc), lambda i, chip_ref: (i, 0))),
        out_shape=jax.ShapeDtypeStruct((br, bc), f32),
        compiler_params=_params(),
    )(chip, partial, got)


def _adamw(name, w, g, m, v):
    rows, cols = w.shape
    rt = rows // 4

    def body(w_ref, g_ref, m_ref, v_ref, d_ref, nm_ref, nv_ref):
        g = g_ref[...]
        nm = ADAM_B1 * m_ref[...] + (1.0 - ADAM_B1) * g
        nv = ADAM_B2 * v_ref[...] + (1.0 - ADAM_B2) * jnp.square(g)
        m_hat = nm / (1.0 - ADAM_B1 ** ADAM_STEP)
        v_hat = nv / (1.0 - ADAM_B2 ** ADAM_STEP)
        d_ref[...] = -ADAM_LR * (m_hat / (jnp.sqrt(v_hat) + ADAM_EPS) + ADAM_WD * w_ref[...])
        nm_ref[...] = nm
        nv_ref[...] = nv

    spec = pl.BlockSpec((rt, cols), lambda i: (i, 0))
    shape = jax.ShapeDtypeStruct((rows, cols), f32)
    return pl.pallas_call(
        body, name="adamw_" + name, grid=(rows // rt,),
        in_specs=[spec] * 4, out_specs=[spec] * 3, out_shape=[shape] * 3,
        compiler_params=_params(),
    )(w, g, m, v)


VEC_NAMES = ("pool_scale", "sgu_ln_g", "sgu_ln_b", "sgu_b", "ln1_g", "ln1_b", "ln2_g", "ln2_b")
WEIGHT_ORDER = ("w_in", "pool_w", "pool_scale", "sgu_ln_g", "sgu_ln_b", "sgu_w", "sgu_b", "w_out", "ln1_g", "ln1_b",
                "w_gate_up", "w_down", "ln2_g", "ln2_b")


def _pack_vecs(parts, extra=None):
    rows = [parts[name].reshape(-1, GROUP) for name in VEC_NAMES]
    if extra is not None:
        rows.append(extra.reshape(-1, GROUP))
    used = sum(r.shape[0] for r in rows)
    return jnp.concatenate(rows + [jnp.zeros((VEC_ROWS - used, GROUP), f32)], axis=0)


def _unpack_vecs(packed, shapes):
    out, at = {}, 0
    for name in VEC_NAMES:
        rows = math.prod(shapes[name]) // GROUP
        out[name] = packed[at:at + rows].reshape(shapes[name])
        at += rows
    return out, packed[at:]


def kernel(x, w_in, pool_w, pool_scale, sgu_ln_g, sgu_ln_b, sgu_w, sgu_b, w_out, ln1_g, ln1_b, w_gate_up, w_down, ln2_g, ln2_b, loss_target, m_w_in, m_pool_w, m_pool_scale, m_sgu_ln_g, m_sgu_ln_b, m_sgu_w, m_sgu_b, m_w_out, m_ln1_g, m_ln1_b, m_w_gate_up, m_w_down, m_ln2_g, m_ln2_b, v_w_in, v_pool_w, v_pool_scale, v_sgu_ln_g, v_sgu_ln_b, v_sgu_w, v_sgu_b, v_w_out, v_ln1_g, v_ln1_b, v_w_gate_up, v_w_down, v_ln2_g, v_ln2_b):
    given = dict(locals())
    batch, seq, _ = x.shape
    tokens = batch * seq
    tile = min(TOKEN_TILE, seq)
    ffn_bwd_tile = min(FFN_BWD_TILE, seq)
    wtile = min(WGRAD_TILE, tokens)
    shapes = {name: given[name].shape for name in WEIGHT_ORDER}

    x2 = x.reshape(tokens, D_MODEL)
    target = loss_target.reshape(tokens, D_MODEL)
    small = (pool_w[0], pool_scale[0][None], sgu_ln_g[0][None], sgu_ln_b[0][None], sgu_w[0],
             jnp.broadcast_to(sgu_b[0][:, :, None], (N_GROUPS, GROUP, GROUP)))
    g1, b1, g2, b2 = ln1_g[0][None], ln1_b[0][None], ln2_g[0][None], ln2_b[0][None]
    shard_b = {name: given[name][0].astype(bf16) for name in BIG}
    core = lax.axis_index("c").astype(jnp.int32).reshape(1)
    chip = (2 * lax.axis_index("x") + lax.axis_index("y")).astype(jnp.int32).reshape(1)

    def gather(name):
        return [_GatherJob({name: shard_b[name]})]

    def halves_summed(name, grad, got):
        return _add_halves(name, grad, got, core)

    def chips_summed(name, sums, got):
        return _add_chips(name, sums[0], got, chip)

    w_in_b = _alone(gather("w_in")[0], "gather_w_in")["w_in"]
    (proj, x_b), (got,) = _fwd_proj(x2, w_in_b, tile, gather("w_out"))
    w_out_b = got["w_out"]
    (mix_b, r1, h_b), (got,) = _fwd_mix(proj, x2, small, w_out_b, g1, b1, tile, seq, gather("w_gate_up"))
    w_gu_b = got["w_gate_up"]
    (gu_b, a_b), (got,) = _fwd_gate_up(h_b, w_gu_b, tile, gather("w_down"))
    w_dn_b = got["w_down"]
    dr2, dr2_b, stats2 = _fwd_down_loss(a_b, w_dn_b, r1, target, g1, b1, g2, b2, tile)

    reduced = {}
    g_down, _ = _wgrad(a_b, dr2_b, D_MODEL, wtile, "wgrad_down")
    (dgu_b,), (got,) = _bwd_gate_up(dr2_b, gu_b, w_dn_b, ffn_bwd_tile, [_SwapHalvesJob({"w_down": g_down})])
    sums_down = halves_summed("w_down", g_down, got["w_down"])
    g_gu, (got,) = _wgrad(h_b, dgu_b, D_FF, wtile, "wgrad_gate_up", [_SwapChipsJob({"w_down": sums_down[1]})])
    reduced["w_down"] = chips_summed("w_down", sums_down, got["w_down"])
    (dr1, dr1_b, stats1), (got,) = _bwd_ffn_in(dgu_b, w_gu_b, dr2, r1, g1, b1, tile,
                                               [_SwapHalvesJob({"w_gate_up": g_gu})])
    sums_gu = halves_summed("w_gate_up", g_gu, got["w_gate_up"])
    g_out, _ = _wgrad(mix_b, dr1_b, D_MODEL, wtile, "wgrad_out")
    (dpool, dpre_b, g_pool_w, g_sgu_w, g_sgu_b, vecs), (got_gu, got_out) = _bwd_mix(
        dr1_b, proj, small, w_out_b, tile, seq,
        [_SwapChipsJob({"w_gate_up": sums_gu[1]}), _SwapHalvesJob({"w_out": g_out})])
    reduced["w_gate_up"] = chips_summed("w_gate_up", sums_gu, got_gu["w_gate_up"])
    sums_out = halves_summed("w_out", g_out, got_out["w_out"])
    grad_x, dproj_b = _bwd_in(dpool, dpre_b, dr1, w_in_b, tile, seq)
    g_in, (got,) = _wgrad(x_b, dproj_b, IN_COLS, wtile, "wgrad_in", [_SwapChipsJob({"w_out": sums_out[1]})])
    reduced["w_out"] = chips_summed("w_out", sums_out, got["w_out"])

    last = {
        "w_in": g_in,
        "pool_w": g_pool_w.reshape(SQUARE_ROWS, GROUP),
        "sgu_w": g_sgu_w.reshape(SQUARE_ROWS, GROUP),
        "vecs": _pack_vecs({"pool_scale": vecs[0], "sgu_ln_g": vecs[1], "sgu_ln_b": vecs[2], "sgu_b": g_sgu_b[:, :, 0],
                            "ln1_g": stats1[0], "ln1_b": stats1[1], "ln2_g": stats2[0], "ln2_b": stats2[1]},
                           extra=stats2[2]),
    }
    got = _alone(_SwapHalvesJob(last), "reduce_swap_halves")
    sums = {name: halves_summed(name, last[name], got[name]) for name in last}
    got = _alone(_SwapChipsJob({name: sums[name][1] for name in last}), "reduce_swap_chips")
    reduced.update({name: chips_summed(name, sums[name], got[name]) for name in last})
    shared = _alone(_ShareJob({name: reduced[name] for name in BIG + SMALL}), "reduce_share")

    grad, delta, new_m, new_v = {}, {}, {}, {}
    for name in BIG:
        grad[name] = shared[name][None]
        d, nm, nv = _adamw(name, given[name][0], shared[name], given["m_" + name][0], given["v_" + name][0])
        delta[name], new_m[name], new_v[name] = d[None], nm[None], nv[None]
    for name in ("pool_w", "sgu_w"):
        flat = [given[pre + name].reshape(SQUARE_ROWS, GROUP) for pre in ("", "m_", "v_")]
        d, nm, nv = _adamw(name, flat[0], shared[name], flat[1], flat[2])
        for out, rows in ((grad, shared[name]), (delta, d), (new_m, nm), (new_v, nv)):
            out[name] = rows.reshape(shapes[name])
    packed = [_pack_vecs({name: given[pre + name] for name in VEC_NAMES}) for pre in ("", "m_", "v_")]
    d, nm, nv = _adamw("vecs", packed[0], shared["vecs"], packed[1], packed[2])
    for out, rows in ((grad, shared["vecs"]), (delta, d), (new_m, nm), (new_v, nv)):
        out.update(_unpack_vecs(rows, shapes)[0])

    sq_err = _unpack_vecs(shared["vecs"], shapes)[1][:LOSS_ROWS]
    loss = jnp.sum(sq_err) * (0.5 / D_MODEL)
    return (loss, grad_x.reshape(x.shape), *[grad[name] for name in WEIGHT_ORDER],
            *[delta[name] for name in WEIGHT_ORDER], *[new_m[name] for name in WEIGHT_ORDER],
            *[new_v[name] for name in WEIGHT_ORDER])
```

```python
import math

import jax
import jax.numpy as jnp
from jax import lax
from jax.experimental import pallas as pl
from jax.experimental.pallas import tpu as pltpu

f32 = jnp.float32
bf16 = jnp.bfloat16
MESH = pl.DeviceIdType.MESH

D_MODEL = 1024
POOL_WIDTH = 512
SGU_WIDTH = 512
IN_COLS = POOL_WIDTH + 2 * SGU_WIDTH
D_FF = 2816
POOL_WINDOWS = (2, 4, 8, 16)
GROUP = 128
N_GROUPS = 4
HALO = 16
LN_EPS = 1e-5
ALPHA = float(2.0 ** 0.25)
N_CHIPS = 4

ADAM_LR = 0.001
ADAM_B1 = 0.9
ADAM_B2 = 0.999
ADAM_EPS = 1e-08
ADAM_WD = 0.01
ADAM_STEP = 10

TOKEN_TILE = 512
FFN_BWD_TILE = 512
FF_CHUNK = 256
ROW_SUB = 256
WGRAD_TILE = 1024
V7X_VMEM_LIMIT = 56 * 1024 * 1024

SQUARE_ROWS = N_GROUPS * GROUP
VEC_ROWS = 64
LOSS_ROWS = D_MODEL // GROUP


def _params(**kw):
    return pltpu.CompilerParams(vmem_limit_bytes=V7X_VMEM_LIMIT, **kw)


def _mm(a, b):
    return jnp.dot(a, b, preferred_element_type=f32)


def _mm_nt(a, b):
    return lax.dot_general(a, b, (((1,), (1,)), ((), ())), preferred_element_type=f32)


def _mm_tn(a, b):
    return lax.dot_general(a, b, (((0,), (0,)), ((), ())), preferred_element_type=f32)


def _ln_fwd(r, g, b):
    mu = jnp.mean(r, axis=-1, keepdims=True)
    xc = r - mu
    var = jnp.mean(xc * xc, axis=-1, keepdims=True)
    rstd = lax.rsqrt(var + LN_EPS)
    xhat = xc * rstd
    return xhat * g + b, xhat, rstd


def _ln_bwd(dout, xhat, rstd, g):
    dxhat = dout * g
    m1 = jnp.mean(dxhat, axis=-1, keepdims=True)
    m2 = jnp.mean(dxhat * xhat, axis=-1, keepdims=True)
    return rstd * (dxhat - m1 - xhat * m2)


def _col_sum(a):
    return jnp.sum(a, axis=0, keepdims=True)


def _gelu_parts(z):
    cdf = 0.5 * (1.0 + lax.erf(z * (1.0 / math.sqrt(2.0))))
    pdf = jnp.exp(-0.5 * z * z) * (1.0 / math.sqrt(2.0 * math.pi))
    return cdf, pdf


def _inv_counts(seq_tile, rows):
    pos = seq_tile * rows + lax.broadcasted_iota(jnp.int32, (rows, GROUP), 0) + 1
    return [1.0 / jnp.minimum(pos, w).astype(f32) for w in POOL_WINDOWS]


def _window_sums(e, back):
    n = e.shape[0]

    def shifted(a, s):
        return pltpu.roll(a, s if back else n - s, 0)

    s2 = e + shifted(e, 1)
    s4 = s2[:, GROUP:] + shifted(s2[:, GROUP:], 2)
    s8 = s4[:, GROUP:] + shifted(s4[:, GROUP:], 4)
    s16 = s8[:, GROUP:] + shifted(s8[:, GROUP:], 8)
    return [s2[:, :GROUP], s4[:, :GROUP], s8[:, :GROUP], s16]


def _pooled_groups(xp, halo, inv):
    sums = _window_sums(jnp.concatenate([halo, xp], axis=0), back=True)
    return [sums[g][HALO:] * inv[g] - xp[:, g * GROUP:(g + 1) * GROUP] for g in range(N_GROUPS)]


def _tril_mask():
    r = lax.broadcasted_iota(jnp.int32, (GROUP, GROUP), 0)
    c = lax.broadcasted_iota(jnp.int32, (GROUP, GROUP), 1)
    return (r >= c).astype(f32)


def _gs(g):
    return slice(g * GROUP, (g + 1) * GROUP)


class _Cut:
    def __init__(self, rows, cols, by_cols):
        self.rows, self.cols, self.by_cols = rows, cols, by_cols
        if by_cols:
            self.block_shape = (rows // 2, cols // N_CHIPS)
            self.shard_shape = (rows, cols // N_CHIPS)
        else:
            self.block_shape = (rows // (2 * N_CHIPS), cols)
            self.shard_shape = (rows // N_CHIPS, cols)

    def block(self, ref, chip, half):
        br, bc = self.block_shape
        if self.by_cols:
            return ref.at[pl.ds(pl.multiple_of(half * br, 16), br), pl.ds(pl.multiple_of(chip * bc, 128), bc)]
        return ref.at[pl.ds(pl.multiple_of((2 * chip + half) * br, 8), br), :]

    def shard(self, ref, chip):
        sr, sc = self.shard_shape
        if self.by_cols:
            return ref.at[:, pl.ds(pl.multiple_of(chip * sc, 128), sc)]
        return ref.at[pl.ds(pl.multiple_of(chip * sr, 16), sr), :]

    def half_of_shard(self, ref, half):
        br = self.block_shape[0]
        return ref.at[pl.ds(pl.multiple_of(half * br, 8), br), :]

    def block_index(self, chip, half):
        return (half, chip) if self.by_cols else (2 * chip + half, 0)


CUTS = {
    "w_in": _Cut(D_MODEL, IN_COLS, True),
    "w_out": _Cut(D_MODEL, D_MODEL, False),
    "w_gate_up": _Cut(D_MODEL, 2 * D_FF, True),
    "w_down": _Cut(D_FF, D_MODEL, False),
    "pool_w": _Cut(SQUARE_ROWS, GROUP, False),
    "sgu_w": _Cut(SQUARE_ROWS, GROUP, False),
    "vecs": _Cut(VEC_ROWS, GROUP, False),
}
BIG = ("w_in", "w_out", "w_gate_up", "w_down")
SMALL = ("pool_w", "sgu_w", "vecs")
ANY = pl.BlockSpec(memory_space=pl.ANY)


def _wire_dtype(name):
    return bf16 if name in BIG else f32


def _place():
    x, y, c = lax.axis_index("x"), lax.axis_index("y"), lax.axis_index("c")
    others = [(1 - x, y), (x, 1 - y), (1 - x, 1 - y)]
    return x, y, c, 2 * x + y, others, [2 * ox + oy for ox, oy in others]


def _remote(src, dst, send_sems, recv_sems, k, to):
    return pltpu.make_async_remote_copy(src_ref=src, dst_ref=dst, send_sem=send_sems.at[k], recv_sem=recv_sems.at[k],
                                        device_id=to, device_id_type=MESH)


class _GatherJob:
    def __init__(self, shards):
        self.names = tuple(shards)
        self.arrays = tuple(shards.values())
        n = len(self.names)
        self.out_shapes = [jax.ShapeDtypeStruct((CUTS[name].rows, CUTS[name].cols), bf16) for name in self.names]
        self.scratch_shapes = ([pltpu.SemaphoreType.DMA((6 * n,)), pltpu.SemaphoreType.DMA((6 * n,)),
                                pltpu.SemaphoreType.DMA((2 * n,))]
                               + [pltpu.VMEM(CUTS[name].shard_shape, bf16) for name in self.names])

    def bind(self, shard_refs, full_refs, scratch):
        self.shards, self.full = shard_refs, full_refs
        self.send_sems, self.recv_sems, self.local_sems = scratch[:3]
        self.stages = scratch[3:]
        return self

    def _sends(self):
        _, _, c, me, others, _ = _place()
        return [_remote(CUTS[name].half_of_shard(self.shards[w], c), CUTS[name].block(self.full[w], me, c),
                        self.send_sems, self.recv_sems, 3 * w + k, (*chip, c))
                for w, name in enumerate(self.names) for k, chip in enumerate(others)]

    def _relays(self, half):
        x, y, c, _, _, other_ids = _place()
        n = len(self.names)
        return [_remote(CUTS[name].block(self.full[w], other_ids[k], half),
                        CUTS[name].block(self.full[w], other_ids[k], half),
                        self.send_sems, self.recv_sems, 3 * n + 3 * w + k, (x, y, 1 - c))
                for w, name in enumerate(self.names) for k in range(3)]

    def _stores(self):
        me = _place()[3]
        return [pltpu.make_async_copy(self.stages[w], CUTS[name].shard(self.full[w], me), self.local_sems.at[2 * w + 1])
                for w, name in enumerate(self.names)]

    def start(self):
        loads = [pltpu.make_async_copy(self.shards[w], self.stages[w], self.local_sems.at[2 * w])
                 for w in range(len(self.names))]
        for cp in loads:
            cp.start()
        for cp in self._sends():
            cp.start()
        for load, store in zip(loads, self._stores()):
            load.wait()
            store.start()

    def finish(self):
        _, _, c, _, others, other_ids = _place()
        relays = self._relays(c)
        for w, name in enumerate(self.names):
            for k, chip in enumerate(others):
                landed = CUTS[name].block(self.full[w], other_ids[k], c)
                _remote(landed, landed, self.send_sems, self.recv_sems, 3 * w + k, (*chip, c)).wait_recv()
                relays[3 * w + k].start()
        for cp in self._relays(1 - c):
            cp.wait_recv()
        for cp in self._sends() + relays:
            cp.wait_send()
        for cp in self._stores():
            cp.wait()


class _SwapHalvesJob:
    def __init__(self, grads):
        self.names = tuple(grads)
        self.arrays = tuple(grads.values())
        n = len(self.names)
        self.out_shapes = [jax.ShapeDtypeStruct((N_CHIPS, *CUTS[name].block_shape), f32) for name in self.names]
        self.scratch_shapes = [pltpu.SemaphoreType.DMA((N_CHIPS * n,)), pltpu.SemaphoreType.DMA((N_CHIPS * n,))]

    def bind(self, g_refs, got_refs, scratch):
        self.g_refs, self.got_refs = g_refs, got_refs
        self.send_sems, self.recv_sems = scratch
        return self

    def _copies(self):
        x, y, c, _, _, _ = _place()
        return [_remote(CUTS[name].block(self.g_refs[a], j, 1 - c), self.got_refs[a].at[j], self.send_sems,
                        self.recv_sems, N_CHIPS * a + j, (x, y, 1 - c))
                for a, name in enumerate(self.names) for j in range(N_CHIPS)]

    def start(self):
        for cp in self._copies():
            cp.start()

    def finish(self):
        for cp in self._copies():
            cp.wait()


class _SwapChipsJob:
    def __init__(self, partials):
        self.names = tuple(partials)
        self.arrays = tuple(partials.values())
        n = len(self.names)
        self.out_shapes = [jax.ShapeDtypeStruct((3, *CUTS[name].block_shape), _wire_dtype(name)) for name in self.names]
        self.scratch_shapes = [pltpu.SemaphoreType.DMA((3 * n,)), pltpu.SemaphoreType.DMA((3 * n,))]

    def bind(self, p_refs, got_refs, scratch):
        self.p_refs, self.got_refs = p_refs, got_refs
        self.send_sems, self.recv_sems = scratch
        return self

    def _copies(self):
        _, _, c, _, others, other_ids = _place()
        return [_remote(self.p_refs[a].at[other_ids[k]], self.got_refs[a].at[k], self.send_sems, self.recv_sems,
                        3 * a + k, (*chip, c))
                for a in range(len(self.names)) for k, chip in enumerate(others)]

    def start(self):
        for cp in self._copies():
            cp.start()

    def finish(self):
        for cp in self._copies():
            cp.wait()


class _ShareJob:
    def __init__(self, reduced):
        self.names = tuple(reduced)
        self.arrays = tuple(reduced.values())
        self.big = [a for a, name in enumerate(self.names) if name in BIG]
        self.small = [a for a, name in enumerate(self.names) if name in SMALL]
        self.out_shapes = [jax.ShapeDtypeStruct(CUTS[name].shard_shape if name in BIG
                                                else (CUTS[name].rows, CUTS[name].cols), f32) for name in self.names]
        n_sems = len(self.big) + 7 * len(self.small)
        self.scratch_shapes = ([pltpu.SemaphoreType.DMA((n_sems,)), pltpu.SemaphoreType.DMA((n_sems,)),
                                pltpu.SemaphoreType.DMA((2 * len(self.names),))]
                               + [pltpu.VMEM(CUTS[name].block_shape, f32) for name in self.names])

    def bind(self, f_refs, out_refs, scratch):
        self.f_refs, self.out_refs = f_refs, out_refs
        self.send_sems, self.recv_sems, self.local_sems = scratch[:3]
        self.stages = scratch[3:]
        return self

    def _sem(self, a, which=0):
        if a in self.big:
            return self.big.index(a)
        return len(self.big) + 7 * self.small.index(a) + which

    def _mine(self, a, half):
        me = _place()[3]
        cut = CUTS[self.names[a]]
        return cut.half_of_shard(self.out_refs[a], half) if a in self.big else cut.block(self.out_refs[a], me, half)

    def _to_sibling(self):
        x, y, c, _, _, _ = _place()
        return [_remote(self.f_refs[a], self._mine(a, c), self.send_sems, self.recv_sems, self._sem(a), (x, y, 1 - c))
                for a in range(len(self.names))]

    def _to_chips(self):
        _, _, c, _, others, _ = _place()
        return [_remote(self.f_refs[a], self._mine(a, c), self.send_sems, self.recv_sems, self._sem(a, 1 + k), (*chip, c))
                for a in self.small for k, chip in enumerate(others)]

    def _passes(self, half):
        x, y, c, _, _, other_ids = _place()
        out = []
        for a in self.small:
            for k in range(3):
                blk = CUTS[self.names[a]].block(self.out_refs[a], other_ids[k], half)
                out.append(_remote(blk, blk, self.send_sems, self.recv_sems, self._sem(a, 4 + k), (x, y, 1 - c)))
        return out

    def _stores(self):
        c = _place()[2]
        return [pltpu.make_async_copy(self.stages[a], self._mine(a, c), self.local_sems.at[2 * a + 1])
                for a in range(len(self.names))]

    def start(self):
        loads = [pltpu.make_async_copy(self.f_refs[a], self.stages[a], self.local_sems.at[2 * a])
                 for a in range(len(self.names))]
        for cp in loads:
            cp.start()
        for cp in self._to_sibling() + self._to_chips():
            cp.start()
        for load, store in zip(loads, self._stores()):
            load.wait()
            store.start()

    def finish(self):
        x, y, c, _, others, other_ids = _place()
        passes = self._passes(c)
        for s, a in enumerate(self.small):
            for k, chip in enumerate(others):
                landed = CUTS[self.names[a]].block(self.out_refs[a], other_ids[k], c)
                _remote(landed, landed, self.send_sems, self.recv_sems, self._sem(a, 1 + k), (*chip, c)).wait_recv()
                passes[3 * s + k].start()
        for a in range(len(self.names)):
            theirs = self._mine(a, 1 - c)
            _remote(theirs, theirs, self.send_sems, self.recv_sems, self._sem(a), (x, y, 1 - c)).wait_recv()
        for cp in self._passes(1 - c):
            cp.wait_recv()
        for cp in self._to_sibling() + self._to_chips() + passes:
            cp.wait_send()
        for cp in self._stores():
            cp.wait()


def _call(body, *, name, grid, in_specs, out_specs, out_shape, args, scratch_shapes=(), jobs=()):
    n_in, n_out, n_scr = len(in_specs), len(out_specs), len(scratch_shapes)
    j_in = [len(j.arrays) for j in jobs]
    j_out = [len(j.out_shapes) for j in jobs]
    j_scr = [len(j.scratch_shapes) for j in jobs]

    def wrapped(*refs):
        refs = list(refs)

        def take(k):
            head = refs[:k]
            del refs[:k]
            return head

        ins, jins = take(n_in), [take(k) for k in j_in]
        outs, jouts = take(n_out), [take(k) for k in j_out]
        scr, jscr = take(n_scr), [take(k) for k in j_scr]
        bound = [j.bind(a, b, c) for j, a, b, c in zip(jobs, jins, jouts, jscr)]
        if not grid:
            for b in bound:
                b.start()
            body(*ins, *outs, *scr)
            for b in bound:
                b.finish()
            return
        if not bound:
            body(*ins, *outs, *scr)
            return
        first = _all([pl.program_id(d) == 0 for d in range(len(grid))])
        last = _all([pl.program_id(d) == grid[d] - 1 for d in range(len(grid))])

        @pl.when(first)
        def _():
            for b in bound:
                b.start()

        body(*ins, *outs, *scr)

        @pl.when(last)
        def _():
            for b in bound:
                b.finish()

    kw = dict(grid=grid) if grid else {}
    results = pl.pallas_call(
        wrapped, name=name,
        in_specs=list(in_specs) + [ANY] * sum(j_in), out_specs=list(out_specs) + [ANY] * sum(j_out),
        out_shape=list(out_shape) + [s for j in jobs for s in j.out_shapes],
        scratch_shapes=list(scratch_shapes) + [s for j in jobs for s in j.scratch_shapes],
        compiler_params=_params(), **kw,
    )(*args, *[a for j in jobs for a in j.arrays])
    results = list(results)
    own, rest = results[:n_out], results[n_out:]
    per_job = []
    for j, k in zip(jobs, j_out):
        per_job.append(dict(zip(j.names, rest[:k])))
        rest = rest[k:]
    return own, per_job


def _all(conds):
    out = conds[0]
    for c in conds[1:]:
        out = jnp.logical_and(out, c)
    return out


def _alone(job, name):
    return _call(lambda: None, name=name, grid=None, in_specs=[], out_specs=[], out_shape=[], args=[], jobs=[job])[1][0]


def _fwd_proj(x2, w_in_b, tile, jobs):
    tokens = x2.shape[0]

    def body(x_ref, w_ref, o_ref, xb_ref):
        xb_ref[...] = x_ref[...].astype(bf16)
        o_ref[...] = _mm(xb_ref[...], w_ref[...])

    return _call(
        body, name="fwd_proj", grid=(tokens // tile,),
        in_specs=[pl.BlockSpec((tile, D_MODEL), lambda i: (i, 0)),
                  pl.BlockSpec((D_MODEL, IN_COLS), lambda i: (0, 0))],
        out_specs=[pl.BlockSpec((tile, IN_COLS), lambda i: (i, 0)), pl.BlockSpec((tile, D_MODEL), lambda i: (i, 0))],
        out_shape=[jax.ShapeDtypeStruct((tokens, IN_COLS), f32), jax.ShapeDtypeStruct((tokens, D_MODEL), bf16)],
        args=(x2, w_in_b), jobs=jobs)


def _small_specs():
    return [pl.BlockSpec((N_GROUPS, GROUP, GROUP), lambda i: (0, 0, 0)),
            pl.BlockSpec((1, POOL_WIDTH), lambda i: (0, 0)),
            pl.BlockSpec((1, SGU_WIDTH), lambda i: (0, 0)),
            pl.BlockSpec((1, SGU_WIDTH), lambda i: (0, 0)),
            pl.BlockSpec((N_GROUPS, GROUP, GROUP), lambda i: (0, 0, 0)),
            pl.BlockSpec((N_GROUPS, GROUP, GROUP), lambda i: (0, 0, 0))]


def _fwd_mix(proj, x2, small, w_out_b, ln1_g, ln1_b, tile, seq, jobs):
    tokens = x2.shape[0]
    tps = seq // tile
    hb = tile // HALO

    def body(proj_ref, halo_ref, x_ref, pw_ref, ps_ref, lg_ref, lb_ref, sw_ref, sb_ref, wout_ref, g1_ref, b1_ref,
             mix_ref, r1_ref, h_ref):
        seq_tile = pl.program_id(0) % tps
        xp = proj_ref[:, :POOL_WIDTH]
        halo = jnp.where(seq_tile == 0, 0.0, halo_ref[...])
        pooled = _pooled_groups(xp, halo, _inv_counts(seq_tile, tile))
        for g in range(N_GROUPS):
            po = _mm(pooled[g].astype(bf16), pw_ref[g].astype(bf16)) * ps_ref[:, _gs(g)]
            mix_ref[:, _gs(g)] = po.astype(bf16)

        cdf, _ = _gelu_parts(proj_ref[:, POOL_WIDTH:])
        zg = proj_ref[:, POOL_WIDTH:] * cdf
        u = zg[:, :SGU_WIDTH]
        vln, _, _ = _ln_fwd(zg[:, SGU_WIDTH:], lg_ref[...], lb_ref[...])
        vb = vln.astype(bf16)
        mask = _tril_mask()
        for h in range(N_GROUPS):
            wm = (sw_ref[h] * mask).astype(bf16)
            bias = sb_ref[h]
            for c in range(tile // GROUP):
                rows = slice(c * GROUP, (c + 1) * GROUP)
                mixed = _mm(wm, vb[rows, _gs(h)]) + bias
                mix_ref[rows, POOL_WIDTH + h * GROUP:POOL_WIDTH + (h + 1) * GROUP] = (u[rows, _gs(h)] * mixed).astype(bf16)

        r1 = ALPHA * x_ref[...] + _mm(mix_ref[...], wout_ref[...])
        r1_ref[...] = r1
        h1, _, _ = _ln_fwd(r1, g1_ref[...], b1_ref[...])
        h_ref[...] = h1.astype(bf16)

    row = lambda i: (i, 0)
    vec = pl.BlockSpec((1, D_MODEL), lambda i: (0, 0))
    return _call(
        body, name="fwd_mix", grid=(tokens // tile,),
        in_specs=[pl.BlockSpec((tile, IN_COLS), row),
                  pl.BlockSpec((HALO, POOL_WIDTH), lambda i: (jnp.maximum(i * hb - 1, 0), 0)),
                  pl.BlockSpec((tile, D_MODEL), row)] + _small_specs()
                 + [pl.BlockSpec((D_MODEL, D_MODEL), lambda i: (0, 0)), vec, vec],
        out_specs=[pl.BlockSpec((tile, D_MODEL), row)] * 3,
        out_shape=[jax.ShapeDtypeStruct((tokens, D_MODEL), bf16),
                   jax.ShapeDtypeStruct((tokens, D_MODEL), f32),
                   jax.ShapeDtypeStruct((tokens, D_MODEL), bf16)],
        args=(proj, proj, x2, *small, w_out_b, ln1_g, ln1_b), jobs=jobs)


def _fwd_gate_up(h_b, w_gu_b, tile, jobs):
    tokens = h_b.shape[0]

    def body(h_ref, w_ref, gu_ref, a_ref):
        hb = h_ref[...]
        for c in range(D_FF // FF_CHUNK):
            gcols = slice(c * FF_CHUNK, (c + 1) * FF_CHUNK)
            ucols = slice(D_FF + c * FF_CHUNK, D_FF + (c + 1) * FF_CHUNK)
            gate = _mm(hb, w_ref[:, gcols])
            up = _mm(hb, w_ref[:, ucols])
            sg = jax.nn.sigmoid(gate)
            silu = gate * sg
            gu_ref[:, gcols] = (up * (sg + silu * (1.0 - sg))).astype(bf16)
            gu_ref[:, ucols] = silu.astype(bf16)
            a_ref[:, gcols] = (silu * up).astype(bf16)

    return _call(
        body, name="fwd_gate_up", grid=(tokens // tile,),
        in_specs=[pl.BlockSpec((tile, D_MODEL), lambda i: (i, 0)),
                  pl.BlockSpec((D_MODEL, 2 * D_FF), lambda i: (0, 0), pipeline_mode=pl.Buffered(1))],
        out_specs=[pl.BlockSpec((tile, 2 * D_FF), lambda i: (i, 0)),
                   pl.BlockSpec((tile, D_FF), lambda i: (i, 0))],
        out_shape=[jax.ShapeDtypeStruct((tokens, 2 * D_FF), bf16),
                   jax.ShapeDtypeStruct((tokens, D_FF), bf16)],
        args=(h_b, w_gu_b), jobs=jobs)


def _fwd_down_loss(a_b, w_dn_b, r1, target, ln1_g, ln1_b, ln2_g, ln2_b, tile):
    tokens = a_b.shape[0]

    def body(a_ref, w_ref, r1_ref, t_ref, g1_ref, b1_ref, g2_ref, b2_ref, dr2_ref, dr2b_ref, st_ref):
        @pl.when(pl.program_id(0) == 0)
        def _():
            st_ref[...] = jnp.zeros_like(st_ref)

        sub = min(ROW_SUB, tile)
        for s in range(tile // sub):
            rows = slice(s * sub, (s + 1) * sub)
            h1, _, _ = _ln_fwd(r1_ref[rows, :], g1_ref[...], b1_ref[...])
            r2 = ALPHA * h1 + _mm(a_ref[rows, :], w_ref[...])
            y, xhat, rstd = _ln_fwd(r2, g2_ref[...], b2_ref[...])
            diff = y - t_ref[rows, :]
            dy = diff * (1.0 / D_MODEL)
            st_ref[0:1, :] += _col_sum(dy * xhat)
            st_ref[1:2, :] += _col_sum(dy)
            st_ref[2:3, :] += _col_sum(diff * diff)
            dr2 = _ln_bwd(dy, xhat, rstd, g2_ref[...])
            dr2_ref[rows, :] = dr2
            dr2b_ref[rows, :] = dr2.astype(bf16)

    row = lambda i: (i, 0)
    vec = pl.BlockSpec((1, D_MODEL), lambda i: (0, 0))
    return _call(
        body, name="fwd_down_loss", grid=(tokens // tile,),
        in_specs=[pl.BlockSpec((tile, D_FF), row), pl.BlockSpec((D_FF, D_MODEL), lambda i: (0, 0)),
                  pl.BlockSpec((tile, D_MODEL), row), pl.BlockSpec((tile, D_MODEL), row), vec, vec, vec, vec],
        out_specs=[pl.BlockSpec((tile, D_MODEL), row), pl.BlockSpec((tile, D_MODEL), row),
                   pl.BlockSpec((8, D_MODEL), lambda i: (0, 0))],
        out_shape=[jax.ShapeDtypeStruct((tokens, D_MODEL), f32), jax.ShapeDtypeStruct((tokens, D_MODEL), bf16),
                   jax.ShapeDtypeStruct((8, D_MODEL), f32)],
        args=(a_b, w_dn_b, r1, target, ln1_g, ln1_b, ln2_g, ln2_b))[0]


def _bwd_gate_up(dr2, gu_b, w_dn_b, tile, jobs):
    tokens = dr2.shape[0]

    def body(d_ref, gu_ref, w_ref, dgu_ref):
        d = d_ref[...].astype(bf16)
        for c in range(D_FF // FF_CHUNK):
            gcols = slice(c * FF_CHUNK, (c + 1) * FF_CHUNK)
            ucols = slice(D_FF + c * FF_CHUNK, D_FF + (c + 1) * FF_CHUNK)
            da = _mm_nt(d, w_ref[gcols, :])
            dgu_ref[:, gcols] = (da * gu_ref[:, gcols].astype(f32)).astype(bf16)
            dgu_ref[:, ucols] = (da * gu_ref[:, ucols].astype(f32)).astype(bf16)

    return _call(
        body, name="bwd_gate_up", grid=(tokens // tile,),
        in_specs=[pl.BlockSpec((tile, D_MODEL), lambda i: (i, 0)),
                  pl.BlockSpec((tile, 2 * D_FF), lambda i: (i, 0)),
                  pl.BlockSpec((D_FF, D_MODEL), lambda i: (0, 0))],
        out_specs=[pl.BlockSpec((tile, 2 * D_FF), lambda i: (i, 0))],
        out_shape=[jax.ShapeDtypeStruct((tokens, 2 * D_FF), bf16)],
        args=(dr2, gu_b, w_dn_b), jobs=jobs)


def _bwd_ffn_in(dgu_b, w_gu_b, dr2, r1, ln1_g, ln1_b, tile, jobs):
    tokens = dr2.shape[0]

    def body(dgu_ref, w_ref, d_ref, r1_ref, g1_ref, b1_ref, dr1_ref, dr1b_ref, st_ref):
        @pl.when(pl.program_id(0) == 0)
        def _():
            st_ref[...] = jnp.zeros_like(st_ref)

        dh = ALPHA * d_ref[...] + _mm_nt(dgu_ref[...], w_ref[...])
        _, xhat, rstd = _ln_fwd(r1_ref[...], g1_ref[...], b1_ref[...])
        st_ref[0:1, :] += _col_sum(dh * xhat)
        st_ref[1:2, :] += _col_sum(dh)
        dr1 = _ln_bwd(dh, xhat, rstd, g1_ref[...])
        dr1_ref[...] = dr1
        dr1b_ref[...] = dr1.astype(bf16)

    row = lambda i: (i, 0)
    vec = pl.BlockSpec((1, D_MODEL), lambda i: (0, 0))
    return _call(
        body, name="bwd_ffn_in", grid=(tokens // tile,),
        in_specs=[pl.BlockSpec((tile, 2 * D_FF), row),
                  pl.BlockSpec((D_MODEL, 2 * D_FF), lambda i: (0, 0), pipeline_mode=pl.Buffered(1)),
                  pl.BlockSpec((tile, D_MODEL), row), pl.BlockSpec((tile, D_MODEL), row), vec, vec],
        out_specs=[pl.BlockSpec((tile, D_MODEL), row), pl.BlockSpec((tile, D_MODEL), row),
                   pl.BlockSpec((8, D_MODEL), lambda i: (0, 0))],
        out_shape=[jax.ShapeDtypeStruct((tokens, D_MODEL), f32), jax.ShapeDtypeStruct((tokens, D_MODEL), bf16),
                   jax.ShapeDtypeStruct((8, D_MODEL), f32)],
        args=(dgu_b, w_gu_b, dr2, r1, ln1_g, ln1_b), jobs=jobs)


def _bwd_mix(dr1, proj, small, w_out_b, tile, seq, jobs):
    tokens = dr1.shape[0]
    tps = seq // tile
    hb = tile // HALO
    steps = tokens // tile

    def body(dr1_ref, proj_ref, halo_ref, wout_ref, pw_ref, ps_ref, lg_ref, lb_ref, sw_ref, sb_ref,
             dpool_ref, dpre_ref, gpw_ref, gsw_ref, gsb_ref, vec_ref, du_ref, dvln_ref):
        step = pl.program_id(0)
        seq_tile = step % tps

        @pl.when(step == 0)
        def _():
            gpw_ref[...] = jnp.zeros_like(gpw_ref)
            gsw_ref[...] = jnp.zeros_like(gsw_ref)
            gsb_ref[...] = jnp.zeros_like(gsb_ref)
            vec_ref[...] = jnp.zeros_like(vec_ref)

        dmix = _mm_nt(dr1_ref[...].astype(bf16), wout_ref[...])

        xp = proj_ref[:, :POOL_WIDTH]
        halo = jnp.where(seq_tile == 0, 0.0, halo_ref[...])
        pooled = _pooled_groups(xp, halo, _inv_counts(seq_tile, tile))
        for g in range(N_GROUPS):
            pb = pooled[g].astype(bf16)
            pwb = pw_ref[g].astype(bf16)
            dpo = dmix[:, _gs(g)]
            vec_ref[0:1, _gs(g)] += _col_sum(dpo * _mm(pb, pwb))
            dpo_b = (dpo * ps_ref[:, _gs(g)]).astype(bf16)
            gpw_ref[g] += _mm_tn(pb, dpo_b)
            dpool_ref[:, _gs(g)] = _mm_nt(dpo_b, pwb)

        pre = proj_ref[:, POOL_WIDTH:]
        cdf, pdf = _gelu_parts(pre)
        zg = pre * cdf
        u = zg[:, :SGU_WIDTH]
        vln, vhat, rstd = _ln_fwd(zg[:, SGU_WIDTH:], lg_ref[...], lb_ref[...])
        vb = vln.astype(bf16)
        mask = _tril_mask()
        for h in range(N_GROUPS):
            wm = (sw_ref[h] * mask).astype(bf16)
            bias = sb_ref[h]
            gsw = jnp.zeros((GROUP, GROUP), f32)
            gsb = jnp.zeros((GROUP, GROUP), f32)
            for c in range(tile // GROUP):
                rows = slice(c * GROUP, (c + 1) * GROUP)
                v_ch = vb[rows, _gs(h)]
                d = dmix[rows, POOL_WIDTH + h * GROUP:POOL_WIDTH + (h + 1) * GROUP]
                du_ref[rows, _gs(h)] = d * (_mm(wm, v_ch) + bias)
                dmixed = d * u[rows, _gs(h)]
                gsb += dmixed
                dmixed_b = dmixed.astype(bf16)
                gsw += _mm_nt(dmixed_b, v_ch)
                dvln_ref[rows, _gs(h)] = _mm_tn(wm, dmixed_b)
            gsw_ref[h] += gsw * mask
            gsb_ref[h] += gsb

        dvln = dvln_ref[...]
        vec_ref[1:2, :] += _col_sum(dvln * vhat)
        vec_ref[2:3, :] += _col_sum(dvln)
        dgelu = cdf + pre * pdf
        dpre_ref[:, :SGU_WIDTH] = (du_ref[...] * dgelu[:, :SGU_WIDTH]).astype(bf16)
        dpre_ref[:, SGU_WIDTH:] = (_ln_bwd(dvln, vhat, rstd, lg_ref[...]) * dgelu[:, SGU_WIDTH:]).astype(bf16)

        @pl.when(step == steps - 1)
        def _():
            for h in range(N_GROUPS):
                gsb_ref[h] = jnp.broadcast_to(jnp.sum(gsb_ref[h], axis=1, keepdims=True), (GROUP, GROUP))

    row = lambda i: (i, 0)
    sq = jax.ShapeDtypeStruct((N_GROUPS, GROUP, GROUP), f32)
    sq_spec = pl.BlockSpec((N_GROUPS, GROUP, GROUP), lambda i: (0, 0, 0))
    return _call(
        body, name="bwd_mix", grid=(steps,),
        in_specs=[pl.BlockSpec((tile, D_MODEL), row), pl.BlockSpec((tile, IN_COLS), row),
                  pl.BlockSpec((HALO, POOL_WIDTH), lambda i: (jnp.maximum(i * hb - 1, 0), 0)),
                  pl.BlockSpec((D_MODEL, D_MODEL), lambda i: (0, 0))] + _small_specs(),
        out_specs=[pl.BlockSpec((tile, POOL_WIDTH), row), pl.BlockSpec((tile, 2 * SGU_WIDTH), row),
                   sq_spec, sq_spec, sq_spec, pl.BlockSpec((8, POOL_WIDTH), lambda i: (0, 0))],
        out_shape=[jax.ShapeDtypeStruct((tokens, POOL_WIDTH), f32), jax.ShapeDtypeStruct((tokens, 2 * SGU_WIDTH), bf16),
                   sq, sq, sq, jax.ShapeDtypeStruct((8, POOL_WIDTH), f32)],
        scratch_shapes=[pltpu.VMEM((tile, SGU_WIDTH), f32), pltpu.VMEM((tile, SGU_WIDTH), f32)],
        args=(dr1, proj, proj, w_out_b, *small), jobs=jobs)


def _bwd_in(dpool, dpre_b, dr1, w_in_b, tile, seq):
    tokens = dr1.shape[0]
    tps = seq // tile
    hb = tile // HALO
    last_halo = tokens // HALO - 1

    def body(dpool_ref, nxt_ref, dpre_ref, dr1_ref, w_ref, dx_ref, dproj_ref):
        seq_tile = pl.program_id(0) % tps
        inv = _inv_counts(seq_tile, tile)
        dpl = dpool_ref[...]
        nxt = jnp.where(seq_tile == tps - 1, 0.0, nxt_ref[...])
        scaled = jnp.concatenate([dpl[:, _gs(g)] * inv[g] for g in range(N_GROUPS)], axis=1)
        scaled_nxt = jnp.concatenate([nxt[:, _gs(g)] * (1.0 / POOL_WINDOWS[g]) for g in range(N_GROUPS)], axis=1)
        sums = _window_sums(jnp.concatenate([scaled, scaled_nxt], axis=0), back=False)
        for g in range(N_GROUPS):
            dproj_ref[:, _gs(g)] = (sums[g][:tile] - dpl[:, _gs(g)]).astype(bf16)
        dproj_ref[:, POOL_WIDTH:] = dpre_ref[...]
        dx_ref[...] = ALPHA * dr1_ref[...] + _mm_nt(dproj_ref[...], w_ref[...])

    row = lambda i: (i, 0)
    return _call(
        body, name="bwd_in", grid=(tokens // tile,),
        in_specs=[pl.BlockSpec((tile, POOL_WIDTH), row),
                  pl.BlockSpec((HALO, POOL_WIDTH), lambda i: (jnp.minimum((i + 1) * hb, last_halo), 0)),
                  pl.BlockSpec((tile, 2 * SGU_WIDTH), row),
                  pl.BlockSpec((tile, D_MODEL), row),
                  pl.BlockSpec((D_MODEL, IN_COLS), lambda i: (0, 0))],
        out_specs=[pl.BlockSpec((tile, D_MODEL), row), pl.BlockSpec((tile, IN_COLS), row)],
        out_shape=[jax.ShapeDtypeStruct((tokens, D_MODEL), f32), jax.ShapeDtypeStruct((tokens, IN_COLS), bf16)],
        args=(dpool, dpool, dpre_b, dr1, w_in_b))[0]


def _wgrad(a, b, col_tile, tile, name, jobs=()):
    tokens, m = a.shape
    n = b.shape[1]

    def body(a_ref, b_ref, o_ref):
        @pl.when(pl.program_id(1) == 0)
        def _():
            o_ref[...] = jnp.zeros_like(o_ref)

        o_ref[...] += _mm_tn(a_ref[...].astype(bf16), b_ref[...].astype(bf16))

    (out,), got = _call(
        body, name=name, grid=(n // col_tile, tokens // tile),
        in_specs=[pl.BlockSpec((tile, m), lambda j, k: (k, 0)),
                  pl.BlockSpec((tile, col_tile), lambda j, k: (k, j))],
        out_specs=[pl.BlockSpec((m, col_tile), lambda j, k: (0, j))],
        out_shape=[jax.ShapeDtypeStruct((m, n), f32)],
        args=(a, b), jobs=jobs)
    return out, got


def _add_halves(name, g, got, core):
    cut = CUTS[name]
    br, bc = cut.block_shape
    wire = _wire_dtype(name)

    def body(core_ref, g_ref, got_ref, o_ref, wire_ref):
        s = g_ref[...] + got_ref[...]
        o_ref[...] = s
        wire_ref[...] = s.astype(wire)

    blocks = pl.BlockSpec((None, br, bc), lambda j, core_ref: (j, 0, 0))
    return pl.pallas_call(
        body, name="reduce_add_halves_" + name,
        grid_spec=pltpu.PrefetchScalarGridSpec(
            num_scalar_prefetch=1, grid=(N_CHIPS,),
            in_specs=[pl.BlockSpec((br, bc), lambda j, core_ref: cut.block_index(j, core_ref[0])), blocks],
            out_specs=[blocks, blocks]),
        out_shape=[jax.ShapeDtypeStruct((N_CHIPS, br, bc), f32), jax.ShapeDtypeStruct((N_CHIPS, br, bc), wire)],
        compiler_params=_params(),
    )(core, g, got)


def _reduce_tail(late, early):
    late_names, early_names = tuple(late), tuple(early)
    names = early_names + late_names
    nl, ne, n = len(late_names), len(early_names), len(names)
    cuts = [CUTS[name] for name in names]
    is_big = [name in BIG for name in names]
    share_base, share_sem = 7 * nl, []
    for i in range(n):
        share_sem.append(share_base)
        share_base += 1 if is_big[i] else 7
    n_in = nl + 2 * ne

    def body(*refs):
        g_refs = refs[:nl]
        sums_refs, got_refs = refs[nl:n_in:2], refs[nl + 1:n_in:2]
        out_refs = refs[n_in:n_in + n]
        send_sems, recv_sems, local_sems = refs[n_in + n:n_in + n + 3]
        vm = refs[n_in + n + 3:]
        own, recv_a, wire, recv_b = vm[0:4 * nl:4], vm[1:4 * nl:4], vm[2:4 * nl:4], vm[3:4 * nl:4]
        acc, gotv = vm[4 * nl::2], vm[4 * nl + 1::2]
        x, y, c, me, others, other_ids = _place()
        sibling = (x, y, 1 - c)

        def mine(i, half):
            return cuts[i].half_of_shard(out_refs[i], half) if is_big[i] else cuts[i].block(out_refs[i], me, half)

        def reduced(i):
            return acc[i] if i < ne else own[i - ne].at[me]

        sent, stores = [], []

        def share(i):
            store = pltpu.make_async_copy(reduced(i), mine(i, c), local_sems.at[4 * nl + 2 * ne + i])
            store.start()
            stores.append(store)
            to = [sibling] if is_big[i] else [sibling] + [(*chip, c) for chip in others]
            for which, device in enumerate(to):
                cp = _remote(reduced(i), mine(i, c), send_sems, recv_sems, share_sem[i] + which, device)
                cp.start()
                sent.append(cp)

        early_loads = []
        for e in range(ne):
            early_loads.append(pltpu.make_async_copy(sums_refs[e].at[me], acc[e], local_sems.at[4 * nl + 2 * e]))
            early_loads.append(pltpu.make_async_copy(got_refs[e], gotv[e], local_sems.at[4 * nl + 2 * e + 1]))
        late_loads = [pltpu.make_async_copy(cuts[ne + l].block(g_refs[l], j, c), own[l].at[j], local_sems.at[4 * l + j])
                      for l in range(nl) for j in range(N_CHIPS)]
        halves = [_remote(cuts[ne + l].block(g_refs[l], j, 1 - c), recv_a[l].at[j], send_sems, recv_sems, 4 * l + j, sibling)
                  for l in range(nl) for j in range(N_CHIPS)]
        for cp in early_loads + late_loads + halves:
            cp.start()

        for cp in early_loads:
            cp.wait()
        for e in range(ne):
            acc[e][...] = ((acc[e][...] + gotv[e][0].astype(f32)) + gotv[e][1].astype(f32)) + gotv[e][2].astype(f32)
            share(e)

        for cp in late_loads:
            cp.wait()
        for cp in halves:
            cp.wait_recv()
        for l in range(nl):
            for j in range(N_CHIPS):
                s = own[l][j] + recv_a[l][j]
                own[l][j] = s
                wire[l][j] = s.astype(wire[l].dtype)
        chips = [_remote(wire[l].at[other_ids[k]], recv_b[l].at[k], send_sems, recv_sems, 4 * nl + 3 * l + k, (*chip, c))
                 for l in range(nl) for k, chip in enumerate(others)]
        for cp in chips:
            cp.start()
        for cp in chips:
            cp.wait_recv()
        for l in range(nl):
            mine_l = own[l].at[me]
            mine_l[...] = ((mine_l[...] + recv_b[l][0].astype(f32)) + recv_b[l][1].astype(f32)) + recv_b[l][2].astype(f32)
            share(ne + l)

        for i in range(n):
            if not is_big[i]:
                for k, chip in enumerate(others):
                    landed = cuts[i].block(out_refs[i], other_ids[k], c)
                    _remote(landed, landed, send_sems, recv_sems, share_sem[i] + 1 + k, (*chip, c)).wait_recv()
                    cp = _remote(landed, landed, send_sems, recv_sems, share_sem[i] + 4 + k, sibling)
                    cp.start()
                    sent.append(cp)
        for i in range(n):
            theirs = mine(i, 1 - c)
            _remote(theirs, theirs, send_sems, recv_sems, share_sem[i], sibling).wait_recv()
            if not is_big[i]:
                for k in range(3):
                    passed = cuts[i].block(out_refs[i], other_ids[k], 1 - c)
                    _remote(passed, passed, send_sems, recv_sems, share_sem[i] + 4 + k, sibling).wait_recv()
        for cp in halves + chips + sent:
            cp.wait_send()
        for cp in stores:
            cp.wait()

    scratch = [pltpu.SemaphoreType.DMA((share_base,)), pltpu.SemaphoreType.DMA((share_base,)),
               pltpu.SemaphoreType.DMA((4 * nl + 2 * ne + n,))]
    for name in late_names:
        block = CUTS[name].block_shape
        scratch += [pltpu.VMEM((N_CHIPS, *block), f32), pltpu.VMEM((N_CHIPS, *block), f32),
                    pltpu.VMEM((N_CHIPS, *block), _wire_dtype(name)), pltpu.VMEM((3, *block), _wire_dtype(name))]
    for name in early_names:
        block = CUTS[name].block_shape
        scratch += [pltpu.VMEM(block, f32), pltpu.VMEM((3, *block), _wire_dtype(name))]
    args = [late[name] for name in late_names] + [a for name in early_names for a in early[name]]
    outs = pl.pallas_call(
        body, name="reduce_tail",
        in_specs=[ANY] * n_in, out_specs=[ANY] * n,
        out_shape=[jax.ShapeDtypeStruct(CUTS[name].shard_shape if name in BIG else (CUTS[name].rows, CUTS[name].cols), f32)
                   for name in names],
        scratch_shapes=scratch, compiler_params=_params(),
    )(*args)
    return dict(zip(names, outs))


def _adamw(name, w, g, m, v):
    rows, cols = w.shape
    rt = rows // 4

    def body(w_ref, g_ref, m_ref, v_ref, d_ref, nm_ref, nv_ref):
        g = g_ref[...]
        nm = ADAM_B1 * m_ref[...] + (1.0 - ADAM_B1) * g
        nv = ADAM_B2 * v_ref[...] + (1.0 - ADAM_B2) * jnp.square(g)
        m_hat = nm / (1.0 - ADAM_B1 ** ADAM_STEP)
        v_hat = nv / (1.0 - ADAM_B2 ** ADAM_STEP)
        d_ref[...] = -ADAM_LR * (m_hat / (jnp.sqrt(v_hat) + ADAM_EPS) + ADAM_WD * w_ref[...])
        nm_ref[...] = nm
        nv_ref[...] = nv

    spec = pl.BlockSpec((rt, cols), lambda i: (i, 0))
    shape = jax.ShapeDtypeStruct((rows, cols), f32)
    return pl.pallas_call(
        body, name="adamw_" + name, grid=(rows // rt,),
        in_specs=[spec] * 4, out_specs=[spec] * 3, out_shape=[shape] * 3,
        compiler_params=_params(),
    )(w, g, m, v)


VEC_NAMES = ("pool_scale", "sgu_ln_g", "sgu_ln_b", "sgu_b", "ln1_g", "ln1_b", "ln2_g", "ln2_b")
WEIGHT_ORDER = ("w_in", "pool_w", "pool_scale", "sgu_ln_g", "sgu_ln_b", "sgu_w", "sgu_b", "w_out", "ln1_g", "ln1_b",
                "w_gate_up", "w_down", "ln2_g", "ln2_b")


def _pack_vecs(parts, extra=None):
    rows = [parts[name].reshape(-1, GROUP) for name in VEC_NAMES]
    if extra is not None:
        rows.append(extra.reshape(-1, GROUP))
    used = sum(r.shape[0] for r in rows)
    return jnp.concatenate(rows + [jnp.zeros((VEC_ROWS - used, GROUP), f32)], axis=0)


def _unpack_vecs(packed, shapes):
    out, at = {}, 0
    for name in VEC_NAMES:
        rows = math.prod(shapes[name]) // GROUP
        out[name] = packed[at:at + rows].reshape(shapes[name])
        at += rows
    return out, packed[at:]


def kernel(x, w_in, pool_w, pool_scale, sgu_ln_g, sgu_ln_b, sgu_w, sgu_b, w_out, ln1_g, ln1_b, w_gate_up, w_down, ln2_g, ln2_b, loss_target, m_w_in, m_pool_w, m_pool_scale, m_sgu_ln_g, m_sgu_ln_b, m_sgu_w, m_sgu_b, m_w_out, m_ln1_g, m_ln1_b, m_w_gate_up, m_w_down, m_ln2_g, m_ln2_b, v_w_in, v_pool_w, v_pool_scale, v_sgu_ln_g, v_sgu_ln_b, v_sgu_w, v_sgu_b, v_w_out, v_ln1_g, v_ln1_b, v_w_gate_up, v_w_down, v_ln2_g, v_ln2_b):
    given = dict(locals())
    batch, seq, _ = x.shape
    tokens = batch * seq
    tile = min(TOKEN_TILE, seq)
    ffn_bwd_tile = min(FFN_BWD_TILE, seq)
    wtile = min(WGRAD_TILE, tokens)
    shapes = {name: given[name].shape for name in WEIGHT_ORDER}

    x2 = x.reshape(tokens, D_MODEL)
    target = loss_target.reshape(tokens, D_MODEL)
    small = (pool_w[0], pool_scale[0][None], sgu_ln_g[0][None], sgu_ln_b[0][None], sgu_w[0],
             jnp.broadcast_to(sgu_b[0][:, :, None], (N_GROUPS, GROUP, GROUP)))
    g1, b1, g2, b2 = ln1_g[0][None], ln1_b[0][None], ln2_g[0][None], ln2_b[0][None]
    shard_b = {name: given[name][0].astype(bf16) for name in BIG}
    core = lax.axis_index("c").astype(jnp.int32).reshape(1)

    def gather(name):
        return [_GatherJob({name: shard_b[name]})]

    def halves_summed(name, grad, got):
        return _add_halves(name, grad, got, core)

    w_in_b = _alone(gather("w_in")[0], "gather_w_in")["w_in"]
    (proj, x_b), (got,) = _fwd_proj(x2, w_in_b, tile, gather("w_out"))
    w_out_b = got["w_out"]
    (mix_b, r1, h_b), (got,) = _fwd_mix(proj, x2, small, w_out_b, g1, b1, tile, seq, gather("w_gate_up"))
    w_gu_b = got["w_gate_up"]
    (gu_b, a_b), (got,) = _fwd_gate_up(h_b, w_gu_b, tile, gather("w_down"))
    w_dn_b = got["w_down"]
    dr2, dr2_b, stats2 = _fwd_down_loss(a_b, w_dn_b, r1, target, g1, b1, g2, b2, tile)

    early = {}
    g_down, _ = _wgrad(a_b, dr2_b, D_MODEL, wtile, "wgrad_down")
    (dgu_b,), (got,) = _bwd_gate_up(dr2_b, gu_b, w_dn_b, ffn_bwd_tile, [_SwapHalvesJob({"w_down": g_down})])
    sums_down = halves_summed("w_down", g_down, got["w_down"])
    g_gu, (got,) = _wgrad(h_b, dgu_b, D_FF, wtile, "wgrad_gate_up", [_SwapChipsJob({"w_down": sums_down[1]})])
    early["w_down"] = (sums_down[0], got["w_down"])
    (dr1, dr1_b, stats1), (got,) = _bwd_ffn_in(dgu_b, w_gu_b, dr2, r1, g1, b1, tile,
                                               [_SwapHalvesJob({"w_gate_up": g_gu})])
    sums_gu = halves_summed("w_gate_up", g_gu, got["w_gate_up"])
    g_out, _ = _wgrad(mix_b, dr1_b, D_MODEL, wtile, "wgrad_out")
    (dpool, dpre_b, g_pool_w, g_sgu_w, g_sgu_b, vecs), (got_gu, got_out) = _bwd_mix(
        dr1_b, proj, small, w_out_b, tile, seq,
        [_SwapChipsJob({"w_gate_up": sums_gu[1]}), _SwapHalvesJob({"w_out": g_out})])
    early["w_gate_up"] = (sums_gu[0], got_gu["w_gate_up"])
    sums_out = halves_summed("w_out", g_out, got_out["w_out"])
    grad_x, dproj_b = _bwd_in(dpool, dpre_b, dr1, w_in_b, tile, seq)
    g_in, (got,) = _wgrad(x_b, dproj_b, IN_COLS, wtile, "wgrad_in", [_SwapChipsJob({"w_out": sums_out[1]})])
    early["w_out"] = (sums_out[0], got["w_out"])

    late = {
        "w_in": g_in,
        "pool_w": g_pool_w.reshape(SQUARE_ROWS, GROUP),
        "sgu_w": g_sgu_w.reshape(SQUARE_ROWS, GROUP),
        "vecs": _pack_vecs({"pool_scale": vecs[0], "sgu_ln_g": vecs[1], "sgu_ln_b": vecs[2], "sgu_b": g_sgu_b[:, :, 0],
                            "ln1_g": stats1[0], "ln1_b": stats1[1], "ln2_g": stats2[0], "ln2_b": stats2[1]},
                           extra=stats2[2]),
    }
    shared = _reduce_tail(late, early)

    grad, delta, new_m, new_v = {}, {}, {}, {}
    for name in BIG:
        grad[name] = shared[name][None]
        d, nm, nv = _adamw(name, given[name][0], shared[name], given["m_" + name][0], given["v_" + name][0])
        delta[name], new_m[name], new_v[name] = d[None], nm[None], nv[None]
    for name in ("pool_w", "sgu_w"):
        flat = [given[pre + name].reshape(SQUARE_ROWS, GROUP) for pre in ("", "m_", "v_")]
        d, nm, nv = _adamw(name, flat[0], shared[name], flat[1], flat[2])
        for out, rows in ((grad, shared[name]), (delta, d), (new_m, nm), (new_v, nv)):
            out[name] = rows.reshape(shapes[name])
    packed = [_pack_vecs({name: given[pre + name] for name in VEC_NAMES}) for pre in ("", "m_", "v_")]
    d, nm, nv = _adamw("vecs", packed[0], shared["vecs"], packed[1], packed[2])
    for out, rows in ((grad, shared["vecs"]), (delta, d), (new_m, nm), (new_v, nv)):
        out.update(_unpack_vecs(rows, shapes)[0])

    sq_err = _unpack_vecs(shared["vecs"], shapes)[1][:LOSS_ROWS]
    loss = jnp.sum(sq_err) * (0.5 / D_MODEL)
    return (loss, grad_x.reshape(x.shape), *[grad[name] for name in WEIGHT_ORDER],
            *[delta[name] for name in WEIGHT_ORDER], *[new_m[name] for name in WEIGHT_ORDER],
            *[new_v[name] for name in WEIGHT_ORDER])
```

```python
import math

import jax
import jax.numpy as jnp
from jax import lax
from jax.experimental import pallas as pl
from jax.experimental.pallas import tpu as pltpu

f32 = jnp.float32
bf16 = jnp.bfloat16
MESH = pl.DeviceIdType.MESH

D_MODEL = 1024
POOL_WIDTH = 512
SGU_WIDTH = 512
IN_COLS = POOL_WIDTH + 2 * SGU_WIDTH
D_FF = 2816
POOL_WINDOWS = (2, 4, 8, 16)
GROUP = 128
N_GROUPS = 4
HALO = 16
LN_EPS = 1e-5
ALPHA = float(2.0 ** 0.25)
N_CHIPS = 4

ADAM_LR = 0.001
ADAM_B1 = 0.9
ADAM_B2 = 0.999
ADAM_EPS = 1e-08
ADAM_WD = 0.01
ADAM_STEP = 10

TOKEN_TILE = 512
FFN_BWD_TILE = 512
FF_CHUNK = 256
ROW_SUB = 256
WGRAD_TILE = 1024
V7X_VMEM_LIMIT = 56 * 1024 * 1024

SQUARE_ROWS = N_GROUPS * GROUP
VEC_ROWS = 64
LOSS_ROWS = D_MODEL // GROUP


def _params(**kw):
    return pltpu.CompilerParams(vmem_limit_bytes=V7X_VMEM_LIMIT, **kw)


def _mm(a, b):
    return jnp.dot(a, b, preferred_element_type=f32)


def _mm_nt(a, b):
    return lax.dot_general(a, b, (((1,), (1,)), ((), ())), preferred_element_type=f32)


def _mm_tn(a, b):
    return lax.dot_general(a, b, (((0,), (0,)), ((), ())), preferred_element_type=f32)


def _ln_fwd(r, g, b):
    mu = jnp.mean(r, axis=-1, keepdims=True)
    xc = r - mu
    var = jnp.mean(xc * xc, axis=-1, keepdims=True)
    rstd = lax.rsqrt(var + LN_EPS)
    xhat = xc * rstd
    return xhat * g + b, xhat, rstd


def _ln_bwd(dout, xhat, rstd, g):
    dxhat = dout * g
    m1 = jnp.mean(dxhat, axis=-1, keepdims=True)
    m2 = jnp.mean(dxhat * xhat, axis=-1, keepdims=True)
    return rstd * (dxhat - m1 - xhat * m2)


def _col_sum(a):
    return jnp.sum(a, axis=0, keepdims=True)


def _gelu_parts(z):
    cdf = 0.5 * (1.0 + lax.erf(z * (1.0 / math.sqrt(2.0))))
    pdf = jnp.exp(-0.5 * z * z) * (1.0 / math.sqrt(2.0 * math.pi))
    return cdf, pdf


def _inv_counts(seq_tile, rows):
    pos = seq_tile * rows + lax.broadcasted_iota(jnp.int32, (rows, GROUP), 0) + 1
    return [1.0 / jnp.minimum(pos, w).astype(f32) for w in POOL_WINDOWS]


def _window_sums(e, back):
    n = e.shape[0]

    def shifted(a, s):
        return pltpu.roll(a, s if back else n - s, 0)

    s2 = e + shifted(e, 1)
    s4 = s2[:, GROUP:] + shifted(s2[:, GROUP:], 2)
    s8 = s4[:, GROUP:] + shifted(s4[:, GROUP:], 4)
    s16 = s8[:, GROUP:] + shifted(s8[:, GROUP:], 8)
    return [s2[:, :GROUP], s4[:, :GROUP], s8[:, :GROUP], s16]


def _pooled_groups(xp, halo, inv):
    sums = _window_sums(jnp.concatenate([halo, xp], axis=0), back=True)
    return [sums[g][HALO:] * inv[g] - xp[:, g * GROUP:(g + 1) * GROUP] for g in range(N_GROUPS)]


def _tril_mask():
    r = lax.broadcasted_iota(jnp.int32, (GROUP, GROUP), 0)
    c = lax.broadcasted_iota(jnp.int32, (GROUP, GROUP), 1)
    return (r >= c).astype(f32)


def _gs(g):
    return slice(g * GROUP, (g + 1) * GROUP)


class _Cut:
    def __init__(self, rows, cols, by_cols):
        self.rows, self.cols, self.by_cols = rows, cols, by_cols
        if by_cols:
            self.block_shape = (rows // 2, cols // N_CHIPS)
            self.shard_shape = (rows, cols // N_CHIPS)
        else:
            self.block_shape = (rows // (2 * N_CHIPS), cols)
            self.shard_shape = (rows // N_CHIPS, cols)

    def block(self, ref, chip, half):
        br, bc = self.block_shape
        if self.by_cols:
            return ref.at[pl.ds(pl.multiple_of(half * br, 16), br), pl.ds(pl.multiple_of(chip * bc, 128), bc)]
        return ref.at[pl.ds(pl.multiple_of((2 * chip + half) * br, 8), br), :]

    def shard(self, ref, chip):
        sr, sc = self.shard_shape
        if self.by_cols:
            return ref.at[:, pl.ds(pl.multiple_of(chip * sc, 128), sc)]
        return ref.at[pl.ds(pl.multiple_of(chip * sr, 16), sr), :]

    def half_of_shard(self, ref, half):
        br = self.block_shape[0]
        return ref.at[pl.ds(pl.multiple_of(half * br, 8), br), :]

    def block_index(self, chip, half):
        return (half, chip) if self.by_cols else (2 * chip + half, 0)


CUTS = {
    "w_in": _Cut(D_MODEL, IN_COLS, True),
    "w_out": _Cut(D_MODEL, D_MODEL, False),
    "w_gate_up": _Cut(D_MODEL, 2 * D_FF, True),
    "w_down": _Cut(D_FF, D_MODEL, False),
    "pool_w": _Cut(SQUARE_ROWS, GROUP, False),
    "sgu_w": _Cut(SQUARE_ROWS, GROUP, False),
    "vecs": _Cut(VEC_ROWS, GROUP, False),
}
BIG = ("w_in", "w_out", "w_gate_up", "w_down")
SMALL = ("pool_w", "sgu_w", "vecs")
ANY = pl.BlockSpec(memory_space=pl.ANY)


def _wire_dtype(name):
    return bf16 if name in BIG else f32


def _place():
    x, y, c = lax.axis_index("x"), lax.axis_index("y"), lax.axis_index("c")
    others = [(1 - x, y), (x, 1 - y), (1 - x, 1 - y)]
    return x, y, c, 2 * x + y, others, [2 * ox + oy for ox, oy in others]


def _remote(src, dst, send_sems, recv_sems, k, to):
    return pltpu.make_async_remote_copy(src_ref=src, dst_ref=dst, send_sem=send_sems.at[k], recv_sem=recv_sems.at[k],
                                        device_id=to, device_id_type=MESH)


class _GatherJob:
    def __init__(self, shards):
        self.names = tuple(shards)
        self.arrays = tuple(shards.values())
        n = len(self.names)
        self.out_shapes = [jax.ShapeDtypeStruct((CUTS[name].rows, CUTS[name].cols), bf16) for name in self.names]
        self.scratch_shapes = ([pltpu.SemaphoreType.DMA((6 * n,)), pltpu.SemaphoreType.DMA((6 * n,)),
                                pltpu.SemaphoreType.DMA((2 * n,))]
                               + [pltpu.VMEM(CUTS[name].shard_shape, bf16) for name in self.names])

    def bind(self, shard_refs, full_refs, scratch):
        self.shards, self.full = shard_refs, full_refs
        self.send_sems, self.recv_sems, self.local_sems = scratch[:3]
        self.stages = scratch[3:]
        return self

    def _sends(self):
        _, _, c, me, others, _ = _place()
        return [_remote(CUTS[name].half_of_shard(self.shards[w], c), CUTS[name].block(self.full[w], me, c),
                        self.send_sems, self.recv_sems, 3 * w + k, (*chip, c))
                for w, name in enumerate(self.names) for k, chip in enumerate(others)]

    def _relays(self, half):
        x, y, c, _, _, other_ids = _place()
        n = len(self.names)
        return [_remote(CUTS[name].block(self.full[w], other_ids[k], half),
                        CUTS[name].block(self.full[w], other_ids[k], half),
                        self.send_sems, self.recv_sems, 3 * n + 3 * w + k, (x, y, 1 - c))
                for w, name in enumerate(self.names) for k in range(3)]

    def _stores(self):
        me = _place()[3]
        return [pltpu.make_async_copy(self.stages[w], CUTS[name].shard(self.full[w], me), self.local_sems.at[2 * w + 1])
                for w, name in enumerate(self.names)]

    def start(self):
        loads = [pltpu.make_async_copy(self.shards[w], self.stages[w], self.local_sems.at[2 * w])
                 for w in range(len(self.names))]
        for cp in loads:
            cp.start()
        for cp in self._sends():
            cp.start()
        for load, store in zip(loads, self._stores()):
            load.wait()
            store.start()

    def finish(self):
        _, _, c, _, others, other_ids = _place()
        relays = self._relays(c)
        for w, name in enumerate(self.names):
            for k, chip in enumerate(others):
                landed = CUTS[name].block(self.full[w], other_ids[k], c)
                _remote(landed, landed, self.send_sems, self.recv_sems, 3 * w + k, (*chip, c)).wait_recv()
                relays[3 * w + k].start()
        for cp in self._relays(1 - c):
            cp.wait_recv()
        for cp in self._sends() + relays:
            cp.wait_send()
        for cp in self._stores():
            cp.wait()


class _SwapHalvesJob:
    def __init__(self, grads):
        self.names = tuple(grads)
        self.arrays = tuple(grads.values())
        n = len(self.names)
        self.out_shapes = [jax.ShapeDtypeStruct((N_CHIPS, *CUTS[name].block_shape), f32) for name in self.names]
        self.scratch_shapes = [pltpu.SemaphoreType.DMA((N_CHIPS * n,)), pltpu.SemaphoreType.DMA((N_CHIPS * n,))]

    def bind(self, g_refs, got_refs, scratch):
        self.g_refs, self.got_refs = g_refs, got_refs
        self.send_sems, self.recv_sems = scratch
        return self

    def _copies(self):
        x, y, c, _, _, _ = _place()
        return [_remote(CUTS[name].block(self.g_refs[a], j, 1 - c), self.got_refs[a].at[j], self.send_sems,
                        self.recv_sems, N_CHIPS * a + j, (x, y, 1 - c))
                for a, name in enumerate(self.names) for j in range(N_CHIPS)]

    def start(self):
        for cp in self._copies():
            cp.start()

    def finish(self):
        for cp in self._copies():
            cp.wait()


class _SwapChipsJob:
    def __init__(self, partials):
        self.names = tuple(partials)
        self.arrays = tuple(partials.values())
        n = len(self.names)
        self.out_shapes = [jax.ShapeDtypeStruct((3, *CUTS[name].block_shape), _wire_dtype(name)) for name in self.names]
        self.scratch_shapes = [pltpu.SemaphoreType.DMA((3 * n,)), pltpu.SemaphoreType.DMA((3 * n,))]

    def bind(self, p_refs, got_refs, scratch):
        self.p_refs, self.got_refs = p_refs, got_refs
        self.send_sems, self.recv_sems = scratch
        return self

    def _copies(self):
        _, _, c, _, others, other_ids = _place()
        return [_remote(self.p_refs[a].at[other_ids[k]], self.got_refs[a].at[k], self.send_sems, self.recv_sems,
                        3 * a + k, (*chip, c))
                for a in range(len(self.names)) for k, chip in enumerate(others)]

    def start(self):
        for cp in self._copies():
            cp.start()

    def finish(self):
        for cp in self._copies():
            cp.wait()


class _ShareJob:
    def __init__(self, reduced):
        self.names = tuple(reduced)
        self.arrays = tuple(reduced.values())
        self.big = [a for a, name in enumerate(self.names) if name in BIG]
        self.small = [a for a, name in enumerate(self.names) if name in SMALL]
        self.out_shapes = [jax.ShapeDtypeStruct(CUTS[name].shard_shape if name in BIG
                                                else (CUTS[name].rows, CUTS[name].cols), f32) for name in self.names]
        n_sems = len(self.big) + 7 * len(self.small)
        self.scratch_shapes = ([pltpu.SemaphoreType.DMA((n_sems,)), pltpu.SemaphoreType.DMA((n_sems,)),
                                pltpu.SemaphoreType.DMA((2 * len(self.names),))]
                               + [pltpu.VMEM(CUTS[name].block_shape, f32) for name in self.names])

    def bind(self, f_refs, out_refs, scratch):
        self.f_refs, self.out_refs = f_refs, out_refs
        self.send_sems, self.recv_sems, self.local_sems = scratch[:3]
        self.stages = scratch[3:]
        return self

    def _sem(self, a, which=0):
        if a in self.big:
            return self.big.index(a)
        return len(self.big) + 7 * self.small.index(a) + which

    def _mine(self, a, half):
        me = _place()[3]
        cut = CUTS[self.names[a]]
        return cut.half_of_shard(self.out_refs[a], half) if a in self.big else cut.block(self.out_refs[a], me, half)

    def _to_sibling(self):
        x, y, c, _, _, _ = _place()
        return [_remote(self.f_refs[a], self._mine(a, c), self.send_sems, self.recv_sems, self._sem(a), (x, y, 1 - c))
                for a in range(len(self.names))]

    def _to_chips(self):
        _, _, c, _, others, _ = _place()
        return [_remote(self.f_refs[a], self._mine(a, c), self.send_sems, self.recv_sems, self._sem(a, 1 + k), (*chip, c))
                for a in self.small for k, chip in enumerate(others)]

    def _passes(self, half):
        x, y, c, _, _, other_ids = _place()
        out = []
        for a in self.small:
            for k in range(3):
                blk = CUTS[self.names[a]].block(self.out_refs[a], other_ids[k], half)
                out.append(_remote(blk, blk, self.send_sems, self.recv_sems, self._sem(a, 4 + k), (x, y, 1 - c)))
        return out

    def _stores(self):
        c = _place()[2]
        return [pltpu.make_async_copy(self.stages[a], self._mine(a, c), self.local_sems.at[2 * a + 1])
                for a in range(len(self.names))]

    def start(self):
        loads = [pltpu.make_async_copy(self.f_refs[a], self.stages[a], self.local_sems.at[2 * a])
                 for a in range(len(self.names))]
        for cp in loads:
            cp.start()
        for cp in self._to_sibling() + self._to_chips():
            cp.start()
        for load, store in zip(loads, self._stores()):
            load.wait()
            store.start()

    def finish(self):
        x, y, c, _, others, other_ids = _place()
        passes = self._passes(c)
        for s, a in enumerate(self.small):
            for k, chip in enumerate(others):
                landed = CUTS[self.names[a]].block(self.out_refs[a], other_ids[k], c)
                _remote(landed, landed, self.send_sems, self.recv_sems, self._sem(a, 1 + k), (*chip, c)).wait_recv()
                passes[3 * s + k].start()
        for a in range(len(self.names)):
            theirs = self._mine(a, 1 - c)
            _remote(theirs, theirs, self.send_sems, self.recv_sems, self._sem(a), (x, y, 1 - c)).wait_recv()
        for cp in self._passes(1 - c):
            cp.wait_recv()
        for cp in self._to_sibling() + self._to_chips() + passes:
            cp.wait_send()
        for cp in self._stores():
            cp.wait()


def _call(body, *, name, grid, in_specs, out_specs, out_shape, args, scratch_shapes=(), jobs=()):
    n_in, n_out, n_scr = len(in_specs), len(out_specs), len(scratch_shapes)
    j_in = [len(j.arrays) for j in jobs]
    j_out = [len(j.out_shapes) for j in jobs]
    j_scr = [len(j.scratch_shapes) for j in jobs]

    def wrapped(*refs):
        refs = list(refs)

        def take(k):
            head = refs[:k]
            del refs[:k]
            return head

        ins, jins = take(n_in), [take(k) for k in j_in]
        outs, jouts = take(n_out), [take(k) for k in j_out]
        scr, jscr = take(n_scr), [take(k) for k in j_scr]
        bound = [j.bind(a, b, c) for j, a, b, c in zip(jobs, jins, jouts, jscr)]
        if not grid:
            for b in bound:
                b.start()
            body(*ins, *outs, *scr)
            for b in bound:
                b.finish()
            return
        if not bound:
            body(*ins, *outs, *scr)
            return
        first = _all([pl.program_id(d) == 0 for d in range(len(grid))])
        last = _all([pl.program_id(d) == grid[d] - 1 for d in range(len(grid))])

        @pl.when(first)
        def _():
            for b in bound:
                b.start()

        body(*ins, *outs, *scr)

        @pl.when(last)
        def _():
            for b in bound:
                b.finish()

    kw = dict(grid=grid) if grid else {}
    results = pl.pallas_call(
        wrapped, name=name,
        in_specs=list(in_specs) + [ANY] * sum(j_in), out_specs=list(out_specs) + [ANY] * sum(j_out),
        out_shape=list(out_shape) + [s for j in jobs for s in j.out_shapes],
        scratch_shapes=list(scratch_shapes) + [s for j in jobs for s in j.scratch_shapes],
        compiler_params=_params(), **kw,
    )(*args, *[a for j in jobs for a in j.arrays])
    results = list(results)
    own, rest = results[:n_out], results[n_out:]
    per_job = []
    for j, k in zip(jobs, j_out):
        per_job.append(dict(zip(j.names, rest[:k])))
        rest = rest[k:]
    return own, per_job


def _all(conds):
    out = conds[0]
    for c in conds[1:]:
        out = jnp.logical_and(out, c)
    return out


def _alone(job, name):
    return _call(lambda: None, name=name, grid=None, in_specs=[], out_specs=[], out_shape=[], args=[], jobs=[job])[1][0]


def _fwd_proj(x2, w_in_b, tile, jobs):
    tokens = x2.shape[0]

    def body(x_ref, w_ref, xp_ref, pre_ref, xb_ref):
        xb_ref[...] = x_ref[...].astype(bf16)
        proj = _mm(xb_ref[...], w_ref[...])
        xp_ref[...] = proj[:, :POOL_WIDTH]
        pre_ref[...] = proj[:, POOL_WIDTH:].astype(bf16)

    row = lambda i: (i, 0)
    return _call(
        body, name="fwd_proj", grid=(tokens // tile,),
        in_specs=[pl.BlockSpec((tile, D_MODEL), row), pl.BlockSpec((D_MODEL, IN_COLS), lambda i: (0, 0))],
        out_specs=[pl.BlockSpec((tile, POOL_WIDTH), row), pl.BlockSpec((tile, 2 * SGU_WIDTH), row),
                   pl.BlockSpec((tile, D_MODEL), row)],
        out_shape=[jax.ShapeDtypeStruct((tokens, POOL_WIDTH), f32), jax.ShapeDtypeStruct((tokens, 2 * SGU_WIDTH), bf16),
                   jax.ShapeDtypeStruct((tokens, D_MODEL), bf16)],
        args=(x2, w_in_b), jobs=jobs)


def _small_specs():
    return [pl.BlockSpec((N_GROUPS, GROUP, GROUP), lambda i: (0, 0, 0)),
            pl.BlockSpec((1, POOL_WIDTH), lambda i: (0, 0)),
            pl.BlockSpec((1, SGU_WIDTH), lambda i: (0, 0)),
            pl.BlockSpec((1, SGU_WIDTH), lambda i: (0, 0)),
            pl.BlockSpec((N_GROUPS, GROUP, GROUP), lambda i: (0, 0, 0)),
            pl.BlockSpec((N_GROUPS, GROUP, GROUP), lambda i: (0, 0, 0))]


def _fwd_mix(xp_all, pre_b, x2, small, w_out_b, ln1_g, ln1_b, tile, seq, jobs):
    tokens = x2.shape[0]
    tps = seq // tile
    hb = tile // HALO

    def body(xp_ref, halo_ref, pre_ref, x_ref, pw_ref, ps_ref, lg_ref, lb_ref, sw_ref, sb_ref, wout_ref, g1_ref, b1_ref,
             mix_ref, r1_ref, h_ref):
        seq_tile = pl.program_id(0) % tps
        xp = xp_ref[...]
        halo = jnp.where(seq_tile == 0, 0.0, halo_ref[...])
        pooled = _pooled_groups(xp, halo, _inv_counts(seq_tile, tile))
        for g in range(N_GROUPS):
            po = _mm(pooled[g].astype(bf16), pw_ref[g].astype(bf16)) * ps_ref[:, _gs(g)]
            mix_ref[:, _gs(g)] = po.astype(bf16)

        pre = pre_ref[...].astype(f32)
        cdf, _ = _gelu_parts(pre)
        zg = pre * cdf
        u = zg[:, :SGU_WIDTH]
        vln, _, _ = _ln_fwd(zg[:, SGU_WIDTH:], lg_ref[...], lb_ref[...])
        vb = vln.astype(bf16)
        mask = _tril_mask()
        for h in range(N_GROUPS):
            wm = (sw_ref[h] * mask).astype(bf16)
            bias = sb_ref[h]
            for c in range(tile // GROUP):
                rows = slice(c * GROUP, (c + 1) * GROUP)
                mixed = _mm(wm, vb[rows, _gs(h)]) + bias
                mix_ref[rows, POOL_WIDTH + h * GROUP:POOL_WIDTH + (h + 1) * GROUP] = (u[rows, _gs(h)] * mixed).astype(bf16)

        r1 = ALPHA * x_ref[...] + _mm(mix_ref[...], wout_ref[...])
        r1_ref[...] = r1
        h1, _, _ = _ln_fwd(r1, g1_ref[...], b1_ref[...])
        h_ref[...] = h1.astype(bf16)

    row = lambda i: (i, 0)
    vec = pl.BlockSpec((1, D_MODEL), lambda i: (0, 0))
    return _call(
        body, name="fwd_mix", grid=(tokens // tile,),
        in_specs=[pl.BlockSpec((tile, POOL_WIDTH), row),
                  pl.BlockSpec((HALO, POOL_WIDTH), lambda i: (jnp.maximum(i * hb - 1, 0), 0)),
                  pl.BlockSpec((tile, 2 * SGU_WIDTH), row),
                  pl.BlockSpec((tile, D_MODEL), row)] + _small_specs()
                 + [pl.BlockSpec((D_MODEL, D_MODEL), lambda i: (0, 0)), vec, vec],
        out_specs=[pl.BlockSpec((tile, D_MODEL), row)] * 3,
        out_shape=[jax.ShapeDtypeStruct((tokens, D_MODEL), bf16),
                   jax.ShapeDtypeStruct((tokens, D_MODEL), f32),
                   jax.ShapeDtypeStruct((tokens, D_MODEL), bf16)],
        args=(xp_all, xp_all, pre_b, x2, *small, w_out_b, ln1_g, ln1_b), jobs=jobs)


def _fwd_gate_up(h_b, w_gu_b, tile, jobs):
    tokens = h_b.shape[0]

    def body(h_ref, w_ref, gu_ref, a_ref):
        hb = h_ref[...]
        for c in range(D_FF // FF_CHUNK):
            gcols = slice(c * FF_CHUNK, (c + 1) * FF_CHUNK)
            ucols = slice(D_FF + c * FF_CHUNK, D_FF + (c + 1) * FF_CHUNK)
            gate = _mm(hb, w_ref[:, gcols])
            up = _mm(hb, w_ref[:, ucols])
            sg = jax.nn.sigmoid(gate)
            silu = gate * sg
            gu_ref[:, gcols] = (up * (sg + silu * (1.0 - sg))).astype(bf16)
            gu_ref[:, ucols] = silu.astype(bf16)
            a_ref[:, gcols] = (silu * up).astype(bf16)

    return _call(
        body, name="fwd_gate_up", grid=(tokens // tile,),
        in_specs=[pl.BlockSpec((tile, D_MODEL), lambda i: (i, 0)),
                  pl.BlockSpec((D_MODEL, 2 * D_FF), lambda i: (0, 0), pipeline_mode=pl.Buffered(1))],
        out_specs=[pl.BlockSpec((tile, 2 * D_FF), lambda i: (i, 0)),
                   pl.BlockSpec((tile, D_FF), lambda i: (i, 0))],
        out_shape=[jax.ShapeDtypeStruct((tokens, 2 * D_FF), bf16),
                   jax.ShapeDtypeStruct((tokens, D_FF), bf16)],
        args=(h_b, w_gu_b), jobs=jobs)


def _fwd_down_loss(a_b, w_dn_b, r1, target, ln1_g, ln1_b, ln2_g, ln2_b, tile):
    tokens = a_b.shape[0]

    def body(a_ref, w_ref, r1_ref, t_ref, g1_ref, b1_ref, g2_ref, b2_ref, dr2_ref, dr2b_ref, st_ref):
        @pl.when(pl.program_id(0) == 0)
        def _():
            st_ref[...] = jnp.zeros_like(st_ref)

        sub = min(ROW_SUB, tile)
        for s in range(tile // sub):
            rows = slice(s * sub, (s + 1) * sub)
            h1, _, _ = _ln_fwd(r1_ref[rows, :], g1_ref[...], b1_ref[...])
            r2 = ALPHA * h1 + _mm(a_ref[rows, :], w_ref[...])
            y, xhat, rstd = _ln_fwd(r2, g2_ref[...], b2_ref[...])
            diff = y - t_ref[rows, :]
            dy = diff * (1.0 / D_MODEL)
            st_ref[0:1, :] += _col_sum(dy * xhat)
            st_ref[1:2, :] += _col_sum(dy)
            st_ref[2:3, :] += _col_sum(diff * diff)
            dr2 = _ln_bwd(dy, xhat, rstd, g2_ref[...])
            dr2_ref[rows, :] = dr2
            dr2b_ref[rows, :] = dr2.astype(bf16)

    row = lambda i: (i, 0)
    vec = pl.BlockSpec((1, D_MODEL), lambda i: (0, 0))
    return _call(
        body, name="fwd_down_loss", grid=(tokens // tile,),
        in_specs=[pl.BlockSpec((tile, D_FF), row), pl.BlockSpec((D_FF, D_MODEL), lambda i: (0, 0)),
                  pl.BlockSpec((tile, D_MODEL), row), pl.BlockSpec((tile, D_MODEL), row), vec, vec, vec, vec],
        out_specs=[pl.BlockSpec((tile, D_MODEL), row), pl.BlockSpec((tile, D_MODEL), row),
                   pl.BlockSpec((8, D_MODEL), lambda i: (0, 0))],
        out_shape=[jax.ShapeDtypeStruct((tokens, D_MODEL), f32), jax.ShapeDtypeStruct((tokens, D_MODEL), bf16),
                   jax.ShapeDtypeStruct((8, D_MODEL), f32)],
        args=(a_b, w_dn_b, r1, target, ln1_g, ln1_b, ln2_g, ln2_b))[0]


def _bwd_gate_up(dr2, gu_b, w_dn_b, tile, jobs):
    tokens = dr2.shape[0]

    def body(d_ref, gu_ref, w_ref, dgu_ref):
        d = d_ref[...].astype(bf16)
        for c in range(D_FF // FF_CHUNK):
            gcols = slice(c * FF_CHUNK, (c + 1) * FF_CHUNK)
            ucols = slice(D_FF + c * FF_CHUNK, D_FF + (c + 1) * FF_CHUNK)
            da = _mm_nt(d, w_ref[gcols, :])
            dgu_ref[:, gcols] = (da * gu_ref[:, gcols].astype(f32)).astype(bf16)
            dgu_ref[:, ucols] = (da * gu_ref[:, ucols].astype(f32)).astype(bf16)

    return _call(
        body, name="bwd_gate_up", grid=(tokens // tile,),
        in_specs=[pl.BlockSpec((tile, D_MODEL), lambda i: (i, 0)),
                  pl.BlockSpec((tile, 2 * D_FF), lambda i: (i, 0)),
                  pl.BlockSpec((D_FF, D_MODEL), lambda i: (0, 0))],
        out_specs=[pl.BlockSpec((tile, 2 * D_FF), lambda i: (i, 0))],
        out_shape=[jax.ShapeDtypeStruct((tokens, 2 * D_FF), bf16)],
        args=(dr2, gu_b, w_dn_b), jobs=jobs)


def _bwd_ffn_in(dgu_b, w_gu_b, dr2, r1, ln1_g, ln1_b, tile, jobs):
    tokens = dr2.shape[0]

    def body(dgu_ref, w_ref, d_ref, r1_ref, g1_ref, b1_ref, dr1_ref, dr1b_ref, st_ref):
        @pl.when(pl.program_id(0) == 0)
        def _():
            st_ref[...] = jnp.zeros_like(st_ref)

        dh = ALPHA * d_ref[...] + _mm_nt(dgu_ref[...], w_ref[...])
        _, xhat, rstd = _ln_fwd(r1_ref[...], g1_ref[...], b1_ref[...])
        st_ref[0:1, :] += _col_sum(dh * xhat)
        st_ref[1:2, :] += _col_sum(dh)
        dr1 = _ln_bwd(dh, xhat, rstd, g1_ref[...])
        dr1_ref[...] = dr1
        dr1b_ref[...] = dr1.astype(bf16)

    row = lambda i: (i, 0)
    vec = pl.BlockSpec((1, D_MODEL), lambda i: (0, 0))
    return _call(
        body, name="bwd_ffn_in", grid=(tokens // tile,),
        in_specs=[pl.BlockSpec((tile, 2 * D_FF), row),
                  pl.BlockSpec((D_MODEL, 2 * D_FF), lambda i: (0, 0), pipeline_mode=pl.Buffered(1)),
                  pl.BlockSpec((tile, D_MODEL), row), pl.BlockSpec((tile, D_MODEL), row), vec, vec],
        out_specs=[pl.BlockSpec((tile, D_MODEL), row), pl.BlockSpec((tile, D_MODEL), row),
                   pl.BlockSpec((8, D_MODEL), lambda i: (0, 0))],
        out_shape=[jax.ShapeDtypeStruct((tokens, D_MODEL), f32), jax.ShapeDtypeStruct((tokens, D_MODEL), bf16),
                   jax.ShapeDtypeStruct((8, D_MODEL), f32)],
        args=(dgu_b, w_gu_b, dr2, r1, ln1_g, ln1_b), jobs=jobs)


def _bwd_mix(dr1, xp_all, pre_b, small, w_out_b, tile, seq, jobs):
    tokens = dr1.shape[0]
    tps = seq // tile
    hb = tile // HALO
    steps = tokens // tile

    def body(dr1_ref, xp_ref, halo_ref, pre_ref, wout_ref, pw_ref, ps_ref, lg_ref, lb_ref, sw_ref, sb_ref,
             dpool_ref, dpre_ref, gpw_ref, gsw_ref, gsb_ref, vec_ref, du_ref, dvln_ref):
        step = pl.program_id(0)
        seq_tile = step % tps

        @pl.when(step == 0)
        def _():
            gpw_ref[...] = jnp.zeros_like(gpw_ref)
            gsw_ref[...] = jnp.zeros_like(gsw_ref)
            gsb_ref[...] = jnp.zeros_like(gsb_ref)
            vec_ref[...] = jnp.zeros_like(vec_ref)

        dmix = _mm_nt(dr1_ref[...].astype(bf16), wout_ref[...])

        xp = xp_ref[...]
        halo = jnp.where(seq_tile == 0, 0.0, halo_ref[...])
        pooled = _pooled_groups(xp, halo, _inv_counts(seq_tile, tile))
        for g in range(N_GROUPS):
            pb = pooled[g].astype(bf16)
            pwb = pw_ref[g].astype(bf16)
            dpo = dmix[:, _gs(g)]
            vec_ref[0:1, _gs(g)] += _col_sum(dpo * _mm(pb, pwb))
            dpo_b = (dpo * ps_ref[:, _gs(g)]).astype(bf16)
            gpw_ref[g] += _mm_tn(pb, dpo_b)
            dpool_ref[:, _gs(g)] = _mm_nt(dpo_b, pwb)

        pre = pre_ref[...].astype(f32)
        cdf, pdf = _gelu_parts(pre)
        zg = pre * cdf
        u = zg[:, :SGU_WIDTH]
        vln, vhat, rstd = _ln_fwd(zg[:, SGU_WIDTH:], lg_ref[...], lb_ref[...])
        vb = vln.astype(bf16)
        mask = _tril_mask()
        for h in range(N_GROUPS):
            wm = (sw_ref[h] * mask).astype(bf16)
            bias = sb_ref[h]
            gsw = jnp.zeros((GROUP, GROUP), f32)
            gsb = jnp.zeros((GROUP, GROUP), f32)
            for c in range(tile // GROUP):
                rows = slice(c * GROUP, (c + 1) * GROUP)
                v_ch = vb[rows, _gs(h)]
                d = dmix[rows, POOL_WIDTH + h * GROUP:POOL_WIDTH + (h + 1) * GROUP]
                du_ref[rows, _gs(h)] = d * (_mm(wm, v_ch) + bias)
                dmixed = d * u[rows, _gs(h)]
                gsb += dmixed
                dmixed_b = dmixed.astype(bf16)
                gsw += _mm_nt(dmixed_b, v_ch)
                dvln_ref[rows, _gs(h)] = _mm_tn(wm, dmixed_b)
            gsw_ref[h] += gsw * mask
            gsb_ref[h] += gsb

        dvln = dvln_ref[...]
        vec_ref[1:2, :] += _col_sum(dvln * vhat)
        vec_ref[2:3, :] += _col_sum(dvln)
        dgelu = cdf + pre * pdf
        dpre_ref[:, :SGU_WIDTH] = (du_ref[...] * dgelu[:, :SGU_WIDTH]).astype(bf16)
        dpre_ref[:, SGU_WIDTH:] = (_ln_bwd(dvln, vhat, rstd, lg_ref[...]) * dgelu[:, SGU_WIDTH:]).astype(bf16)

        @pl.when(step == steps - 1)
        def _():
            for h in range(N_GROUPS):
                gsb_ref[h] = jnp.broadcast_to(jnp.sum(gsb_ref[h], axis=1, keepdims=True), (GROUP, GROUP))

    row = lambda i: (i, 0)
    sq = jax.ShapeDtypeStruct((N_GROUPS, GROUP, GROUP), f32)
    sq_spec = pl.BlockSpec((N_GROUPS, GROUP, GROUP), lambda i: (0, 0, 0))
    return _call(
        body, name="bwd_mix", grid=(steps,),
        in_specs=[pl.BlockSpec((tile, D_MODEL), row), pl.BlockSpec((tile, POOL_WIDTH), row),
                  pl.BlockSpec((HALO, POOL_WIDTH), lambda i: (jnp.maximum(i * hb - 1, 0), 0)),
                  pl.BlockSpec((tile, 2 * SGU_WIDTH), row),
                  pl.BlockSpec((D_MODEL, D_MODEL), lambda i: (0, 0))] + _small_specs(),
        out_specs=[pl.BlockSpec((tile, POOL_WIDTH), row), pl.BlockSpec((tile, 2 * SGU_WIDTH), row),
                   sq_spec, sq_spec, sq_spec, pl.BlockSpec((8, POOL_WIDTH), lambda i: (0, 0))],
        out_shape=[jax.ShapeDtypeStruct((tokens, POOL_WIDTH), f32), jax.ShapeDtypeStruct((tokens, 2 * SGU_WIDTH), bf16),
                   sq, sq, sq, jax.ShapeDtypeStruct((8, POOL_WIDTH), f32)],
        scratch_shapes=[pltpu.VMEM((tile, SGU_WIDTH), f32), pltpu.VMEM((tile, SGU_WIDTH), f32)],
        args=(dr1, xp_all, xp_all, pre_b, w_out_b, *small), jobs=jobs)


def _bwd_in(dpool, dpre_b, dr1, w_in_b, tile, seq):
    tokens = dr1.shape[0]
    tps = seq // tile
    hb = tile // HALO
    last_halo = tokens // HALO - 1

    def body(dpool_ref, nxt_ref, dpre_ref, dr1_ref, w_ref, dx_ref, dproj_ref):
        seq_tile = pl.program_id(0) % tps
        inv = _inv_counts(seq_tile, tile)
        dpl = dpool_ref[...]
        nxt = jnp.where(seq_tile == tps - 1, 0.0, nxt_ref[...])
        scaled = jnp.concatenate([dpl[:, _gs(g)] * inv[g] for g in range(N_GROUPS)], axis=1)
        scaled_nxt = jnp.concatenate([nxt[:, _gs(g)] * (1.0 / POOL_WINDOWS[g]) for g in range(N_GROUPS)], axis=1)
        sums = _window_sums(jnp.concatenate([scaled, scaled_nxt], axis=0), back=False)
        for g in range(N_GROUPS):
            dproj_ref[:, _gs(g)] = (sums[g][:tile] - dpl[:, _gs(g)]).astype(bf16)
        dproj_ref[:, POOL_WIDTH:] = dpre_ref[...]
        dx_ref[...] = ALPHA * dr1_ref[...] + _mm_nt(dproj_ref[...], w_ref[...])

    row = lambda i: (i, 0)
    return _call(
        body, name="bwd_in", grid=(tokens // tile,),
        in_specs=[pl.BlockSpec((tile, POOL_WIDTH), row),
                  pl.BlockSpec((HALO, POOL_WIDTH), lambda i: (jnp.minimum((i + 1) * hb, last_halo), 0)),
                  pl.BlockSpec((tile, 2 * SGU_WIDTH), row),
                  pl.BlockSpec((tile, D_MODEL), row),
                  pl.BlockSpec((D_MODEL, IN_COLS), lambda i: (0, 0))],
        out_specs=[pl.BlockSpec((tile, D_MODEL), row), pl.BlockSpec((tile, IN_COLS), row)],
        out_shape=[jax.ShapeDtypeStruct((tokens, D_MODEL), f32), jax.ShapeDtypeStruct((tokens, IN_COLS), bf16)],
        args=(dpool, dpool, dpre_b, dr1, w_in_b))[0]


def _wgrad(a, b, col_tile, tile, name, jobs=()):
    tokens, m = a.shape
    n = b.shape[1]

    def body(a_ref, b_ref, o_ref):
        @pl.when(pl.program_id(1) == 0)
        def _():
            o_ref[...] = jnp.zeros_like(o_ref)

        o_ref[...] += _mm_tn(a_ref[...].astype(bf16), b_ref[...].astype(bf16))

    (out,), got = _call(
        body, name=name, grid=(n // col_tile, tokens // tile),
        in_specs=[pl.BlockSpec((tile, m), lambda j, k: (k, 0)),
                  pl.BlockSpec((tile, col_tile), lambda j, k: (k, j))],
        out_specs=[pl.BlockSpec((m, col_tile), lambda j, k: (0, j))],
        out_shape=[jax.ShapeDtypeStruct((m, n), f32)],
        args=(a, b), jobs=jobs)
    return out, got


def _add_halves(name, g, got, place):
    cut = CUTS[name]
    br, bc = cut.block_shape
    wire = _wire_dtype(name)

    def body(place_ref, g_ref, got_ref, o_ref, wire_ref):
        s = g_ref[...] + got_ref[...]
        wire_ref[...] = s.astype(wire)

        @pl.when(pl.program_id(0) == place_ref[1])
        def _():
            o_ref[...] = s

    blocks = pl.BlockSpec((None, br, bc), lambda j, place_ref: (j, 0, 0))
    return pl.pallas_call(
        body, name="reduce_add_halves_" + name,
        grid_spec=pltpu.PrefetchScalarGridSpec(
            num_scalar_prefetch=1, grid=(N_CHIPS,),
            in_specs=[pl.BlockSpec((br, bc), lambda j, place_ref: cut.block_index(j, place_ref[0])), blocks],
            out_specs=[pl.BlockSpec((br, bc), lambda j, place_ref: (0, 0)), blocks]),
        out_shape=[jax.ShapeDtypeStruct((br, bc), f32), jax.ShapeDtypeStruct((N_CHIPS, br, bc), wire)],
        compiler_params=_params(),
    )(place, g, got)


def _reduce_tail(late, early):
    late_names, early_names = tuple(late), tuple(early)
    names = early_names + late_names
    nl, ne, n = len(late_names), len(early_names), len(names)
    cuts = [CUTS[name] for name in names]
    is_big = [name in BIG for name in names]
    share_base, share_sem = 7 * nl, []
    for i in range(n):
        share_sem.append(share_base)
        share_base += 1 if is_big[i] else 7
    n_in = nl + 2 * ne

    def body(*refs):
        g_refs = refs[:nl]
        sums_refs, got_refs = refs[nl:n_in:2], refs[nl + 1:n_in:2]
        out_refs = refs[n_in:n_in + n]
        send_sems, recv_sems, local_sems = refs[n_in + n:n_in + n + 3]
        vm = refs[n_in + n + 3:]
        own, recv_a, wire, recv_b = vm[0:4 * nl:4], vm[1:4 * nl:4], vm[2:4 * nl:4], vm[3:4 * nl:4]
        acc, gotv = vm[4 * nl::2], vm[4 * nl + 1::2]
        x, y, c, me, others, other_ids = _place()
        sibling = (x, y, 1 - c)

        def mine(i, half):
            return cuts[i].half_of_shard(out_refs[i], half) if is_big[i] else cuts[i].block(out_refs[i], me, half)

        def reduced(i):
            return acc[i] if i < ne else own[i - ne].at[me]

        sent, stores = [], []

        def share(i):
            store = pltpu.make_async_copy(reduced(i), mine(i, c), local_sems.at[4 * nl + 2 * ne + i])
            store.start()
            stores.append(store)
            to = [sibling] if is_big[i] else [sibling] + [(*chip, c) for chip in others]
            for which, device in enumerate(to):
                cp = _remote(reduced(i), mine(i, c), send_sems, recv_sems, share_sem[i] + which, device)
                cp.start()
                sent.append(cp)

        early_loads = []
        for e in range(ne):
            early_loads.append(pltpu.make_async_copy(sums_refs[e], acc[e], local_sems.at[4 * nl + 2 * e]))
            early_loads.append(pltpu.make_async_copy(got_refs[e], gotv[e], local_sems.at[4 * nl + 2 * e + 1]))
        late_loads = [pltpu.make_async_copy(cuts[ne + l].block(g_refs[l], j, c), own[l].at[j], local_sems.at[4 * l + j])
                      for l in range(nl) for j in range(N_CHIPS)]
        halves = [_remote(cuts[ne + l].block(g_refs[l], j, 1 - c), recv_a[l].at[j], send_sems, recv_sems, 4 * l + j, sibling)
                  for l in range(nl) for j in range(N_CHIPS)]
        for cp in early_loads + late_loads + halves:
            cp.start()

        for cp in early_loads:
            cp.wait()
        for e in range(ne):
            acc[e][...] = ((acc[e][...] + gotv[e][0].astype(f32)) + gotv[e][1].astype(f32)) + gotv[e][2].astype(f32)
            share(e)

        for cp in late_loads:
            cp.wait()
        for cp in halves:
            cp.wait_recv()
        for l in range(nl):
            for j in range(N_CHIPS):
                s = own[l][j] + recv_a[l][j]
                own[l][j] = s
                wire[l][j] = s.astype(wire[l].dtype)
        chips = [_remote(wire[l].at[other_ids[k]], recv_b[l].at[k], send_sems, recv_sems, 4 * nl + 3 * l + k, (*chip, c))
                 for l in range(nl) for k, chip in enumerate(others)]
        for cp in chips:
            cp.start()
        for cp in chips:
            cp.wait_recv()
        for l in range(nl):
            mine_l = own[l].at[me]
            mine_l[...] = ((mine_l[...] + recv_b[l][0].astype(f32)) + recv_b[l][1].astype(f32)) + recv_b[l][2].astype(f32)
            share(ne + l)

        for i in range(n):
            if not is_big[i]:
                for k, chip in enumerate(others):
                    landed = cuts[i].block(out_refs[i], other_ids[k], c)
                    _remote(landed, landed, send_sems, recv_sems, share_sem[i] + 1 + k, (*chip, c)).wait_recv()
                    cp = _remote(landed, landed, send_sems, recv_sems, share_sem[i] + 4 + k, sibling)
                    cp.start()
                    sent.append(cp)
        for i in range(n):
            theirs = mine(i, 1 - c)
            _remote(theirs, theirs, send_sems, recv_sems, share_sem[i], sibling).wait_recv()
            if not is_big[i]:
                for k in range(3):
                    passed = cuts[i].block(out_refs[i], other_ids[k], 1 - c)
                    _remote(passed, passed, send_sems, recv_sems, share_sem[i] + 4 + k, sibling).wait_recv()
        for cp in halves + chips + sent:
            cp.wait_send()
        for cp in stores:
            cp.wait()

    scratch = [pltpu.SemaphoreType.DMA((share_base,)), pltpu.SemaphoreType.DMA((share_base,)),
               pltpu.SemaphoreType.DMA((4 * nl + 2 * ne + n,))]
    for name in late_names:
        block = CUTS[name].block_shape
        scratch += [pltpu.VMEM((N_CHIPS, *block), f32), pltpu.VMEM((N_CHIPS, *block), f32),
                    pltpu.VMEM((N_CHIPS, *block), _wire_dtype(name)), pltpu.VMEM((3, *block), _wire_dtype(name))]
    for name in early_names:
        block = CUTS[name].block_shape
        scratch += [pltpu.VMEM(block, f32), pltpu.VMEM((3, *block), _wire_dtype(name))]
    args = [late[name] for name in late_names] + [a for name in early_names for a in early[name]]
    outs = pl.pallas_call(
        body, name="reduce_tail",
        in_specs=[ANY] * n_in, out_specs=[ANY] * n,
        out_shape=[jax.ShapeDtypeStruct(CUTS[name].shard_shape if name in BIG else (CUTS[name].rows, CUTS[name].cols), f32)
                   for name in names],
        scratch_shapes=scratch, compiler_params=_params(),
    )(*args)
    return dict(zip(names, outs))


def _adamw_refs(w_ref, g_ref, m_ref, v_ref, d_ref, nm_ref, nv_ref):
    g = g_ref[...]
    nm = ADAM_B1 * m_ref[...] + (1.0 - ADAM_B1) * g
    nv = ADAM_B2 * v_ref[...] + (1.0 - ADAM_B2) * jnp.square(g)
    m_hat = nm / (1.0 - ADAM_B1 ** ADAM_STEP)
    v_hat = nv / (1.0 - ADAM_B2 ** ADAM_STEP)
    d_ref[...] = -ADAM_LR * (m_hat / (jnp.sqrt(v_hat) + ADAM_EPS) + ADAM_WD * w_ref[...])
    nm_ref[...] = nm
    nv_ref[...] = nv


def _adamw_small(ws, gs, ms, vs):
    n = len(ws)

    def body(*refs):
        for i in range(n):
            _adamw_refs(*[refs[k * n + i] for k in range(7)])

    whole = pl.BlockSpec(memory_space=pltpu.VMEM)
    outs = pl.pallas_call(
        body, name="adamw_small",
        in_specs=[whole] * (4 * n), out_specs=[whole] * (3 * n),
        out_shape=[jax.ShapeDtypeStruct(w.shape, f32) for w in ws] * 3,
        compiler_params=_params(),
    )(*ws, *gs, *ms, *vs)
    return outs[:n], outs[n:2 * n], outs[2 * n:]


def _adamw(name, w, g, m, v):
    rows, cols = w.shape
    rt = rows // 4

    def body(w_ref, g_ref, m_ref, v_ref, d_ref, nm_ref, nv_ref):
        _adamw_refs(w_ref, g_ref, m_ref, v_ref, d_ref, nm_ref, nv_ref)

    spec = pl.BlockSpec((rt, cols), lambda i: (i, 0))
    shape = jax.ShapeDtypeStruct((rows, cols), f32)
    return pl.pallas_call(
        body, name="adamw_" + name, grid=(rows // rt,),
        in_specs=[spec] * 4, out_specs=[spec] * 3, out_shape=[shape] * 3,
        compiler_params=_params(),
    )(w, g, m, v)


VEC_NAMES = ("pool_scale", "sgu_ln_g", "sgu_ln_b", "sgu_b", "ln1_g", "ln1_b", "ln2_g", "ln2_b")
WEIGHT_ORDER = ("w_in", "pool_w", "pool_scale", "sgu_ln_g", "sgu_ln_b", "sgu_w", "sgu_b", "w_out", "ln1_g", "ln1_b",
                "w_gate_up", "w_down", "ln2_g", "ln2_b")


def _pack_vecs(parts, extra=None):
    rows = [parts[name].reshape(-1, GROUP) for name in VEC_NAMES]
    if extra is not None:
        rows.append(extra.reshape(-1, GROUP))
    used = sum(r.shape[0] for r in rows)
    return jnp.concatenate(rows + [jnp.zeros((VEC_ROWS - used, GROUP), f32)], axis=0)


def _unpack_vecs(packed, shapes):
    out, at = {}, 0
    for name in VEC_NAMES:
        rows = math.prod(shapes[name]) // GROUP
        out[name] = packed[at:at + rows].reshape(shapes[name])
        at += rows
    return out, packed[at:]


def kernel(x, w_in, pool_w, pool_scale, sgu_ln_g, sgu_ln_b, sgu_w, sgu_b, w_out, ln1_g, ln1_b, w_gate_up, w_down, ln2_g, ln2_b, loss_target, m_w_in, m_pool_w, m_pool_scale, m_sgu_ln_g, m_sgu_ln_b, m_sgu_w, m_sgu_b, m_w_out, m_ln1_g, m_ln1_b, m_w_gate_up, m_w_down, m_ln2_g, m_ln2_b, v_w_in, v_pool_w, v_pool_scale, v_sgu_ln_g, v_sgu_ln_b, v_sgu_w, v_sgu_b, v_w_out, v_ln1_g, v_ln1_b, v_w_gate_up, v_w_down, v_ln2_g, v_ln2_b):
    given = dict(locals())
    batch, seq, _ = x.shape
    tokens = batch * seq
    tile = min(TOKEN_TILE, seq)
    ffn_bwd_tile = min(FFN_BWD_TILE, seq)
    wtile = min(WGRAD_TILE, tokens)
    shapes = {name: given[name].shape for name in WEIGHT_ORDER}

    x2 = x.reshape(tokens, D_MODEL)
    target = loss_target.reshape(tokens, D_MODEL)
    small = (pool_w[0], pool_scale[0][None], sgu_ln_g[0][None], sgu_ln_b[0][None], sgu_w[0],
             jnp.broadcast_to(sgu_b[0][:, :, None], (N_GROUPS, GROUP, GROUP)))
    g1, b1, g2, b2 = ln1_g[0][None], ln1_b[0][None], ln2_g[0][None], ln2_b[0][None]
    shard_b = {name: given[name][0].astype(bf16) for name in BIG}
    place = jnp.stack([lax.axis_index("c"), 2 * lax.axis_index("x") + lax.axis_index("y")]).astype(jnp.int32)

    def gather(name):
        return [_GatherJob({name: shard_b[name]})]

    def halves_summed(name, grad, got):
        return _add_halves(name, grad, got, place)

    w_in_b = _alone(gather("w_in")[0], "gather_w_in")["w_in"]
    (xp_all, pre_b, x_b), (got,) = _fwd_proj(x2, w_in_b, tile, gather("w_out"))
    w_out_b = got["w_out"]
    (mix_b, r1, h_b), (got,) = _fwd_mix(xp_all, pre_b, x2, small, w_out_b, g1, b1, tile, seq, gather("w_gate_up"))
    w_gu_b = got["w_gate_up"]
    (gu_b, a_b), (got,) = _fwd_gate_up(h_b, w_gu_b, tile, gather("w_down"))
    w_dn_b = got["w_down"]
    dr2, dr2_b, stats2 = _fwd_down_loss(a_b, w_dn_b, r1, target, g1, b1, g2, b2, tile)

    early = {}
    g_down, _ = _wgrad(a_b, dr2_b, D_MODEL, wtile, "wgrad_down")
    (dgu_b,), (got,) = _bwd_gate_up(dr2_b, gu_b, w_dn_b, ffn_bwd_tile, [_SwapHalvesJob({"w_down": g_down})])
    sums_down = halves_summed("w_down", g_down, got["w_down"])
    g_gu, (got,) = _wgrad(h_b, dgu_b, D_FF, wtile, "wgrad_gate_up", [_SwapChipsJob({"w_down": sums_down[1]})])
    early["w_down"] = (sums_down[0], got["w_down"])
    (dr1, dr1_b, stats1), (got,) = _bwd_ffn_in(dgu_b, w_gu_b, dr2, r1, g1, b1, tile,
                                               [_SwapHalvesJob({"w_gate_up": g_gu})])
    sums_gu = halves_summed("w_gate_up", g_gu, got["w_gate_up"])
    g_out, _ = _wgrad(mix_b, dr1_b, D_MODEL, wtile, "wgrad_out")
    (dpool, dpre_b, g_pool_w, g_sgu_w, g_sgu_b, vecs), (got_gu, got_out) = _bwd_mix(
        dr1_b, xp_all, pre_b, small, w_out_b, tile, seq,
        [_SwapChipsJob({"w_gate_up": sums_gu[1]}), _SwapHalvesJob({"w_out": g_out})])
    early["w_gate_up"] = (sums_gu[0], got_gu["w_gate_up"])
    sums_out = halves_summed("w_out", g_out, got_out["w_out"])
    grad_x, dproj_b = _bwd_in(dpool, dpre_b, dr1, w_in_b, tile, seq)
    g_in, (got,) = _wgrad(x_b, dproj_b, IN_COLS, wtile, "wgrad_in", [_SwapChipsJob({"w_out": sums_out[1]})])
    early["w_out"] = (sums_out[0], got["w_out"])

    late = {
        "w_in": g_in,
        "pool_w": g_pool_w.reshape(SQUARE_ROWS, GROUP),
        "sgu_w": g_sgu_w.reshape(SQUARE_ROWS, GROUP),
        "vecs": _pack_vecs({"pool_scale": vecs[0], "sgu_ln_g": vecs[1], "sgu_ln_b": vecs[2], "sgu_b": g_sgu_b[:, :, 0],
                            "ln1_g": stats1[0], "ln1_b": stats1[1], "ln2_g": stats2[0], "ln2_b": stats2[1]},
                           extra=stats2[2]),
    }
    shared = _reduce_tail(late, early)

    grad, delta, new_m, new_v = {}, {}, {}, {}
    for name in BIG:
        grad[name] = shared[name][None]
        d, nm, nv = _adamw(name, given[name][0], shared[name], given["m_" + name][0], given["v_" + name][0])
        delta[name], new_m[name], new_v[name] = d[None], nm[None], nv[None]
    vec_grads, after = _unpack_vecs(shared["vecs"], shapes)
    grad.update(vec_grads)
    for name in ("pool_w", "sgu_w"):
        grad[name] = shared[name].reshape(shapes[name])
    small_names = ("pool_w", "sgu_w") + VEC_NAMES
    state = {pre: [given[pre + name] for name in small_names] for pre in ("", "m_", "v_")}
    ds, nms, nvs = _adamw_small(state[""], [grad[name] for name in small_names], state["m_"], state["v_"])
    delta.update(zip(small_names, ds))
    new_m.update(zip(small_names, nms))
    new_v.update(zip(small_names, nvs))

    sq_err = after[:LOSS_ROWS]
    loss = jnp.sum(sq_err) * (0.5 / D_MODEL)
    return (loss, grad_x.reshape(x.shape), *[grad[name] for name in WEIGHT_ORDER],
            *[delta[name] for name in WEIGHT_ORDER], *[new_m[name] for name in WEIGHT_ORDER],
            *[new_v[name] for name in WEIGHT_ORDER])
```

```python
import math

import jax
import jax.numpy as jnp
from jax import lax
from jax.experimental import pallas as pl
from jax.experimental.pallas import tpu as pltpu

f32 = jnp.float32
bf16 = jnp.bfloat16
MESH = pl.DeviceIdType.MESH

D_MODEL = 1024
POOL_WIDTH = 512
SGU_WIDTH = 512
IN_COLS = POOL_WIDTH + 2 * SGU_WIDTH
D_FF = 2816
POOL_WINDOWS = (2, 4, 8, 16)
GROUP = 128
N_GROUPS = 4
HALO = 16
LN_EPS = 1e-5
ALPHA = float(2.0 ** 0.25)
N_CHIPS = 4

ADAM_LR = 0.001
ADAM_B1 = 0.9
ADAM_B2 = 0.999
ADAM_EPS = 1e-08
ADAM_WD = 0.01
ADAM_STEP = 10

TOKEN_TILE = 512
FFN_BWD_TILE = 512
FF_CHUNK = 256
ROW_SUB = 256
WGRAD_TILE = 1024
V7X_VMEM_LIMIT = 56 * 1024 * 1024

SQUARE_ROWS = N_GROUPS * GROUP
VEC_ROWS = 64
LOSS_ROWS = D_MODEL // GROUP


def _params(**kw):
    return pltpu.CompilerParams(vmem_limit_bytes=V7X_VMEM_LIMIT, **kw)


def _mm(a, b):
    return jnp.dot(a, b, preferred_element_type=f32)


def _mm_nt(a, b):
    return lax.dot_general(a, b, (((1,), (1,)), ((), ())), preferred_element_type=f32)


def _mm_tn(a, b):
    return lax.dot_general(a, b, (((0,), (0,)), ((), ())), preferred_element_type=f32)


def _ln_fwd(r, g, b):
    mu = jnp.mean(r, axis=-1, keepdims=True)
    xc = r - mu
    var = jnp.mean(xc * xc, axis=-1, keepdims=True)
    rstd = lax.rsqrt(var + LN_EPS)
    xhat = xc * rstd
    return xhat * g + b, xhat, rstd


def _ln_bwd(dout, xhat, rstd, g):
    dxhat = dout * g
    m1 = jnp.mean(dxhat, axis=-1, keepdims=True)
    m2 = jnp.mean(dxhat * xhat, axis=-1, keepdims=True)
    return rstd * (dxhat - m1 - xhat * m2)


def _col_sum(a):
    return jnp.sum(a, axis=0, keepdims=True)


def _gelu_parts(z):
    cdf = 0.5 * (1.0 + lax.erf(z * (1.0 / math.sqrt(2.0))))
    pdf = jnp.exp(-0.5 * z * z) * (1.0 / math.sqrt(2.0 * math.pi))
    return cdf, pdf


def _inv_counts(seq_tile, rows):
    pos = seq_tile * rows + lax.broadcasted_iota(jnp.int32, (rows, GROUP), 0) + 1
    return [1.0 / jnp.minimum(pos, w).astype(f32) for w in POOL_WINDOWS]


def _window_sums(e, back):
    n = e.shape[0]

    def shifted(a, s):
        return pltpu.roll(a, s if back else n - s, 0)

    s2 = e + shifted(e, 1)
    s4 = s2[:, GROUP:] + shifted(s2[:, GROUP:], 2)
    s8 = s4[:, GROUP:] + shifted(s4[:, GROUP:], 4)
    s16 = s8[:, GROUP:] + shifted(s8[:, GROUP:], 8)
    return [s2[:, :GROUP], s4[:, :GROUP], s8[:, :GROUP], s16]


def _pooled_groups(xp, halo, inv):
    sums = _window_sums(jnp.concatenate([halo, xp], axis=0), back=True)
    return [sums[g][HALO:] * inv[g] - xp[:, g * GROUP:(g + 1) * GROUP] for g in range(N_GROUPS)]


def _tril_mask():
    r = lax.broadcasted_iota(jnp.int32, (GROUP, GROUP), 0)
    c = lax.broadcasted_iota(jnp.int32, (GROUP, GROUP), 1)
    return (r >= c).astype(f32)


def _gs(g):
    return slice(g * GROUP, (g + 1) * GROUP)


class _Cut:
    def __init__(self, rows, cols, by_cols):
        self.rows, self.cols, self.by_cols = rows, cols, by_cols
        if by_cols:
            self.block_shape = (rows // 2, cols // N_CHIPS)
            self.shard_shape = (rows, cols // N_CHIPS)
        else:
            self.block_shape = (rows // (2 * N_CHIPS), cols)
            self.shard_shape = (rows // N_CHIPS, cols)

    def block(self, ref, chip, half):
        br, bc = self.block_shape
        if self.by_cols:
            return ref.at[pl.ds(pl.multiple_of(half * br, 16), br), pl.ds(pl.multiple_of(chip * bc, 128), bc)]
        return ref.at[pl.ds(pl.multiple_of((2 * chip + half) * br, 8), br), :]

    def shard(self, ref, chip):
        sr, sc = self.shard_shape
        if self.by_cols:
            return ref.at[:, pl.ds(pl.multiple_of(chip * sc, 128), sc)]
        return ref.at[pl.ds(pl.multiple_of(chip * sr, 16), sr), :]

    def half_of_shard(self, ref, half):
        br = self.block_shape[0]
        return ref.at[pl.ds(pl.multiple_of(half * br, 8), br), :]

    def block_index(self, chip, half):
        return (half, chip) if self.by_cols else (2 * chip + half, 0)


CUTS = {
    "w_in": _Cut(D_MODEL, IN_COLS, True),
    "w_out": _Cut(D_MODEL, D_MODEL, False),
    "w_gate_up": _Cut(D_MODEL, 2 * D_FF, True),
    "w_down": _Cut(D_FF, D_MODEL, False),
    "w_gate_up_top": _Cut(D_MODEL // 2, 2 * D_FF, True),
    "w_gate_up_bottom": _Cut(D_MODEL // 2, 2 * D_FF, True),
    "pool_w": _Cut(SQUARE_ROWS, GROUP, False),
    "sgu_w": _Cut(SQUARE_ROWS, GROUP, False),
    "vecs": _Cut(VEC_ROWS, GROUP, False),
}
BIG = ("w_in", "w_out", "w_gate_up", "w_down")
SMALL = ("pool_w", "sgu_w", "vecs")
ANY = pl.BlockSpec(memory_space=pl.ANY)


def _wire_dtype(name):
    return bf16 if name in BIG else f32


def _place():
    x, y, c = lax.axis_index("x"), lax.axis_index("y"), lax.axis_index("c")
    others = [(1 - x, y), (x, 1 - y), (1 - x, 1 - y)]
    return x, y, c, 2 * x + y, others, [2 * ox + oy for ox, oy in others]


def _remote(src, dst, send_sems, recv_sems, k, to):
    return pltpu.make_async_remote_copy(src_ref=src, dst_ref=dst, send_sem=send_sems.at[k], recv_sem=recv_sems.at[k],
                                        device_id=to, device_id_type=MESH)


class _GatherJob:
    def __init__(self, shards):
        self.names = tuple(shards)
        self.arrays = tuple(shards.values())
        n = len(self.names)
        self.out_shapes = [jax.ShapeDtypeStruct((CUTS[name].rows, CUTS[name].cols), bf16) for name in self.names]
        self.scratch_shapes = ([pltpu.SemaphoreType.DMA((6 * n,)), pltpu.SemaphoreType.DMA((6 * n,)),
                                pltpu.SemaphoreType.DMA((2 * n,))]
                               + [pltpu.VMEM(CUTS[name].shard_shape, bf16) for name in self.names])

    def bind(self, shard_refs, full_refs, scratch):
        self.shards, self.full = shard_refs, full_refs
        self.send_sems, self.recv_sems, self.local_sems = scratch[:3]
        self.stages = scratch[3:]
        return self

    def _sends(self):
        _, _, c, me, others, _ = _place()
        return [_remote(CUTS[name].half_of_shard(self.shards[w], c), CUTS[name].block(self.full[w], me, c),
                        self.send_sems, self.recv_sems, 3 * w + k, (*chip, c))
                for w, name in enumerate(self.names) for k, chip in enumerate(others)]

    def _relays(self, half):
        x, y, c, _, _, other_ids = _place()
        n = len(self.names)
        return [_remote(CUTS[name].block(self.full[w], other_ids[k], half),
                        CUTS[name].block(self.full[w], other_ids[k], half),
                        self.send_sems, self.recv_sems, 3 * n + 3 * w + k, (x, y, 1 - c))
                for w, name in enumerate(self.names) for k in range(3)]

    def _stores(self):
        me = _place()[3]
        return [pltpu.make_async_copy(self.stages[w], CUTS[name].shard(self.full[w], me), self.local_sems.at[2 * w + 1])
                for w, name in enumerate(self.names)]

    def start(self):
        loads = [pltpu.make_async_copy(self.shards[w], self.stages[w], self.local_sems.at[2 * w])
                 for w in range(len(self.names))]
        for cp in loads:
            cp.start()
        for cp in self._sends():
            cp.start()
        for load, store in zip(loads, self._stores()):
            load.wait()
            store.start()

    def finish(self):
        _, _, c, _, others, other_ids = _place()
        relays = self._relays(c)
        for w, name in enumerate(self.names):
            for k, chip in enumerate(others):
                landed = CUTS[name].block(self.full[w], other_ids[k], c)
                _remote(landed, landed, self.send_sems, self.recv_sems, 3 * w + k, (*chip, c)).wait_recv()
                relays[3 * w + k].start()
        for cp in self._relays(1 - c):
            cp.wait_recv()
        for cp in self._sends() + relays:
            cp.wait_send()
        for cp in self._stores():
            cp.wait()


class _SwapHalvesJob:
    def __init__(self, grads):
        self.names = tuple(grads)
        self.arrays = tuple(grads.values())
        n = len(self.names)
        self.out_shapes = [jax.ShapeDtypeStruct((N_CHIPS, *CUTS[name].block_shape), f32) for name in self.names]
        self.scratch_shapes = [pltpu.SemaphoreType.DMA((N_CHIPS * n,)), pltpu.SemaphoreType.DMA((N_CHIPS * n,))]

    def bind(self, g_refs, got_refs, scratch):
        self.g_refs, self.got_refs = g_refs, got_refs
        self.send_sems, self.recv_sems = scratch
        return self

    def _copies(self):
        x, y, c, _, _, _ = _place()
        return [_remote(CUTS[name].block(self.g_refs[a], j, 1 - c), self.got_refs[a].at[j], self.send_sems,
                        self.recv_sems, N_CHIPS * a + j, (x, y, 1 - c))
                for a, name in enumerate(self.names) for j in range(N_CHIPS)]

    def start(self):
        for cp in self._copies():
            cp.start()

    def finish(self):
        for cp in self._copies():
            cp.wait()


class _SwapChipsJob:
    def __init__(self, partials):
        self.names = tuple(partials)
        self.arrays = tuple(partials.values())
        n = len(self.names)
        self.out_shapes = [jax.ShapeDtypeStruct((3, *CUTS[name].block_shape), _wire_dtype(name)) for name in self.names]
        self.scratch_shapes = [pltpu.SemaphoreType.DMA((3 * n,)), pltpu.SemaphoreType.DMA((3 * n,))]

    def bind(self, p_refs, got_refs, scratch):
        self.p_refs, self.got_refs = p_refs, got_refs
        self.send_sems, self.recv_sems = scratch
        return self

    def _copies(self):
        _, _, c, _, others, other_ids = _place()
        return [_remote(self.p_refs[a].at[other_ids[k]], self.got_refs[a].at[k], self.send_sems, self.recv_sems,
                        3 * a + k, (*chip, c))
                for a in range(len(self.names)) for k, chip in enumerate(others)]

    def start(self):
        for cp in self._copies():
            cp.start()

    def finish(self):
        for cp in self._copies():
            cp.wait()


class _ShareJob:
    def __init__(self, reduced):
        self.names = tuple(reduced)
        self.arrays = tuple(reduced.values())
        self.big = [a for a, name in enumerate(self.names) if name in BIG]
        self.small = [a for a, name in enumerate(self.names) if name in SMALL]
        self.out_shapes = [jax.ShapeDtypeStruct(CUTS[name].shard_shape if name in BIG
                                                else (CUTS[name].rows, CUTS[name].cols), f32) for name in self.names]
        n_sems = len(self.big) + 7 * len(self.small)
        self.scratch_shapes = ([pltpu.SemaphoreType.DMA((n_sems,)), pltpu.SemaphoreType.DMA((n_sems,)),
                                pltpu.SemaphoreType.DMA((2 * len(self.names),))]
                               + [pltpu.VMEM(CUTS[name].block_shape, f32) for name in self.names])

    def bind(self, f_refs, out_refs, scratch):
        self.f_refs, self.out_refs = f_refs, out_refs
        self.send_sems, self.recv_sems, self.local_sems = scratch[:3]
        self.stages = scratch[3:]
        return self

    def _sem(self, a, which=0):
        if a in self.big:
            return self.big.index(a)
        return len(self.big) + 7 * self.small.index(a) + which

    def _mine(self, a, half):
        me = _place()[3]
        cut = CUTS[self.names[a]]
        return cut.half_of_shard(self.out_refs[a], half) if a in self.big else cut.block(self.out_refs[a], me, half)

    def _to_sibling(self):
        x, y, c, _, _, _ = _place()
        return [_remote(self.f_refs[a], self._mine(a, c), self.send_sems, self.recv_sems, self._sem(a), (x, y, 1 - c))
                for a in range(len(self.names))]

    def _to_chips(self):
        _, _, c, _, others, _ = _place()
        return [_remote(self.f_refs[a], self._mine(a, c), self.send_sems, self.recv_sems, self._sem(a, 1 + k), (*chip, c))
                for a in self.small for k, chip in enumerate(others)]

    def _passes(self, half):
        x, y, c, _, _, other_ids = _place()
        out = []
        for a in self.small:
            for k in range(3):
                blk = CUTS[self.names[a]].block(self.out_refs[a], other_ids[k], half)
                out.append(_remote(blk, blk, self.send_sems, self.recv_sems, self._sem(a, 4 + k), (x, y, 1 - c)))
        return out

    def _stores(self):
        c = _place()[2]
        return [pltpu.make_async_copy(self.stages[a], self._mine(a, c), self.local_sems.at[2 * a + 1])
                for a in range(len(self.names))]

    def start(self):
        loads = [pltpu.make_async_copy(self.f_refs[a], self.stages[a], self.local_sems.at[2 * a])
                 for a in range(len(self.names))]
        for cp in loads:
            cp.start()
        for cp in self._to_sibling() + self._to_chips():
            cp.start()
        for load, store in zip(loads, self._stores()):
            load.wait()
            store.start()

    def finish(self):
        x, y, c, _, others, other_ids = _place()
        passes = self._passes(c)
        for s, a in enumerate(self.small):
            for k, chip in enumerate(others):
                landed = CUTS[self.names[a]].block(self.out_refs[a], other_ids[k], c)
                _remote(landed, landed, self.send_sems, self.recv_sems, self._sem(a, 1 + k), (*chip, c)).wait_recv()
                passes[3 * s + k].start()
        for a in range(len(self.names)):
            theirs = self._mine(a, 1 - c)
            _remote(theirs, theirs, self.send_sems, self.recv_sems, self._sem(a), (x, y, 1 - c)).wait_recv()
        for cp in self._passes(1 - c):
            cp.wait_recv()
        for cp in self._to_sibling() + self._to_chips() + passes:
            cp.wait_send()
        for cp in self._stores():
            cp.wait()


def _call(body, *, name, grid, in_specs, out_specs, out_shape, args, scratch_shapes=(), jobs=()):
    n_in, n_out, n_scr = len(in_specs), len(out_specs), len(scratch_shapes)
    j_in = [len(j.arrays) for j in jobs]
    j_out = [len(j.out_shapes) for j in jobs]
    j_scr = [len(j.scratch_shapes) for j in jobs]

    def wrapped(*refs):
        refs = list(refs)

        def take(k):
            head = refs[:k]
            del refs[:k]
            return head

        ins, jins = take(n_in), [take(k) for k in j_in]
        outs, jouts = take(n_out), [take(k) for k in j_out]
        scr, jscr = take(n_scr), [take(k) for k in j_scr]
        bound = [j.bind(a, b, c) for j, a, b, c in zip(jobs, jins, jouts, jscr)]
        if not grid:
            for b in bound:
                b.start()
            body(*ins, *outs, *scr)
            for b in bound:
                b.finish()
            return
        if not bound:
            body(*ins, *outs, *scr)
            return
        first = _all([pl.program_id(d) == 0 for d in range(len(grid))])
        last = _all([pl.program_id(d) == grid[d] - 1 for d in range(len(grid))])

        @pl.when(first)
        def _():
            for b in bound:
                b.start()

        body(*ins, *outs, *scr)

        @pl.when(last)
        def _():
            for b in bound:
                b.finish()

    kw = dict(grid=grid) if grid else {}
    results = pl.pallas_call(
        wrapped, name=name,
        in_specs=list(in_specs) + [ANY] * sum(j_in), out_specs=list(out_specs) + [ANY] * sum(j_out),
        out_shape=list(out_shape) + [s for j in jobs for s in j.out_shapes],
        scratch_shapes=list(scratch_shapes) + [s for j in jobs for s in j.scratch_shapes],
        compiler_params=_params(), **kw,
    )(*args, *[a for j in jobs for a in j.arrays])
    results = list(results)
    own, rest = results[:n_out], results[n_out:]
    per_job = []
    for j, k in zip(jobs, j_out):
        per_job.append(dict(zip(j.names, rest[:k])))
        rest = rest[k:]
    return own, per_job


def _all(conds):
    out = conds[0]
    for c in conds[1:]:
        out = jnp.logical_and(out, c)
    return out


def _alone(job, name):
    return _call(lambda: None, name=name, grid=None, in_specs=[], out_specs=[], out_shape=[], args=[], jobs=[job])[1][0]


def _cast_x(x2, tile, jobs):
    tokens = x2.shape[0]

    def body(x_ref, xb_ref):
        xb_ref[...] = x_ref[...].astype(bf16)

    row = lambda i: (i, 0)
    return _call(
        body, name="cast_x", grid=(tokens // tile,),
        in_specs=[pl.BlockSpec((tile, D_MODEL), row)], out_specs=[pl.BlockSpec((tile, D_MODEL), row)],
        out_shape=[jax.ShapeDtypeStruct((tokens, D_MODEL), bf16)], args=(x2,), jobs=jobs)


def _fwd_proj(x_b, w_in_b, tile, jobs):
    tokens = x_b.shape[0]

    def body(x_ref, w_ref, xp_ref, pre_ref):
        proj = _mm(x_ref[...], w_ref[...])
        xp_ref[...] = proj[:, :POOL_WIDTH]
        pre_ref[...] = proj[:, POOL_WIDTH:].astype(bf16)

    row = lambda i: (i, 0)
    return _call(
        body, name="fwd_proj", grid=(tokens // tile,),
        in_specs=[pl.BlockSpec((tile, D_MODEL), row), pl.BlockSpec((D_MODEL, IN_COLS), lambda i: (0, 0))],
        out_specs=[pl.BlockSpec((tile, POOL_WIDTH), row), pl.BlockSpec((tile, 2 * SGU_WIDTH), row)],
        out_shape=[jax.ShapeDtypeStruct((tokens, POOL_WIDTH), f32), jax.ShapeDtypeStruct((tokens, 2 * SGU_WIDTH), bf16)],
        args=(x_b, w_in_b), jobs=jobs)


def _small_specs():
    return [pl.BlockSpec((N_GROUPS, GROUP, GROUP), lambda i: (0, 0, 0)),
            pl.BlockSpec((1, POOL_WIDTH), lambda i: (0, 0)),
            pl.BlockSpec((1, SGU_WIDTH), lambda i: (0, 0)),
            pl.BlockSpec((1, SGU_WIDTH), lambda i: (0, 0)),
            pl.BlockSpec((N_GROUPS, GROUP, GROUP), lambda i: (0, 0, 0)),
            pl.BlockSpec((N_GROUPS, GROUP, GROUP), lambda i: (0, 0, 0))]


def _fwd_mix(xp_all, pre_b, x2, small, w_out_b, ln1_g, ln1_b, tile, seq, jobs):
    tokens = x2.shape[0]
    tps = seq // tile
    hb = tile // HALO

    def body(xp_ref, halo_ref, pre_ref, x_ref, pw_ref, ps_ref, lg_ref, lb_ref, sw_ref, sb_ref, wout_ref, g1_ref, b1_ref,
             mix_ref, r1_ref, h_ref):
        seq_tile = pl.program_id(0) % tps
        xp = xp_ref[...]
        halo = jnp.where(seq_tile == 0, 0.0, halo_ref[...])
        pooled = _pooled_groups(xp, halo, _inv_counts(seq_tile, tile))
        for g in range(N_GROUPS):
            po = _mm(pooled[g].astype(bf16), pw_ref[g].astype(bf16)) * ps_ref[:, _gs(g)]
            mix_ref[:, _gs(g)] = po.astype(bf16)

        pre = pre_ref[...].astype(f32)
        cdf, _ = _gelu_parts(pre)
        zg = pre * cdf
        u = zg[:, :SGU_WIDTH]
        vln, _, _ = _ln_fwd(zg[:, SGU_WIDTH:], lg_ref[...], lb_ref[...])
        vb = vln.astype(bf16)
        mask = _tril_mask()
        for h in range(N_GROUPS):
            wm = (sw_ref[h] * mask).astype(bf16)
            bias = sb_ref[h]
            for c in range(tile // GROUP):
                rows = slice(c * GROUP, (c + 1) * GROUP)
                mixed = _mm(wm, vb[rows, _gs(h)]) + bias
                mix_ref[rows, POOL_WIDTH + h * GROUP:POOL_WIDTH + (h + 1) * GROUP] = (u[rows, _gs(h)] * mixed).astype(bf16)

        r1 = ALPHA * x_ref[...] + _mm(mix_ref[...], wout_ref[...])
        r1_ref[...] = r1
        h1, _, _ = _ln_fwd(r1, g1_ref[...], b1_ref[...])
        h_ref[...] = h1.astype(bf16)

    row = lambda i: (i, 0)
    vec = pl.BlockSpec((1, D_MODEL), lambda i: (0, 0))
    return _call(
        body, name="fwd_mix", grid=(tokens // tile,),
        in_specs=[pl.BlockSpec((tile, POOL_WIDTH), row),
                  pl.BlockSpec((HALO, POOL_WIDTH), lambda i: (jnp.maximum(i * hb - 1, 0), 0)),
                  pl.BlockSpec((tile, 2 * SGU_WIDTH), row),
                  pl.BlockSpec((tile, D_MODEL), row)] + _small_specs()
                 + [pl.BlockSpec((D_MODEL, D_MODEL), lambda i: (0, 0)), vec, vec],
        out_specs=[pl.BlockSpec((tile, D_MODEL), row)] * 3,
        out_shape=[jax.ShapeDtypeStruct((tokens, D_MODEL), bf16),
                   jax.ShapeDtypeStruct((tokens, D_MODEL), f32),
                   jax.ShapeDtypeStruct((tokens, D_MODEL), bf16)],
        args=(xp_all, xp_all, pre_b, x2, *small, w_out_b, ln1_g, ln1_b), jobs=jobs)


def _fwd_gate_up(h_b, w_top, w_bottom, tile, jobs):
    tokens = h_b.shape[0]
    half = D_MODEL // 2

    def body(h_ref, wt_ref, wb_ref, gu_ref, a_ref):
        ht, hb = h_ref[:, :half], h_ref[:, half:]
        for c in range(D_FF // FF_CHUNK):
            gcols = slice(c * FF_CHUNK, (c + 1) * FF_CHUNK)
            ucols = slice(D_FF + c * FF_CHUNK, D_FF + (c + 1) * FF_CHUNK)
            gate = _mm(ht, wt_ref[:, gcols]) + _mm(hb, wb_ref[:, gcols])
            up = _mm(ht, wt_ref[:, ucols]) + _mm(hb, wb_ref[:, ucols])
            sg = jax.nn.sigmoid(gate)
            silu = gate * sg
            gu_ref[:, gcols] = (up * (sg + silu * (1.0 - sg))).astype(bf16)
            gu_ref[:, ucols] = silu.astype(bf16)
            a_ref[:, gcols] = (silu * up).astype(bf16)

    return _call(
        body, name="fwd_gate_up", grid=(tokens // tile,),
        in_specs=[pl.BlockSpec((tile, D_MODEL), lambda i: (i, 0)),
                  pl.BlockSpec((half, 2 * D_FF), lambda i: (0, 0), pipeline_mode=pl.Buffered(1)),
                  pl.BlockSpec((half, 2 * D_FF), lambda i: (0, 0), pipeline_mode=pl.Buffered(1))],
        out_specs=[pl.BlockSpec((tile, 2 * D_FF), lambda i: (i, 0)),
                   pl.BlockSpec((tile, D_FF), lambda i: (i, 0))],
        out_shape=[jax.ShapeDtypeStruct((tokens, 2 * D_FF), bf16),
                   jax.ShapeDtypeStruct((tokens, D_FF), bf16)],
        args=(h_b, w_top, w_bottom), jobs=jobs)


def _fwd_down_loss(a_b, w_dn_b, r1, target, ln1_g, ln1_b, ln2_g, ln2_b, tile):
    tokens = a_b.shape[0]

    def body(a_ref, w_ref, r1_ref, t_ref, g1_ref, b1_ref, g2_ref, b2_ref, dr2_ref, dr2b_ref, st_ref):
        @pl.when(pl.program_id(0) == 0)
        def _():
            st_ref[...] = jnp.zeros_like(st_ref)

        sub = min(ROW_SUB, tile)
        for s in range(tile // sub):
            rows = slice(s * sub, (s + 1) * sub)
            h1, _, _ = _ln_fwd(r1_ref[rows, :], g1_ref[...], b1_ref[...])
            r2 = ALPHA * h1 + _mm(a_ref[rows, :], w_ref[...])
            y, xhat, rstd = _ln_fwd(r2, g2_ref[...], b2_ref[...])
            diff = y - t_ref[rows, :]
            dy = diff * (1.0 / D_MODEL)
            st_ref[0:1, :] += _col_sum(dy * xhat)
            st_ref[1:2, :] += _col_sum(dy)
            st_ref[2:3, :] += _col_sum(diff * diff)
            dr2 = _ln_bwd(dy, xhat, rstd, g2_ref[...])
            dr2_ref[rows, :] = dr2
            dr2b_ref[rows, :] = dr2.astype(bf16)

    row = lambda i: (i, 0)
    vec = pl.BlockSpec((1, D_MODEL), lambda i: (0, 0))
    return _call(
        body, name="fwd_down_loss", grid=(tokens // tile,),
        in_specs=[pl.BlockSpec((tile, D_FF), row), pl.BlockSpec((D_FF, D_MODEL), lambda i: (0, 0)),
                  pl.BlockSpec((tile, D_MODEL), row), pl.BlockSpec((tile, D_MODEL), row), vec, vec, vec, vec],
        out_specs=[pl.BlockSpec((tile, D_MODEL), row), pl.BlockSpec((tile, D_MODEL), row),
                   pl.BlockSpec((8, D_MODEL), lambda i: (0, 0))],
        out_shape=[jax.ShapeDtypeStruct((tokens, D_MODEL), f32), jax.ShapeDtypeStruct((tokens, D_MODEL), bf16),
                   jax.ShapeDtypeStruct((8, D_MODEL), f32)],
        args=(a_b, w_dn_b, r1, target, ln1_g, ln1_b, ln2_g, ln2_b))[0]


def _bwd_gate_up(dr2, gu_b, w_dn_b, tile, jobs):
    tokens = dr2.shape[0]

    def body(d_ref, gu_ref, w_ref, dgu_ref):
        d = d_ref[...].astype(bf16)
        for c in range(D_FF // FF_CHUNK):
            gcols = slice(c * FF_CHUNK, (c + 1) * FF_CHUNK)
            ucols = slice(D_FF + c * FF_CHUNK, D_FF + (c + 1) * FF_CHUNK)
            da = _mm_nt(d, w_ref[gcols, :])
            dgu_ref[:, gcols] = (da * gu_ref[:, gcols].astype(f32)).astype(bf16)
            dgu_ref[:, ucols] = (da * gu_ref[:, ucols].astype(f32)).astype(bf16)

    return _call(
        body, name="bwd_gate_up", grid=(tokens // tile,),
        in_specs=[pl.BlockSpec((tile, D_MODEL), lambda i: (i, 0)),
                  pl.BlockSpec((tile, 2 * D_FF), lambda i: (i, 0)),
                  pl.BlockSpec((D_FF, D_MODEL), lambda i: (0, 0))],
        out_specs=[pl.BlockSpec((tile, 2 * D_FF), lambda i: (i, 0))],
        out_shape=[jax.ShapeDtypeStruct((tokens, 2 * D_FF), bf16)],
        args=(dr2, gu_b, w_dn_b), jobs=jobs)


def _bwd_ffn_in(dgu_b, w_top, w_bottom, dr2, r1, ln1_g, ln1_b, tile, jobs):
    tokens = dr2.shape[0]

    def body(dgu_ref, wt_ref, wb_ref, d_ref, r1_ref, g1_ref, b1_ref, dr1_ref, dr1b_ref, st_ref):
        @pl.when(pl.program_id(0) == 0)
        def _():
            st_ref[...] = jnp.zeros_like(st_ref)

        dgu = dgu_ref[...]
        dh = ALPHA * d_ref[...] + jnp.concatenate([_mm_nt(dgu, wt_ref[...]), _mm_nt(dgu, wb_ref[...])], axis=1)
        _, xhat, rstd = _ln_fwd(r1_ref[...], g1_ref[...], b1_ref[...])
        st_ref[0:1, :] += _col_sum(dh * xhat)
        st_ref[1:2, :] += _col_sum(dh)
        dr1 = _ln_bwd(dh, xhat, rstd, g1_ref[...])
        dr1_ref[...] = dr1
        dr1b_ref[...] = dr1.astype(bf16)

    row = lambda i: (i, 0)
    vec = pl.BlockSpec((1, D_MODEL), lambda i: (0, 0))
    return _call(
        body, name="bwd_ffn_in", grid=(tokens // tile,),
        in_specs=[pl.BlockSpec((tile, 2 * D_FF), row),
                  pl.BlockSpec((D_MODEL // 2, 2 * D_FF), lambda i: (0, 0), pipeline_mode=pl.Buffered(1)),
                  pl.BlockSpec((D_MODEL // 2, 2 * D_FF), lambda i: (0, 0), pipeline_mode=pl.Buffered(1)),
                  pl.BlockSpec((tile, D_MODEL), row), pl.BlockSpec((tile, D_MODEL), row), vec, vec],
        out_specs=[pl.BlockSpec((tile, D_MODEL), row), pl.BlockSpec((tile, D_MODEL), row),
                   pl.BlockSpec((8, D_MODEL), lambda i: (0, 0))],
        out_shape=[jax.ShapeDtypeStruct((tokens, D_MODEL), f32), jax.ShapeDtypeStruct((tokens, D_MODEL), bf16),
                   jax.ShapeDtypeStruct((8, D_MODEL), f32)],
        args=(dgu_b, w_top, w_bottom, dr2, r1, ln1_g, ln1_b), jobs=jobs)


def _bwd_mix(dr1, xp_all, pre_b, small, w_out_b, tile, seq, jobs):
    tokens = dr1.shape[0]
    tps = seq // tile
    hb = tile // HALO
    steps = tokens // tile

    def body(dr1_ref, xp_ref, halo_ref, pre_ref, wout_ref, pw_ref, ps_ref, lg_ref, lb_ref, sw_ref, sb_ref,
             dpool_ref, dpre_ref, gpw_ref, gsw_ref, gsb_ref, vec_ref, du_ref, dvln_ref):
        step = pl.program_id(0)
        seq_tile = step % tps

        @pl.when(step == 0)
        def _():
            gpw_ref[...] = jnp.zeros_like(gpw_ref)
            gsw_ref[...] = jnp.zeros_like(gsw_ref)
            gsb_ref[...] = jnp.zeros_like(gsb_ref)
            vec_ref[...] = jnp.zeros_like(vec_ref)

        dmix = _mm_nt(dr1_ref[...].astype(bf16), wout_ref[...])

        xp = xp_ref[...]
        halo = jnp.where(seq_tile == 0, 0.0, halo_ref[...])
        pooled = _pooled_groups(xp, halo, _inv_counts(seq_tile, tile))
        for g in range(N_GROUPS):
            pb = pooled[g].astype(bf16)
            pwb = pw_ref[g].astype(bf16)
            dpo = dmix[:, _gs(g)]
            vec_ref[0:1, _gs(g)] += _col_sum(dpo * _mm(pb, pwb))
            dpo_b = (dpo * ps_ref[:, _gs(g)]).astype(bf16)
            gpw_ref[g] += _mm_tn(pb, dpo_b)
            dpool_ref[:, _gs(g)] = _mm_nt(dpo_b, pwb)

        pre = pre_ref[...].astype(f32)
        cdf, pdf = _gelu_parts(pre)
        zg = pre * cdf
        u = zg[:, :SGU_WIDTH]
        vln, vhat, rstd = _ln_fwd(zg[:, SGU_WIDTH:], lg_ref[...], lb_ref[...])
        vb = vln.astype(bf16)
        mask = _tril_mask()
        for h in range(N_GROUPS):
            wm = (sw_ref[h] * mask).astype(bf16)
            bias = sb_ref[h]
            gsw = jnp.zeros((GROUP, GROUP), f32)
            gsb = jnp.zeros((GROUP, GROUP), f32)
            for c in range(tile // GROUP):
                rows = slice(c * GROUP, (c + 1) * GROUP)
                v_ch = vb[rows, _gs(h)]
                d = dmix[rows, POOL_WIDTH + h * GROUP:POOL_WIDTH + (h + 1) * GROUP]
                du_ref[rows, _gs(h)] = d * (_mm(wm, v_ch) + bias)
                dmixed = d * u[rows, _gs(h)]
                gsb += dmixed
                dmixed_b = dmixed.astype(bf16)
                gsw += _mm_nt(dmixed_b, v_ch)
                dvln_ref[rows, _gs(h)] = _mm_tn(wm, dmixed_b)
            gsw_ref[h] += gsw * mask
            gsb_ref[h] += gsb

        dvln = dvln_ref[...]
        vec_ref[1:2, :] += _col_sum(dvln * vhat)
        vec_ref[2:3, :] += _col_sum(dvln)
        dgelu = cdf + pre * pdf
        dpre_ref[:, :SGU_WIDTH] = (du_ref[...] * dgelu[:, :SGU_WIDTH]).astype(bf16)
        dpre_ref[:, SGU_WIDTH:] = (_ln_bwd(dvln, vhat, rstd, lg_ref[...]) * dgelu[:, SGU_WIDTH:]).astype(bf16)

        @pl.when(step == steps - 1)
        def _():
            for h in range(N_GROUPS):
                gsb_ref[h] = jnp.broadcast_to(jnp.sum(gsb_ref[h], axis=1, keepdims=True), (GROUP, GROUP))

    row = lambda i: (i, 0)
    sq = jax.ShapeDtypeStruct((N_GROUPS, GROUP, GROUP), f32)
    sq_spec = pl.BlockSpec((N_GROUPS, GROUP, GROUP), lambda i: (0, 0, 0))
    return _call(
        body, name="bwd_mix", grid=(steps,),
        in_specs=[pl.BlockSpec((tile, D_MODEL), row), pl.BlockSpec((tile, POOL_WIDTH), row),
                  pl.BlockSpec((HALO, POOL_WIDTH), lambda i: (jnp.maximum(i * hb - 1, 0), 0)),
                  pl.BlockSpec((tile, 2 * SGU_WIDTH), row),
                  pl.BlockSpec((D_MODEL, D_MODEL), lambda i: (0, 0))] + _small_specs(),
        out_specs=[pl.BlockSpec((tile, POOL_WIDTH), row), pl.BlockSpec((tile, 2 * SGU_WIDTH), row),
                   sq_spec, sq_spec, sq_spec, pl.BlockSpec((8, POOL_WIDTH), lambda i: (0, 0))],
        out_shape=[jax.ShapeDtypeStruct((tokens, POOL_WIDTH), f32), jax.ShapeDtypeStruct((tokens, 2 * SGU_WIDTH), bf16),
                   sq, sq, sq, jax.ShapeDtypeStruct((8, POOL_WIDTH), f32)],
        scratch_shapes=[pltpu.VMEM((tile, SGU_WIDTH), f32), pltpu.VMEM((tile, SGU_WIDTH), f32)],
        args=(dr1, xp_all, xp_all, pre_b, w_out_b, *small), jobs=jobs)


def _bwd_in(dpool, dpre_b, dr1, w_in_b, tile, seq):
    tokens = dr1.shape[0]
    tps = seq // tile
    hb = tile // HALO
    last_halo = tokens // HALO - 1

    def body(dpool_ref, nxt_ref, dpre_ref, dr1_ref, w_ref, dx_ref, dproj_ref):
        seq_tile = pl.program_id(0) % tps
        inv = _inv_counts(seq_tile, tile)
        dpl = dpool_ref[...]
        nxt = jnp.where(seq_tile == tps - 1, 0.0, nxt_ref[...])
        scaled = jnp.concatenate([dpl[:, _gs(g)] * inv[g] for g in range(N_GROUPS)], axis=1)
        scaled_nxt = jnp.concatenate([nxt[:, _gs(g)] * (1.0 / POOL_WINDOWS[g]) for g in range(N_GROUPS)], axis=1)
        sums = _window_sums(jnp.concatenate([scaled, scaled_nxt], axis=0), back=False)
        for g in range(N_GROUPS):
            dproj_ref[:, _gs(g)] = (sums[g][:tile] - dpl[:, _gs(g)]).astype(bf16)
        dproj_ref[:, POOL_WIDTH:] = dpre_ref[...]
        dx_ref[...] = ALPHA * dr1_ref[...] + _mm_nt(dproj_ref[...], w_ref[...])

    row = lambda i: (i, 0)
    return _call(
        body, name="bwd_in", grid=(tokens // tile,),
        in_specs=[pl.BlockSpec((tile, POOL_WIDTH), row),
                  pl.BlockSpec((HALO, POOL_WIDTH), lambda i: (jnp.minimum((i + 1) * hb, last_halo), 0)),
                  pl.BlockSpec((tile, 2 * SGU_WIDTH), row),
                  pl.BlockSpec((tile, D_MODEL), row),
                  pl.BlockSpec((D_MODEL, IN_COLS), lambda i: (0, 0))],
        out_specs=[pl.BlockSpec((tile, D_MODEL), row), pl.BlockSpec((tile, IN_COLS), row)],
        out_shape=[jax.ShapeDtypeStruct((tokens, D_MODEL), f32), jax.ShapeDtypeStruct((tokens, IN_COLS), bf16)],
        args=(dpool, dpool, dpre_b, dr1, w_in_b))[0]


def _wgrad(a, b, col_tile, tile, name, jobs=()):
    tokens, m = a.shape
    n = b.shape[1]

    def body(a_ref, b_ref, o_ref):
        @pl.when(pl.program_id(1) == 0)
        def _():
            o_ref[...] = jnp.zeros_like(o_ref)

        o_ref[...] += _mm_tn(a_ref[...].astype(bf16), b_ref[...].astype(bf16))

    (out,), got = _call(
        body, name=name, grid=(n // col_tile, tokens // tile),
        in_specs=[pl.BlockSpec((tile, m), lambda j, k: (k, 0)),
                  pl.BlockSpec((tile, col_tile), lambda j, k: (k, j))],
        out_specs=[pl.BlockSpec((m, col_tile), lambda j, k: (0, j))],
        out_shape=[jax.ShapeDtypeStruct((m, n), f32)],
        args=(a, b), jobs=jobs)
    return out, got


def _add_halves(name, g, got, place):
    cut = CUTS[name]
    br, bc = cut.block_shape
    wire = _wire_dtype(name)

    def body(place_ref, g_ref, got_ref, o_ref, wire_ref):
        s = g_ref[...] + got_ref[...]
        wire_ref[...] = s.astype(wire)

        @pl.when(pl.program_id(0) == place_ref[1])
        def _():
            o_ref[...] = s

    blocks = pl.BlockSpec((None, br, bc), lambda j, place_ref: (j, 0, 0))
    return pl.pallas_call(
        body, name="reduce_add_halves_" + name,
        grid_spec=pltpu.PrefetchScalarGridSpec(
            num_scalar_prefetch=1, grid=(N_CHIPS,),
            in_specs=[pl.BlockSpec((br, bc), lambda j, place_ref: cut.block_index(j, place_ref[0])), blocks],
            out_specs=[pl.BlockSpec((br, bc), lambda j, place_ref: (0, 0)), blocks]),
        out_shape=[jax.ShapeDtypeStruct((br, bc), f32), jax.ShapeDtypeStruct((N_CHIPS, br, bc), wire)],
        compiler_params=_params(),
    )(place, g, got)


def _reduce_tail(late, early):
    late_names, early_names = tuple(late), tuple(early)
    names = early_names + late_names
    nl, ne, n = len(late_names), len(early_names), len(names)
    cuts = [CUTS[name] for name in names]
    is_big = [name in BIG for name in names]
    share_base, share_sem = 7 * nl, []
    for i in range(n):
        share_sem.append(share_base)
        share_base += 1 if is_big[i] else 7
    n_in = nl + 2 * ne

    def body(*refs):
        g_refs = refs[:nl]
        sums_refs, got_refs = refs[nl:n_in:2], refs[nl + 1:n_in:2]
        out_refs = refs[n_in:n_in + n]
        send_sems, recv_sems, local_sems = refs[n_in + n:n_in + n + 3]
        vm = refs[n_in + n + 3:]
        own, recv_a, wire, recv_b = vm[0:4 * nl:4], vm[1:4 * nl:4], vm[2:4 * nl:4], vm[3:4 * nl:4]
        acc, gotv = vm[4 * nl::2], vm[4 * nl + 1::2]
        x, y, c, me, others, other_ids = _place()
        sibling = (x, y, 1 - c)

        def mine(i, half):
            return cuts[i].half_of_shard(out_refs[i], half) if is_big[i] else cuts[i].block(out_refs[i], me, half)

        def reduced(i):
            return acc[i] if i < ne else own[i - ne].at[me]

        sent, stores = [], []

        def share(i):
            store = pltpu.make_async_copy(reduced(i), mine(i, c), local_sems.at[4 * nl + 2 * ne + i])
            store.start()
            stores.append(store)
            to = [sibling] if is_big[i] else [sibling] + [(*chip, c) for chip in others]
            for which, device in enumerate(to):
                cp = _remote(reduced(i), mine(i, c), send_sems, recv_sems, share_sem[i] + which, device)
                cp.start()
                sent.append(cp)

        early_loads = []
        for e in range(ne):
            early_loads.append(pltpu.make_async_copy(sums_refs[e], acc[e], local_sems.at[4 * nl + 2 * e]))
            early_loads.append(pltpu.make_async_copy(got_refs[e], gotv[e], local_sems.at[4 * nl + 2 * e + 1]))
        late_loads = [pltpu.make_async_copy(cuts[ne + l].block(g_refs[l], j, c), own[l].at[j], local_sems.at[4 * l + j])
                      for l in range(nl) for j in range(N_CHIPS)]
        halves = [_remote(cuts[ne + l].block(g_refs[l], j, 1 - c), recv_a[l].at[j], send_sems, recv_sems, 4 * l + j, sibling)
                  for l in range(nl) for j in range(N_CHIPS)]
        for cp in early_loads + late_loads + halves:
            cp.start()

        for cp in early_loads:
            cp.wait()
        for e in range(ne):
            acc[e][...] = ((acc[e][...] + gotv[e][0].astype(f32)) + gotv[e][1].astype(f32)) + gotv[e][2].astype(f32)
            share(e)

        for cp in late_loads:
            cp.wait()
        for cp in halves:
            cp.wait_recv()
        for l in range(nl):
            for j in range(N_CHIPS):
                s = own[l][j] + recv_a[l][j]
                own[l][j] = s
                wire[l][j] = s.astype(wire[l].dtype)
        chips = [_remote(wire[l].at[other_ids[k]], recv_b[l].at[k], send_sems, recv_sems, 4 * nl + 3 * l + k, (*chip, c))
                 for l in range(nl) for k, chip in enumerate(others)]
        for cp in chips:
            cp.start()
        for cp in chips:
            cp.wait_recv()
        for l in range(nl):
            mine_l = own[l].at[me]
            mine_l[...] = ((mine_l[...] + recv_b[l][0].astype(f32)) + recv_b[l][1].astype(f32)) + recv_b[l][2].astype(f32)
            share(ne + l)

        for i in range(n):
            if not is_big[i]:
                for k, chip in enumerate(others):
                    landed = cuts[i].block(out_refs[i], other_ids[k], c)
                    _remote(landed, landed, send_sems, recv_sems, share_sem[i] + 1 + k, (*chip, c)).wait_recv()
                    cp = _remote(landed, landed, send_sems, recv_sems, share_sem[i] + 4 + k, sibling)
                    cp.start()
                    sent.append(cp)
        for i in range(n):
            theirs = mine(i, 1 - c)
            _remote(theirs, theirs, send_sems, recv_sems, share_sem[i], sibling).wait_recv()
            if not is_big[i]:
                for k in range(3):
                    passed = cuts[i].block(out_refs[i], other_ids[k], 1 - c)
                    _remote(passed, passed, send_sems, recv_sems, share_sem[i] + 4 + k, sibling).wait_recv()
        for cp in halves + chips + sent:
            cp.wait_send()
        for cp in stores:
            cp.wait()

    scratch = [pltpu.SemaphoreType.DMA((share_base,)), pltpu.SemaphoreType.DMA((share_base,)),
               pltpu.SemaphoreType.DMA((4 * nl + 2 * ne + n,))]
    for name in late_names:
        block = CUTS[name].block_shape
        scratch += [pltpu.VMEM((N_CHIPS, *block), f32), pltpu.VMEM((N_CHIPS, *block), f32),
                    pltpu.VMEM((N_CHIPS, *block), _wire_dtype(name)), pltpu.VMEM((3, *block), _wire_dtype(name))]
    for name in early_names:
        block = CUTS[name].block_shape
        scratch += [pltpu.VMEM(block, f32), pltpu.VMEM((3, *block), _wire_dtype(name))]
    args = [late[name] for name in late_names] + [a for name in early_names for a in early[name]]
    outs = pl.pallas_call(
        body, name="reduce_tail",
        in_specs=[ANY] * n_in, out_specs=[ANY] * n,
        out_shape=[jax.ShapeDtypeStruct(CUTS[name].shard_shape if name in BIG else (CUTS[name].rows, CUTS[name].cols), f32)
                   for name in names],
        scratch_shapes=scratch, compiler_params=_params(),
    )(*args)
    return dict(zip(names, outs))


def _adamw_refs(w_ref, g_ref, m_ref, v_ref, d_ref, nm_ref, nv_ref):
    g = g_ref[...]
    nm = ADAM_B1 * m_ref[...] + (1.0 - ADAM_B1) * g
    nv = ADAM_B2 * v_ref[...] + (1.0 - ADAM_B2) * jnp.square(g)
    m_hat = nm / (1.0 - ADAM_B1 ** ADAM_STEP)
    v_hat = nv / (1.0 - ADAM_B2 ** ADAM_STEP)
    d_ref[...] = -ADAM_LR * (m_hat / (jnp.sqrt(v_hat) + ADAM_EPS) + ADAM_WD * w_ref[...])
    nm_ref[...] = nm
    nv_ref[...] = nv


def _adamw_small(ws, gs, ms, vs):
    n = len(ws)

    def body(*refs):
        for i in range(n):
            _adamw_refs(*[refs[k * n + i] for k in range(7)])

    whole = pl.BlockSpec(memory_space=pltpu.VMEM)
    outs = pl.pallas_call(
        body, name="adamw_small",
        in_specs=[whole] * (4 * n), out_specs=[whole] * (3 * n),
        out_shape=[jax.ShapeDtypeStruct(w.shape, f32) for w in ws] * 3,
        compiler_params=_params(),
    )(*ws, *gs, *ms, *vs)
    return outs[:n], outs[n:2 * n], outs[2 * n:]


def _adamw(name, w, g, m, v):
    rows, cols = w.shape
    rt = rows // 4

    def body(w_ref, g_ref, m_ref, v_ref, d_ref, nm_ref, nv_ref):
        _adamw_refs(w_ref, g_ref, m_ref, v_ref, d_ref, nm_ref, nv_ref)

    spec = pl.BlockSpec((rt, cols), lambda i: (i, 0))
    shape = jax.ShapeDtypeStruct((rows, cols), f32)
    return pl.pallas_call(
        body, name="adamw_" + name, grid=(rows // rt,),
        in_specs=[spec] * 4, out_specs=[spec] * 3, out_shape=[shape] * 3,
        compiler_params=_params(),
    )(w, g, m, v)


VEC_NAMES = ("pool_scale", "sgu_ln_g", "sgu_ln_b", "sgu_b", "ln1_g", "ln1_b", "ln2_g", "ln2_b")
WEIGHT_ORDER = ("w_in", "pool_w", "pool_scale", "sgu_ln_g", "sgu_ln_b", "sgu_w", "sgu_b", "w_out", "ln1_g", "ln1_b",
                "w_gate_up", "w_down", "ln2_g", "ln2_b")


def _pack_vecs(parts, extra=None):
    rows = [parts[name].reshape(-1, GROUP) for name in VEC_NAMES]
    if extra is not None:
        rows.append(extra.reshape(-1, GROUP))
    used = sum(r.shape[0] for r in rows)
    return jnp.concatenate(rows + [jnp.zeros((VEC_ROWS - used, GROUP), f32)], axis=0)


def _unpack_vecs(packed, shapes):
    out, at = {}, 0
    for name in VEC_NAMES:
        rows = math.prod(shapes[name]) // GROUP
        out[name] = packed[at:at + rows].reshape(shapes[name])
        at += rows
    return out, packed[at:]


def kernel(x, w_in, pool_w, pool_scale, sgu_ln_g, sgu_ln_b, sgu_w, sgu_b, w_out, ln1_g, ln1_b, w_gate_up, w_down, ln2_g, ln2_b, loss_target, m_w_in, m_pool_w, m_pool_scale, m_sgu_ln_g, m_sgu_ln_b, m_sgu_w, m_sgu_b, m_w_out, m_ln1_g, m_ln1_b, m_w_gate_up, m_w_down, m_ln2_g, m_ln2_b, v_w_in, v_pool_w, v_pool_scale, v_sgu_ln_g, v_sgu_ln_b, v_sgu_w, v_sgu_b, v_w_out, v_ln1_g, v_ln1_b, v_w_gate_up, v_w_down, v_ln2_g, v_ln2_b):
    given = dict(locals())
    batch, seq, _ = x.shape
    tokens = batch * seq
    tile = min(TOKEN_TILE, seq)
    ffn_bwd_tile = min(FFN_BWD_TILE, seq)
    wtile = min(WGRAD_TILE, tokens)
    shapes = {name: given[name].shape for name in WEIGHT_ORDER}

    x2 = x.reshape(tokens, D_MODEL)
    target = loss_target.reshape(tokens, D_MODEL)
    small = (pool_w[0], pool_scale[0][None], sgu_ln_g[0][None], sgu_ln_b[0][None], sgu_w[0],
             jnp.broadcast_to(sgu_b[0][:, :, None], (N_GROUPS, GROUP, GROUP)))
    g1, b1, g2, b2 = ln1_g[0][None], ln1_b[0][None], ln2_g[0][None], ln2_b[0][None]
    shard_b = {name: given[name][0].astype(bf16) for name in BIG}
    shard_b["w_gate_up_top"] = shard_b["w_gate_up"][:D_MODEL // 2]
    shard_b["w_gate_up_bottom"] = shard_b["w_gate_up"][D_MODEL // 2:]
    place = jnp.stack([lax.axis_index("c"), 2 * lax.axis_index("x") + lax.axis_index("y")]).astype(jnp.int32)

    def gather(*names):
        return [_GatherJob({name: shard_b[name] for name in names})]

    def halves_summed(name, grad, got):
        return _add_halves(name, grad, got, place)

    (x_b,), (got,) = _cast_x(x2, 2 * tile, gather("w_in"))
    w_in_b = got["w_in"]
    (xp_all, pre_b), (got,) = _fwd_proj(x_b, w_in_b, tile, gather("w_out", "w_gate_up_top"))
    w_out_b, w_top = got["w_out"], got["w_gate_up_top"]
    (mix_b, r1, h_b), (got,) = _fwd_mix(xp_all, pre_b, x2, small, w_out_b, g1, b1, tile, seq, gather("w_gate_up_bottom"))
    w_bottom = got["w_gate_up_bottom"]
    (gu_b, a_b), (got,) = _fwd_gate_up(h_b, w_top, w_bottom, tile, gather("w_down"))
    w_dn_b = got["w_down"]
    dr2, dr2_b, stats2 = _fwd_down_loss(a_b, w_dn_b, r1, target, g1, b1, g2, b2, tile)

    early = {}
    g_down, _ = _wgrad(a_b, dr2_b, D_MODEL, wtile, "wgrad_down")
    (dgu_b,), (got,) = _bwd_gate_up(dr2_b, gu_b, w_dn_b, ffn_bwd_tile, [_SwapHalvesJob({"w_down": g_down})])
    sums_down = halves_summed("w_down", g_down, got["w_down"])
    g_gu, (got,) = _wgrad(h_b, dgu_b, D_FF, wtile, "wgrad_gate_up", [_SwapChipsJob({"w_down": sums_down[1]})])
    early["w_down"] = (sums_down[0], got["w_down"])
    (dr1, dr1_b, stats1), (got,) = _bwd_ffn_in(dgu_b, w_top, w_bottom, dr2, r1, g1, b1, tile,
                                               [_SwapHalvesJob({"w_gate_up": g_gu})])
    sums_gu = halves_summed("w_gate_up", g_gu, got["w_gate_up"])
    g_out, _ = _wgrad(mix_b, dr1_b, D_MODEL, wtile, "wgrad_out")
    (dpool, dpre_b, g_pool_w, g_sgu_w, g_sgu_b, vecs), (got_gu, got_out) = _bwd_mix(
        dr1_b, xp_all, pre_b, small, w_out_b, tile, seq,
        [_SwapChipsJob({"w_gate_up": sums_gu[1]}), _SwapHalvesJob({"w_out": g_out})])
    early["w_gate_up"] = (sums_gu[0], got_gu["w_gate_up"])
    sums_out = halves_summed("w_out", g_out, got_out["w_out"])
    grad_x, dproj_b = _bwd_in(dpool, dpre_b, dr1, w_in_b, tile, seq)
    g_in, (got,) = _wgrad(x_b, dproj_b, IN_COLS, wtile, "wgrad_in", [_SwapChipsJob({"w_out": sums_out[1]})])
    early["w_out"] = (sums_out[0], got["w_out"])

    late = {
        "w_in": g_in,
        "pool_w": g_pool_w.reshape(SQUARE_ROWS, GROUP),
        "sgu_w": g_sgu_w.reshape(SQUARE_ROWS, GROUP),
        "vecs": _pack_vecs({"pool_scale": vecs[0], "sgu_ln_g": vecs[1], "sgu_ln_b": vecs[2], "sgu_b": g_sgu_b[:, :, 0],
                            "ln1_g": stats1[0], "ln1_b": stats1[1], "ln2_g": stats2[0], "ln2_b": stats2[1]},
                           extra=stats2[2]),
    }
    shared = _reduce_tail(late, early)

    grad, delta, new_m, new_v = {}, {}, {}, {}
    for name in BIG:
        grad[name] = shared[name][None]
        d, nm, nv = _adamw(name, given[name][0], shared[name], given["m_" + name][0], given["v_" + name][0])
        delta[name], new_m[name], new_v[name] = d[None], nm[None], nv[None]
    vec_grads, after = _unpack_vecs(shared["vecs"], shapes)
    grad.update(vec_grads)
    for name in ("pool_w", "sgu_w"):
        grad[name] = shared[name].reshape(shapes[name])
    small_names = ("pool_w", "sgu_w") + VEC_NAMES
    state = {pre: [given[pre + name] for name in small_names] for pre in ("", "m_", "v_")}
    ds, nms, nvs = _adamw_small(state[""], [grad[name] for name in small_names], state["m_"], state["v_"])
    delta.update(zip(small_names, ds))
    new_m.update(zip(small_names, nms))
    new_v.update(zip(small_names, nvs))

    sq_err = after[:LOSS_ROWS]
    loss = jnp.sum(sq_err) * (0.5 / D_MODEL)
    return (loss, grad_x.reshape(x.shape), *[grad[name] for name in WEIGHT_ORDER],
            *[delta[name] for name in WEIGHT_ORDER], *[new_m[name] for name in WEIGHT_ORDER],
            *[new_v[name] for name in WEIGHT_ORDER])
```

```python
import math

import jax
import jax.numpy as jnp
from jax import lax
from jax.experimental import pallas as pl
from jax.experimental.pallas import tpu as pltpu

f32 = jnp.float32
bf16 = jnp.bfloat16
MESH = pl.DeviceIdType.MESH

D_MODEL = 1024
POOL_WIDTH = 512
SGU_WIDTH = 512
IN_COLS = POOL_WIDTH + 2 * SGU_WIDTH
D_FF = 2816
POOL_WINDOWS = (2, 4, 8, 16)
GROUP = 128
N_GROUPS = 4
HALO = 16
LN_EPS = 1e-5
ALPHA = float(2.0 ** 0.25)
N_CHIPS = 4

ADAM_LR = 0.001
ADAM_B1 = 0.9
ADAM_B2 = 0.999
ADAM_EPS = 1e-08
ADAM_WD = 0.01
ADAM_STEP = 10

TOKEN_TILE = 512
FFN_BWD_TILE = 512
FF_CHUNK = 256
ROW_SUB = 256
WGRAD_TILE = 1024
TOP_ROWS = 256
V7X_VMEM_LIMIT = 56 * 1024 * 1024

SQUARE_ROWS = N_GROUPS * GROUP
VEC_ROWS = 64
LOSS_ROWS = D_MODEL // GROUP


def _params(**kw):
    return pltpu.CompilerParams(vmem_limit_bytes=V7X_VMEM_LIMIT, **kw)


def _mm(a, b):
    return jnp.dot(a, b, preferred_element_type=f32)


def _mm_nt(a, b):
    return lax.dot_general(a, b, (((1,), (1,)), ((), ())), preferred_element_type=f32)


def _mm_tn(a, b):
    return lax.dot_general(a, b, (((0,), (0,)), ((), ())), preferred_element_type=f32)


def _ln_fwd(r, g, b):
    mu = jnp.mean(r, axis=-1, keepdims=True)
    xc = r - mu
    var = jnp.mean(xc * xc, axis=-1, keepdims=True)
    rstd = lax.rsqrt(var + LN_EPS)
    xhat = xc * rstd
    return xhat * g + b, xhat, rstd


def _ln_bwd(dout, xhat, rstd, g):
    dxhat = dout * g
    m1 = jnp.mean(dxhat, axis=-1, keepdims=True)
    m2 = jnp.mean(dxhat * xhat, axis=-1, keepdims=True)
    return rstd * (dxhat - m1 - xhat * m2)


def _col_sum(a):
    return jnp.sum(a, axis=0, keepdims=True)


def _gelu_parts(z):
    cdf = 0.5 * (1.0 + lax.erf(z * (1.0 / math.sqrt(2.0))))
    pdf = jnp.exp(-0.5 * z * z) * (1.0 / math.sqrt(2.0 * math.pi))
    return cdf, pdf


def _inv_counts(seq_tile, rows):
    pos = seq_tile * rows + lax.broadcasted_iota(jnp.int32, (rows, GROUP), 0) + 1
    return [1.0 / jnp.minimum(pos, w).astype(f32) for w in POOL_WINDOWS]


def _window_sums(e, back):
    n = e.shape[0]

    def shifted(a, s):
        return pltpu.roll(a, s if back else n - s, 0)

    s2 = e + shifted(e, 1)
    s4 = s2[:, GROUP:] + shifted(s2[:, GROUP:], 2)
    s8 = s4[:, GROUP:] + shifted(s4[:, GROUP:], 4)
    s16 = s8[:, GROUP:] + shifted(s8[:, GROUP:], 8)
    return [s2[:, :GROUP], s4[:, :GROUP], s8[:, :GROUP], s16]


def _pooled_groups(xp, halo, inv):
    sums = _window_sums(jnp.concatenate([halo, xp], axis=0), back=True)
    return [sums[g][HALO:] * inv[g] - xp[:, g * GROUP:(g + 1) * GROUP] for g in range(N_GROUPS)]


def _tril_mask():
    r = lax.broadcasted_iota(jnp.int32, (GROUP, GROUP), 0)
    c = lax.broadcasted_iota(jnp.int32, (GROUP, GROUP), 1)
    return (r >= c).astype(f32)


def _gs(g):
    return slice(g * GROUP, (g + 1) * GROUP)


class _Cut:
    def __init__(self, rows, cols, by_cols):
        self.rows, self.cols, self.by_cols = rows, cols, by_cols
        if by_cols:
            self.block_shape = (rows // 2, cols // N_CHIPS)
            self.shard_shape = (rows, cols // N_CHIPS)
        else:
            self.block_shape = (rows // (2 * N_CHIPS), cols)
            self.shard_shape = (rows // N_CHIPS, cols)

    def block(self, ref, chip, half):
        br, bc = self.block_shape
        if self.by_cols:
            return ref.at[pl.ds(pl.multiple_of(half * br, 16), br), pl.ds(pl.multiple_of(chip * bc, 128), bc)]
        return ref.at[pl.ds(pl.multiple_of((2 * chip + half) * br, 8), br), :]

    def shard(self, ref, chip):
        sr, sc = self.shard_shape
        if self.by_cols:
            return ref.at[:, pl.ds(pl.multiple_of(chip * sc, 128), sc)]
        return ref.at[pl.ds(pl.multiple_of(chip * sr, 16), sr), :]

    def half_of_shard(self, ref, half):
        br = self.block_shape[0]
        return ref.at[pl.ds(pl.multiple_of(half * br, 8), br), :]

    def block_index(self, chip, half):
        return (half, chip) if self.by_cols else (2 * chip + half, 0)


CUTS = {
    "w_in": _Cut(D_MODEL, IN_COLS, True),
    "w_out": _Cut(D_MODEL, D_MODEL, False),
    "w_gate_up": _Cut(D_MODEL, 2 * D_FF, True),
    "w_down": _Cut(D_FF, D_MODEL, False),
    "w_gate_up_top": _Cut(TOP_ROWS, 2 * D_FF, True),
    "w_gate_up_bottom": _Cut(D_MODEL - TOP_ROWS, 2 * D_FF, True),
    "pool_w": _Cut(SQUARE_ROWS, GROUP, False),
    "sgu_w": _Cut(SQUARE_ROWS, GROUP, False),
    "vecs": _Cut(VEC_ROWS, GROUP, False),
}
BIG = ("w_in", "w_out", "w_gate_up", "w_down")
SMALL = ("pool_w", "sgu_w", "vecs")
ANY = pl.BlockSpec(memory_space=pl.ANY)


def _wire_dtype(name):
    return bf16 if name in BIG else f32


def _place():
    x, y, c = lax.axis_index("x"), lax.axis_index("y"), lax.axis_index("c")
    others = [(1 - x, y), (x, 1 - y), (1 - x, 1 - y)]
    return x, y, c, 2 * x + y, others, [2 * ox + oy for ox, oy in others]


def _remote(src, dst, send_sems, recv_sems, k, to):
    return pltpu.make_async_remote_copy(src_ref=src, dst_ref=dst, send_sem=send_sems.at[k], recv_sem=recv_sems.at[k],
                                        device_id=to, device_id_type=MESH)


class _GatherJob:
    def __init__(self, shards):
        self.names = tuple(shards)
        self.arrays = tuple(shards.values())
        n = len(self.names)
        self.out_shapes = [jax.ShapeDtypeStruct((CUTS[name].rows, CUTS[name].cols), bf16) for name in self.names]
        self.scratch_shapes = ([pltpu.SemaphoreType.DMA((6 * n,)), pltpu.SemaphoreType.DMA((6 * n,)),
                                pltpu.SemaphoreType.DMA((2 * n,))]
                               + [pltpu.VMEM(CUTS[name].shard_shape, bf16) for name in self.names])

    def bind(self, shard_refs, full_refs, scratch):
        self.shards, self.full = shard_refs, full_refs
        self.send_sems, self.recv_sems, self.local_sems = scratch[:3]
        self.stages = scratch[3:]
        return self

    def _sends(self):
        _, _, c, me, others, _ = _place()
        return [_remote(CUTS[name].half_of_shard(self.shards[w], c), CUTS[name].block(self.full[w], me, c),
                        self.send_sems, self.recv_sems, 3 * w + k, (*chip, c))
                for w, name in enumerate(self.names) for k, chip in enumerate(others)]

    def _relays(self, half):
        x, y, c, _, _, other_ids = _place()
        n = len(self.names)
        return [_remote(CUTS[name].block(self.full[w], other_ids[k], half),
                        CUTS[name].block(self.full[w], other_ids[k], half),
                        self.send_sems, self.recv_sems, 3 * n + 3 * w + k, (x, y, 1 - c))
                for w, name in enumerate(self.names) for k in range(3)]

    def _stores(self):
        me = _place()[3]
        return [pltpu.make_async_copy(self.stages[w], CUTS[name].shard(self.full[w], me), self.local_sems.at[2 * w + 1])
                for w, name in enumerate(self.names)]

    def start(self):
        loads = [pltpu.make_async_copy(self.shards[w], self.stages[w], self.local_sems.at[2 * w])
                 for w in range(len(self.names))]
        for cp in loads:
            cp.start()
        for cp in self._sends():
            cp.start()
        for load, store in zip(loads, self._stores()):
            load.wait()
            store.start()

    def finish(self):
        _, _, c, _, others, other_ids = _place()
        relays = self._relays(c)
        for w, name in enumerate(self.names):
            for k, chip in enumerate(others):
                landed = CUTS[name].block(self.full[w], other_ids[k], c)
                _remote(landed, landed, self.send_sems, self.recv_sems, 3 * w + k, (*chip, c)).wait_recv()
                relays[3 * w + k].start()
        for cp in self._relays(1 - c):
            cp.wait_recv()
        for cp in self._sends() + relays:
            cp.wait_send()
        for cp in self._stores():
            cp.wait()


class _SwapHalvesJob:
    def __init__(self, grads):
        self.names = tuple(grads)
        self.arrays = tuple(grads.values())
        n = len(self.names)
        self.out_shapes = [jax.ShapeDtypeStruct((N_CHIPS, *CUTS[name].block_shape), f32) for name in self.names]
        self.scratch_shapes = [pltpu.SemaphoreType.DMA((N_CHIPS * n,)), pltpu.SemaphoreType.DMA((N_CHIPS * n,))]

    def bind(self, g_refs, got_refs, scratch):
        self.g_refs, self.got_refs = g_refs, got_refs
        self.send_sems, self.recv_sems = scratch
        return self

    def _copies(self):
        x, y, c, _, _, _ = _place()
        return [_remote(CUTS[name].block(self.g_refs[a], j, 1 - c), self.got_refs[a].at[j], self.send_sems,
                        self.recv_sems, N_CHIPS * a + j, (x, y, 1 - c))
                for a, name in enumerate(self.names) for j in range(N_CHIPS)]

    def start(self):
        for cp in self._copies():
            cp.start()

    def finish(self):
        for cp in self._copies():
            cp.wait()


class _SwapChipsJob:
    def __init__(self, partials):
        self.names = tuple(partials)
        self.arrays = tuple(partials.values())
        n = len(self.names)
        self.out_shapes = [jax.ShapeDtypeStruct((3, *CUTS[name].block_shape), _wire_dtype(name)) for name in self.names]
        self.scratch_shapes = [pltpu.SemaphoreType.DMA((3 * n,)), pltpu.SemaphoreType.DMA((3 * n,))]

    def bind(self, p_refs, got_refs, scratch):
        self.p_refs, self.got_refs = p_refs, got_refs
        self.send_sems, self.recv_sems = scratch
        return self

    def _copies(self):
        _, _, c, _, others, other_ids = _place()
        return [_remote(self.p_refs[a].at[other_ids[k]], self.got_refs[a].at[k], self.send_sems, self.recv_sems,
                        3 * a + k, (*chip, c))
                for a in range(len(self.names)) for k, chip in enumerate(others)]

    def start(self):
        for cp in self._copies():
            cp.start()

    def finish(self):
        for cp in self._copies():
            cp.wait()


class _ShareJob:
    def __init__(self, reduced):
        self.names = tuple(reduced)
        self.arrays = tuple(reduced.values())
        self.big = [a for a, name in enumerate(self.names) if name in BIG]
        self.small = [a for a, name in enumerate(self.names) if name in SMALL]
        self.out_shapes = [jax.ShapeDtypeStruct(CUTS[name].shard_shape if name in BIG
                                                else (CUTS[name].rows, CUTS[name].cols), f32) for name in self.names]
        n_sems = len(self.big) + 7 * len(self.small)
        self.scratch_shapes = ([pltpu.SemaphoreType.DMA((n_sems,)), pltpu.SemaphoreType.DMA((n_sems,)),
                                pltpu.SemaphoreType.DMA((2 * len(self.names),))]
                               + [pltpu.VMEM(CUTS[name].block_shape, f32) for name in self.names])

    def bind(self, f_refs, out_refs, scratch):
        self.f_refs, self.out_refs = f_refs, out_refs
        self.send_sems, self.recv_sems, self.local_sems = scratch[:3]
        self.stages = scratch[3:]
        return self

    def _sem(self, a, which=0):
        if a in self.big:
            return self.big.index(a)
        return len(self.big) + 7 * self.small.index(a) + which

    def _mine(self, a, half):
        me = _place()[3]
        cut = CUTS[self.names[a]]
        return cut.half_of_shard(self.out_refs[a], half) if a in self.big else cut.block(self.out_refs[a], me, half)

    def _to_sibling(self):
        x, y, c, _, _, _ = _place()
        return [_remote(self.f_refs[a], self._mine(a, c), self.send_sems, self.recv_sems, self._sem(a), (x, y, 1 - c))
                for a in range(len(self.names))]

    def _to_chips(self):
        _, _, c, _, others, _ = _place()
        return [_remote(self.f_refs[a], self._mine(a, c), self.send_sems, self.recv_sems, self._sem(a, 1 + k), (*chip, c))
                for a in self.small for k, chip in enumerate(others)]

    def _passes(self, half):
        x, y, c, _, _, other_ids = _place()
        out = []
        for a in self.small:
            for k in range(3):
                blk = CUTS[self.names[a]].block(self.out_refs[a], other_ids[k], half)
                out.append(_remote(blk, blk, self.send_sems, self.recv_sems, self._sem(a, 4 + k), (x, y, 1 - c)))
        return out

    def _stores(self):
        c = _place()[2]
        return [pltpu.make_async_copy(self.stages[a], self._mine(a, c), self.local_sems.at[2 * a + 1])
                for a in range(len(self.names))]

    def start(self):
        loads = [pltpu.make_async_copy(self.f_refs[a], self.stages[a], self.local_sems.at[2 * a])
                 for a in range(len(self.names))]
        for cp in loads:
            cp.start()
        for cp in self._to_sibling() + self._to_chips():
            cp.start()
        for load, store in zip(loads, self._stores()):
            load.wait()
            store.start()

    def finish(self):
        x, y, c, _, others, other_ids = _place()
        passes = self._passes(c)
        for s, a in enumerate(self.small):
            for k, chip in enumerate(others):
                landed = CUTS[self.names[a]].block(self.out_refs[a], other_ids[k], c)
                _remote(landed, landed, self.send_sems, self.recv_sems, self._sem(a, 1 + k), (*chip, c)).wait_recv()
                passes[3 * s + k].start()
        for a in range(len(self.names)):
            theirs = self._mine(a, 1 - c)
            _remote(theirs, theirs, self.send_sems, self.recv_sems, self._sem(a), (x, y, 1 - c)).wait_recv()
        for cp in self._passes(1 - c):
            cp.wait_recv()
        for cp in self._to_sibling() + self._to_chips() + passes:
            cp.wait_send()
        for cp in self._stores():
            cp.wait()


def _call(body, *, name, grid, in_specs, out_specs, out_shape, args, scratch_shapes=(), jobs=()):
    n_in, n_out, n_scr = len(in_specs), len(out_specs), len(scratch_shapes)
    j_in = [len(j.arrays) for j in jobs]
    j_out = [len(j.out_shapes) for j in jobs]
    j_scr = [len(j.scratch_shapes) for j in jobs]

    def wrapped(*refs):
        refs = list(refs)

        def take(k):
            head = refs[:k]
            del refs[:k]
            return head

        ins, jins = take(n_in), [take(k) for k in j_in]
        outs, jouts = take(n_out), [take(k) for k in j_out]
        scr, jscr = take(n_scr), [take(k) for k in j_scr]
        bound = [j.bind(a, b, c) for j, a, b, c in zip(jobs, jins, jouts, jscr)]
        if not grid:
            for b in bound:
                b.start()
            body(*ins, *outs, *scr)
            for b in bound:
                b.finish()
            return
        if not bound:
            body(*ins, *outs, *scr)
            return
        first = _all([pl.program_id(d) == 0 for d in range(len(grid))])
        last = _all([pl.program_id(d) == grid[d] - 1 for d in range(len(grid))])

        @pl.when(first)
        def _():
            for b in bound:
                b.start()

        body(*ins, *outs, *scr)

        @pl.when(last)
        def _():
            for b in bound:
                b.finish()

    kw = dict(grid=grid) if grid else {}
    results = pl.pallas_call(
        wrapped, name=name,
        in_specs=list(in_specs) + [ANY] * sum(j_in), out_specs=list(out_specs) + [ANY] * sum(j_out),
        out_shape=list(out_shape) + [s for j in jobs for s in j.out_shapes],
        scratch_shapes=list(scratch_shapes) + [s for j in jobs for s in j.scratch_shapes],
        compiler_params=_params(), **kw,
    )(*args, *[a for j in jobs for a in j.arrays])
    results = list(results)
    own, rest = results[:n_out], results[n_out:]
    per_job = []
    for j, k in zip(jobs, j_out):
        per_job.append(dict(zip(j.names, rest[:k])))
        rest = rest[k:]
    return own, per_job


def _all(conds):
    out = conds[0]
    for c in conds[1:]:
        out = jnp.logical_and(out, c)
    return out


def _alone(job, name):
    return _call(lambda: None, name=name, grid=None, in_specs=[], out_specs=[], out_shape=[], args=[], jobs=[job])[1][0]


def _cast_x(x2, tile, jobs):
    tokens = x2.shape[0]

    def body(x_ref, xb_ref):
        xb_ref[...] = x_ref[...].astype(bf16)

    row = lambda i: (i, 0)
    return _call(
        body, name="cast_x", grid=(tokens // tile,),
        in_specs=[pl.BlockSpec((tile, D_MODEL), row)], out_specs=[pl.BlockSpec((tile, D_MODEL), row)],
        out_shape=[jax.ShapeDtypeStruct((tokens, D_MODEL), bf16)], args=(x2,), jobs=jobs)


def _fwd_proj(x_b, w_in_b, tile, jobs):
    tokens = x_b.shape[0]

    def body(x_ref, w_ref, xp_ref, pre_ref):
        proj = _mm(x_ref[...], w_ref[...])
        xp_ref[...] = proj[:, :POOL_WIDTH]
        pre_ref[...] = proj[:, POOL_WIDTH:].astype(bf16)

    row = lambda i: (i, 0)
    return _call(
        body, name="fwd_proj", grid=(tokens // tile,),
        in_specs=[pl.BlockSpec((tile, D_MODEL), row), pl.BlockSpec((D_MODEL, IN_COLS), lambda i: (0, 0))],
        out_specs=[pl.BlockSpec((tile, POOL_WIDTH), row), pl.BlockSpec((tile, 2 * SGU_WIDTH), row)],
        out_shape=[jax.ShapeDtypeStruct((tokens, POOL_WIDTH), f32), jax.ShapeDtypeStruct((tokens, 2 * SGU_WIDTH), bf16)],
        args=(x_b, w_in_b), jobs=jobs)


def _small_specs():
    return [pl.BlockSpec((N_GROUPS, GROUP, GROUP), lambda i: (0, 0, 0)),
            pl.BlockSpec((1, POOL_WIDTH), lambda i: (0, 0)),
            pl.BlockSpec((1, SGU_WIDTH), lambda i: (0, 0)),
            pl.BlockSpec((1, SGU_WIDTH), lambda i: (0, 0)),
            pl.BlockSpec((N_GROUPS, GROUP, GROUP), lambda i: (0, 0, 0)),
            pl.BlockSpec((N_GROUPS, GROUP, GROUP), lambda i: (0, 0, 0))]


def _fwd_mix(xp_all, pre_b, x2, small, w_out_b, ln1_g, ln1_b, tile, seq, jobs):
    tokens = x2.shape[0]
    tps = seq // tile
    hb = tile // HALO

    def body(xp_ref, halo_ref, pre_ref, x_ref, pw_ref, ps_ref, lg_ref, lb_ref, sw_ref, sb_ref, wout_ref, g1_ref, b1_ref,
             mix_ref, r1_ref, h_ref):
        seq_tile = pl.program_id(0) % tps
        xp = xp_ref[...]
        halo = jnp.where(seq_tile == 0, 0.0, halo_ref[...])
        pooled = _pooled_groups(xp, halo, _inv_counts(seq_tile, tile))
        for g in range(N_GROUPS):
            po = _mm(pooled[g].astype(bf16), pw_ref[g].astype(bf16)) * ps_ref[:, _gs(g)]
            mix_ref[:, _gs(g)] = po.astype(bf16)

        pre = pre_ref[...].astype(f32)
        cdf, _ = _gelu_parts(pre)
        zg = pre * cdf
        u = zg[:, :SGU_WIDTH]
        vln, _, _ = _ln_fwd(zg[:, SGU_WIDTH:], lg_ref[...], lb_ref[...])
        vb = vln.astype(bf16)
        mask = _tril_mask()
        for h in range(N_GROUPS):
            wm = (sw_ref[h] * mask).astype(bf16)
            bias = sb_ref[h]
            for c in range(tile // GROUP):
                rows = slice(c * GROUP, (c + 1) * GROUP)
                mixed = _mm(wm, vb[rows, _gs(h)]) + bias
                mix_ref[rows, POOL_WIDTH + h * GROUP:POOL_WIDTH + (h + 1) * GROUP] = (u[rows, _gs(h)] * mixed).astype(bf16)

        r1 = ALPHA * x_ref[...] + _mm(mix_ref[...], wout_ref[...])
        r1_ref[...] = r1
        h1, _, _ = _ln_fwd(r1, g1_ref[...], b1_ref[...])
        h_ref[...] = h1.astype(bf16)

    row = lambda i: (i, 0)
    vec = pl.BlockSpec((1, D_MODEL), lambda i: (0, 0))
    return _call(
        body, name="fwd_mix", grid=(tokens // tile,),
        in_specs=[pl.BlockSpec((tile, POOL_WIDTH), row),
                  pl.BlockSpec((HALO, POOL_WIDTH), lambda i: (jnp.maximum(i * hb - 1, 0), 0)),
                  pl.BlockSpec((tile, 2 * SGU_WIDTH), row),
                  pl.BlockSpec((tile, D_MODEL), row)] + _small_specs()
                 + [pl.BlockSpec((D_MODEL, D_MODEL), lambda i: (0, 0)), vec, vec],
        out_specs=[pl.BlockSpec((tile, D_MODEL), row)] * 3,
        out_shape=[jax.ShapeDtypeStruct((tokens, D_MODEL), bf16),
                   jax.ShapeDtypeStruct((tokens, D_MODEL), f32),
                   jax.ShapeDtypeStruct((tokens, D_MODEL), bf16)],
        args=(xp_all, xp_all, pre_b, x2, *small, w_out_b, ln1_g, ln1_b), jobs=jobs)


def _fwd_gate_up(h_b, w_top, w_bottom, tile, jobs):
    tokens = h_b.shape[0]
    def body(h_ref, wt_ref, wb_ref, gu_ref, a_ref):
        ht, hb = h_ref[:, :TOP_ROWS], h_ref[:, TOP_ROWS:]
        for c in range(D_FF // FF_CHUNK):
            gcols = slice(c * FF_CHUNK, (c + 1) * FF_CHUNK)
            ucols = slice(D_FF + c * FF_CHUNK, D_FF + (c + 1) * FF_CHUNK)
            gate = _mm(ht, wt_ref[:, gcols]) + _mm(hb, wb_ref[:, gcols])
            up = _mm(ht, wt_ref[:, ucols]) + _mm(hb, wb_ref[:, ucols])
            sg = jax.nn.sigmoid(gate)
            silu = gate * sg
            gu_ref[:, gcols] = (up * (sg + silu * (1.0 - sg))).astype(bf16)
            gu_ref[:, ucols] = silu.astype(bf16)
            a_ref[:, gcols] = (silu * up).astype(bf16)

    return _call(
        body, name="fwd_gate_up", grid=(tokens // tile,),
        in_specs=[pl.BlockSpec((tile, D_MODEL), lambda i: (i, 0)),
                  pl.BlockSpec((TOP_ROWS, 2 * D_FF), lambda i: (0, 0), pipeline_mode=pl.Buffered(1)),
                  pl.BlockSpec((D_MODEL - TOP_ROWS, 2 * D_FF), lambda i: (0, 0), pipeline_mode=pl.Buffered(1))],
        out_specs=[pl.BlockSpec((tile, 2 * D_FF), lambda i: (i, 0)),
                   pl.BlockSpec((tile, D_FF), lambda i: (i, 0))],
        out_shape=[jax.ShapeDtypeStruct((tokens, 2 * D_FF), bf16),
                   jax.ShapeDtypeStruct((tokens, D_FF), bf16)],
        args=(h_b, w_top, w_bottom), jobs=jobs)


def _fwd_down_loss(a_b, w_dn_b, r1, target, ln1_g, ln1_b, ln2_g, ln2_b, tile):
    tokens = a_b.shape[0]

    def body(a_ref, w_ref, r1_ref, t_ref, g1_ref, b1_ref, g2_ref, b2_ref, dr2_ref, dr2b_ref, st_ref):
        @pl.when(pl.program_id(0) == 0)
        def _():
            st_ref[...] = jnp.zeros_like(st_ref)

        sub = min(ROW_SUB, tile)
        for s in range(tile // sub):
            rows = slice(s * sub, (s + 1) * sub)
            h1, _, _ = _ln_fwd(r1_ref[rows, :], g1_ref[...], b1_ref[...])
            r2 = ALPHA * h1 + _mm(a_ref[rows, :], w_ref[...])
            y, xhat, rstd = _ln_fwd(r2, g2_ref[...], b2_ref[...])
            diff = y - t_ref[rows, :]
            dy = diff * (1.0 / D_MODEL)
            st_ref[0:1, :] += _col_sum(dy * xhat)
            st_ref[1:2, :] += _col_sum(dy)
            st_ref[2:3, :] += _col_sum(diff * diff)
            dr2 = _ln_bwd(dy, xhat, rstd, g2_ref[...])
            dr2_ref[rows, :] = dr2
            dr2b_ref[rows, :] = dr2.astype(bf16)

    row = lambda i: (i, 0)
    vec = pl.BlockSpec((1, D_MODEL), lambda i: (0, 0))
    return _call(
        body, name="fwd_down_loss", grid=(tokens // tile,),
        in_specs=[pl.BlockSpec((tile, D_FF), row), pl.BlockSpec((D_FF, D_MODEL), lambda i: (0, 0)),
                  pl.BlockSpec((tile, D_MODEL), row), pl.BlockSpec((tile, D_MODEL), row), vec, vec, vec, vec],
        out_specs=[pl.BlockSpec((tile, D_MODEL), row), pl.BlockSpec((tile, D_MODEL), row),
                   pl.BlockSpec((8, D_MODEL), lambda i: (0, 0))],
        out_shape=[jax.ShapeDtypeStruct((tokens, D_MODEL), f32), jax.ShapeDtypeStruct((tokens, D_MODEL), bf16),
                   jax.ShapeDtypeStruct((8, D_MODEL), f32)],
        args=(a_b, w_dn_b, r1, target, ln1_g, ln1_b, ln2_g, ln2_b))[0]


def _bwd_gate_up(dr2, gu_b, w_dn_b, tile, jobs):
    tokens = dr2.shape[0]

    def body(d_ref, gu_ref, w_ref, dgu_ref):
        d = d_ref[...].astype(bf16)
        for c in range(D_FF // FF_CHUNK):
            gcols = slice(c * FF_CHUNK, (c + 1) * FF_CHUNK)
            ucols = slice(D_FF + c * FF_CHUNK, D_FF + (c + 1) * FF_CHUNK)
            da = _mm_nt(d, w_ref[gcols, :])
            dgu_ref[:, gcols] = (da * gu_ref[:, gcols].astype(f32)).astype(bf16)
            dgu_ref[:, ucols] = (da * gu_ref[:, ucols].astype(f32)).astype(bf16)

    return _call(
        body, name="bwd_gate_up", grid=(tokens // tile,),
        in_specs=[pl.BlockSpec((tile, D_MODEL), lambda i: (i, 0)),
                  pl.BlockSpec((tile, 2 * D_FF), lambda i: (i, 0)),
                  pl.BlockSpec((D_FF, D_MODEL), lambda i: (0, 0))],
        out_specs=[pl.BlockSpec((tile, 2 * D_FF), lambda i: (i, 0))],
        out_shape=[jax.ShapeDtypeStruct((tokens, 2 * D_FF), bf16)],
        args=(dr2, gu_b, w_dn_b), jobs=jobs)


def _bwd_ffn_in(dgu_b, w_top, w_bottom, dr2, r1, ln1_g, ln1_b, tile, jobs):
    tokens = dr2.shape[0]

    def body(dgu_ref, wt_ref, wb_ref, d_ref, r1_ref, g1_ref, b1_ref, dr1_ref, dr1b_ref, st_ref):
        @pl.when(pl.program_id(0) == 0)
        def _():
            st_ref[...] = jnp.zeros_like(st_ref)

        dgu = dgu_ref[...]
        dh = ALPHA * d_ref[...] + jnp.concatenate([_mm_nt(dgu, wt_ref[...]), _mm_nt(dgu, wb_ref[...])], axis=1)
        _, xhat, rstd = _ln_fwd(r1_ref[...], g1_ref[...], b1_ref[...])
        st_ref[0:1, :] += _col_sum(dh * xhat)
        st_ref[1:2, :] += _col_sum(dh)
        dr1 = _ln_bwd(dh, xhat, rstd, g1_ref[...])
        dr1_ref[...] = dr1
        dr1b_ref[...] = dr1.astype(bf16)

    row = lambda i: (i, 0)
    vec = pl.BlockSpec((1, D_MODEL), lambda i: (0, 0))
    return _call(
        body, name="bwd_ffn_in", grid=(tokens // tile,),
        in_specs=[pl.BlockSpec((tile, 2 * D_FF), row),
                  pl.BlockSpec((TOP_ROWS, 2 * D_FF), lambda i: (0, 0), pipeline_mode=pl.Buffered(1)),
                  pl.BlockSpec((D_MODEL - TOP_ROWS, 2 * D_FF), lambda i: (0, 0), pipeline_mode=pl.Buffered(1)),
                  pl.BlockSpec((tile, D_MODEL), row), pl.BlockSpec((tile, D_MODEL), row), vec, vec],
        out_specs=[pl.BlockSpec((tile, D_MODEL), row), pl.BlockSpec((tile, D_MODEL), row),
                   pl.BlockSpec((8, D_MODEL), lambda i: (0, 0))],
        out_shape=[jax.ShapeDtypeStruct((tokens, D_MODEL), f32), jax.ShapeDtypeStruct((tokens, D_MODEL), bf16),
                   jax.ShapeDtypeStruct((8, D_MODEL), f32)],
        args=(dgu_b, w_top, w_bottom, dr2, r1, ln1_g, ln1_b), jobs=jobs)


def _bwd_mix(dr1, xp_all, pre_b, small, w_out_b, tile, seq, jobs):
    tokens = dr1.shape[0]
    tps = seq // tile
    hb = tile // HALO
    steps = tokens // tile

    def body(dr1_ref, xp_ref, halo_ref, pre_ref, wout_ref, pw_ref, ps_ref, lg_ref, lb_ref, sw_ref, sb_ref,
             dpool_ref, dpre_ref, gpw_ref, gsw_ref, gsb_ref, vec_ref, du_ref, dvln_ref):
        step = pl.program_id(0)
        seq_tile = step % tps

        @pl.when(step == 0)
        def _():
            gpw_ref[...] = jnp.zeros_like(gpw_ref)
            gsw_ref[...] = jnp.zeros_like(gsw_ref)
            gsb_ref[...] = jnp.zeros_like(gsb_ref)
            vec_ref[...] = jnp.zeros_like(vec_ref)

        dmix = _mm_nt(dr1_ref[...].astype(bf16), wout_ref[...])

        xp = xp_ref[...]
        halo = jnp.where(seq_tile == 0, 0.0, halo_ref[...])
        pooled = _pooled_groups(xp, halo, _inv_counts(seq_tile, tile))
        for g in range(N_GROUPS):
            pb = pooled[g].astype(bf16)
            pwb = pw_ref[g].astype(bf16)
            dpo = dmix[:, _gs(g)]
            vec_ref[0:1, _gs(g)] += _col_sum(dpo * _mm(pb, pwb))
            dpo_b = (dpo * ps_ref[:, _gs(g)]).astype(bf16)
            gpw_ref[g] += _mm_tn(pb, dpo_b)
            dpool_ref[:, _gs(g)] = _mm_nt(dpo_b, pwb)

        pre = pre_ref[...].astype(f32)
        cdf, pdf = _gelu_parts(pre)
        zg = pre * cdf
        u = zg[:, :SGU_WIDTH]
        vln, vhat, rstd = _ln_fwd(zg[:, SGU_WIDTH:], lg_ref[...], lb_ref[...])
        vb = vln.astype(bf16)
        mask = _tril_mask()
        for h in range(N_GROUPS):
            wm = (sw_ref[h] * mask).astype(bf16)
            bias = sb_ref[h]
            gsw = jnp.zeros((GROUP, GROUP), f32)
            gsb = jnp.zeros((GROUP, GROUP), f32)
            for c in range(tile // GROUP):
                rows = slice(c * GROUP, (c + 1) * GROUP)
                v_ch = vb[rows, _gs(h)]
                d = dmix[rows, POOL_WIDTH + h * GROUP:POOL_WIDTH + (h + 1) * GROUP]
                du_ref[rows, _gs(h)] = d * (_mm(wm, v_ch) + bias)
                dmixed = d * u[rows, _gs(h)]
                gsb += dmixed
                dmixed_b = dmixed.astype(bf16)
                gsw += _mm_nt(dmixed_b, v_ch)
                dvln_ref[rows, _gs(h)] = _mm_tn(wm, dmixed_b)
            gsw_ref[h] += gsw * mask
            gsb_ref[h] += gsb

        dvln = dvln_ref[...]
        vec_ref[1:2, :] += _col_sum(dvln * vhat)
        vec_ref[2:3, :] += _col_sum(dvln)
        dgelu = cdf + pre * pdf
        dpre_ref[:, :SGU_WIDTH] = (du_ref[...] * dgelu[:, :SGU_WIDTH]).astype(bf16)
        dpre_ref[:, SGU_WIDTH:] = (_ln_bwd(dvln, vhat, rstd, lg_ref[...]) * dgelu[:, SGU_WIDTH:]).astype(bf16)

        @pl.when(step == steps - 1)
        def _():
            for h in range(N_GROUPS):
                gsb_ref[h] = jnp.broadcast_to(jnp.sum(gsb_ref[h], axis=1, keepdims=True), (GROUP, GROUP))

    row = lambda i: (i, 0)
    sq = jax.ShapeDtypeStruct((N_GROUPS, GROUP, GROUP), f32)
    sq_spec = pl.BlockSpec((N_GROUPS, GROUP, GROUP), lambda i: (0, 0, 0))
    return _call(
        body, name="bwd_mix", grid=(steps,),
        in_specs=[pl.BlockSpec((tile, D_MODEL), row), pl.BlockSpec((tile, POOL_WIDTH), row),
                  pl.BlockSpec((HALO, POOL_WIDTH), lambda i: (jnp.maximum(i * hb - 1, 0), 0)),
                  pl.BlockSpec((tile, 2 * SGU_WIDTH), row),
                  pl.BlockSpec((D_MODEL, D_MODEL), lambda i: (0, 0))] + _small_specs(),
        out_specs=[pl.BlockSpec((tile, POOL_WIDTH), row), pl.BlockSpec((tile, 2 * SGU_WIDTH), row),
                   sq_spec, sq_spec, sq_spec, pl.BlockSpec((8, POOL_WIDTH), lambda i: (0, 0))],
        out_shape=[jax.ShapeDtypeStruct((tokens, POOL_WIDTH), f32), jax.ShapeDtypeStruct((tokens, 2 * SGU_WIDTH), bf16),
                   sq, sq, sq, jax.ShapeDtypeStruct((8, POOL_WIDTH), f32)],
        scratch_shapes=[pltpu.VMEM((tile, SGU_WIDTH), f32), pltpu.VMEM((tile, SGU_WIDTH), f32)],
        args=(dr1, xp_all, xp_all, pre_b, w_out_b, *small), jobs=jobs)


def _bwd_in(dpool, dpre_b, dr1, w_in_b, tile, seq):
    tokens = dr1.shape[0]
    tps = seq // tile
    hb = tile // HALO
    last_halo = tokens // HALO - 1

    def body(dpool_ref, nxt_ref, dpre_ref, dr1_ref, w_ref, dx_ref, dproj_ref):
        seq_tile = pl.program_id(0) % tps
        inv = _inv_counts(seq_tile, tile)
        dpl = dpool_ref[...]
        nxt = jnp.where(seq_tile == tps - 1, 0.0, nxt_ref[...])
        scaled = jnp.concatenate([dpl[:, _gs(g)] * inv[g] for g in range(N_GROUPS)], axis=1)
        scaled_nxt = jnp.concatenate([nxt[:, _gs(g)] * (1.0 / POOL_WINDOWS[g]) for g in range(N_GROUPS)], axis=1)
        sums = _window_sums(jnp.concatenate([scaled, scaled_nxt], axis=0), back=False)
        for g in range(N_GROUPS):
            dproj_ref[:, _gs(g)] = (sums[g][:tile] - dpl[:, _gs(g)]).astype(bf16)
        dproj_ref[:, POOL_WIDTH:] = dpre_ref[...]
        dx_ref[...] = ALPHA * dr1_ref[...] + _mm_nt(dproj_ref[...], w_ref[...])

    row = lambda i: (i, 0)
    return _call(
        body, name="bwd_in", grid=(tokens // tile,),
        in_specs=[pl.BlockSpec((tile, POOL_WIDTH), row),
                  pl.BlockSpec((HALO, POOL_WIDTH), lambda i: (jnp.minimum((i + 1) * hb, last_halo), 0)),
                  pl.BlockSpec((tile, 2 * SGU_WIDTH), row),
                  pl.BlockSpec((tile, D_MODEL), row),
                  pl.BlockSpec((D_MODEL, IN_COLS), lambda i: (0, 0))],
        out_specs=[pl.BlockSpec((tile, D_MODEL), row), pl.BlockSpec((tile, IN_COLS), row)],
        out_shape=[jax.ShapeDtypeStruct((tokens, D_MODEL), f32), jax.ShapeDtypeStruct((tokens, IN_COLS), bf16)],
        args=(dpool, dpool, dpre_b, dr1, w_in_b))[0]


def _wgrad(a, b, col_tile, tile, name, jobs=()):
    tokens, m = a.shape
    n = b.shape[1]

    def body(a_ref, b_ref, o_ref):
        @pl.when(pl.program_id(1) == 0)
        def _():
            o_ref[...] = jnp.zeros_like(o_ref)

        o_ref[...] += _mm_tn(a_ref[...].astype(bf16), b_ref[...].astype(bf16))

    (out,), got = _call(
        body, name=name, grid=(n // col_tile, tokens // tile),
        in_specs=[pl.BlockSpec((tile, m), lambda j, k: (k, 0)),
                  pl.BlockSpec((tile, col_tile), lambda j, k: (k, j))],
        out_specs=[pl.BlockSpec((m, col_tile), lambda j, k: (0, j))],
        out_shape=[jax.ShapeDtypeStruct((m, n), f32)],
        args=(a, b), jobs=jobs)
    return out, got


def _add_halves(name, g, got, place):
    cut = CUTS[name]
    br, bc = cut.block_shape
    wire = _wire_dtype(name)

    def body(place_ref, g_ref, got_ref, o_ref, wire_ref):
        s = g_ref[...] + got_ref[...]
        wire_ref[...] = s.astype(wire)

        @pl.when(pl.program_id(0) == place_ref[1])
        def _():
            o_ref[...] = s

    blocks = pl.BlockSpec((None, br, bc), lambda j, place_ref: (j, 0, 0))
    return pl.pallas_call(
        body, name="reduce_add_halves_" + name,
        grid_spec=pltpu.PrefetchScalarGridSpec(
            num_scalar_prefetch=1, grid=(N_CHIPS,),
            in_specs=[pl.BlockSpec((br, bc), lambda j, place_ref: cut.block_index(j, place_ref[0])), blocks],
            out_specs=[pl.BlockSpec((br, bc), lambda j, place_ref: (0, 0)), blocks]),
        out_shape=[jax.ShapeDtypeStruct((br, bc), f32), jax.ShapeDtypeStruct((N_CHIPS, br, bc), wire)],
        compiler_params=_params(),
    )(place, g, got)


def _reduce_tail(late, early):
    late_names, early_names = tuple(late), tuple(early)
    names = early_names + late_names
    nl, ne, n = len(late_names), len(early_names), len(names)
    cuts = [CUTS[name] for name in names]
    is_big = [name in BIG for name in names]
    share_base, share_sem = 7 * nl, []
    for i in range(n):
        share_sem.append(share_base)
        share_base += 1 if is_big[i] else 7
    n_in = nl + 2 * ne

    def body(*refs):
        g_refs = refs[:nl]
        sums_refs, got_refs = refs[nl:n_in:2], refs[nl + 1:n_in:2]
        out_refs = refs[n_in:n_in + n]
        send_sems, recv_sems, local_sems = refs[n_in + n:n_in + n + 3]
        vm = refs[n_in + n + 3:]
        own, recv_a, wire, recv_b = vm[0:4 * nl:4], vm[1:4 * nl:4], vm[2:4 * nl:4], vm[3:4 * nl:4]
        acc, gotv = vm[4 * nl::2], vm[4 * nl + 1::2]
        x, y, c, me, others, other_ids = _place()
        sibling = (x, y, 1 - c)

        def mine(i, half):
            return cuts[i].half_of_shard(out_refs[i], half) if is_big[i] else cuts[i].block(out_refs[i], me, half)

        def reduced(i):
            return acc[i] if i < ne else own[i - ne].at[me]

        sent, stores = [], []

        def share(i):
            store = pltpu.make_async_copy(reduced(i), mine(i, c), local_sems.at[4 * nl + 2 * ne + i])
            store.start()
            stores.append(store)
            to = [sibling] if is_big[i] else [sibling] + [(*chip, c) for chip in others]
            for which, device in enumerate(to):
                cp = _remote(reduced(i), mine(i, c), send_sems, recv_sems, share_sem[i] + which, device)
                cp.start()
                sent.append(cp)

        early_loads = []
        for e in range(ne):
            early_loads.append(pltpu.make_async_copy(sums_refs[e], acc[e], local_sems.at[4 * nl + 2 * e]))
            early_loads.append(pltpu.make_async_copy(got_refs[e], gotv[e], local_sems.at[4 * nl + 2 * e + 1]))
        late_loads = [pltpu.make_async_copy(cuts[ne + l].block(g_refs[l], j, c), own[l].at[j], local_sems.at[4 * l + j])
                      for l in range(nl) for j in range(N_CHIPS)]
        halves = [_remote(cuts[ne + l].block(g_refs[l], j, 1 - c), recv_a[l].at[j], send_sems, recv_sems, 4 * l + j, sibling)
                  for l in range(nl) for j in range(N_CHIPS)]
        for cp in early_loads + late_loads + halves:
            cp.start()

        for cp in early_loads:
            cp.wait()
        for e in range(ne):
            acc[e][...] = ((acc[e][...] + gotv[e][0].astype(f32)) + gotv[e][1].astype(f32)) + gotv[e][2].astype(f32)
            share(e)

        for cp in late_loads:
            cp.wait()
        for cp in halves:
            cp.wait_recv()
        for l in range(nl):
            for j in range(N_CHIPS):
                s = own[l][j] + recv_a[l][j]
                own[l][j] = s
                wire[l][j] = s.astype(wire[l].dtype)
        chips = [_remote(wire[l].at[other_ids[k]], recv_b[l].at[k], send_sems, recv_sems, 4 * nl + 3 * l + k, (*chip, c))
                 for l in range(nl) for k, chip in enumerate(others)]
        for cp in chips:
            cp.start()
        for cp in chips:
            cp.wait_recv()
        for l in range(nl):
            mine_l = own[l].at[me]
            mine_l[...] = ((mine_l[...] + recv_b[l][0].astype(f32)) + recv_b[l][1].astype(f32)) + recv_b[l][2].astype(f32)
            share(ne + l)

        for i in range(n):
            if not is_big[i]:
                for k, chip in enumerate(others):
                    landed = cuts[i].block(out_refs[i], other_ids[k], c)
                    _remote(landed, landed, send_sems, recv_sems, share_sem[i] + 1 + k, (*chip, c)).wait_recv()
                    cp = _remote(landed, landed, send_sems, recv_sems, share_sem[i] + 4 + k, sibling)
                    cp.start()
                    sent.append(cp)
        for i in range(n):
            theirs = mine(i, 1 - c)
            _remote(theirs, theirs, send_sems, recv_sems, share_sem[i], sibling).wait_recv()
            if not is_big[i]:
                for k in range(3):
                    passed = cuts[i].block(out_refs[i], other_ids[k], 1 - c)
                    _remote(passed, passed, send_sems, recv_sems, share_sem[i] + 4 + k, sibling).wait_recv()
        for cp in halves + chips + sent:
            cp.wait_send()
        for cp in stores:
            cp.wait()

    scratch = [pltpu.SemaphoreType.DMA((share_base,)), pltpu.SemaphoreType.DMA((share_base,)),
               pltpu.SemaphoreType.DMA((4 * nl + 2 * ne + n,))]
    for name in late_names:
        block = CUTS[name].block_shape
        scratch += [pltpu.VMEM((N_CHIPS, *block), f32), pltpu.VMEM((N_CHIPS, *block), f32),
                    pltpu.VMEM((N_CHIPS, *block), _wire_dtype(name)), pltpu.VMEM((3, *block), _wire_dtype(name))]
    for name in early_names:
        block = CUTS[name].block_shape
        scratch += [pltpu.VMEM(block, f32), pltpu.VMEM((3, *block), _wire_dtype(name))]
    args = [late[name] for name in late_names] + [a for name in early_names for a in early[name]]
    outs = pl.pallas_call(
        body, name="reduce_tail",
        in_specs=[ANY] * n_in, out_specs=[ANY] * n,
        out_shape=[jax.ShapeDtypeStruct(CUTS[name].shard_shape if name in BIG else (CUTS[name].rows, CUTS[name].cols), f32)
                   for name in names],
        scratch_shapes=scratch, compiler_params=_params(),
    )(*args)
    return dict(zip(names, outs))


def _adamw_refs(w_ref, g_ref, m_ref, v_ref, d_ref, nm_ref, nv_ref):
    g = g_ref[...]
    nm = ADAM_B1 * m_ref[...] + (1.0 - ADAM_B1) * g
    nv = ADAM_B2 * v_ref[...] + (1.0 - ADAM_B2) * jnp.square(g)
    m_hat = nm / (1.0 - ADAM_B1 ** ADAM_STEP)
    v_hat = nv / (1.0 - ADAM_B2 ** ADAM_STEP)
    d_ref[...] = -ADAM_LR * (m_hat / (jnp.sqrt(v_hat) + ADAM_EPS) + ADAM_WD * w_ref[...])
    nm_ref[...] = nm
    nv_ref[...] = nv


def _adamw_small(ws, gs, ms, vs):
    n = len(ws)

    def body(*refs):
        for i in range(n):
            _adamw_refs(*[refs[k * n + i] for k in range(7)])

    whole = pl.BlockSpec(memory_space=pltpu.VMEM)
    outs = pl.pallas_call(
        body, name="adamw_small",
        in_specs=[whole] * (4 * n), out_specs=[whole] * (3 * n),
        out_shape=[jax.ShapeDtypeStruct(w.shape, f32) for w in ws] * 3,
        compiler_params=_params(),
    )(*ws, *gs, *ms, *vs)
    return outs[:n], outs[n:2 * n], outs[2 * n:]


def _adamw(name, w, g, m, v):
    rows, cols = w.shape
    rt = rows // 4

    def body(w_ref, g_ref, m_ref, v_ref, d_ref, nm_ref, nv_ref):
        _adamw_refs(w_ref, g_ref, m_ref, v_ref, d_ref, nm_ref, nv_ref)

    spec = pl.BlockSpec((rt, cols), lambda i: (i, 0))
    shape = jax.ShapeDtypeStruct((rows, cols), f32)
    return pl.pallas_call(
        body, name="adamw_" + name, grid=(rows // rt,),
        in_specs=[spec] * 4, out_specs=[spec] * 3, out_shape=[shape] * 3,
        compiler_params=_params(),
    )(w, g, m, v)


VEC_NAMES = ("pool_scale", "sgu_ln_g", "sgu_ln_b", "sgu_b", "ln1_g", "ln1_b", "ln2_g", "ln2_b")
WEIGHT_ORDER = ("w_in", "pool_w", "pool_scale", "sgu_ln_g", "sgu_ln_b", "sgu_w", "sgu_b", "w_out", "ln1_g", "ln1_b",
                "w_gate_up", "w_down", "ln2_g", "ln2_b")


def _pack_vecs(parts, extra=None):
    rows = [parts[name].reshape(-1, GROUP) for name in VEC_NAMES]
    if extra is not None:
        rows.append(extra.reshape(-1, GROUP))
    used = sum(r.shape[0] for r in rows)
    return jnp.concatenate(rows + [jnp.zeros((VEC_ROWS - used, GROUP), f32)], axis=0)


def _unpack_vecs(packed, shapes):
    out, at = {}, 0
    for name in VEC_NAMES:
        rows = math.prod(shapes[name]) // GROUP
        out[name] = packed[at:at + rows].reshape(shapes[name])
        at += rows
    return out, packed[at:]


def kernel(x, w_in, pool_w, pool_scale, sgu_ln_g, sgu_ln_b, sgu_w, sgu_b, w_out, ln1_g, ln1_b, w_gate_up, w_down, ln2_g, ln2_b, loss_target, m_w_in, m_pool_w, m_pool_scale, m_sgu_ln_g, m_sgu_ln_b, m_sgu_w, m_sgu_b, m_w_out, m_ln1_g, m_ln1_b, m_w_gate_up, m_w_down, m_ln2_g, m_ln2_b, v_w_in, v_pool_w, v_pool_scale, v_sgu_ln_g, v_sgu_ln_b, v_sgu_w, v_sgu_b, v_w_out, v_ln1_g, v_ln1_b, v_w_gate_up, v_w_down, v_ln2_g, v_ln2_b):
    given = dict(locals())
    batch, seq, _ = x.shape
    tokens = batch * seq
    tile = min(TOKEN_TILE, seq)
    ffn_bwd_tile = min(FFN_BWD_TILE, seq)
    wtile = min(WGRAD_TILE, tokens)
    shapes = {name: given[name].shape for name in WEIGHT_ORDER}

    x2 = x.reshape(tokens, D_MODEL)
    target = loss_target.reshape(tokens, D_MODEL)
    small = (pool_w[0], pool_scale[0][None], sgu_ln_g[0][None], sgu_ln_b[0][None], sgu_w[0],
             jnp.broadcast_to(sgu_b[0][:, :, None], (N_GROUPS, GROUP, GROUP)))
    g1, b1, g2, b2 = ln1_g[0][None], ln1_b[0][None], ln2_g[0][None], ln2_b[0][None]
    shard_b = {name: given[name][0].astype(bf16) for name in BIG}
    shard_b["w_gate_up_top"] = shard_b["w_gate_up"][:TOP_ROWS]
    shard_b["w_gate_up_bottom"] = shard_b["w_gate_up"][TOP_ROWS:]
    place = jnp.stack([lax.axis_index("c"), 2 * lax.axis_index("x") + lax.axis_index("y")]).astype(jnp.int32)

    def gather(*names):
        return [_GatherJob({name: shard_b[name] for name in names})]

    def halves_summed(name, grad, got):
        return _add_halves(name, grad, got, place)

    (x_b,), (got,) = _cast_x(x2, 2 * tile, gather("w_in"))
    w_in_b = got["w_in"]
    (xp_all, pre_b), (got,) = _fwd_proj(x_b, w_in_b, tile, gather("w_out", "w_gate_up_top"))
    w_out_b, w_top = got["w_out"], got["w_gate_up_top"]
    (mix_b, r1, h_b), (got,) = _fwd_mix(xp_all, pre_b, x2, small, w_out_b, g1, b1, tile, seq, gather("w_gate_up_bottom"))
    w_bottom = got["w_gate_up_bottom"]
    (gu_b, a_b), (got,) = _fwd_gate_up(h_b, w_top, w_bottom, tile, gather("w_down"))
    w_dn_b = got["w_down"]
    dr2, dr2_b, stats2 = _fwd_down_loss(a_b, w_dn_b, r1, target, g1, b1, g2, b2, tile)

    early = {}
    g_down, _ = _wgrad(a_b, dr2_b, D_MODEL, wtile, "wgrad_down")
    (dgu_b,), (got,) = _bwd_gate_up(dr2_b, gu_b, w_dn_b, ffn_bwd_tile, [_SwapHalvesJob({"w_down": g_down})])
    sums_down = halves_summed("w_down", g_down, got["w_down"])
    g_gu, (got,) = _wgrad(h_b, dgu_b, D_FF, wtile, "wgrad_gate_up", [_SwapChipsJob({"w_down": sums_down[1]})])
    early["w_down"] = (sums_down[0], got["w_down"])
    (dr1, dr1_b, stats1), (got,) = _bwd_ffn_in(dgu_b, w_top, w_bottom, dr2, r1, g1, b1, tile,
                                               [_SwapHalvesJob({"w_gate_up": g_gu})])
    sums_gu = halves_summed("w_gate_up", g_gu, got["w_gate_up"])
    g_out, _ = _wgrad(mix_b, dr1_b, D_MODEL, wtile, "wgrad_out")
    (dpool, dpre_b, g_pool_w, g_sgu_w, g_sgu_b, vecs), (got_gu, got_out) = _bwd_mix(
        dr1_b, xp_all, pre_b, small, w_out_b, tile, seq,
        [_SwapChipsJob({"w_gate_up": sums_gu[1]}), _SwapHalvesJob({"w_out": g_out})])
    early["w_gate_up"] = (sums_gu[0], got_gu["w_gate_up"])
    sums_out = halves_summed("w_out", g_out, got_out["w_out"])
    grad_x, dproj_b = _bwd_in(dpool, dpre_b, dr1, w_in_b, tile, seq)
    g_in, (got,) = _wgrad(x_b, dproj_b, IN_COLS, wtile, "wgrad_in", [_SwapChipsJob({"w_out": sums_out[1]})])
    early["w_out"] = (sums_out[0], got["w_out"])

    late = {
        "w_in": g_in,
        "pool_w": g_pool_w.reshape(SQUARE_ROWS, GROUP),
        "sgu_w": g_sgu_w.reshape(SQUARE_ROWS, GROUP),
        "vecs": _pack_vecs({"pool_scale": vecs[0], "sgu_ln_g": vecs[1], "sgu_ln_b": vecs[2], "sgu_b": g_sgu_b[:, :, 0],
                            "ln1_g": stats1[0], "ln1_b": stats1[1], "ln2_g": stats2[0], "ln2_b": stats2[1]},
                           extra=stats2[2]),
    }
    shared = _reduce_tail(late, early)

    grad, delta, new_m, new_v = {}, {}, {}, {}
    for name in BIG:
        grad[name] = shared[name][None]
        d, nm, nv = _adamw(name, given[name][0], shared[name], given["m_" + name][0], given["v_" + name][0])
        delta[name], new_m[name], new_v[name] = d[None], nm[None], nv[None]
    vec_grads, after = _unpack_vecs(shared["vecs"], shapes)
    grad.update(vec_grads)
    for name in ("pool_w", "sgu_w"):
        grad[name] = shared[name].reshape(shapes[name])
    small_names = ("pool_w", "sgu_w") + VEC_NAMES
    state = {pre: [given[pre + name] for name in small_names] for pre in ("", "m_", "v_")}
    ds, nms, nvs = _adamw_small(state[""], [grad[name] for name in small_names], state["m_"], state["v_"])
    delta.update(zip(small_names, ds))
    new_m.update(zip(small_names, nms))
    new_v.update(zip(small_names, nvs))

    sq_err = after[:LOSS_ROWS]
    loss = jnp.sum(sq_err) * (0.5 / D_MODEL)
    return (loss, grad_x.reshape(x.shape), *[grad[name] for name in WEIGHT_ORDER],
            *[delta[name] for name in WEIGHT_ORDER], *[new_m[name] for name in WEIGHT_ORDER],
            *[new_v[name] for name in WEIGHT_ORDER])
```

```python
import math

import jax
import jax.numpy as jnp
from jax import lax
from jax.experimental import pallas as pl
from jax.experimental.pallas import tpu as pltpu

f32 = jnp.float32
bf16 = jnp.bfloat16
MESH = pl.DeviceIdType.MESH

D_MODEL = 1024
POOL_WIDTH = 512
SGU_WIDTH = 512
IN_COLS = POOL_WIDTH + 2 * SGU_WIDTH
D_FF = 2816
POOL_WINDOWS = (2, 4, 8, 16)
GROUP = 128
N_GROUPS = 4
HALO = 16
LN_EPS = 1e-5
ALPHA = float(2.0 ** 0.25)
N_CHIPS = 4

ADAM_LR = 0.001
ADAM_B1 = 0.9
ADAM_B2 = 0.999
ADAM_EPS = 1e-08
ADAM_WD = 0.01
ADAM_STEP = 10

TOKEN_TILE = 512
FFN_BWD_TILE = 512
FF_CHUNK = 256
ROW_SUB = 256
WGRAD_TILE = 1024
TOP_ROWS = 512
V7X_VMEM_LIMIT = 56 * 1024 * 1024

SQUARE_ROWS = N_GROUPS * GROUP
VEC_ROWS = 64
LOSS_ROWS = D_MODEL // GROUP


def _params(**kw):
    return pltpu.CompilerParams(vmem_limit_bytes=V7X_VMEM_LIMIT, **kw)


def _mm(a, b):
    return jnp.dot(a, b, preferred_element_type=f32)


def _mm_nt(a, b):
    return lax.dot_general(a, b, (((1,), (1,)), ((), ())), preferred_element_type=f32)


def _mm_tn(a, b):
    return lax.dot_general(a, b, (((0,), (0,)), ((), ())), preferred_element_type=f32)


def _ln_fwd(r, g, b):
    mu = jnp.mean(r, axis=-1, keepdims=True)
    xc = r - mu
    var = jnp.mean(xc * xc, axis=-1, keepdims=True)
    rstd = lax.rsqrt(var + LN_EPS)
    xhat = xc * rstd
    return xhat * g + b, xhat, rstd


def _ln_bwd(dout, xhat, rstd, g):
    dxhat = dout * g
    m1 = jnp.mean(dxhat, axis=-1, keepdims=True)
    m2 = jnp.mean(dxhat * xhat, axis=-1, keepdims=True)
    return rstd * (dxhat - m1 - xhat * m2)


def _col_sum(a):
    return jnp.sum(a, axis=0, keepdims=True)


def _gelu_parts(z):
    cdf = 0.5 * (1.0 + lax.erf(z * (1.0 / math.sqrt(2.0))))
    pdf = jnp.exp(-0.5 * z * z) * (1.0 / math.sqrt(2.0 * math.pi))
    return cdf, pdf


def _inv_counts(seq_tile, rows):
    pos = seq_tile * rows + lax.broadcasted_iota(jnp.int32, (rows, GROUP), 0) + 1
    return [1.0 / jnp.minimum(pos, w).astype(f32) for w in POOL_WINDOWS]


def _window_sums(e, back):
    n = e.shape[0]

    def shifted(a, s):
        return pltpu.roll(a, s if back else n - s, 0)

    s2 = e + shifted(e, 1)
    s4 = s2[:, GROUP:] + shifted(s2[:, GROUP:], 2)
    s8 = s4[:, GROUP:] + shifted(s4[:, GROUP:], 4)
    s16 = s8[:, GROUP:] + shifted(s8[:, GROUP:], 8)
    return [s2[:, :GROUP], s4[:, :GROUP], s8[:, :GROUP], s16]


def _pooled_groups(xp, halo, inv):
    sums = _window_sums(jnp.concatenate([halo, xp], axis=0), back=True)
    return [sums[g][HALO:] * inv[g] - xp[:, g * GROUP:(g + 1) * GROUP] for g in range(N_GROUPS)]


def _tril_mask():
    r = lax.broadcasted_iota(jnp.int32, (GROUP, GROUP), 0)
    c = lax.broadcasted_iota(jnp.int32, (GROUP, GROUP), 1)
    return (r >= c).astype(f32)


def _gs(g):
    return slice(g * GROUP, (g + 1) * GROUP)


class _Cut:
    def __init__(self, rows, cols, by_cols):
        self.rows, self.cols, self.by_cols = rows, cols, by_cols
        if by_cols:
            self.block_shape = (rows // 2, cols // N_CHIPS)
            self.shard_shape = (rows, cols // N_CHIPS)
        else:
            self.block_shape = (rows // (2 * N_CHIPS), cols)
            self.shard_shape = (rows // N_CHIPS, cols)

    def block(self, ref, chip, half):
        br, bc = self.block_shape
        if self.by_cols:
            return ref.at[pl.ds(pl.multiple_of(half * br, 16), br), pl.ds(pl.multiple_of(chip * bc, 128), bc)]
        return ref.at[pl.ds(pl.multiple_of((2 * chip + half) * br, 8), br), :]

    def shard(self, ref, chip):
        sr, sc = self.shard_shape
        if self.by_cols:
            return ref.at[:, pl.ds(pl.multiple_of(chip * sc, 128), sc)]
        return ref.at[pl.ds(pl.multiple_of(chip * sr, 16), sr), :]

    def half_of_shard(self, ref, half):
        br = self.block_shape[0]
        return ref.at[pl.ds(pl.multiple_of(half * br, 8), br), :]

    def block_index(self, chip, half):
        return (half, chip) if self.by_cols else (2 * chip + half, 0)


CUTS = {
    "w_in": _Cut(D_MODEL, IN_COLS, True),
    "w_out": _Cut(D_MODEL, D_MODEL, False),
    "w_gate_up": _Cut(D_MODEL, 2 * D_FF, True),
    "w_down": _Cut(D_FF, D_MODEL, False),
    "w_gate_up_top": _Cut(TOP_ROWS, 2 * D_FF, True),
    "w_gate_up_bottom": _Cut(D_MODEL - TOP_ROWS, 2 * D_FF, True),
    "pool_w": _Cut(SQUARE_ROWS, GROUP, False),
    "sgu_w": _Cut(SQUARE_ROWS, GROUP, False),
    "vecs": _Cut(VEC_ROWS, GROUP, False),
}
BIG = ("w_in", "w_out", "w_gate_up", "w_down")
SMALL = ("pool_w", "sgu_w", "vecs")
ANY = pl.BlockSpec(memory_space=pl.ANY)


def _wire_dtype(name):
    return bf16 if name in BIG else f32


def _place():
    x, y, c = lax.axis_index("x"), lax.axis_index("y"), lax.axis_index("c")
    others = [(1 - x, y), (x, 1 - y), (1 - x, 1 - y)]
    return x, y, c, 2 * x + y, others, [2 * ox + oy for ox, oy in others]


def _remote(src, dst, send_sems, recv_sems, k, to):
    return pltpu.make_async_remote_copy(src_ref=src, dst_ref=dst, send_sem=send_sems.at[k], recv_sem=recv_sems.at[k],
                                        device_id=to, device_id_type=MESH)


class _GatherJob:
    def __init__(self, shards):
        self.names = tuple(shards)
        self.arrays = tuple(shards.values())
        n = len(self.names)
        self.out_shapes = [jax.ShapeDtypeStruct((CUTS[name].rows, CUTS[name].cols), bf16) for name in self.names]
        self.scratch_shapes = ([pltpu.SemaphoreType.DMA((6 * n,)), pltpu.SemaphoreType.DMA((6 * n,)),
                                pltpu.SemaphoreType.DMA((2 * n,))]
                               + [pltpu.VMEM(CUTS[name].shard_shape, bf16) for name in self.names])

    def bind(self, shard_refs, full_refs, scratch):
        self.shards, self.full = shard_refs, full_refs
        self.send_sems, self.recv_sems, self.local_sems = scratch[:3]
        self.stages = scratch[3:]
        return self

    def _sends(self):
        _, _, c, me, others, _ = _place()
        return [_remote(CUTS[name].half_of_shard(self.shards[w], c), CUTS[name].block(self.full[w], me, c),
                        self.send_sems, self.recv_sems, 3 * w + k, (*chip, c))
                for w, name in enumerate(self.names) for k, chip in enumerate(others)]

    def _relays(self, half):
        x, y, c, _, _, other_ids = _place()
        n = len(self.names)
        return [_remote(CUTS[name].block(self.full[w], other_ids[k], half),
                        CUTS[name].block(self.full[w], other_ids[k], half),
                        self.send_sems, self.recv_sems, 3 * n + 3 * w + k, (x, y, 1 - c))
                for w, name in enumerate(self.names) for k in range(3)]

    def _stores(self):
        me = _place()[3]
        return [pltpu.make_async_copy(self.stages[w], CUTS[name].shard(self.full[w], me), self.local_sems.at[2 * w + 1])
                for w, name in enumerate(self.names)]

    def start(self):
        loads = [pltpu.make_async_copy(self.shards[w], self.stages[w], self.local_sems.at[2 * w])
                 for w in range(len(self.names))]
        for cp in loads:
            cp.start()
        for cp in self._sends():
            cp.start()
        for load, store in zip(loads, self._stores()):
            load.wait()
            store.start()

    def finish(self):
        _, _, c, _, others, other_ids = _place()
        relays = self._relays(c)
        for w, name in enumerate(self.names):
            for k, chip in enumerate(others):
                landed = CUTS[name].block(self.full[w], other_ids[k], c)
                _remote(landed, landed, self.send_sems, self.recv_sems, 3 * w + k, (*chip, c)).wait_recv()
                relays[3 * w + k].start()
        for cp in self._relays(1 - c):
            cp.wait_recv()
        for cp in self._sends() + relays:
            cp.wait_send()
        for cp in self._stores():
            cp.wait()


class _SwapHalvesJob:
    def __init__(self, grads):
        self.names = tuple(grads)
        self.arrays = tuple(grads.values())
        n = len(self.names)
        self.out_shapes = [jax.ShapeDtypeStruct((N_CHIPS, *CUTS[name].block_shape), f32) for name in self.names]
        self.scratch_shapes = [pltpu.SemaphoreType.DMA((N_CHIPS * n,)), pltpu.SemaphoreType.DMA((N_CHIPS * n,))]

    def bind(self, g_refs, got_refs, scratch):
        self.g_refs, self.got_refs = g_refs, got_refs
        self.send_sems, self.recv_sems = scratch
        return self

    def _copies(self):
        x, y, c, _, _, _ = _place()
        return [_remote(CUTS[name].block(self.g_refs[a], j, 1 - c), self.got_refs[a].at[j], self.send_sems,
                        self.recv_sems, N_CHIPS * a + j, (x, y, 1 - c))
                for a, name in enumerate(self.names) for j in range(N_CHIPS)]

    def start(self):
        for cp in self._copies():
            cp.start()

    def finish(self):
        for cp in self._copies():
            cp.wait()


class _SwapChipsJob:
    def __init__(self, partials):
        self.names = tuple(partials)
        self.arrays = tuple(partials.values())
        n = len(self.names)
        self.out_shapes = [jax.ShapeDtypeStruct((3, *CUTS[name].block_shape), _wire_dtype(name)) for name in self.names]
        self.scratch_shapes = [pltpu.SemaphoreType.DMA((3 * n,)), pltpu.SemaphoreType.DMA((3 * n,))]

    def bind(self, p_refs, got_refs, scratch):
        self.p_refs, self.got_refs = p_refs, got_refs
        self.send_sems, self.recv_sems = scratch
        return self

    def _copies(self):
        _, _, c, _, others, other_ids = _place()
        return [_remote(self.p_refs[a].at[other_ids[k]], self.got_refs[a].at[k], self.send_sems, self.recv_sems,
                        3 * a + k, (*chip, c))
                for a in range(len(self.names)) for k, chip in enumerate(others)]

    def start(self):
        for cp in self._copies():
            cp.start()

    def finish(self):
        for cp in self._copies():
            cp.wait()


class _ShareJob:
    def __init__(self, reduced):
        self.names = tuple(reduced)
        self.arrays = tuple(reduced.values())
        self.big = [a for a, name in enumerate(self.names) if name in BIG]
        self.small = [a for a, name in enumerate(self.names) if name in SMALL]
        self.out_shapes = [jax.ShapeDtypeStruct(CUTS[name].shard_shape if name in BIG
                                                else (CUTS[name].rows, CUTS[name].cols), f32) for name in self.names]
        n_sems = len(self.big) + 7 * len(self.small)
        self.scratch_shapes = ([pltpu.SemaphoreType.DMA((n_sems,)), pltpu.SemaphoreType.DMA((n_sems,)),
                                pltpu.SemaphoreType.DMA((2 * len(self.names),))]
                               + [pltpu.VMEM(CUTS[name].block_shape, f32) for name in self.names])

    def bind(self, f_refs, out_refs, scratch):
        self.f_refs, self.out_refs = f_refs, out_refs
        self.send_sems, self.recv_sems, self.local_sems = scratch[:3]
        self.stages = scratch[3:]
        return self

    def _sem(self, a, which=0):
        if a in self.big:
            return self.big.index(a)
        return len(self.big) + 7 * self.small.index(a) + which

    def _mine(self, a, half):
        me = _place()[3]
        cut = CUTS[self.names[a]]
        return cut.half_of_shard(self.out_refs[a], half) if a in self.big else cut.block(self.out_refs[a], me, half)

    def _to_sibling(self):
        x, y, c, _, _, _ = _place()
        return [_remote(self.f_refs[a], self._mine(a, c), self.send_sems, self.recv_sems, self._sem(a), (x, y, 1 - c))
                for a in range(len(self.names))]

    def _to_chips(self):
        _, _, c, _, others, _ = _place()
        return [_remote(self.f_refs[a], self._mine(a, c), self.send_sems, self.recv_sems, self._sem(a, 1 + k), (*chip, c))
                for a in self.small for k, chip in enumerate(others)]

    def _passes(self, half):
        x, y, c, _, _, other_ids = _place()
        out = []
        for a in self.small:
            for k in range(3):
                blk = CUTS[self.names[a]].block(self.out_refs[a], other_ids[k], half)
                out.append(_remote(blk, blk, self.send_sems, self.recv_sems, self._sem(a, 4 + k), (x, y, 1 - c)))
        return out

    def _stores(self):
        c = _place()[2]
        return [pltpu.make_async_copy(self.stages[a], self._mine(a, c), self.local_sems.at[2 * a + 1])
                for a in range(len(self.names))]

    def start(self):
        loads = [pltpu.make_async_copy(self.f_refs[a], self.stages[a], self.local_sems.at[2 * a])
                 for a in range(len(self.names))]
        for cp in loads:
            cp.start()
        for cp in self._to_sibling() + self._to_chips():
            cp.start()
        for load, store in zip(loads, self._stores()):
            load.wait()
            store.start()

    def finish(self):
        x, y, c, _, others, other_ids = _place()
        passes = self._passes(c)
        for s, a in enumerate(self.small):
            for k, chip in enumerate(others):
                landed = CUTS[self.names[a]].block(self.out_refs[a], other_ids[k], c)
                _remote(landed, landed, self.send_sems, self.recv_sems, self._sem(a, 1 + k), (*chip, c)).wait_recv()
                passes[3 * s + k].start()
        for a in range(len(self.names)):
            theirs = self._mine(a, 1 - c)
            _remote(theirs, theirs, self.send_sems, self.recv_sems, self._sem(a), (x, y, 1 - c)).wait_recv()
        for cp in self._passes(1 - c):
            cp.wait_recv()
        for cp in self._to_sibling() + self._to_chips() + passes:
            cp.wait_send()
        for cp in self._stores():
            cp.wait()


def _call(body, *, name, grid, in_specs, out_specs, out_shape, args, scratch_shapes=(), jobs=()):
    n_in, n_out, n_scr = len(in_specs), len(out_specs), len(scratch_shapes)
    j_in = [len(j.arrays) for j in jobs]
    j_out = [len(j.out_shapes) for j in jobs]
    j_scr = [len(j.scratch_shapes) for j in jobs]

    def wrapped(*refs):
        refs = list(refs)

        def take(k):
            head = refs[:k]
            del refs[:k]
            return head

        ins, jins = take(n_in), [take(k) for k in j_in]
        outs, jouts = take(n_out), [take(k) for k in j_out]
        scr, jscr = take(n_scr), [take(k) for k in j_scr]
        bound = [j.bind(a, b, c) for j, a, b, c in zip(jobs, jins, jouts, jscr)]
        if not grid:
            for b in bound:
                b.start()
            body(*ins, *outs, *scr)
            for b in bound:
                b.finish()
            return
        if not bound:
            body(*ins, *outs, *scr)
            return
        first = _all([pl.program_id(d) == 0 for d in range(len(grid))])
        last = _all([pl.program_id(d) == grid[d] - 1 for d in range(len(grid))])

        @pl.when(first)
        def _():
            for b in bound:
                b.start()

        body(*ins, *outs, *scr)

        @pl.when(last)
        def _():
            for b in bound:
                b.finish()

    kw = dict(grid=grid) if grid else {}
    results = pl.pallas_call(
        wrapped, name=name,
        in_specs=list(in_specs) + [ANY] * sum(j_in), out_specs=list(out_specs) + [ANY] * sum(j_out),
        out_shape=list(out_shape) + [s for j in jobs for s in j.out_shapes],
        scratch_shapes=list(scratch_shapes) + [s for j in jobs for s in j.scratch_shapes],
        compiler_params=_params(), **kw,
    )(*args, *[a for j in jobs for a in j.arrays])
    results = list(results)
    own, rest = results[:n_out], results[n_out:]
    per_job = []
    for j, k in zip(jobs, j_out):
        per_job.append(dict(zip(j.names, rest[:k])))
        rest = rest[k:]
    return own, per_job


def _all(conds):
    out = conds[0]
    for c in conds[1:]:
        out = jnp.logical_and(out, c)
    return out


def _alone(job, name):
    return _call(lambda: None, name=name, grid=None, in_specs=[], out_specs=[], out_shape=[], args=[], jobs=[job])[1][0]


def _cast_x(x2, tile, jobs):
    tokens = x2.shape[0]

    def body(x_ref, xb_ref):
        xb_ref[...] = x_ref[...].astype(bf16)

    row = lambda i: (i, 0)
    return _call(
        body, name="cast_x", grid=(tokens // tile,),
        in_specs=[pl.BlockSpec((tile, D_MODEL), row)], out_specs=[pl.BlockSpec((tile, D_MODEL), row)],
        out_shape=[jax.ShapeDtypeStruct((tokens, D_MODEL), bf16)], args=(x2,), jobs=jobs)


def _fwd_proj(x_b, w_in_b, tile, jobs):
    tokens = x_b.shape[0]

    def body(x_ref, w_ref, xp_ref, pre_ref):
        proj = _mm(x_ref[...], w_ref[...])
        xp_ref[...] = proj[:, :POOL_WIDTH]
        pre_ref[...] = proj[:, POOL_WIDTH:].astype(bf16)

    row = lambda i: (i, 0)
    return _call(
        body, name="fwd_proj", grid=(tokens // tile,),
        in_specs=[pl.BlockSpec((tile, D_MODEL), row), pl.BlockSpec((D_MODEL, IN_COLS), lambda i: (0, 0))],
        out_specs=[pl.BlockSpec((tile, POOL_WIDTH), row), pl.BlockSpec((tile, 2 * SGU_WIDTH), row)],
        out_shape=[jax.ShapeDtypeStruct((tokens, POOL_WIDTH), f32), jax.ShapeDtypeStruct((tokens, 2 * SGU_WIDTH), bf16)],
        args=(x_b, w_in_b), jobs=jobs)


def _small_specs():
    return [pl.BlockSpec((N_GROUPS, GROUP, GROUP), lambda i: (0, 0, 0)),
            pl.BlockSpec((1, POOL_WIDTH), lambda i: (0, 0)),
            pl.BlockSpec((1, SGU_WIDTH), lambda i: (0, 0)),
            pl.BlockSpec((1, SGU_WIDTH), lambda i: (0, 0)),
            pl.BlockSpec((N_GROUPS, GROUP, GROUP), lambda i: (0, 0, 0)),
            pl.BlockSpec((N_GROUPS, GROUP, GROUP), lambda i: (0, 0, 0))]


def _fwd_mix(xp_all, pre_b, x2, small, w_out_b, ln1_g, ln1_b, tile, seq, jobs):
    tokens = x2.shape[0]
    tps = seq // tile
    hb = tile // HALO

    def body(xp_ref, halo_ref, pre_ref, x_ref, pw_ref, ps_ref, lg_ref, lb_ref, sw_ref, sb_ref, wout_ref, g1_ref, b1_ref,
             mix_ref, r1_ref, h_ref):
        seq_tile = pl.program_id(0) % tps
        xp = xp_ref[...]
        halo = jnp.where(seq_tile == 0, 0.0, halo_ref[...])
        pooled = _pooled_groups(xp, halo, _inv_counts(seq_tile, tile))
        for g in range(N_GROUPS):
            po = _mm(pooled[g].astype(bf16), pw_ref[g].astype(bf16)) * ps_ref[:, _gs(g)]
            mix_ref[:, _gs(g)] = po.astype(bf16)

        pre = pre_ref[...].astype(f32)
        cdf, _ = _gelu_parts(pre)
        zg = pre * cdf
        u = zg[:, :SGU_WIDTH]
        vln, _, _ = _ln_fwd(zg[:, SGU_WIDTH:], lg_ref[...], lb_ref[...])
        vb = vln.astype(bf16)
        mask = _tril_mask()
        for h in range(N_GROUPS):
            wm = (sw_ref[h] * mask).astype(bf16)
            bias = sb_ref[h]
            for c in range(tile // GROUP):
                rows = slice(c * GROUP, (c + 1) * GROUP)
                mixed = _mm(wm, vb[rows, _gs(h)]) + bias
                mix_ref[rows, POOL_WIDTH + h * GROUP:POOL_WIDTH + (h + 1) * GROUP] = (u[rows, _gs(h)] * mixed).astype(bf16)

        r1 = ALPHA * x_ref[...] + _mm(mix_ref[...], wout_ref[...])
        r1_ref[...] = r1
        h1, _, _ = _ln_fwd(r1, g1_ref[...], b1_ref[...])
        h_ref[...] = h1.astype(bf16)

    row = lambda i: (i, 0)
    vec = pl.BlockSpec((1, D_MODEL), lambda i: (0, 0))
    return _call(
        body, name="fwd_mix", grid=(tokens // tile,),
        in_specs=[pl.BlockSpec((tile, POOL_WIDTH), row),
                  pl.BlockSpec((HALO, POOL_WIDTH), lambda i: (jnp.maximum(i * hb - 1, 0), 0)),
                  pl.BlockSpec((tile, 2 * SGU_WIDTH), row),
                  pl.BlockSpec((tile, D_MODEL), row)] + _small_specs()
                 + [pl.BlockSpec((D_MODEL, D_MODEL), lambda i: (0, 0)), vec, vec],
        out_specs=[pl.BlockSpec((tile, D_MODEL), row)] * 3,
        out_shape=[jax.ShapeDtypeStruct((tokens, D_MODEL), bf16),
                   jax.ShapeDtypeStruct((tokens, D_MODEL), f32),
                   jax.ShapeDtypeStruct((tokens, D_MODEL), bf16)],
        args=(xp_all, xp_all, pre_b, x2, *small, w_out_b, ln1_g, ln1_b), jobs=jobs)


def _fwd_gate_up(h_b, w_top, w_bottom, tile, jobs):
    tokens = h_b.shape[0]
    def body(h_ref, wt_ref, wb_ref, gu_ref, a_ref):
        ht, hb = h_ref[:, :TOP_ROWS], h_ref[:, TOP_ROWS:]
        for c in range(D_FF // FF_CHUNK):
            gcols = slice(c * FF_CHUNK, (c + 1) * FF_CHUNK)
            ucols = slice(D_FF + c * FF_CHUNK, D_FF + (c + 1) * FF_CHUNK)
            gate = _mm(ht, wt_ref[:, gcols]) + _mm(hb, wb_ref[:, gcols])
            up = _mm(ht, wt_ref[:, ucols]) + _mm(hb, wb_ref[:, ucols])
            sg = jax.nn.sigmoid(gate)
            silu = gate * sg
            gu_ref[:, gcols] = (up * (sg + silu * (1.0 - sg))).astype(bf16)
            gu_ref[:, ucols] = silu.astype(bf16)
            a_ref[:, gcols] = (silu * up).astype(bf16)

    return _call(
        body, name="fwd_gate_up", grid=(tokens // tile,),
        in_specs=[pl.BlockSpec((tile, D_MODEL), lambda i: (i, 0)),
                  pl.BlockSpec((TOP_ROWS, 2 * D_FF), lambda i: (0, 0), pipeline_mode=pl.Buffered(1)),
                  pl.BlockSpec((D_MODEL - TOP_ROWS, 2 * D_FF), lambda i: (0, 0), pipeline_mode=pl.Buffered(1))],
        out_specs=[pl.BlockSpec((tile, 2 * D_FF), lambda i: (i, 0)),
                   pl.BlockSpec((tile, D_FF), lambda i: (i, 0))],
        out_shape=[jax.ShapeDtypeStruct((tokens, 2 * D_FF), bf16),
                   jax.ShapeDtypeStruct((tokens, D_FF), bf16)],
        args=(h_b, w_top, w_bottom), jobs=jobs)


def _fwd_down_loss(a_b, w_dn_b, r1, target, ln1_g, ln1_b, ln2_g, ln2_b, tile):
    tokens = a_b.shape[0]

    def body(a_ref, w_ref, r1_ref, t_ref, g1_ref, b1_ref, g2_ref, b2_ref, dr2_ref, dr2b_ref, st_ref):
        @pl.when(pl.program_id(0) == 0)
        def _():
            st_ref[...] = jnp.zeros_like(st_ref)

        sub = min(ROW_SUB, tile)
        for s in range(tile // sub):
            rows = slice(s * sub, (s + 1) * sub)
            h1, _, _ = _ln_fwd(r1_ref[rows, :], g1_ref[...], b1_ref[...])
            r2 = ALPHA * h1 + _mm(a_ref[rows, :], w_ref[...])
            y, xhat, rstd = _ln_fwd(r2, g2_ref[...], b2_ref[...])
            diff = y - t_ref[rows, :]
            dy = diff * (1.0 / D_MODEL)
            st_ref[0:1, :] += _col_sum(dy * xhat)
            st_ref[1:2, :] += _col_sum(dy)
            st_ref[2:3, :] += _col_sum(diff * diff)
            dr2 = _ln_bwd(dy, xhat, rstd, g2_ref[...])
            dr2_ref[rows, :] = dr2
            dr2b_ref[rows, :] = dr2.astype(bf16)

    row = lambda i: (i, 0)
    vec = pl.BlockSpec((1, D_MODEL), lambda i: (0, 0))
    return _call(
        body, name="fwd_down_loss", grid=(tokens // tile,),
        in_specs=[pl.BlockSpec((tile, D_FF), row), pl.BlockSpec((D_FF, D_MODEL), lambda i: (0, 0)),
                  pl.BlockSpec((tile, D_MODEL), row), pl.BlockSpec((tile, D_MODEL), row), vec, vec, vec, vec],
        out_specs=[pl.BlockSpec((tile, D_MODEL), row), pl.BlockSpec((tile, D_MODEL), row),
                   pl.BlockSpec((8, D_MODEL), lambda i: (0, 0))],
        out_shape=[jax.ShapeDtypeStruct((tokens, D_MODEL), f32), jax.ShapeDtypeStruct((tokens, D_MODEL), bf16),
                   jax.ShapeDtypeStruct((8, D_MODEL), f32)],
        args=(a_b, w_dn_b, r1, target, ln1_g, ln1_b, ln2_g, ln2_b))[0]


def _bwd_gate_up(dr2, gu_b, w_dn_b, tile, jobs):
    tokens = dr2.shape[0]

    def body(d_ref, gu_ref, w_ref, dgu_ref):
        d = d_ref[...].astype(bf16)
        for c in range(D_FF // FF_CHUNK):
            gcols = slice(c * FF_CHUNK, (c + 1) * FF_CHUNK)
            ucols = slice(D_FF + c * FF_CHUNK, D_FF + (c + 1) * FF_CHUNK)
            da = _mm_nt(d, w_ref[gcols, :])
            dgu_ref[:, gcols] = (da * gu_ref[:, gcols].astype(f32)).astype(bf16)
            dgu_ref[:, ucols] = (da * gu_ref[:, ucols].astype(f32)).astype(bf16)

    return _call(
        body, name="bwd_gate_up", grid=(tokens // tile,),
        in_specs=[pl.BlockSpec((tile, D_MODEL), lambda i: (i, 0)),
                  pl.BlockSpec((tile, 2 * D_FF), lambda i: (i, 0)),
                  pl.BlockSpec((D_FF, D_MODEL), lambda i: (0, 0))],
        out_specs=[pl.BlockSpec((tile, 2 * D_FF), lambda i: (i, 0))],
        out_shape=[jax.ShapeDtypeStruct((tokens, 2 * D_FF), bf16)],
        args=(dr2, gu_b, w_dn_b), jobs=jobs)


def _bwd_ffn_in(dgu_b, w_top, w_bottom, dr2, r1, ln1_g, ln1_b, tile, jobs):
    tokens = dr2.shape[0]

    def body(dgu_ref, wt_ref, wb_ref, d_ref, r1_ref, g1_ref, b1_ref, dr1_ref, dr1b_ref, st_ref):
        @pl.when(pl.program_id(0) == 0)
        def _():
            st_ref[...] = jnp.zeros_like(st_ref)

        dgu = dgu_ref[...]
        dh = ALPHA * d_ref[...] + jnp.concatenate([_mm_nt(dgu, wt_ref[...]), _mm_nt(dgu, wb_ref[...])], axis=1)
        _, xhat, rstd = _ln_fwd(r1_ref[...], g1_ref[...], b1_ref[...])
        st_ref[0:1, :] += _col_sum(dh * xhat)
        st_ref[1:2, :] += _col_sum(dh)
        dr1 = _ln_bwd(dh, xhat, rstd, g1_ref[...])
        dr1_ref[...] = dr1
        dr1b_ref[...] = dr1.astype(bf16)

    row = lambda i: (i, 0)
    vec = pl.BlockSpec((1, D_MODEL), lambda i: (0, 0))
    return _call(
        body, name="bwd_ffn_in", grid=(tokens // tile,),
        in_specs=[pl.BlockSpec((tile, 2 * D_FF), row),
                  pl.BlockSpec((TOP_ROWS, 2 * D_FF), lambda i: (0, 0), pipeline_mode=pl.Buffered(1)),
                  pl.BlockSpec((D_MODEL - TOP_ROWS, 2 * D_FF), lambda i: (0, 0), pipeline_mode=pl.Buffered(1)),
                  pl.BlockSpec((tile, D_MODEL), row), pl.BlockSpec((tile, D_MODEL), row), vec, vec],
        out_specs=[pl.BlockSpec((tile, D_MODEL), row), pl.BlockSpec((tile, D_MODEL), row),
                   pl.BlockSpec((8, D_MODEL), lambda i: (0, 0))],
        out_shape=[jax.ShapeDtypeStruct((tokens, D_MODEL), f32), jax.ShapeDtypeStruct((tokens, D_MODEL), bf16),
                   jax.ShapeDtypeStruct((8, D_MODEL), f32)],
        args=(dgu_b, w_top, w_bottom, dr2, r1, ln1_g, ln1_b), jobs=jobs)


def _bwd_mix(dr1, xp_all, pre_b, small, w_out_b, tile, seq, jobs):
    tokens = dr1.shape[0]
    tps = seq // tile
    hb = tile // HALO
    steps = tokens // tile

    def body(dr1_ref, xp_ref, halo_ref, pre_ref, wout_ref, pw_ref, ps_ref, lg_ref, lb_ref, sw_ref, sb_ref,
             dpool_ref, dpre_ref, gpw_ref, gsw_ref, gsb_ref, vec_ref, du_ref, dvln_ref):
        step = pl.program_id(0)
        seq_tile = step % tps

        @pl.when(step == 0)
        def _():
            gpw_ref[...] = jnp.zeros_like(gpw_ref)
            gsw_ref[...] = jnp.zeros_like(gsw_ref)
            gsb_ref[...] = jnp.zeros_like(gsb_ref)
            vec_ref[...] = jnp.zeros_like(vec_ref)

        dmix = _mm_nt(dr1_ref[...].astype(bf16), wout_ref[...])

        xp = xp_ref[...]
        halo = jnp.where(seq_tile == 0, 0.0, halo_ref[...])
        pooled = _pooled_groups(xp, halo, _inv_counts(seq_tile, tile))
        for g in range(N_GROUPS):
            pb = pooled[g].astype(bf16)
            pwb = pw_ref[g].astype(bf16)
            dpo = dmix[:, _gs(g)]
            vec_ref[0:1, _gs(g)] += _col_sum(dpo * _mm(pb, pwb))
            dpo_b = (dpo * ps_ref[:, _gs(g)]).astype(bf16)
            gpw_ref[g] += _mm_tn(pb, dpo_b)
            dpool_ref[:, _gs(g)] = _mm_nt(dpo_b, pwb)

        pre = pre_ref[...].astype(f32)
        cdf, pdf = _gelu_parts(pre)
        zg = pre * cdf
        u = zg[:, :SGU_WIDTH]
        vln, vhat, rstd = _ln_fwd(zg[:, SGU_WIDTH:], lg_ref[...], lb_ref[...])
        vb = vln.astype(bf16)
        mask = _tril_mask()
        for h in range(N_GROUPS):
            wm = (sw_ref[h] * mask).astype(bf16)
            bias = sb_ref[h]
            gsw = jnp.zeros((GROUP, GROUP), f32)
            gsb = jnp.zeros((GROUP, GROUP), f32)
            for c in range(tile // GROUP):
                rows = slice(c * GROUP, (c + 1) * GROUP)
                v_ch = vb[rows, _gs(h)]
                d = dmix[rows, POOL_WIDTH + h * GROUP:POOL_WIDTH + (h + 1) * GROUP]
                du_ref[rows, _gs(h)] = d * (_mm(wm, v_ch) + bias)
                dmixed = d * u[rows, _gs(h)]
                gsb += dmixed
                dmixed_b = dmixed.astype(bf16)
                gsw += _mm_nt(dmixed_b, v_ch)
                dvln_ref[rows, _gs(h)] = _mm_tn(wm, dmixed_b)
            gsw_ref[h] += gsw * mask
            gsb_ref[h] += gsb

        dvln = dvln_ref[...]
        vec_ref[1:2, :] += _col_sum(dvln * vhat)
        vec_ref[2:3, :] += _col_sum(dvln)
        dgelu = cdf + pre * pdf
        dpre_ref[:, :SGU_WIDTH] = (du_ref[...] * dgelu[:, :SGU_WIDTH]).astype(bf16)
        dpre_ref[:, SGU_WIDTH:] = (_ln_bwd(dvln, vhat, rstd, lg_ref[...]) * dgelu[:, SGU_WIDTH:]).astype(bf16)

        @pl.when(step == steps - 1)
        def _():
            for h in range(N_GROUPS):
                gsb_ref[h] = jnp.broadcast_to(jnp.sum(gsb_ref[h], axis=1, keepdims=True), (GROUP, GROUP))

    row = lambda i: (i, 0)
    sq = jax.ShapeDtypeStruct((N_GROUPS, GROUP, GROUP), f32)
    sq_spec = pl.BlockSpec((N_GROUPS, GROUP, GROUP), lambda i: (0, 0, 0))
    return _call(
        body, name="bwd_mix", grid=(steps,),
        in_specs=[pl.BlockSpec((tile, D_MODEL), row), pl.BlockSpec((tile, POOL_WIDTH), row),
                  pl.BlockSpec((HALO, POOL_WIDTH), lambda i: (jnp.maximum(i * hb - 1, 0), 0)),
                  pl.BlockSpec((tile, 2 * SGU_WIDTH), row),
                  pl.BlockSpec((D_MODEL, D_MODEL), lambda i: (0, 0))] + _small_specs(),
        out_specs=[pl.BlockSpec((tile, POOL_WIDTH), row), pl.BlockSpec((tile, 2 * SGU_WIDTH), row),
                   sq_spec, sq_spec, sq_spec, pl.BlockSpec((8, POOL_WIDTH), lambda i: (0, 0))],
        out_shape=[jax.ShapeDtypeStruct((tokens, POOL_WIDTH), f32), jax.ShapeDtypeStruct((tokens, 2 * SGU_WIDTH), bf16),
                   sq, sq, sq, jax.ShapeDtypeStruct((8, POOL_WIDTH), f32)],
        scratch_shapes=[pltpu.VMEM((tile, SGU_WIDTH), f32), pltpu.VMEM((tile, SGU_WIDTH), f32)],
        args=(dr1, xp_all, xp_all, pre_b, w_out_b, *small), jobs=jobs)


def _bwd_in(dpool, dpre_b, dr1, w_in_b, tile, seq):
    tokens = dr1.shape[0]
    tps = seq // tile
    hb = tile // HALO
    last_halo = tokens // HALO - 1

    def body(dpool_ref, nxt_ref, dpre_ref, dr1_ref, w_ref, dx_ref, dproj_ref):
        seq_tile = pl.program_id(0) % tps
        inv = _inv_counts(seq_tile, tile)
        dpl = dpool_ref[...]
        nxt = jnp.where(seq_tile == tps - 1, 0.0, nxt_ref[...])
        scaled = jnp.concatenate([dpl[:, _gs(g)] * inv[g] for g in range(N_GROUPS)], axis=1)
        scaled_nxt = jnp.concatenate([nxt[:, _gs(g)] * (1.0 / POOL_WINDOWS[g]) for g in range(N_GROUPS)], axis=1)
        sums = _window_sums(jnp.concatenate([scaled, scaled_nxt], axis=0), back=False)
        for g in range(N_GROUPS):
            dproj_ref[:, _gs(g)] = (sums[g][:tile] - dpl[:, _gs(g)]).astype(bf16)
        dproj_ref[:, POOL_WIDTH:] = dpre_ref[...]
        dx_ref[...] = ALPHA * dr1_ref[...] + _mm_nt(dproj_ref[...], w_ref[...])

    row = lambda i: (i, 0)
    return _call(
        body, name="bwd_in", grid=(tokens // tile,),
        in_specs=[pl.BlockSpec((tile, POOL_WIDTH), row),
                  pl.BlockSpec((HALO, POOL_WIDTH), lambda i: (jnp.minimum((i + 1) * hb, last_halo), 0)),
                  pl.BlockSpec((tile, 2 * SGU_WIDTH), row),
                  pl.BlockSpec((tile, D_MODEL), row),
                  pl.BlockSpec((D_MODEL, IN_COLS), lambda i: (0, 0))],
        out_specs=[pl.BlockSpec((tile, D_MODEL), row), pl.BlockSpec((tile, IN_COLS), row)],
        out_shape=[jax.ShapeDtypeStruct((tokens, D_MODEL), f32), jax.ShapeDtypeStruct((tokens, IN_COLS), bf16)],
        args=(dpool, dpool, dpre_b, dr1, w_in_b))[0]


def _wgrad(a, b, col_tile, tile, name, jobs=()):
    tokens, m = a.shape
    n = b.shape[1]

    def body(a_ref, b_ref, o_ref):
        @pl.when(pl.program_id(1) == 0)
        def _():
            o_ref[...] = jnp.zeros_like(o_ref)

        o_ref[...] += _mm_tn(a_ref[...].astype(bf16), b_ref[...].astype(bf16))

    (out,), got = _call(
        body, name=name, grid=(n // col_tile, tokens // tile),
        in_specs=[pl.BlockSpec((tile, m), lambda j, k: (k, 0)),
                  pl.BlockSpec((tile, col_tile), lambda j, k: (k, j))],
        out_specs=[pl.BlockSpec((m, col_tile), lambda j, k: (0, j))],
        out_shape=[jax.ShapeDtypeStruct((m, n), f32)],
        args=(a, b), jobs=jobs)
    return out, got


def _add_halves(name, g, got, place):
    cut = CUTS[name]
    br, bc = cut.block_shape
    wire = _wire_dtype(name)

    def body(place_ref, g_ref, got_ref, o_ref, wire_ref):
        s = g_ref[...] + got_ref[...]
        wire_ref[...] = s.astype(wire)

        @pl.when(pl.program_id(0) == place_ref[1])
        def _():
            o_ref[...] = s

    blocks = pl.BlockSpec((None, br, bc), lambda j, place_ref: (j, 0, 0))
    return pl.pallas_call(
        body, name="reduce_add_halves_" + name,
        grid_spec=pltpu.PrefetchScalarGridSpec(
            num_scalar_prefetch=1, grid=(N_CHIPS,),
            in_specs=[pl.BlockSpec((br, bc), lambda j, place_ref: cut.block_index(j, place_ref[0])), blocks],
            out_specs=[pl.BlockSpec((br, bc), lambda j, place_ref: (0, 0)), blocks]),
        out_shape=[jax.ShapeDtypeStruct((br, bc), f32), jax.ShapeDtypeStruct((N_CHIPS, br, bc), wire)],
        compiler_params=_params(),
    )(place, g, got)


def _reduce_tail(late, early):
    late_names, early_names = tuple(late), tuple(early)
    names = early_names + late_names
    nl, ne, n = len(late_names), len(early_names), len(names)
    cuts = [CUTS[name] for name in names]
    is_big = [name in BIG for name in names]
    share_base, share_sem = 7 * nl, []
    for i in range(n):
        share_sem.append(share_base)
        share_base += 1 if is_big[i] else 7
    n_in = nl + 2 * ne

    def body(*refs):
        g_refs = refs[:nl]
        sums_refs, got_refs = refs[nl:n_in:2], refs[nl + 1:n_in:2]
        out_refs = refs[n_in:n_in + n]
        send_sems, recv_sems, local_sems = refs[n_in + n:n_in + n + 3]
        vm = refs[n_in + n + 3:]
        own, recv_a, wire, recv_b = vm[0:4 * nl:4], vm[1:4 * nl:4], vm[2:4 * nl:4], vm[3:4 * nl:4]
        acc, gotv = vm[4 * nl::2], vm[4 * nl + 1::2]
        x, y, c, me, others, other_ids = _place()
        sibling = (x, y, 1 - c)

        def mine(i, half):
            return cuts[i].half_of_shard(out_refs[i], half) if is_big[i] else cuts[i].block(out_refs[i], me, half)

        def reduced(i):
            return acc[i] if i < ne else own[i - ne].at[me]

        sent, stores = [], []

        def share(i):
            store = pltpu.make_async_copy(reduced(i), mine(i, c), local_sems.at[4 * nl + 2 * ne + i])
            store.start()
            stores.append(store)
            to = [sibling] if is_big[i] else [sibling] + [(*chip, c) for chip in others]
            for which, device in enumerate(to):
                cp = _remote(reduced(i), mine(i, c), send_sems, recv_sems, share_sem[i] + which, device)
                cp.start()
                sent.append(cp)

        early_loads = []
        for e in range(ne):
            early_loads.append(pltpu.make_async_copy(sums_refs[e], acc[e], local_sems.at[4 * nl + 2 * e]))
            early_loads.append(pltpu.make_async_copy(got_refs[e], gotv[e], local_sems.at[4 * nl + 2 * e + 1]))
        late_loads = [pltpu.make_async_copy(cuts[ne + l].block(g_refs[l], j, c), own[l].at[j], local_sems.at[4 * l + j])
                      for l in range(nl) for j in range(N_CHIPS)]
        halves = [_remote(cuts[ne + l].block(g_refs[l], j, 1 - c), recv_a[l].at[j], send_sems, recv_sems, 4 * l + j, sibling)
                  for l in range(nl) for j in range(N_CHIPS)]
        for cp in early_loads + late_loads + halves:
            cp.start()

        for cp in early_loads:
            cp.wait()
        for e in range(ne):
            acc[e][...] = ((acc[e][...] + gotv[e][0].astype(f32)) + gotv[e][1].astype(f32)) + gotv[e][2].astype(f32)
            share(e)

        for cp in late_loads:
            cp.wait()
        for cp in halves:
            cp.wait_recv()
        for l in range(nl):
            for j in range(N_CHIPS):
                s = own[l][j] + recv_a[l][j]
                own[l][j] = s
                wire[l][j] = s.astype(wire[l].dtype)
        chips = [_remote(wire[l].at[other_ids[k]], recv_b[l].at[k], send_sems, recv_sems, 4 * nl + 3 * l + k, (*chip, c))
                 for l in range(nl) for k, chip in enumerate(others)]
        for cp in chips:
            cp.start()
        for cp in chips:
            cp.wait_recv()
        for l in range(nl):
            mine_l = own[l].at[me]
            mine_l[...] = ((mine_l[...] + recv_b[l][0].astype(f32)) + recv_b[l][1].astype(f32)) + recv_b[l][2].astype(f32)
            share(ne + l)

        for i in range(n):
            if not is_big[i]:
                for k, chip in enumerate(others):
                    landed = cuts[i].block(out_refs[i], other_ids[k], c)
                    _remote(landed, landed, send_sems, recv_sems, share_sem[i] + 1 + k, (*chip, c)).wait_recv()
                    cp = _remote(landed, landed, send_sems, recv_sems, share_sem[i] + 4 + k, sibling)
                    cp.start()
                    sent.append(cp)
        for i in range(n):
            theirs = mine(i, 1 - c)
            _remote(theirs, theirs, send_sems, recv_sems, share_sem[i], sibling).wait_recv()
            if not is_big[i]:
                for k in range(3):
                    passed = cuts[i].block(out_refs[i], other_ids[k], 1 - c)
                    _remote(passed, passed, send_sems, recv_sems, share_sem[i] + 4 + k, sibling).wait_recv()
        for cp in halves + chips + sent:
            cp.wait_send()
        for cp in stores:
            cp.wait()

    scratch = [pltpu.SemaphoreType.DMA((share_base,)), pltpu.SemaphoreType.DMA((share_base,)),
               pltpu.SemaphoreType.DMA((4 * nl + 2 * ne + n,))]
    for name in late_names:
        block = CUTS[name].block_shape
        scratch += [pltpu.VMEM((N_CHIPS, *block), f32), pltpu.VMEM((N_CHIPS, *block), f32),
                    pltpu.VMEM((N_CHIPS, *block), _wire_dtype(name)), pltpu.VMEM((3, *block), _wire_dtype(name))]
    for name in early_names:
        block = CUTS[name].block_shape
        scratch += [pltpu.VMEM(block, f32), pltpu.VMEM((3, *block), _wire_dtype(name))]
    args = [late[name] for name in late_names] + [a for name in early_names for a in early[name]]
    outs = pl.pallas_call(
        body, name="reduce_tail",
        in_specs=[ANY] * n_in, out_specs=[ANY] * n,
        out_shape=[jax.ShapeDtypeStruct(CUTS[name].shard_shape if name in BIG else (CUTS[name].rows, CUTS[name].cols), f32)
                   for name in names],
        scratch_shapes=scratch, compiler_params=_params(),
    )(*args)
    return dict(zip(names, outs))


def _adamw_refs(w_ref, g_ref, m_ref, v_ref, d_ref, nm_ref, nv_ref):
    g = g_ref[...]
    nm = ADAM_B1 * m_ref[...] + (1.0 - ADAM_B1) * g
    nv = ADAM_B2 * v_ref[...] + (1.0 - ADAM_B2) * jnp.square(g)
    m_hat = nm / (1.0 - ADAM_B1 ** ADAM_STEP)
    v_hat = nv / (1.0 - ADAM_B2 ** ADAM_STEP)
    d_ref[...] = -ADAM_LR * (m_hat / (jnp.sqrt(v_hat) + ADAM_EPS) + ADAM_WD * w_ref[...])
    nm_ref[...] = nm
    nv_ref[...] = nv


def _adamw_small(ws, gs, ms, vs):
    n = len(ws)

    def body(*refs):
        for i in range(n):
            _adamw_refs(*[refs[k * n + i] for k in range(7)])

    whole = pl.BlockSpec(memory_space=pltpu.VMEM)
    outs = pl.pallas_call(
        body, name="adamw_small",
        in_specs=[whole] * (4 * n), out_specs=[whole] * (3 * n),
        out_shape=[jax.ShapeDtypeStruct(w.shape, f32) for w in ws] * 3,
        compiler_params=_params(),
    )(*ws, *gs, *ms, *vs)
    return outs[:n], outs[n:2 * n], outs[2 * n:]


def _adamw(name, w, g, m, v):
    rows, cols = w.shape
    rt = rows // 4

    def body(w_ref, g_ref, m_ref, v_ref, d_ref, nm_ref, nv_ref):
        _adamw_refs(w_ref, g_ref, m_ref, v_ref, d_ref, nm_ref, nv_ref)

    spec = pl.BlockSpec((rt, cols), lambda i: (i, 0))
    shape = jax.ShapeDtypeStruct((rows, cols), f32)
    return pl.pallas_call(
        body, name="adamw_" + name, grid=(rows // rt,),
        in_specs=[spec] * 4, out_specs=[spec] * 3, out_shape=[shape] * 3,
        compiler_params=_params(),
    )(w, g, m, v)


VEC_NAMES = ("pool_scale", "sgu_ln_g", "sgu_ln_b", "sgu_b", "ln1_g", "ln1_b", "ln2_g", "ln2_b")
WEIGHT_ORDER = ("w_in", "pool_w", "pool_scale", "sgu_ln_g", "sgu_ln_b", "sgu_w", "sgu_b", "w_out", "ln1_g", "ln1_b",
                "w_gate_up", "w_down", "ln2_g", "ln2_b")


def _pack_vecs(parts, extra=None):
    rows = [parts[name].reshape(-1, GROUP) for name in VEC_NAMES]
    if extra is not None:
        rows.append(extra.reshape(-1, GROUP))
    used = sum(r.shape[0] for r in rows)
    return jnp.concatenate(rows + [jnp.zeros((VEC_ROWS - used, GROUP), f32)], axis=0)


def _unpack_vecs(packed, shapes):
    out, at = {}, 0
    for name in VEC_NAMES:
        rows = math.prod(shapes[name]) // GROUP
        out[name] = packed[at:at + rows].reshape(shapes[name])
        at += rows
    return out, packed[at:]


def kernel(x, w_in, pool_w, pool_scale, sgu_ln_g, sgu_ln_b, sgu_w, sgu_b, w_out, ln1_g, ln1_b, w_gate_up, w_down, ln2_g, ln2_b, loss_target, m_w_in, m_pool_w, m_pool_scale, m_sgu_ln_g, m_sgu_ln_b, m_sgu_w, m_sgu_b, m_w_out, m_ln1_g, m_ln1_b, m_w_gate_up, m_w_down, m_ln2_g, m_ln2_b, v_w_in, v_pool_w, v_pool_scale, v_sgu_ln_g, v_sgu_ln_b, v_sgu_w, v_sgu_b, v_w_out, v_ln1_g, v_ln1_b, v_w_gate_up, v_w_down, v_ln2_g, v_ln2_b):
    given = dict(locals())
    batch, seq, _ = x.shape
    tokens = batch * seq
    tile = min(TOKEN_TILE, seq)
    ffn_bwd_tile = min(FFN_BWD_TILE, seq)
    wtile = min(WGRAD_TILE, tokens)
    shapes = {name: given[name].shape for name in WEIGHT_ORDER}

    x2 = x.reshape(tokens, D_MODEL)
    target = loss_target.reshape(tokens, D_MODEL)
    small = (pool_w[0], pool_scale[0][None], sgu_ln_g[0][None], sgu_ln_b[0][None], sgu_w[0],
             jnp.broadcast_to(sgu_b[0][:, :, None], (N_GROUPS, GROUP, GROUP)))
    g1, b1, g2, b2 = ln1_g[0][None], ln1_b[0][None], ln2_g[0][None], ln2_b[0][None]
    shard_b = {name: given[name][0].astype(bf16) for name in BIG}
    shard_b["w_gate_up_top"] = shard_b["w_gate_up"][:TOP_ROWS]
    shard_b["w_gate_up_bottom"] = shard_b["w_gate_up"][TOP_ROWS:]
    place = jnp.stack([lax.axis_index("c"), 2 * lax.axis_index("x") + lax.axis_index("y")]).astype(jnp.int32)

    def gather(*names):
        return [_GatherJob({name: shard_b[name] for name in names})]

    def halves_summed(name, grad, got):
        return _add_halves(name, grad, got, place)

    (x_b,), (got,) = _cast_x(x2, 2 * tile, gather("w_in", "w_out"))
    w_in_b, w_out_b = got["w_in"], got["w_out"]
    (xp_all, pre_b), (got,) = _fwd_proj(x_b, w_in_b, tile, gather("w_gate_up_top"))
    w_top = got["w_gate_up_top"]
    (mix_b, r1, h_b), (got,) = _fwd_mix(xp_all, pre_b, x2, small, w_out_b, g1, b1, tile, seq, gather("w_gate_up_bottom"))
    w_bottom = got["w_gate_up_bottom"]
    (gu_b, a_b), (got,) = _fwd_gate_up(h_b, w_top, w_bottom, tile, gather("w_down"))
    w_dn_b = got["w_down"]
    dr2, dr2_b, stats2 = _fwd_down_loss(a_b, w_dn_b, r1, target, g1, b1, g2, b2, tile)

    early = {}
    g_down, _ = _wgrad(a_b, dr2_b, D_MODEL, wtile, "wgrad_down")
    (dgu_b,), (got,) = _bwd_gate_up(dr2_b, gu_b, w_dn_b, ffn_bwd_tile, [_SwapHalvesJob({"w_down": g_down})])
    sums_down = halves_summed("w_down", g_down, got["w_down"])
    g_gu, (got,) = _wgrad(h_b, dgu_b, D_FF, wtile, "wgrad_gate_up", [_SwapChipsJob({"w_down": sums_down[1]})])
    early["w_down"] = (sums_down[0], got["w_down"])
    (dr1, dr1_b, stats1), (got,) = _bwd_ffn_in(dgu_b, w_top, w_bottom, dr2, r1, g1, b1, tile,
                                               [_SwapHalvesJob({"w_gate_up": g_gu})])
    sums_gu = halves_summed("w_gate_up", g_gu, got["w_gate_up"])
    g_out, _ = _wgrad(mix_b, dr1_b, D_MODEL, wtile, "wgrad_out")
    (dpool, dpre_b, g_pool_w, g_sgu_w, g_sgu_b, vecs), (got_gu, got_out) = _bwd_mix(
        dr1_b, xp_all, pre_b, small, w_out_b, tile, seq,
        [_SwapChipsJob({"w_gate_up": sums_gu[1]}), _SwapHalvesJob({"w_out": g_out})])
    early["w_gate_up"] = (sums_gu[0], got_gu["w_gate_up"])
    sums_out = halves_summed("w_out", g_out, got_out["w_out"])
    grad_x, dproj_b = _bwd_in(dpool, dpre_b, dr1, w_in_b, tile, seq)
    g_in, (got,) = _wgrad(x_b, dproj_b, IN_COLS, wtile, "wgrad_in", [_SwapChipsJob({"w_out": sums_out[1]})])
    early["w_out"] = (sums_out[0], got["w_out"])

    late = {
        "w_in": g_in,
        "pool_w": g_pool_w.reshape(SQUARE_ROWS, GROUP),
        "sgu_w": g_sgu_w.reshape(SQUARE_ROWS, GROUP),
        "vecs": _pack_vecs({"pool_scale": vecs[0], "sgu_ln_g": vecs[1], "sgu_ln_b": vecs[2], "sgu_b": g_sgu_b[:, :, 0],
                            "ln1_g": stats1[0], "ln1_b": stats1[1], "ln2_g": stats2[0], "ln2_b": stats2[1]},
                           extra=stats2[2]),
    }
    shared = _reduce_tail(late, early)

    grad, delta, new_m, new_v = {}, {}, {}, {}
    for name in BIG:
        grad[name] = shared[name][None]
        d, nm, nv = _adamw(name, given[name][0], shared[name], given["m_" + name][0], given["v_" + name][0])
        delta[name], new_m[name], new_v[name] = d[None], nm[None], nv[None]
    vec_grads, after = _unpack_vecs(shared["vecs"], shapes)
    grad.update(vec_grads)
    for name in ("pool_w", "sgu_w"):
        grad[name] = shared[name].reshape(shapes[name])
    small_names = ("pool_w", "sgu_w") + VEC_NAMES
    state = {pre: [given[pre + name] for name in small_names] for pre in ("", "m_", "v_")}
    ds, nms, nvs = _adamw_small(state[""], [grad[name] for name in small_names], state["m_"], state["v_"])
    delta.update(zip(small_names, ds))
    new_m.update(zip(small_names, nms))
    new_v.update(zip(small_names, nvs))

    sq_err = after[:LOSS_ROWS]
    loss = jnp.sum(sq_err) * (0.5 / D_MODEL)
    return (loss, grad_x.reshape(x.shape), *[grad[name] for name in WEIGHT_ORDER],
            *[delta[name] for name in WEIGHT_ORDER], *[new_m[name] for name in WEIGHT_ORDER],
            *[new_v[name] for name in WEIGHT_ORDER])
```

```python
import math

import jax
import jax.numpy as jnp
from jax import lax
from jax.experimental import pallas as pl
from jax.experimental.pallas import tpu as pltpu

f32 = jnp.float32
bf16 = jnp.bfloat16
MESH = pl.DeviceIdType.MESH

D_MODEL = 1024
POOL_WIDTH = 512
SGU_WIDTH = 512
IN_COLS = POOL_WIDTH + 2 * SGU_WIDTH
D_FF = 2816
POOL_WINDOWS = (2, 4, 8, 16)
GROUP = 128
N_GROUPS = 4
HALO = 16
LN_EPS = 1e-5
ALPHA = float(2.0 ** 0.25)
N_CHIPS = 4

ADAM_LR = 0.001
ADAM_B1 = 0.9
ADAM_B2 = 0.999
ADAM_EPS = 1e-08
ADAM_WD = 0.01
ADAM_STEP = 10

TOKEN_TILE = 512
FFN_BWD_TILE = 512
FF_CHUNK = 256
ROW_SUB = 256
WGRAD_TILE = 1024
TOP_ROWS = 256
V7X_VMEM_LIMIT = 56 * 1024 * 1024

SQUARE_ROWS = N_GROUPS * GROUP
VEC_ROWS = 64
LOSS_ROWS = D_MODEL // GROUP


def _params(**kw):
    return pltpu.CompilerParams(vmem_limit_bytes=V7X_VMEM_LIMIT, **kw)


def _mm(a, b):
    return jnp.dot(a, b, preferred_element_type=f32)


def _mm_nt(a, b):
    return lax.dot_general(a, b, (((1,), (1,)), ((), ())), preferred_element_type=f32)


def _mm_tn(a, b):
    return lax.dot_general(a, b, (((0,), (0,)), ((), ())), preferred_element_type=f32)


def _ln_fwd(r, g, b):
    mu = jnp.mean(r, axis=-1, keepdims=True)
    xc = r - mu
    var = jnp.mean(xc * xc, axis=-1, keepdims=True)
    rstd = lax.rsqrt(var + LN_EPS)
    xhat = xc * rstd
    return xhat * g + b, xhat, rstd


def _ln_bwd(dout, xhat, rstd, g):
    dxhat = dout * g
    m1 = jnp.mean(dxhat, axis=-1, keepdims=True)
    m2 = jnp.mean(dxhat * xhat, axis=-1, keepdims=True)
    return rstd * (dxhat - m1 - xhat * m2)


def _col_sum(a):
    return jnp.sum(a, axis=0, keepdims=True)


def _gelu_parts(z):
    cdf = 0.5 * (1.0 + lax.erf(z * (1.0 / math.sqrt(2.0))))
    pdf = jnp.exp(-0.5 * z * z) * (1.0 / math.sqrt(2.0 * math.pi))
    return cdf, pdf


def _inv_counts(seq_tile, rows):
    pos = seq_tile * rows + lax.broadcasted_iota(jnp.int32, (rows, GROUP), 0) + 1
    return [1.0 / jnp.minimum(pos, w).astype(f32) for w in POOL_WINDOWS]


def _window_sums(e, back):
    n = e.shape[0]

    def shifted(a, s):
        return pltpu.roll(a, s if back else n - s, 0)

    s2 = e + shifted(e, 1)
    s4 = s2[:, GROUP:] + shifted(s2[:, GROUP:], 2)
    s8 = s4[:, GROUP:] + shifted(s4[:, GROUP:], 4)
    s16 = s8[:, GROUP:] + shifted(s8[:, GROUP:], 8)
    return [s2[:, :GROUP], s4[:, :GROUP], s8[:, :GROUP], s16]


def _pooled_groups(xp, halo, inv):
    sums = _window_sums(jnp.concatenate([halo, xp], axis=0), back=True)
    return [sums[g][HALO:] * inv[g] - xp[:, g * GROUP:(g + 1) * GROUP] for g in range(N_GROUPS)]


def _tril_mask():
    r = lax.broadcasted_iota(jnp.int32, (GROUP, GROUP), 0)
    c = lax.broadcasted_iota(jnp.int32, (GROUP, GROUP), 1)
    return (r >= c).astype(f32)


def _gs(g):
    return slice(g * GROUP, (g + 1) * GROUP)


class _Cut:
    def __init__(self, rows, cols, by_cols):
        self.rows, self.cols, self.by_cols = rows, cols, by_cols
        if by_cols:
            self.block_shape = (rows // 2, cols // N_CHIPS)
            self.shard_shape = (rows, cols // N_CHIPS)
        else:
            self.block_shape = (rows // (2 * N_CHIPS), cols)
            self.shard_shape = (rows // N_CHIPS, cols)

    def block(self, ref, chip, half):
        br, bc = self.block_shape
        if self.by_cols:
            return ref.at[pl.ds(pl.multiple_of(half * br, 16), br), pl.ds(pl.multiple_of(chip * bc, 128), bc)]
        return ref.at[pl.ds(pl.multiple_of((2 * chip + half) * br, 8), br), :]

    def shard(self, ref, chip):
        sr, sc = self.shard_shape
        if self.by_cols:
            return ref.at[:, pl.ds(pl.multiple_of(chip * sc, 128), sc)]
        return ref.at[pl.ds(pl.multiple_of(chip * sr, 16), sr), :]

    def half_of_shard(self, ref, half):
        br = self.block_shape[0]
        return ref.at[pl.ds(pl.multiple_of(half * br, 8), br), :]

    def block_index(self, chip, half):
        return (half, chip) if self.by_cols else (2 * chip + half, 0)


CUTS = {
    "w_in": _Cut(D_MODEL, IN_COLS, True),
    "w_out": _Cut(D_MODEL, D_MODEL, False),
    "w_gate_up": _Cut(D_MODEL, 2 * D_FF, True),
    "w_down": _Cut(D_FF, D_MODEL, False),
    "w_gate_up_top": _Cut(TOP_ROWS, 2 * D_FF, True),
    "w_gate_up_bottom": _Cut(D_MODEL - TOP_ROWS, 2 * D_FF, True),
    "pool_w": _Cut(SQUARE_ROWS, GROUP, False),
    "sgu_w": _Cut(SQUARE_ROWS, GROUP, False),
    "vecs": _Cut(VEC_ROWS, GROUP, False),
}
BIG = ("w_in", "w_out", "w_gate_up", "w_down")
SMALL = ("pool_w", "sgu_w", "vecs")
ANY = pl.BlockSpec(memory_space=pl.ANY)


def _wire_dtype(name):
    return bf16 if name in BIG else f32


def _place():
    x, y, c = lax.axis_index("x"), lax.axis_index("y"), lax.axis_index("c")
    others = [(1 - x, y), (x, 1 - y), (1 - x, 1 - y)]
    return x, y, c, 2 * x + y, others, [2 * ox + oy for ox, oy in others]


def _remote(src, dst, send_sems, recv_sems, k, to):
    return pltpu.make_async_remote_copy(src_ref=src, dst_ref=dst, send_sem=send_sems.at[k], recv_sem=recv_sems.at[k],
                                        device_id=to, device_id_type=MESH)


class _GatherJob:
    def __init__(self, shards):
        self.names = tuple(shards)
        self.arrays = tuple(shards.values())
        n = len(self.names)
        self.out_shapes = [jax.ShapeDtypeStruct((CUTS[name].rows, CUTS[name].cols), bf16) for name in self.names]
        self.scratch_shapes = ([pltpu.SemaphoreType.DMA((6 * n,)), pltpu.SemaphoreType.DMA((6 * n,)),
                                pltpu.SemaphoreType.DMA((2 * n,))]
                               + [pltpu.VMEM(CUTS[name].shard_shape, bf16) for name in self.names])

    def bind(self, shard_refs, full_refs, scratch):
        self.shards, self.full = shard_refs, full_refs
        self.send_sems, self.recv_sems, self.local_sems = scratch[:3]
        self.stages = scratch[3:]
        return self

    def _sends(self):
        _, _, c, me, others, _ = _place()
        return [_remote(CUTS[name].half_of_shard(self.shards[w], c), CUTS[name].block(self.full[w], me, c),
                        self.send_sems, self.recv_sems, 3 * w + k, (*chip, c))
                for w, name in enumerate(self.names) for k, chip in enumerate(others)]

    def _relays(self, half):
        x, y, c, _, _, other_ids = _place()
        n = len(self.names)
        return [_remote(CUTS[name].block(self.full[w], other_ids[k], half),
                        CUTS[name].block(self.full[w], other_ids[k], half),
                        self.send_sems, self.recv_sems, 3 * n + 3 * w + k, (x, y, 1 - c))
                for w, name in enumerate(self.names) for k in range(3)]

    def _stores(self):
        me = _place()[3]
        return [pltpu.make_async_copy(self.stages[w], CUTS[name].shard(self.full[w], me), self.local_sems.at[2 * w + 1])
                for w, name in enumerate(self.names)]

    def start(self):
        loads = [pltpu.make_async_copy(self.shards[w], self.stages[w], self.local_sems.at[2 * w])
                 for w in range(len(self.names))]
        for cp in loads:
            cp.start()
        for cp in self._sends():
            cp.start()
        for load, store in zip(loads, self._stores()):
            load.wait()
            store.start()

    def relay(self):
        _, _, c, _, others, other_ids = _place()
        relays = self._relays(c)
        for w, name in enumerate(self.names):
            for k, chip in enumerate(others):
                landed = CUTS[name].block(self.full[w], other_ids[k], c)
                _remote(landed, landed, self.send_sems, self.recv_sems, 3 * w + k, (*chip, c)).wait_recv()
                relays[3 * w + k].start()

    def finish(self):
        c = _place()[2]
        for cp in self._relays(1 - c):
            cp.wait_recv()
        for cp in self._sends() + self._relays(c):
            cp.wait_send()
        for cp in self._stores():
            cp.wait()


class _SwapHalvesJob:
    def __init__(self, grads):
        self.names = tuple(grads)
        self.arrays = tuple(grads.values())
        n = len(self.names)
        self.out_shapes = [jax.ShapeDtypeStruct((N_CHIPS, *CUTS[name].block_shape), f32) for name in self.names]
        self.scratch_shapes = [pltpu.SemaphoreType.DMA((N_CHIPS * n,)), pltpu.SemaphoreType.DMA((N_CHIPS * n,))]

    def bind(self, g_refs, got_refs, scratch):
        self.g_refs, self.got_refs = g_refs, got_refs
        self.send_sems, self.recv_sems = scratch
        return self

    def _copies(self):
        x, y, c, _, _, _ = _place()
        return [_remote(CUTS[name].block(self.g_refs[a], j, 1 - c), self.got_refs[a].at[j], self.send_sems,
                        self.recv_sems, N_CHIPS * a + j, (x, y, 1 - c))
                for a, name in enumerate(self.names) for j in range(N_CHIPS)]

    def start(self):
        for cp in self._copies():
            cp.start()

    def finish(self):
        for cp in self._copies():
            cp.wait()


class _SwapChipsJob:
    def __init__(self, partials):
        self.names = tuple(partials)
        self.arrays = tuple(partials.values())
        n = len(self.names)
        self.out_shapes = [jax.ShapeDtypeStruct((3, *CUTS[name].block_shape), _wire_dtype(name)) for name in self.names]
        self.scratch_shapes = [pltpu.SemaphoreType.DMA((3 * n,)), pltpu.SemaphoreType.DMA((3 * n,))]

    def bind(self, p_refs, got_refs, scratch):
        self.p_refs, self.got_refs = p_refs, got_refs
        self.send_sems, self.recv_sems = scratch
        return self

    def _copies(self):
        _, _, c, _, others, other_ids = _place()
        return [_remote(self.p_refs[a].at[other_ids[k]], self.got_refs[a].at[k], self.send_sems, self.recv_sems,
                        3 * a + k, (*chip, c))
                for a in range(len(self.names)) for k, chip in enumerate(others)]

    def start(self):
        for cp in self._copies():
            cp.start()

    def finish(self):
        for cp in self._copies():
            cp.wait()


class _ShareJob:
    def __init__(self, reduced):
        self.names = tuple(reduced)
        self.arrays = tuple(reduced.values())
        self.big = [a for a, name in enumerate(self.names) if name in BIG]
        self.small = [a for a, name in enumerate(self.names) if name in SMALL]
        self.out_shapes = [jax.ShapeDtypeStruct(CUTS[name].shard_shape if name in BIG
                                                else (CUTS[name].rows, CUTS[name].cols), f32) for name in self.names]
        n_sems = len(self.big) + 7 * len(self.small)
        self.scratch_shapes = ([pltpu.SemaphoreType.DMA((n_sems,)), pltpu.SemaphoreType.DMA((n_sems,)),
                                pltpu.SemaphoreType.DMA((2 * len(self.names),))]
                               + [pltpu.VMEM(CUTS[name].block_shape, f32) for name in self.names])

    def bind(self, f_refs, out_refs, scratch):
        self.f_refs, self.out_refs = f_refs, out_refs
        self.send_sems, self.recv_sems, self.local_sems = scratch[:3]
        self.stages = scratch[3:]
        return self

    def _sem(self, a, which=0):
        if a in self.big:
            return self.big.index(a)
        return len(self.big) + 7 * self.small.index(a) + which

    def _mine(self, a, half):
        me = _place()[3]
        cut = CUTS[self.names[a]]
        return cut.half_of_shard(self.out_refs[a], half) if a in self.big else cut.block(self.out_refs[a], me, half)

    def _to_sibling(self):
        x, y, c, _, _, _ = _place()
        return [_remote(self.f_refs[a], self._mine(a, c), self.send_sems, self.recv_sems, self._sem(a), (x, y, 1 - c))
                for a in range(len(self.names))]

    def _to_chips(self):
        _, _, c, _, others, _ = _place()
        return [_remote(self.f_refs[a], self._mine(a, c), self.send_sems, self.recv_sems, self._sem(a, 1 + k), (*chip, c))
                for a in self.small for k, chip in enumerate(others)]

    def _passes(self, half):
        x, y, c, _, _, other_ids = _place()
        out = []
        for a in self.small:
            for k in range(3):
                blk = CUTS[self.names[a]].block(self.out_refs[a], other_ids[k], half)
                out.append(_remote(blk, blk, self.send_sems, self.recv_sems, self._sem(a, 4 + k), (x, y, 1 - c)))
        return out

    def _stores(self):
        c = _place()[2]
        return [pltpu.make_async_copy(self.stages[a], self._mine(a, c), self.local_sems.at[2 * a + 1])
                for a in range(len(self.names))]

    def start(self):
        loads = [pltpu.make_async_copy(self.f_refs[a], self.stages[a], self.local_sems.at[2 * a])
                 for a in range(len(self.names))]
        for cp in loads:
            cp.start()
        for cp in self._to_sibling() + self._to_chips():
            cp.start()
        for load, store in zip(loads, self._stores()):
            load.wait()
            store.start()

    def finish(self):
        x, y, c, _, others, other_ids = _place()
        passes = self._passes(c)
        for s, a in enumerate(self.small):
            for k, chip in enumerate(others):
                landed = CUTS[self.names[a]].block(self.out_refs[a], other_ids[k], c)
                _remote(landed, landed, self.send_sems, self.recv_sems, self._sem(a, 1 + k), (*chip, c)).wait_recv()
                passes[3 * s + k].start()
        for a in range(len(self.names)):
            theirs = self._mine(a, 1 - c)
            _remote(theirs, theirs, self.send_sems, self.recv_sems, self._sem(a), (x, y, 1 - c)).wait_recv()
        for cp in self._passes(1 - c):
            cp.wait_recv()
        for cp in self._to_sibling() + self._to_chips() + passes:
            cp.wait_send()
        for cp in self._stores():
            cp.wait()


def _call(body, *, name, grid, in_specs, out_specs, out_shape, args, scratch_shapes=(), jobs=()):
    n_in, n_out, n_scr = len(in_specs), len(out_specs), len(scratch_shapes)
    j_in = [len(j.arrays) for j in jobs]
    j_out = [len(j.out_shapes) for j in jobs]
    j_scr = [len(j.scratch_shapes) for j in jobs]

    def wrapped(*refs):
        refs = list(refs)

        def take(k):
            head = refs[:k]
            del refs[:k]
            return head

        ins, jins = take(n_in), [take(k) for k in j_in]
        outs, jouts = take(n_out), [take(k) for k in j_out]
        scr, jscr = take(n_scr), [take(k) for k in j_scr]
        bound = [j.bind(a, b, c) for j, a, b, c in zip(jobs, jins, jouts, jscr)]
        relaying = [b for b in bound if hasattr(b, "relay")]
        if not grid:
            for b in bound:
                b.start()
            for b in relaying:
                b.relay()
            body(*ins, *outs, *scr)
            for b in bound:
                b.finish()
            return
        if not bound:
            body(*ins, *outs, *scr)
            return
        first = _all([pl.program_id(d) == 0 for d in range(len(grid))])
        last = _all([pl.program_id(d) == grid[d] - 1 for d in range(len(grid))])

        @pl.when(first)
        def _():
            for b in bound:
                b.start()

        if relaying:
            @pl.when(_all([pl.program_id(d) == (3 * grid[d]) // 4 for d in range(len(grid))]))
            def _():
                for b in relaying:
                    b.relay()

        body(*ins, *outs, *scr)

        @pl.when(last)
        def _():
            for b in bound:
                b.finish()

    kw = dict(grid=grid) if grid else {}
    results = pl.pallas_call(
        wrapped, name=name,
        in_specs=list(in_specs) + [ANY] * sum(j_in), out_specs=list(out_specs) + [ANY] * sum(j_out),
        out_shape=list(out_shape) + [s for j in jobs for s in j.out_shapes],
        scratch_shapes=list(scratch_shapes) + [s for j in jobs for s in j.scratch_shapes],
        compiler_params=_params(), **kw,
    )(*args, *[a for j in jobs for a in j.arrays])
    results = list(results)
    own, rest = results[:n_out], results[n_out:]
    per_job = []
    for j, k in zip(jobs, j_out):
        per_job.append(dict(zip(j.names, rest[:k])))
        rest = rest[k:]
    return own, per_job


def _all(conds):
    out = conds[0]
    for c in conds[1:]:
        out = jnp.logical_and(out, c)
    return out


def _alone(job, name):
    return _call(lambda: None, name=name, grid=None, in_specs=[], out_specs=[], out_shape=[], args=[], jobs=[job])[1][0]


def _cast_x(x2, tile, jobs):
    tokens = x2.shape[0]

    def body(x_ref, xb_ref):
        xb_ref[...] = x_ref[...].astype(bf16)

    row = lambda i: (i, 0)
    return _call(
        body, name="cast_x", grid=(tokens // tile,),
        in_specs=[pl.BlockSpec((tile, D_MODEL), row)], out_specs=[pl.BlockSpec((tile, D_MODEL), row)],
        out_shape=[jax.ShapeDtypeStruct((tokens, D_MODEL), bf16)], args=(x2,), jobs=jobs)


def _fwd_proj(x_b, w_in_b, tile, jobs):
    tokens = x_b.shape[0]

    def body(x_ref, w_ref, xp_ref, pre_ref):
        proj = _mm(x_ref[...], w_ref[...])
        xp_ref[...] = proj[:, :POOL_WIDTH]
        pre_ref[...] = proj[:, POOL_WIDTH:].astype(bf16)

    row = lambda i: (i, 0)
    return _call(
        body, name="fwd_proj", grid=(tokens // tile,),
        in_specs=[pl.BlockSpec((tile, D_MODEL), row), pl.BlockSpec((D_MODEL, IN_COLS), lambda i: (0, 0))],
        out_specs=[pl.BlockSpec((tile, POOL_WIDTH), row), pl.BlockSpec((tile, 2 * SGU_WIDTH), row)],
        out_shape=[jax.ShapeDtypeStruct((tokens, POOL_WIDTH), f32), jax.ShapeDtypeStruct((tokens, 2 * SGU_WIDTH), bf16)],
        args=(x_b, w_in_b), jobs=jobs)


def _small_specs():
    return [pl.BlockSpec((N_GROUPS, GROUP, GROUP), lambda i: (0, 0, 0)),
            pl.BlockSpec((1, POOL_WIDTH), lambda i: (0, 0)),
            pl.BlockSpec((1, SGU_WIDTH), lambda i: (0, 0)),
            pl.BlockSpec((1, SGU_WIDTH), lambda i: (0, 0)),
            pl.BlockSpec((N_GROUPS, GROUP, GROUP), lambda i: (0, 0, 0)),
            pl.BlockSpec((N_GROUPS, GROUP, GROUP), lambda i: (0, 0, 0))]


def _fwd_mix(xp_all, pre_b, x2, small, w_out_b, ln1_g, ln1_b, tile, seq, jobs):
    tokens = x2.shape[0]
    tps = seq // tile
    hb = tile // HALO

    def body(xp_ref, halo_ref, pre_ref, x_ref, pw_ref, ps_ref, lg_ref, lb_ref, sw_ref, sb_ref, wout_ref, g1_ref, b1_ref,
             mix_ref, r1_ref, h_ref):
        seq_tile = pl.program_id(0) % tps
        xp = xp_ref[...]
        halo = jnp.where(seq_tile == 0, 0.0, halo_ref[...])
        pooled = _pooled_groups(xp, halo, _inv_counts(seq_tile, tile))
        for g in range(N_GROUPS):
            po = _mm(pooled[g].astype(bf16), pw_ref[g].astype(bf16)) * ps_ref[:, _gs(g)]
            mix_ref[:, _gs(g)] = po.astype(bf16)

        pre = pre_ref[...].astype(f32)
        cdf, _ = _gelu_parts(pre)
        zg = pre * cdf
        u = zg[:, :SGU_WIDTH]
        vln, _, _ = _ln_fwd(zg[:, SGU_WIDTH:], lg_ref[...], lb_ref[...])
        vb = vln.astype(bf16)
        mask = _tril_mask()
        for h in range(N_GROUPS):
            wm = (sw_ref[h] * mask).astype(bf16)
            bias = sb_ref[h]
            for c in range(tile // GROUP):
                rows = slice(c * GROUP, (c + 1) * GROUP)
                mixed = _mm(wm, vb[rows, _gs(h)]) + bias
                mix_ref[rows, POOL_WIDTH + h * GROUP:POOL_WIDTH + (h + 1) * GROUP] = (u[rows, _gs(h)] * mixed).astype(bf16)

        r1 = ALPHA * x_ref[...] + _mm(mix_ref[...], wout_ref[...])
        r1_ref[...] = r1
        h1, _, _ = _ln_fwd(r1, g1_ref[...], b1_ref[...])
        h_ref[...] = h1.astype(bf16)

    row = lambda i: (i, 0)
    vec = pl.BlockSpec((1, D_MODEL), lambda i: (0, 0))
    return _call(
        body, name="fwd_mix", grid=(tokens // tile,),
        in_specs=[pl.BlockSpec((tile, POOL_WIDTH), row),
                  pl.BlockSpec((HALO, POOL_WIDTH), lambda i: (jnp.maximum(i * hb - 1, 0), 0)),
                  pl.BlockSpec((tile, 2 * SGU_WIDTH), row),
                  pl.BlockSpec((tile, D_MODEL), row)] + _small_specs()
                 + [pl.BlockSpec((D_MODEL, D_MODEL), lambda i: (0, 0)), vec, vec],
        out_specs=[pl.BlockSpec((tile, D_MODEL), row)] * 3,
        out_shape=[jax.ShapeDtypeStruct((tokens, D_MODEL), bf16),
                   jax.ShapeDtypeStruct((tokens, D_MODEL), f32),
                   jax.ShapeDtypeStruct((tokens, D_MODEL), bf16)],
        args=(xp_all, xp_all, pre_b, x2, *small, w_out_b, ln1_g, ln1_b), jobs=jobs)


def _fwd_gate_up(h_b, w_top, w_bottom, tile, jobs):
    tokens = h_b.shape[0]
    def body(h_ref, wt_ref, wb_ref, gu_ref, a_ref):
        ht, hb = h_ref[:, :TOP_ROWS], h_ref[:, TOP_ROWS:]
        for c in range(D_FF // FF_CHUNK):
            gcols = slice(c * FF_CHUNK, (c + 1) * FF_CHUNK)
            ucols = slice(D_FF + c * FF_CHUNK, D_FF + (c + 1) * FF_CHUNK)
            gate = _mm(ht, wt_ref[:, gcols]) + _mm(hb, wb_ref[:, gcols])
            up = _mm(ht, wt_ref[:, ucols]) + _mm(hb, wb_ref[:, ucols])
            sg = jax.nn.sigmoid(gate)
            silu = gate * sg
            gu_ref[:, gcols] = (up * (sg + silu * (1.0 - sg))).astype(bf16)
            gu_ref[:, ucols] = silu.astype(bf16)
            a_ref[:, gcols] = (silu * up).astype(bf16)

    return _call(
        body, name="fwd_gate_up", grid=(tokens // tile,),
        in_specs=[pl.BlockSpec((tile, D_MODEL), lambda i: (i, 0)),
                  pl.BlockSpec((TOP_ROWS, 2 * D_FF), lambda i: (0, 0), pipeline_mode=pl.Buffered(1)),
                  pl.BlockSpec((D_MODEL - TOP_ROWS, 2 * D_FF), lambda i: (0, 0), pipeline_mode=pl.Buffered(1))],
        out_specs=[pl.BlockSpec((tile, 2 * D_FF), lambda i: (i, 0)),
                   pl.BlockSpec((tile, D_FF), lambda i: (i, 0))],
        out_shape=[jax.ShapeDtypeStruct((tokens, 2 * D_FF), bf16),
                   jax.ShapeDtypeStruct((tokens, D_FF), bf16)],
        args=(h_b, w_top, w_bottom), jobs=jobs)


def _fwd_down_loss(a_b, w_dn_b, r1, target, ln1_g, ln1_b, ln2_g, ln2_b, tile):
    tokens = a_b.shape[0]

    def body(a_ref, w_ref, r1_ref, t_ref, g1_ref, b1_ref, g2_ref, b2_ref, dr2_ref, dr2b_ref, st_ref):
        @pl.when(pl.program_id(0) == 0)
        def _():
            st_ref[...] = jnp.zeros_like(st_ref)

        sub = min(ROW_SUB, tile)
        for s in range(tile // sub):
            rows = slice(s * sub, (s + 1) * sub)
            h1, _, _ = _ln_fwd(r1_ref[rows, :], g1_ref[...], b1_ref[...])
            r2 = ALPHA * h1 + _mm(a_ref[rows, :], w_ref[...])
            y, xhat, rstd = _ln_fwd(r2, g2_ref[...], b2_ref[...])
            diff = y - t_ref[rows, :]
            dy = diff * (1.0 / D_MODEL)
            st_ref[0:1, :] += _col_sum(dy * xhat)
            st_ref[1:2, :] += _col_sum(dy)
            st_ref[2:3, :] += _col_sum(diff * diff)
            dr2 = _ln_bwd(dy, xhat, rstd, g2_ref[...])
            dr2_ref[rows, :] = dr2
            dr2b_ref[rows, :] = dr2.astype(bf16)

    row = lambda i: (i, 0)
    vec = pl.BlockSpec((1, D_MODEL), lambda i: (0, 0))
    return _call(
        body, name="fwd_down_loss", grid=(tokens // tile,),
        in_specs=[pl.BlockSpec((tile, D_FF), row), pl.BlockSpec((D_FF, D_MODEL), lambda i: (0, 0)),
                  pl.BlockSpec((tile, D_MODEL), row), pl.BlockSpec((tile, D_MODEL), row), vec, vec, vec, vec],
        out_specs=[pl.BlockSpec((tile, D_MODEL), row), pl.BlockSpec((tile, D_MODEL), row),
                   pl.BlockSpec((8, D_MODEL), lambda i: (0, 0))],
        out_shape=[jax.ShapeDtypeStruct((tokens, D_MODEL), f32), jax.ShapeDtypeStruct((tokens, D_MODEL), bf16),
                   jax.ShapeDtypeStruct((8, D_MODEL), f32)],
        args=(a_b, w_dn_b, r1, target, ln1_g, ln1_b, ln2_g, ln2_b))[0]


def _bwd_gate_up(dr2, gu_b, w_dn_b, tile, jobs):
    tokens = dr2.shape[0]

    def body(d_ref, gu_ref, w_ref, dgu_ref):
        d = d_ref[...].astype(bf16)
        for c in range(D_FF // FF_CHUNK):
            gcols = slice(c * FF_CHUNK, (c + 1) * FF_CHUNK)
            ucols = slice(D_FF + c * FF_CHUNK, D_FF + (c + 1) * FF_CHUNK)
            da = _mm_nt(d, w_ref[gcols, :])
            dgu_ref[:, gcols] = (da * gu_ref[:, gcols].astype(f32)).astype(bf16)
            dgu_ref[:, ucols] = (da * gu_ref[:, ucols].astype(f32)).astype(bf16)

    return _call(
        body, name="bwd_gate_up", grid=(tokens // tile,),
        in_specs=[pl.BlockSpec((tile, D_MODEL), lambda i: (i, 0)),
                  pl.BlockSpec((tile, 2 * D_FF), lambda i: (i, 0)),
                  pl.BlockSpec((D_FF, D_MODEL), lambda i: (0, 0))],
        out_specs=[pl.BlockSpec((tile, 2 * D_FF), lambda i: (i, 0))],
        out_shape=[jax.ShapeDtypeStruct((tokens, 2 * D_FF), bf16)],
        args=(dr2, gu_b, w_dn_b), jobs=jobs)


def _bwd_ffn_in(dgu_b, w_top, w_bottom, dr2, r1, ln1_g, ln1_b, tile, jobs):
    tokens = dr2.shape[0]

    def body(dgu_ref, wt_ref, wb_ref, d_ref, r1_ref, g1_ref, b1_ref, dr1_ref, dr1b_ref, st_ref):
        @pl.when(pl.program_id(0) == 0)
        def _():
            st_ref[...] = jnp.zeros_like(st_ref)

        dgu = dgu_ref[...]
        dh = ALPHA * d_ref[...] + jnp.concatenate([_mm_nt(dgu, wt_ref[...]), _mm_nt(dgu, wb_ref[...])], axis=1)
        _, xhat, rstd = _ln_fwd(r1_ref[...], g1_ref[...], b1_ref[...])
        st_ref[0:1, :] += _col_sum(dh * xhat)
        st_ref[1:2, :] += _col_sum(dh)
        dr1 = _ln_bwd(dh, xhat, rstd, g1_ref[...])
        dr1_ref[...] = dr1
        dr1b_ref[...] = dr1.astype(bf16)

    row = lambda i: (i, 0)
    vec = pl.BlockSpec((1, D_MODEL), lambda i: (0, 0))
    return _call(
        body, name="bwd_ffn_in", grid=(tokens // tile,),
        in_specs=[pl.BlockSpec((tile, 2 * D_FF), row),
                  pl.BlockSpec((TOP_ROWS, 2 * D_FF), lambda i: (0, 0), pipeline_mode=pl.Buffered(1)),
                  pl.BlockSpec((D_MODEL - TOP_ROWS, 2 * D_FF), lambda i: (0, 0), pipeline_mode=pl.Buffered(1)),
                  pl.BlockSpec((tile, D_MODEL), row), pl.BlockSpec((tile, D_MODEL), row), vec, vec],
        out_specs=[pl.BlockSpec((tile, D_MODEL), row), pl.BlockSpec((tile, D_MODEL), row),
                   pl.BlockSpec((8, D_MODEL), lambda i: (0, 0))],
        out_shape=[jax.ShapeDtypeStruct((tokens, D_MODEL), f32), jax.ShapeDtypeStruct((tokens, D_MODEL), bf16),
                   jax.ShapeDtypeStruct((8, D_MODEL), f32)],
        args=(dgu_b, w_top, w_bottom, dr2, r1, ln1_g, ln1_b), jobs=jobs)


def _bwd_mix(dr1, xp_all, pre_b, small, w_out_b, tile, seq, jobs):
    tokens = dr1.shape[0]
    tps = seq // tile
    hb = tile // HALO
    steps = tokens // tile

    def body(dr1_ref, xp_ref, halo_ref, pre_ref, wout_ref, pw_ref, ps_ref, lg_ref, lb_ref, sw_ref, sb_ref,
             dpool_ref, dpre_ref, gpw_ref, gsw_ref, gsb_ref, vec_ref, du_ref, dvln_ref):
        step = pl.program_id(0)
        seq_tile = step % tps

        @pl.when(step == 0)
        def _():
            gpw_ref[...] = jnp.zeros_like(gpw_ref)
            gsw_ref[...] = jnp.zeros_like(gsw_ref)
            gsb_ref[...] = jnp.zeros_like(gsb_ref)
            vec_ref[...] = jnp.zeros_like(vec_ref)

        dmix = _mm_nt(dr1_ref[...].astype(bf16), wout_ref[...])

        xp = xp_ref[...]
        halo = jnp.where(seq_tile == 0, 0.0, halo_ref[...])
        pooled = _pooled_groups(xp, halo, _inv_counts(seq_tile, tile))
        for g in range(N_GROUPS):
            pb = pooled[g].astype(bf16)
            pwb = pw_ref[g].astype(bf16)
            dpo = dmix[:, _gs(g)]
            vec_ref[0:1, _gs(g)] += _col_sum(dpo * _mm(pb, pwb))
            dpo_b = (dpo * ps_ref[:, _gs(g)]).astype(bf16)
            gpw_ref[g] += _mm_tn(pb, dpo_b)
            dpool_ref[:, _gs(g)] = _mm_nt(dpo_b, pwb)

        pre = pre_ref[...].astype(f32)
        cdf, pdf = _gelu_parts(pre)
        zg = pre * cdf
        u = zg[:, :SGU_WIDTH]
        vln, vhat, rstd = _ln_fwd(zg[:, SGU_WIDTH:], lg_ref[...], lb_ref[...])
        vb = vln.astype(bf16)
        mask = _tril_mask()
        for h in range(N_GROUPS):
            wm = (sw_ref[h] * mask).astype(bf16)
            bias = sb_ref[h]
            gsw = jnp.zeros((GROUP, GROUP), f32)
            gsb = jnp.zeros((GROUP, GROUP), f32)
            for c in range(tile // GROUP):
                rows = slice(c * GROUP, (c + 1) * GROUP)
                v_ch = vb[rows, _gs(h)]
                d = dmix[rows, POOL_WIDTH + h * GROUP:POOL_WIDTH + (h + 1) * GROUP]
                du_ref[rows, _gs(h)] = d * (_mm(wm, v_ch) + bias)
                dmixed = d * u[rows, _gs(h)]
                gsb += dmixed
                dmixed_b = dmixed.astype(bf16)
                gsw += _mm_nt(dmixed_b, v_ch)
                dvln_ref[rows, _gs(h)] = _mm_tn(wm, dmixed_b)
            gsw_ref[h] += gsw * mask
            gsb_ref[h] += gsb

        dvln = dvln_ref[...]
        vec_ref[1:2, :] += _col_sum(dvln * vhat)
        vec_ref[2:3, :] += _col_sum(dvln)
        dgelu = cdf + pre * pdf
        dpre_ref[:, :SGU_WIDTH] = (du_ref[...] * dgelu[:, :SGU_WIDTH]).astype(bf16)
        dpre_ref[:, SGU_WIDTH:] = (_ln_bwd(dvln, vhat, rstd, lg_ref[...]) * dgelu[:, SGU_WIDTH:]).astype(bf16)

        @pl.when(step == steps - 1)
        def _():
            for h in range(N_GROUPS):
                gsb_ref[h] = jnp.broadcast_to(jnp.sum(gsb_ref[h], axis=1, keepdims=True), (GROUP, GROUP))

    row = lambda i: (i, 0)
    sq = jax.ShapeDtypeStruct((N_GROUPS, GROUP, GROUP), f32)
    sq_spec = pl.BlockSpec((N_GROUPS, GROUP, GROUP), lambda i: (0, 0, 0))
    return _call(
        body, name="bwd_mix", grid=(steps,),
        in_specs=[pl.BlockSpec((tile, D_MODEL), row), pl.BlockSpec((tile, POOL_WIDTH), row),
                  pl.BlockSpec((HALO, POOL_WIDTH), lambda i: (jnp.maximum(i * hb - 1, 0), 0)),
                  pl.BlockSpec((tile, 2 * SGU_WIDTH), row),
                  pl.BlockSpec((D_MODEL, D_MODEL), lambda i: (0, 0))] + _small_specs(),
        out_specs=[pl.BlockSpec((tile, POOL_WIDTH), row), pl.BlockSpec((tile, 2 * SGU_WIDTH), row),
                   sq_spec, sq_spec, sq_spec, pl.BlockSpec((8, POOL_WIDTH), lambda i: (0, 0))],
        out_shape=[jax.ShapeDtypeStruct((tokens, POOL_WIDTH), f32), jax.ShapeDtypeStruct((tokens, 2 * SGU_WIDTH), bf16),
                   sq, sq, sq, jax.ShapeDtypeStruct((8, POOL_WIDTH), f32)],
        scratch_shapes=[pltpu.VMEM((tile, SGU_WIDTH), f32), pltpu.VMEM((tile, SGU_WIDTH), f32)],
        args=(dr1, xp_all, xp_all, pre_b, w_out_b, *small), jobs=jobs)


def _bwd_in(dpool, dpre_b, dr1, w_in_b, tile, seq):
    tokens = dr1.shape[0]
    tps = seq // tile
    hb = tile // HALO
    last_halo = tokens // HALO - 1

    def body(dpool_ref, nxt_ref, dpre_ref, dr1_ref, w_ref, dx_ref, dproj_ref):
        seq_tile = pl.program_id(0) % tps
        inv = _inv_counts(seq_tile, tile)
        dpl = dpool_ref[...]
        nxt = jnp.where(seq_tile == tps - 1, 0.0, nxt_ref[...])
        scaled = jnp.concatenate([dpl[:, _gs(g)] * inv[g] for g in range(N_GROUPS)], axis=1)
        scaled_nxt = jnp.concatenate([nxt[:, _gs(g)] * (1.0 / POOL_WINDOWS[g]) for g in range(N_GROUPS)], axis=1)
        sums = _window_sums(jnp.concatenate([scaled, scaled_nxt], axis=0), back=False)
        for g in range(N_GROUPS):
            dproj_ref[:, _gs(g)] = (sums[g][:tile] - dpl[:, _gs(g)]).astype(bf16)
        dproj_ref[:, POOL_WIDTH:] = dpre_ref[...]
        dx_ref[...] = ALPHA * dr1_ref[...] + _mm_nt(dproj_ref[...], w_ref[...])

    row = lambda i: (i, 0)
    return _call(
        body, name="bwd_in", grid=(tokens // tile,),
        in_specs=[pl.BlockSpec((tile, POOL_WIDTH), row),
                  pl.BlockSpec((HALO, POOL_WIDTH), lambda i: (jnp.minimum((i + 1) * hb, last_halo), 0)),
                  pl.BlockSpec((tile, 2 * SGU_WIDTH), row),
                  pl.BlockSpec((tile, D_MODEL), row),
                  pl.BlockSpec((D_MODEL, IN_COLS), lambda i: (0, 0))],
        out_specs=[pl.BlockSpec((tile, D_MODEL), row), pl.BlockSpec((tile, IN_COLS), row)],
        out_shape=[jax.ShapeDtypeStruct((tokens, D_MODEL), f32), jax.ShapeDtypeStruct((tokens, IN_COLS), bf16)],
        args=(dpool, dpool, dpre_b, dr1, w_in_b))[0]


def _wgrad(a, b, col_tile, tile, name, jobs=()):
    tokens, m = a.shape
    n = b.shape[1]

    def body(a_ref, b_ref, o_ref):
        @pl.when(pl.program_id(1) == 0)
        def _():
            o_ref[...] = jnp.zeros_like(o_ref)

        o_ref[...] += _mm_tn(a_ref[...].astype(bf16), b_ref[...].astype(bf16))

    (out,), got = _call(
        body, name=name, grid=(n // col_tile, tokens // tile),
        in_specs=[pl.BlockSpec((tile, m), lambda j, k: (k, 0)),
                  pl.BlockSpec((tile, col_tile), lambda j, k: (k, j))],
        out_specs=[pl.BlockSpec((m, col_tile), lambda j, k: (0, j))],
        out_shape=[jax.ShapeDtypeStruct((m, n), f32)],
        args=(a, b), jobs=jobs)
    return out, got


def _add_halves(name, g, got, place):
    cut = CUTS[name]
    br, bc = cut.block_shape
    wire = _wire_dtype(name)

    def body(place_ref, g_ref, got_ref, o_ref, wire_ref):
        s = g_ref[...] + got_ref[...]
        wire_ref[...] = s.astype(wire)

        @pl.when(pl.program_id(0) == place_ref[1])
        def _():
            o_ref[...] = s

    blocks = pl.BlockSpec((None, br, bc), lambda j, place_ref: (j, 0, 0))
    return pl.pallas_call(
        body, name="reduce_add_halves_" + name,
        grid_spec=pltpu.PrefetchScalarGridSpec(
            num_scalar_prefetch=1, grid=(N_CHIPS,),
            in_specs=[pl.BlockSpec((br, bc), lambda j, place_ref: cut.block_index(j, place_ref[0])), blocks],
            out_specs=[pl.BlockSpec((br, bc), lambda j, place_ref: (0, 0)), blocks]),
        out_shape=[jax.ShapeDtypeStruct((br, bc), f32), jax.ShapeDtypeStruct((N_CHIPS, br, bc), wire)],
        compiler_params=_params(),
    )(place, g, got)


def _reduce_tail(late, early):
    late_names, early_names = tuple(late), tuple(early)
    names = early_names + late_names
    nl, ne, n = len(late_names), len(early_names), len(names)
    cuts = [CUTS[name] for name in names]
    is_big = [name in BIG for name in names]
    share_base, share_sem = 7 * nl, []
    for i in range(n):
        share_sem.append(share_base)
        share_base += 1 if is_big[i] else 7
    n_in = nl + 2 * ne

    def body(*refs):
        g_refs = refs[:nl]
        sums_refs, got_refs = refs[nl:n_in:2], refs[nl + 1:n_in:2]
        out_refs = refs[n_in:n_in + n]
        send_sems, recv_sems, local_sems = refs[n_in + n:n_in + n + 3]
        vm = refs[n_in + n + 3:]
        own, recv_a, wire, recv_b = vm[0:4 * nl:4], vm[1:4 * nl:4], vm[2:4 * nl:4], vm[3:4 * nl:4]
        acc, gotv = vm[4 * nl::2], vm[4 * nl + 1::2]
        x, y, c, me, others, other_ids = _place()
        sibling = (x, y, 1 - c)

        def mine(i, half):
            return cuts[i].half_of_shard(out_refs[i], half) if is_big[i] else cuts[i].block(out_refs[i], me, half)

        def reduced(i):
            return acc[i] if i < ne else own[i - ne].at[me]

        sent, stores = [], []

        def share(i):
            store = pltpu.make_async_copy(reduced(i), mine(i, c), local_sems.at[4 * nl + 2 * ne + i])
            store.start()
            stores.append(store)
            to = [sibling] if is_big[i] else [sibling] + [(*chip, c) for chip in others]
            for which, device in enumerate(to):
                cp = _remote(reduced(i), mine(i, c), send_sems, recv_sems, share_sem[i] + which, device)
                cp.start()
                sent.append(cp)

        early_loads = []
        for e in range(ne):
            early_loads.append(pltpu.make_async_copy(sums_refs[e], acc[e], local_sems.at[4 * nl + 2 * e]))
            early_loads.append(pltpu.make_async_copy(got_refs[e], gotv[e], local_sems.at[4 * nl + 2 * e + 1]))
        late_loads = [pltpu.make_async_copy(cuts[ne + l].block(g_refs[l], j, c), own[l].at[j], local_sems.at[4 * l + j])
                      for l in range(nl) for j in range(N_CHIPS)]
        halves = [_remote(cuts[ne + l].block(g_refs[l], j, 1 - c), recv_a[l].at[j], send_sems, recv_sems, 4 * l + j, sibling)
                  for l in range(nl) for j in range(N_CHIPS)]
        for cp in early_loads + late_loads + halves:
            cp.start()

        for cp in early_loads:
            cp.wait()
        for e in range(ne):
            acc[e][...] = ((acc[e][...] + gotv[e][0].astype(f32)) + gotv[e][1].astype(f32)) + gotv[e][2].astype(f32)
            share(e)

        for cp in late_loads:
            cp.wait()
        for cp in halves:
            cp.wait_recv()
        for l in range(nl):
            for j in range(N_CHIPS):
                s = own[l][j] + recv_a[l][j]
                own[l][j] = s
                wire[l][j] = s.astype(wire[l].dtype)
        chips = [_remote(wire[l].at[other_ids[k]], recv_b[l].at[k], send_sems, recv_sems, 4 * nl + 3 * l + k, (*chip, c))
                 for l in range(nl) for k, chip in enumerate(others)]
        for cp in chips:
            cp.start()
        for cp in chips:
            cp.wait_recv()
        for l in range(nl):
            mine_l = own[l].at[me]
            mine_l[...] = ((mine_l[...] + recv_b[l][0].astype(f32)) + recv_b[l][1].astype(f32)) + recv_b[l][2].astype(f32)
            share(ne + l)

        for i in range(n):
            if not is_big[i]:
                for k, chip in enumerate(others):
                    landed = cuts[i].block(out_refs[i], other_ids[k], c)
                    _remote(landed, landed, send_sems, recv_sems, share_sem[i] + 1 + k, (*chip, c)).wait_recv()
                    cp = _remote(landed, landed, send_sems, recv_sems, share_sem[i] + 4 + k, sibling)
                    cp.start()
                    sent.append(cp)
        for i in range(n):
            theirs = mine(i, 1 - c)
            _remote(theirs, theirs, send_sems, recv_sems, share_sem[i], sibling).wait_recv()
            if not is_big[i]:
                for k in range(3):
                    passed = cuts[i].block(out_refs[i], other_ids[k], 1 - c)
                    _remote(passed, passed, send_sems, recv_sems, share_sem[i] + 4 + k, sibling).wait_recv()
        for cp in halves + chips + sent:
            cp.wait_send()
        for cp in stores:
            cp.wait()

    scratch = [pltpu.SemaphoreType.DMA((share_base,)), pltpu.SemaphoreType.DMA((share_base,)),
               pltpu.SemaphoreType.DMA((4 * nl + 2 * ne + n,))]
    for name in late_names:
        block = CUTS[name].block_shape
        scratch += [pltpu.VMEM((N_CHIPS, *block), f32), pltpu.VMEM((N_CHIPS, *block), f32),
                    pltpu.VMEM((N_CHIPS, *block), _wire_dtype(name)), pltpu.VMEM((3, *block), _wire_dtype(name))]
    for name in early_names:
        block = CUTS[name].block_shape
        scratch += [pltpu.VMEM(block, f32), pltpu.VMEM((3, *block), _wire_dtype(name))]
    args = [late[name] for name in late_names] + [a for name in early_names for a in early[name]]
    outs = pl.pallas_call(
        body, name="reduce_tail",
        in_specs=[ANY] * n_in, out_specs=[ANY] * n,
        out_shape=[jax.ShapeDtypeStruct(CUTS[name].shard_shape if name in BIG else (CUTS[name].rows, CUTS[name].cols), f32)
                   for name in names],
        scratch_shapes=scratch, compiler_params=_params(),
    )(*args)
    return dict(zip(names, outs))


def _adamw_refs(w_ref, g_ref, m_ref, v_ref, d_ref, nm_ref, nv_ref):
    g = g_ref[...]
    nm = ADAM_B1 * m_ref[...] + (1.0 - ADAM_B1) * g
    nv = ADAM_B2 * v_ref[...] + (1.0 - ADAM_B2) * jnp.square(g)
    m_hat = nm / (1.0 - ADAM_B1 ** ADAM_STEP)
    v_hat = nv / (1.0 - ADAM_B2 ** ADAM_STEP)
    d_ref[...] = -ADAM_LR * (m_hat / (jnp.sqrt(v_hat) + ADAM_EPS) + ADAM_WD * w_ref[...])
    nm_ref[...] = nm
    nv_ref[...] = nv


def _adamw_small(ws, gs, ms, vs):
    n = len(ws)

    def body(*refs):
        for i in range(n):
            _adamw_refs(*[refs[k * n + i] for k in range(7)])

    whole = pl.BlockSpec(memory_space=pltpu.VMEM)
    outs = pl.pallas_call(
        body, name="adamw_small",
        in_specs=[whole] * (4 * n), out_specs=[whole] * (3 * n),
        out_shape=[jax.ShapeDtypeStruct(w.shape, f32) for w in ws] * 3,
        compiler_params=_params(),
    )(*ws, *gs, *ms, *vs)
    return outs[:n], outs[n:2 * n], outs[2 * n:]


def _adamw(name, w, g, m, v):
    rows, cols = w.shape
    rt = rows // 4

    def body(w_ref, g_ref, m_ref, v_ref, d_ref, nm_ref, nv_ref):
        _adamw_refs(w_ref, g_ref, m_ref, v_ref, d_ref, nm_ref, nv_ref)

    spec = pl.BlockSpec((rt, cols), lambda i: (i, 0))
    shape = jax.ShapeDtypeStruct((rows, cols), f32)
    return pl.pallas_call(
        body, name="adamw_" + name, grid=(rows // rt,),
        in_specs=[spec] * 4, out_specs=[spec] * 3, out_shape=[shape] * 3,
        compiler_params=_params(),
    )(w, g, m, v)


VEC_NAMES = ("pool_scale", "sgu_ln_g", "sgu_ln_b", "sgu_b", "ln1_g", "ln1_b", "ln2_g", "ln2_b")
WEIGHT_ORDER = ("w_in", "pool_w", "pool_scale", "sgu_ln_g", "sgu_ln_b", "sgu_w", "sgu_b", "w_out", "ln1_g", "ln1_b",
                "w_gate_up", "w_down", "ln2_g", "ln2_b")


def _pack_vecs(parts, extra=None):
    rows = [parts[name].reshape(-1, GROUP) for name in VEC_NAMES]
    if extra is not None:
        rows.append(extra.reshape(-1, GROUP))
    used = sum(r.shape[0] for r in rows)
    return jnp.concatenate(rows + [jnp.zeros((VEC_ROWS - used, GROUP), f32)], axis=0)


def _unpack_vecs(packed, shapes):
    out, at = {}, 0
    for name in VEC_NAMES:
        rows = math.prod(shapes[name]) // GROUP
        out[name] = packed[at:at + rows].reshape(shapes[name])
        at += rows
    return out, packed[at:]


def kernel(x, w_in, pool_w, pool_scale, sgu_ln_g, sgu_ln_b, sgu_w, sgu_b, w_out, ln1_g, ln1_b, w_gate_up, w_down, ln2_g, ln2_b, loss_target, m_w_in, m_pool_w, m_pool_scale, m_sgu_ln_g, m_sgu_ln_b, m_sgu_w, m_sgu_b, m_w_out, m_ln1_g, m_ln1_b, m_w_gate_up, m_w_down, m_ln2_g, m_ln2_b, v_w_in, v_pool_w, v_pool_scale, v_sgu_ln_g, v_sgu_ln_b, v_sgu_w, v_sgu_b, v_w_out, v_ln1_g, v_ln1_b, v_w_gate_up, v_w_down, v_ln2_g, v_ln2_b):
    given = dict(locals())
    batch, seq, _ = x.shape
    tokens = batch * seq
    tile = min(TOKEN_TILE, seq)
    ffn_bwd_tile = min(FFN_BWD_TILE, seq)
    wtile = min(WGRAD_TILE, tokens)
    shapes = {name: given[name].shape for name in WEIGHT_ORDER}

    x2 = x.reshape(tokens, D_MODEL)
    target = loss_target.reshape(tokens, D_MODEL)
    small = (pool_w[0], pool_scale[0][None], sgu_ln_g[0][None], sgu_ln_b[0][None], sgu_w[0],
             jnp.broadcast_to(sgu_b[0][:, :, None], (N_GROUPS, GROUP, GROUP)))
    g1, b1, g2, b2 = ln1_g[0][None], ln1_b[0][None], ln2_g[0][None], ln2_b[0][None]
    shard_b = {name: given[name][0].astype(bf16) for name in BIG}
    shard_b["w_gate_up_top"] = shard_b["w_gate_up"][:TOP_ROWS]
    shard_b["w_gate_up_bottom"] = shard_b["w_gate_up"][TOP_ROWS:]
    place = jnp.stack([lax.axis_index("c"), 2 * lax.axis_index("x") + lax.axis_index("y")]).astype(jnp.int32)

    def gather(*names):
        return [_GatherJob({name: shard_b[name] for name in names})]

    def halves_summed(name, grad, got):
        return _add_halves(name, grad, got, place)

    (x_b,), (got,) = _cast_x(x2, 2 * tile, gather("w_in"))
    w_in_b = got["w_in"]
    (xp_all, pre_b), (got,) = _fwd_proj(x_b, w_in_b, tile, gather("w_out", "w_gate_up_top"))
    w_out_b, w_top = got["w_out"], got["w_gate_up_top"]
    (mix_b, r1, h_b), (got,) = _fwd_mix(xp_all, pre_b, x2, small, w_out_b, g1, b1, tile, seq, gather("w_gate_up_bottom"))
    w_bottom = got["w_gate_up_bottom"]
    (gu_b, a_b), (got,) = _fwd_gate_up(h_b, w_top, w_bottom, tile, gather("w_down"))
    w_dn_b = got["w_down"]
    dr2, dr2_b, stats2 = _fwd_down_loss(a_b, w_dn_b, r1, target, g1, b1, g2, b2, tile)

    early = {}
    g_down, _ = _wgrad(a_b, dr2_b, D_MODEL, wtile, "wgrad_down")
    (dgu_b,), (got,) = _bwd_gate_up(dr2_b, gu_b, w_dn_b, ffn_bwd_tile, [_SwapHalvesJob({"w_down": g_down})])
    sums_down = halves_summed("w_down", g_down, got["w_down"])
    g_gu, (got,) = _wgrad(h_b, dgu_b, D_FF, wtile, "wgrad_gate_up", [_SwapChipsJob({"w_down": sums_down[1]})])
    early["w_down"] = (sums_down[0], got["w_down"])
    (dr1, dr1_b, stats1), (got,) = _bwd_ffn_in(dgu_b, w_top, w_bottom, dr2, r1, g1, b1, tile,
                                               [_SwapHalvesJob({"w_gate_up": g_gu})])
    sums_gu = halves_summed("w_gate_up", g_gu, got["w_gate_up"])
    g_out, _ = _wgrad(mix_b, dr1_b, D_MODEL, wtile, "wgrad_out")
    (dpool, dpre_b, g_pool_w, g_sgu_w, g_sgu_b, vecs), (got_gu, got_out) = _bwd_mix(
        dr1_b, xp_all, pre_b, small, w_out_b, tile, seq,
        [_SwapChipsJob({"w_gate_up": sums_gu[1]}), _SwapHalvesJob({"w_out": g_out})])
    early["w_gate_up"] = (sums_gu[0], got_gu["w_gate_up"])
    sums_out = halves_summed("w_out", g_out, got_out["w_out"])
    grad_x, dproj_b = _bwd_in(dpool, dpre_b, dr1, w_in_b, tile, seq)
    g_in, (got,) = _wgrad(x_b, dproj_b, IN_COLS, wtile, "wgrad_in", [_SwapChipsJob({"w_out": sums_out[1]})])
    early["w_out"] = (sums_out[0], got["w_out"])

    late = {
        "w_in": g_in,
        "pool_w": g_pool_w.reshape(SQUARE_ROWS, GROUP),
        "sgu_w": g_sgu_w.reshape(SQUARE_ROWS, GROUP),
        "vecs": _pack_vecs({"pool_scale": vecs[0], "sgu_ln_g": vecs[1], "sgu_ln_b": vecs[2], "sgu_b": g_sgu_b[:, :, 0],
                            "ln1_g": stats1[0], "ln1_b": stats1[1], "ln2_g": stats2[0], "ln2_b": stats2[1]},
                           extra=stats2[2]),
    }
    shared = _reduce_tail(late, early)

    grad, delta, new_m, new_v = {}, {}, {}, {}
    for name in BIG:
        grad[name] = shared[name][None]
        d, nm, nv = _adamw(name, given[name][0], shared[name], given["m_" + name][0], given["v_" + name][0])
        delta[name], new_m[name], new_v[name] = d[None], nm[None], nv[None]
    vec_grads, after = _unpack_vecs(shared["vecs"], shapes)
    grad.update(vec_grads)
    for name in ("pool_w", "sgu_w"):
        grad[name] = shared[name].reshape(shapes[name])
    small_names = ("pool_w", "sgu_w") + VEC_NAMES
    state = {pre: [given[pre + name] for name in small_names] for pre in ("", "m_", "v_")}
    ds, nms, nvs = _adamw_small(state[""], [grad[name] for name in small_names], state["m_"], state["v_"])
    delta.update(zip(small_names, ds))
    new_m.update(zip(small_names, nms))
    new_v.update(zip(small_names, nvs))

    sq_err = after[:LOSS_ROWS]
    loss = jnp.sum(sq_err) * (0.5 / D_MODEL)
    return (loss, grad_x.reshape(x.shape), *[grad[name] for name in WEIGHT_ORDER],
            *[delta[name] for name in WEIGHT_ORDER], *[new_m[name] for name in WEIGHT_ORDER],
            *[new_v[name] for name in WEIGHT_ORDER])
```

```python
import math

import jax
import jax.numpy as jnp
from jax import lax
from jax.experimental import pallas as pl
from jax.experimental.pallas import tpu as pltpu

f32 = jnp.float32
bf16 = jnp.bfloat16
MESH = pl.DeviceIdType.MESH

D_MODEL = 1024
POOL_WIDTH = 512
SGU_WIDTH = 512
IN_COLS = POOL_WIDTH + 2 * SGU_WIDTH
D_FF = 2816
POOL_WINDOWS = (2, 4, 8, 16)
GROUP = 128
N_GROUPS = 4
HALO = 16
LN_EPS = 1e-5
ALPHA = float(2.0 ** 0.25)
N_CHIPS = 4

ADAM_LR = 0.001
ADAM_B1 = 0.9
ADAM_B2 = 0.999
ADAM_EPS = 1e-08
ADAM_WD = 0.01
ADAM_STEP = 10

TOKEN_TILE = 512
FFN_BWD_TILE = 512
FF_CHUNK = 256
ROW_SUB = 256
WGRAD_TILE = 1024
TOP_ROWS = 256
V7X_VMEM_LIMIT = 56 * 1024 * 1024

SQUARE_ROWS = N_GROUPS * GROUP
VEC_ROWS = 64
LOSS_ROWS = D_MODEL // GROUP


def _params(**kw):
    return pltpu.CompilerParams(vmem_limit_bytes=V7X_VMEM_LIMIT, **kw)


def _mm(a, b):
    return jnp.dot(a, b, preferred_element_type=f32)


def _mm_nt(a, b):
    return lax.dot_general(a, b, (((1,), (1,)), ((), ())), preferred_element_type=f32)


def _mm_tn(a, b):
    return lax.dot_general(a, b, (((0,), (0,)), ((), ())), preferred_element_type=f32)


def _ln_fwd(r, g, b):
    mu = jnp.mean(r, axis=-1, keepdims=True)
    xc = r - mu
    var = jnp.mean(xc * xc, axis=-1, keepdims=True)
    rstd = lax.rsqrt(var + LN_EPS)
    xhat = xc * rstd
    return xhat * g + b, xhat, rstd


def _ln_bwd(dout, xhat, rstd, g):
    dxhat = dout * g
    m1 = jnp.mean(dxhat, axis=-1, keepdims=True)
    m2 = jnp.mean(dxhat * xhat, axis=-1, keepdims=True)
    return rstd * (dxhat - m1 - xhat * m2)


def _col_sum(a):
    return jnp.sum(a, axis=0, keepdims=True)


def _gelu_parts(z):
    cdf = 0.5 * (1.0 + lax.erf(z * (1.0 / math.sqrt(2.0))))
    pdf = jnp.exp(-0.5 * z * z) * (1.0 / math.sqrt(2.0 * math.pi))
    return cdf, pdf


def _inv_counts(seq_tile, rows):
    pos = seq_tile * rows + lax.broadcasted_iota(jnp.int32, (rows, GROUP), 0) + 1
    return [1.0 / jnp.minimum(pos, w).astype(f32) for w in POOL_WINDOWS]


def _window_sums(e, back):
    n = e.shape[0]

    def shifted(a, s):
        return pltpu.roll(a, s if back else n - s, 0)

    s2 = e + shifted(e, 1)
    s4 = s2[:, GROUP:] + shifted(s2[:, GROUP:], 2)
    s8 = s4[:, GROUP:] + shifted(s4[:, GROUP:], 4)
    s16 = s8[:, GROUP:] + shifted(s8[:, GROUP:], 8)
    return [s2[:, :GROUP], s4[:, :GROUP], s8[:, :GROUP], s16]


def _pooled_groups(xp, halo, inv):
    sums = _window_sums(jnp.concatenate([halo, xp], axis=0), back=True)
    return [sums[g][HALO:] * inv[g] - xp[:, g * GROUP:(g + 1) * GROUP] for g in range(N_GROUPS)]


def _tril_mask():
    r = lax.broadcasted_iota(jnp.int32, (GROUP, GROUP), 0)
    c = lax.broadcasted_iota(jnp.int32, (GROUP, GROUP), 1)
    return (r >= c).astype(f32)


def _gs(g):
    return slice(g * GROUP, (g + 1) * GROUP)


class _Cut:
    def __init__(self, rows, cols, by_cols):
        self.rows, self.cols, self.by_cols = rows, cols, by_cols
        if by_cols:
            self.block_shape = (rows // 2, cols // N_CHIPS)
            self.shard_shape = (rows, cols // N_CHIPS)
        else:
            self.block_shape = (rows // (2 * N_CHIPS), cols)
            self.shard_shape = (rows // N_CHIPS, cols)

    def block(self, ref, chip, half):
        br, bc = self.block_shape
        if self.by_cols:
            return ref.at[pl.ds(pl.multiple_of(half * br, 16), br), pl.ds(pl.multiple_of(chip * bc, 128), bc)]
        return ref.at[pl.ds(pl.multiple_of((2 * chip + half) * br, 8), br), :]

    def shard(self, ref, chip):
        sr, sc = self.shard_shape
        if self.by_cols:
            return ref.at[:, pl.ds(pl.multiple_of(chip * sc, 128), sc)]
        return ref.at[pl.ds(pl.multiple_of(chip * sr, 16), sr), :]

    def half_of_shard(self, ref, half):
        br = self.block_shape[0]
        return ref.at[pl.ds(pl.multiple_of(half * br, 8), br), :]

    def block_index(self, chip, half):
        return (half, chip) if self.by_cols else (2 * chip + half, 0)


CUTS = {
    "w_in": _Cut(D_MODEL, IN_COLS, True),
    "w_out": _Cut(D_MODEL, D_MODEL, False),
    "w_gate_up": _Cut(D_MODEL, 2 * D_FF, True),
    "w_down": _Cut(D_FF, D_MODEL, False),
    "w_gate_up_top": _Cut(TOP_ROWS, 2 * D_FF, True),
    "w_gate_up_bottom": _Cut(D_MODEL - TOP_ROWS, 2 * D_FF, True),
    "pool_w": _Cut(SQUARE_ROWS, GROUP, False),
    "sgu_w": _Cut(SQUARE_ROWS, GROUP, False),
    "vecs": _Cut(VEC_ROWS, GROUP, False),
}
BIG = ("w_in", "w_out", "w_gate_up", "w_down")
ANY = pl.BlockSpec(memory_space=pl.ANY)


def _wire_dtype(name):
    return bf16 if name in BIG else f32


def _place():
    x, y, c = lax.axis_index("x"), lax.axis_index("y"), lax.axis_index("c")
    others = [(1 - x, y), (x, 1 - y), (1 - x, 1 - y)]
    return x, y, c, 2 * x + y, others, [2 * ox + oy for ox, oy in others]


def _remote(src, dst, send_sems, recv_sems, k, to):
    return pltpu.make_async_remote_copy(src_ref=src, dst_ref=dst, send_sem=send_sems.at[k], recv_sem=recv_sems.at[k],
                                        device_id=to, device_id_type=MESH)


class _GatherJob:
    def __init__(self, shards):
        self.names = tuple(shards)
        self.arrays = tuple(shards.values())
        n = len(self.names)
        self.out_shapes = [jax.ShapeDtypeStruct((CUTS[name].rows, CUTS[name].cols), bf16) for name in self.names]
        self.scratch_shapes = ([pltpu.SemaphoreType.DMA((6 * n,)), pltpu.SemaphoreType.DMA((6 * n,)),
                                pltpu.SemaphoreType.DMA((2 * n,))]
                               + [pltpu.VMEM(CUTS[name].shard_shape, bf16) for name in self.names])

    def bind(self, shard_refs, full_refs, scratch):
        self.shards, self.full = shard_refs, full_refs
        self.send_sems, self.recv_sems, self.local_sems = scratch[:3]
        self.stages = scratch[3:]
        return self

    def _sends(self):
        _, _, c, me, others, _ = _place()
        return [_remote(CUTS[name].half_of_shard(self.shards[w], c), CUTS[name].block(self.full[w], me, c),
                        self.send_sems, self.recv_sems, 3 * w + k, (*chip, c))
                for w, name in enumerate(self.names) for k, chip in enumerate(others)]

    def _relays(self, half):
        x, y, c, _, _, other_ids = _place()
        n = len(self.names)
        return [_remote(CUTS[name].block(self.full[w], other_ids[k], half),
                        CUTS[name].block(self.full[w], other_ids[k], half),
                        self.send_sems, self.recv_sems, 3 * n + 3 * w + k, (x, y, 1 - c))
                for w, name in enumerate(self.names) for k in range(3)]

    def _stores(self):
        me = _place()[3]
        return [pltpu.make_async_copy(self.stages[w], CUTS[name].shard(self.full[w], me), self.local_sems.at[2 * w + 1])
                for w, name in enumerate(self.names)]

    def start(self):
        loads = [pltpu.make_async_copy(self.shards[w], self.stages[w], self.local_sems.at[2 * w])
                 for w in range(len(self.names))]
        for cp in loads:
            cp.start()
        for cp in self._sends():
            cp.start()
        for load, store in zip(loads, self._stores()):
            load.wait()
            store.start()

    def relay(self):
        _, _, c, _, others, other_ids = _place()
        relays = self._relays(c)
        for w, name in enumerate(self.names):
            for k, chip in enumerate(others):
                landed = CUTS[name].block(self.full[w], other_ids[k], c)
                _remote(landed, landed, self.send_sems, self.recv_sems, 3 * w + k, (*chip, c)).wait_recv()
                relays[3 * w + k].start()

    def finish(self):
        c = _place()[2]
        for cp in self._relays(1 - c):
            cp.wait_recv()
        for cp in self._sends() + self._relays(c):
            cp.wait_send()
        for cp in self._stores():
            cp.wait()


class _SwapHalvesJob:
    def __init__(self, grads):
        self.names = tuple(grads)
        self.arrays = tuple(grads.values())
        n = len(self.names)
        self.out_shapes = [jax.ShapeDtypeStruct((N_CHIPS, *CUTS[name].block_shape), f32) for name in self.names]
        self.scratch_shapes = [pltpu.SemaphoreType.DMA((N_CHIPS * n,)), pltpu.SemaphoreType.DMA((N_CHIPS * n,))]

    def bind(self, g_refs, got_refs, scratch):
        self.g_refs, self.got_refs = g_refs, got_refs
        self.send_sems, self.recv_sems = scratch
        return self

    def _copies(self):
        x, y, c, _, _, _ = _place()
        return [_remote(CUTS[name].block(self.g_refs[a], j, 1 - c), self.got_refs[a].at[j], self.send_sems,
                        self.recv_sems, N_CHIPS * a + j, (x, y, 1 - c))
                for a, name in enumerate(self.names) for j in range(N_CHIPS)]

    def start(self):
        for cp in self._copies():
            cp.start()

    def finish(self):
        for cp in self._copies():
            cp.wait()


class _SwapChipsJob:
    def __init__(self, partials):
        self.names = tuple(partials)
        self.arrays = tuple(partials.values())
        n = len(self.names)
        self.out_shapes = [jax.ShapeDtypeStruct((3, *CUTS[name].block_shape), _wire_dtype(name)) for name in self.names]
        self.scratch_shapes = [pltpu.SemaphoreType.DMA((3 * n,)), pltpu.SemaphoreType.DMA((3 * n,))]

    def bind(self, p_refs, got_refs, scratch):
        self.p_refs, self.got_refs = p_refs, got_refs
        self.send_sems, self.recv_sems = scratch
        return self

    def _copies(self):
        _, _, c, _, others, other_ids = _place()
        return [_remote(self.p_refs[a].at[other_ids[k]], self.got_refs[a].at[k], self.send_sems, self.recv_sems,
                        3 * a + k, (*chip, c))
                for a in range(len(self.names)) for k, chip in enumerate(others)]

    def start(self):
        for cp in self._copies():
            cp.start()

    def finish(self):
        for cp in self._copies():
            cp.wait()


def _call(body, *, name, grid, in_specs, out_specs, out_shape, args, scratch_shapes=(), jobs=()):
    n_in, n_out, n_scr = len(in_specs), len(out_specs), len(scratch_shapes)
    j_in = [len(j.arrays) for j in jobs]
    j_out = [len(j.out_shapes) for j in jobs]
    j_scr = [len(j.scratch_shapes) for j in jobs]

    def wrapped(*refs):
        refs = list(refs)

        def take(k):
            head = refs[:k]
            del refs[:k]
            return head

        ins, jins = take(n_in), [take(k) for k in j_in]
        outs, jouts = take(n_out), [take(k) for k in j_out]
        scr, jscr = take(n_scr), [take(k) for k in j_scr]
        bound = [j.bind(a, b, c) for j, a, b, c in zip(jobs, jins, jouts, jscr)]
        relaying = [b for b in bound if hasattr(b, "relay")]
        if not bound:
            body(*ins, *outs, *scr)
            return
        first = _all([pl.program_id(d) == 0 for d in range(len(grid))])
        last = _all([pl.program_id(d) == grid[d] - 1 for d in range(len(grid))])

        @pl.when(first)
        def _():
            for b in bound:
                b.start()

        if relaying:
            @pl.when(_all([pl.program_id(d) == (3 * grid[d]) // 4 for d in range(len(grid))]))
            def _():
                for b in relaying:
                    b.relay()

        body(*ins, *outs, *scr)

        @pl.when(last)
        def _():
            for b in bound:
                b.finish()

    results = pl.pallas_call(
        wrapped, name=name, grid=grid,
        in_specs=list(in_specs) + [ANY] * sum(j_in), out_specs=list(out_specs) + [ANY] * sum(j_out),
        out_shape=list(out_shape) + [s for j in jobs for s in j.out_shapes],
        scratch_shapes=list(scratch_shapes) + [s for j in jobs for s in j.scratch_shapes],
        compiler_params=_params(),
    )(*args, *[a for j in jobs for a in j.arrays])
    results = list(results)
    own, rest = results[:n_out], results[n_out:]
    per_job = []
    for j, k in zip(jobs, j_out):
        per_job.append(dict(zip(j.names, rest[:k])))
        rest = rest[k:]
    return own, per_job


def _all(conds):
    out = conds[0]
    for c in conds[1:]:
        out = jnp.logical_and(out, c)
    return out


def _cast_x(x2, tile, jobs):
    tokens = x2.shape[0]

    def body(x_ref, xb_ref):
        xb_ref[...] = x_ref[...].astype(bf16)

    row = lambda i: (i, 0)
    return _call(
        body, name="cast_x", grid=(tokens // tile,),
        in_specs=[pl.BlockSpec((tile, D_MODEL), row)], out_specs=[pl.BlockSpec((tile, D_MODEL), row)],
        out_shape=[jax.ShapeDtypeStruct((tokens, D_MODEL), bf16)], args=(x2,), jobs=jobs)


def _fwd_proj(x_b, w_in_b, tile, jobs):
    tokens = x_b.shape[0]

    def body(x_ref, w_ref, xp_ref, pre_ref):
        proj = _mm(x_ref[...], w_ref[...])
        xp_ref[...] = proj[:, :POOL_WIDTH]
        pre_ref[...] = proj[:, POOL_WIDTH:].astype(bf16)

    row = lambda i: (i, 0)
    return _call(
        body, name="fwd_proj", grid=(tokens // tile,),
        in_specs=[pl.BlockSpec((tile, D_MODEL), row), pl.BlockSpec((D_MODEL, IN_COLS), lambda i: (0, 0))],
        out_specs=[pl.BlockSpec((tile, POOL_WIDTH), row), pl.BlockSpec((tile, 2 * SGU_WIDTH), row)],
        out_shape=[jax.ShapeDtypeStruct((tokens, POOL_WIDTH), f32), jax.ShapeDtypeStruct((tokens, 2 * SGU_WIDTH), bf16)],
        args=(x_b, w_in_b), jobs=jobs)


def _small_specs():
    return [pl.BlockSpec((N_GROUPS, GROUP, GROUP), lambda i: (0, 0, 0)),
            pl.BlockSpec((1, POOL_WIDTH), lambda i: (0, 0)),
            pl.BlockSpec((1, SGU_WIDTH), lambda i: (0, 0)),
            pl.BlockSpec((1, SGU_WIDTH), lambda i: (0, 0)),
            pl.BlockSpec((N_GROUPS, GROUP, GROUP), lambda i: (0, 0, 0)),
            pl.BlockSpec((N_GROUPS, GROUP, GROUP), lambda i: (0, 0, 0))]


def _fwd_mix(xp_all, pre_b, x2, small, w_out_b, ln1_g, ln1_b, tile, seq, jobs):
    tokens = x2.shape[0]
    tps = seq // tile
    hb = tile // HALO

    def body(xp_ref, halo_ref, pre_ref, x_ref, pw_ref, ps_ref, lg_ref, lb_ref, sw_ref, sb_ref, wout_ref, g1_ref, b1_ref,
             mix_ref, r1_ref, h_ref):
        seq_tile = pl.program_id(0) % tps
        xp = xp_ref[...]
        halo = jnp.where(seq_tile == 0, 0.0, halo_ref[...])
        pooled = _pooled_groups(xp, halo, _inv_counts(seq_tile, tile))
        for g in range(N_GROUPS):
            po = _mm(pooled[g].astype(bf16), pw_ref[g].astype(bf16)) * ps_ref[:, _gs(g)]
            mix_ref[:, _gs(g)] = po.astype(bf16)

        pre = pre_ref[...].astype(f32)
        cdf, _ = _gelu_parts(pre)
        zg = pre * cdf
        u = zg[:, :SGU_WIDTH]
        vln, _, _ = _ln_fwd(zg[:, SGU_WIDTH:], lg_ref[...], lb_ref[...])
        vb = vln.astype(bf16)
        mask = _tril_mask()
        for h in range(N_GROUPS):
            wm = (sw_ref[h] * mask).astype(bf16)
            bias = sb_ref[h]
            for c in range(tile // GROUP):
                rows = slice(c * GROUP, (c + 1) * GROUP)
                mixed = _mm(wm, vb[rows, _gs(h)]) + bias
                mix_ref[rows, POOL_WIDTH + h * GROUP:POOL_WIDTH + (h + 1) * GROUP] = (u[rows, _gs(h)] * mixed).astype(bf16)

        r1 = ALPHA * x_ref[...] + _mm(mix_ref[...], wout_ref[...])
        r1_ref[...] = r1
        h1, _, _ = _ln_fwd(r1, g1_ref[...], b1_ref[...])
        h_ref[...] = h1.astype(bf16)

    row = lambda i: (i, 0)
    vec = pl.BlockSpec((1, D_MODEL), lambda i: (0, 0))
    return _call(
        body, name="fwd_mix", grid=(tokens // tile,),
        in_specs=[pl.BlockSpec((tile, POOL_WIDTH), row),
                  pl.BlockSpec((HALO, POOL_WIDTH), lambda i: (jnp.maximum(i * hb - 1, 0), 0)),
                  pl.BlockSpec((tile, 2 * SGU_WIDTH), row),
                  pl.BlockSpec((tile, D_MODEL), row)] + _small_specs()
                 + [pl.BlockSpec((D_MODEL, D_MODEL), lambda i: (0, 0)), vec, vec],
        out_specs=[pl.BlockSpec((tile, D_MODEL), row)] * 3,
        out_shape=[jax.ShapeDtypeStruct((tokens, D_MODEL), bf16),
                   jax.ShapeDtypeStruct((tokens, D_MODEL), f32),
                   jax.ShapeDtypeStruct((tokens, D_MODEL), bf16)],
        args=(xp_all, xp_all, pre_b, x2, *small, w_out_b, ln1_g, ln1_b), jobs=jobs)


def _fwd_gate_up(h_b, w_top, w_bottom, tile, jobs):
    tokens = h_b.shape[0]
    def body(h_ref, wt_ref, wb_ref, gu_ref, a_ref):
        ht, hb = h_ref[:, :TOP_ROWS], h_ref[:, TOP_ROWS:]
        for c in range(D_FF // FF_CHUNK):
            gcols = slice(c * FF_CHUNK, (c + 1) * FF_CHUNK)
            ucols = slice(D_FF + c * FF_CHUNK, D_FF + (c + 1) * FF_CHUNK)
            gate = _mm(ht, wt_ref[:, gcols]) + _mm(hb, wb_ref[:, gcols])
            up = _mm(ht, wt_ref[:, ucols]) + _mm(hb, wb_ref[:, ucols])
            sg = jax.nn.sigmoid(gate)
            silu = gate * sg
            gu_ref[:, gcols] = (up * (sg + silu * (1.0 - sg))).astype(bf16)
            gu_ref[:, ucols] = silu.astype(bf16)
            a_ref[:, gcols] = (silu * up).astype(bf16)

    return _call(
        body, name="fwd_gate_up", grid=(tokens // tile,),
        in_specs=[pl.BlockSpec((tile, D_MODEL), lambda i: (i, 0)),
                  pl.BlockSpec((TOP_ROWS, 2 * D_FF), lambda i: (0, 0), pipeline_mode=pl.Buffered(1)),
                  pl.BlockSpec((D_MODEL - TOP_ROWS, 2 * D_FF), lambda i: (0, 0), pipeline_mode=pl.Buffered(1))],
        out_specs=[pl.BlockSpec((tile, 2 * D_FF), lambda i: (i, 0)),
                   pl.BlockSpec((tile, D_FF), lambda i: (i, 0))],
        out_shape=[jax.ShapeDtypeStruct((tokens, 2 * D_FF), bf16),
                   jax.ShapeDtypeStruct((tokens, D_FF), bf16)],
        args=(h_b, w_top, w_bottom), jobs=jobs)


def _fwd_down_loss(a_b, w_dn_b, r1, target, ln1_g, ln1_b, ln2_g, ln2_b, tile):
    tokens = a_b.shape[0]

    def body(a_ref, w_ref, r1_ref, t_ref, g1_ref, b1_ref, g2_ref, b2_ref, dr2_ref, dr2b_ref, st_ref):
        @pl.when(pl.program_id(0) == 0)
        def _():
            st_ref[...] = jnp.zeros_like(st_ref)

        sub = min(ROW_SUB, tile)
        for s in range(tile // sub):
            rows = slice(s * sub, (s + 1) * sub)
            h1, _, _ = _ln_fwd(r1_ref[rows, :], g1_ref[...], b1_ref[...])
            r2 = ALPHA * h1 + _mm(a_ref[rows, :], w_ref[...])
            y, xhat, rstd = _ln_fwd(r2, g2_ref[...], b2_ref[...])
            diff = y - t_ref[rows, :]
            dy = diff * (1.0 / D_MODEL)
            st_ref[0:1, :] += _col_sum(dy * xhat)
            st_ref[1:2, :] += _col_sum(dy)
            st_ref[2:3, :] += _col_sum(diff * diff)
            dr2 = _ln_bwd(dy, xhat, rstd, g2_ref[...])
            dr2_ref[rows, :] = dr2
            dr2b_ref[rows, :] = dr2.astype(bf16)

    row = lambda i: (i, 0)
    vec = pl.BlockSpec((1, D_MODEL), lambda i: (0, 0))
    return _call(
        body, name="fwd_down_loss", grid=(tokens // tile,),
        in_specs=[pl.BlockSpec((tile, D_FF), row), pl.BlockSpec((D_FF, D_MODEL), lambda i: (0, 0)),
                  pl.BlockSpec((tile, D_MODEL), row), pl.BlockSpec((tile, D_MODEL), row), vec, vec, vec, vec],
        out_specs=[pl.BlockSpec((tile, D_MODEL), row), pl.BlockSpec((tile, D_MODEL), row),
                   pl.BlockSpec((8, D_MODEL), lambda i: (0, 0))],
        out_shape=[jax.ShapeDtypeStruct((tokens, D_MODEL), f32), jax.ShapeDtypeStruct((tokens, D_MODEL), bf16),
                   jax.ShapeDtypeStruct((8, D_MODEL), f32)],
        args=(a_b, w_dn_b, r1, target, ln1_g, ln1_b, ln2_g, ln2_b))[0]


def _bwd_gate_up(dr2, gu_b, w_dn_b, tile, jobs):
    tokens = dr2.shape[0]

    def body(d_ref, gu_ref, w_ref, dgu_ref):
        d = d_ref[...].astype(bf16)
        for c in range(D_FF // FF_CHUNK):
            gcols = slice(c * FF_CHUNK, (c + 1) * FF_CHUNK)
            ucols = slice(D_FF + c * FF_CHUNK, D_FF + (c + 1) * FF_CHUNK)
            da = _mm_nt(d, w_ref[gcols, :])
            dgu_ref[:, gcols] = (da * gu_ref[:, gcols].astype(f32)).astype(bf16)
            dgu_ref[:, ucols] = (da * gu_ref[:, ucols].astype(f32)).astype(bf16)

    return _call(
        body, name="bwd_gate_up", grid=(tokens // tile,),
        in_specs=[pl.BlockSpec((tile, D_MODEL), lambda i: (i, 0)),
                  pl.BlockSpec((tile, 2 * D_FF), lambda i: (i, 0)),
                  pl.BlockSpec((D_FF, D_MODEL), lambda i: (0, 0))],
        out_specs=[pl.BlockSpec((tile, 2 * D_FF), lambda i: (i, 0))],
        out_shape=[jax.ShapeDtypeStruct((tokens, 2 * D_FF), bf16)],
        args=(dr2, gu_b, w_dn_b), jobs=jobs)


def _bwd_ffn_in(dgu_b, w_top, w_bottom, dr2, r1, ln1_g, ln1_b, tile, jobs):
    tokens = dr2.shape[0]

    def body(dgu_ref, wt_ref, wb_ref, d_ref, r1_ref, g1_ref, b1_ref, dr1_ref, dr1b_ref, st_ref):
        @pl.when(pl.program_id(0) == 0)
        def _():
            st_ref[...] = jnp.zeros_like(st_ref)

        dgu = dgu_ref[...]
        dh = ALPHA * d_ref[...] + jnp.concatenate([_mm_nt(dgu, wt_ref[...]), _mm_nt(dgu, wb_ref[...])], axis=1)
        _, xhat, rstd = _ln_fwd(r1_ref[...], g1_ref[...], b1_ref[...])
        st_ref[0:1, :] += _col_sum(dh * xhat)
        st_ref[1:2, :] += _col_sum(dh)
        dr1 = _ln_bwd(dh, xhat, rstd, g1_ref[...])
        dr1_ref[...] = dr1
        dr1b_ref[...] = dr1.astype(bf16)

    row = lambda i: (i, 0)
    vec = pl.BlockSpec((1, D_MODEL), lambda i: (0, 0))
    return _call(
        body, name="bwd_ffn_in", grid=(tokens // tile,),
        in_specs=[pl.BlockSpec((tile, 2 * D_FF), row),
                  pl.BlockSpec((TOP_ROWS, 2 * D_FF), lambda i: (0, 0), pipeline_mode=pl.Buffered(1)),
                  pl.BlockSpec((D_MODEL - TOP_ROWS, 2 * D_FF), lambda i: (0, 0), pipeline_mode=pl.Buffered(1)),
                  pl.BlockSpec((tile, D_MODEL), row), pl.BlockSpec((tile, D_MODEL), row), vec, vec],
        out_specs=[pl.BlockSpec((tile, D_MODEL), row), pl.BlockSpec((tile, D_MODEL), row),
                   pl.BlockSpec((8, D_MODEL), lambda i: (0, 0))],
        out_shape=[jax.ShapeDtypeStruct((tokens, D_MODEL), f32), jax.ShapeDtypeStruct((tokens, D_MODEL), bf16),
                   jax.ShapeDtypeStruct((8, D_MODEL), f32)],
        args=(dgu_b, w_top, w_bottom, dr2, r1, ln1_g, ln1_b), jobs=jobs)


def _bwd_mix(dr1, xp_all, pre_b, small, w_out_b, tile, seq, jobs):
    tokens = dr1.shape[0]
    tps = seq // tile
    hb = tile // HALO
    steps = tokens // tile

    def body(dr1_ref, xp_ref, halo_ref, pre_ref, wout_ref, pw_ref, ps_ref, lg_ref, lb_ref, sw_ref, sb_ref,
             dpool_ref, dpre_ref, gpw_ref, gsw_ref, gsb_ref, vec_ref, du_ref, dvln_ref):
        step = pl.program_id(0)
        seq_tile = step % tps

        @pl.when(step == 0)
        def _():
            gpw_ref[...] = jnp.zeros_like(gpw_ref)
            gsw_ref[...] = jnp.zeros_like(gsw_ref)
            gsb_ref[...] = jnp.zeros_like(gsb_ref)
            vec_ref[...] = jnp.zeros_like(vec_ref)

        dmix = _mm_nt(dr1_ref[...].astype(bf16), wout_ref[...])

        xp = xp_ref[...]
        halo = jnp.where(seq_tile == 0, 0.0, halo_ref[...])
        pooled = _pooled_groups(xp, halo, _inv_counts(seq_tile, tile))
        for g in range(N_GROUPS):
            pb = pooled[g].astype(bf16)
            pwb = pw_ref[g].astype(bf16)
            dpo = dmix[:, _gs(g)]
            vec_ref[0:1, _gs(g)] += _col_sum(dpo * _mm(pb, pwb))
            dpo_b = (dpo * ps_ref[:, _gs(g)]).astype(bf16)
            gpw_ref[g] += _mm_tn(pb, dpo_b)
            dpool_ref[:, _gs(g)] = _mm_nt(dpo_b, pwb)

        pre = pre_ref[...].astype(f32)
        cdf, pdf = _gelu_parts(pre)
        zg = pre * cdf
        u = zg[:, :SGU_WIDTH]
        vln, vhat, rstd = _ln_fwd(zg[:, SGU_WIDTH:], lg_ref[...], lb_ref[...])
        vb = vln.astype(bf16)
        mask = _tril_mask()
        for h in range(N_GROUPS):
            wm = (sw_ref[h] * mask).astype(bf16)
            bias = sb_ref[h]
            gsw = jnp.zeros((GROUP, GROUP), f32)
            gsb = jnp.zeros((GROUP, GROUP), f32)
            for c in range(tile // GROUP):
                rows = slice(c * GROUP, (c + 1) * GROUP)
                v_ch = vb[rows, _gs(h)]
                d = dmix[rows, POOL_WIDTH + h * GROUP:POOL_WIDTH + (h + 1) * GROUP]
                du_ref[rows, _gs(h)] = d * (_mm(wm, v_ch) + bias)
                dmixed = d * u[rows, _gs(h)]
                gsb += dmixed
                dmixed_b = dmixed.astype(bf16)
                gsw += _mm_nt(dmixed_b, v_ch)
                dvln_ref[rows, _gs(h)] = _mm_tn(wm, dmixed_b)
            gsw_ref[h] += gsw * mask
            gsb_ref[h] += gsb

        dvln = dvln_ref[...]
        vec_ref[1:2, :] += _col_sum(dvln * vhat)
        vec_ref[2:3, :] += _col_sum(dvln)
        dgelu = cdf + pre * pdf
        dpre_ref[:, :SGU_WIDTH] = (du_ref[...] * dgelu[:, :SGU_WIDTH]).astype(bf16)
        dpre_ref[:, SGU_WIDTH:] = (_ln_bwd(dvln, vhat, rstd, lg_ref[...]) * dgelu[:, SGU_WIDTH:]).astype(bf16)

        @pl.when(step == steps - 1)
        def _():
            for h in range(N_GROUPS):
                gsb_ref[h] = jnp.broadcast_to(jnp.sum(gsb_ref[h], axis=1, keepdims=True), (GROUP, GROUP))

    row = lambda i: (i, 0)
    sq = jax.ShapeDtypeStruct((N_GROUPS, GROUP, GROUP), f32)
    sq_spec = pl.BlockSpec((N_GROUPS, GROUP, GROUP), lambda i: (0, 0, 0))
    return _call(
        body, name="bwd_mix", grid=(steps,),
        in_specs=[pl.BlockSpec((tile, D_MODEL), row), pl.BlockSpec((tile, POOL_WIDTH), row),
                  pl.BlockSpec((HALO, POOL_WIDTH), lambda i: (jnp.maximum(i * hb - 1, 0), 0)),
                  pl.BlockSpec((tile, 2 * SGU_WIDTH), row),
                  pl.BlockSpec((D_MODEL, D_MODEL), lambda i: (0, 0))] + _small_specs(),
        out_specs=[pl.BlockSpec((tile, POOL_WIDTH), row), pl.BlockSpec((tile, 2 * SGU_WIDTH), row),
                   sq_spec, sq_spec, sq_spec, pl.BlockSpec((8, POOL_WIDTH), lambda i: (0, 0))],
        out_shape=[jax.ShapeDtypeStruct((tokens, POOL_WIDTH), f32), jax.ShapeDtypeStruct((tokens, 2 * SGU_WIDTH), bf16),
                   sq, sq, sq, jax.ShapeDtypeStruct((8, POOL_WIDTH), f32)],
        scratch_shapes=[pltpu.VMEM((tile, SGU_WIDTH), f32), pltpu.VMEM((tile, SGU_WIDTH), f32)],
        args=(dr1, xp_all, xp_all, pre_b, w_out_b, *small), jobs=jobs)


def _bwd_in(dpool, dpre_b, dr1, w_in_b, tile, seq):
    tokens = dr1.shape[0]
    tps = seq // tile
    hb = tile // HALO
    last_halo = tokens // HALO - 1

    def body(dpool_ref, nxt_ref, dpre_ref, dr1_ref, w_ref, dx_ref, dxp_ref, dproj_ref):
        seq_tile = pl.program_id(0) % tps
        inv = _inv_counts(seq_tile, tile)
        dpl = dpool_ref[...]
        nxt = jnp.where(seq_tile == tps - 1, 0.0, nxt_ref[...])
        scaled = jnp.concatenate([dpl[:, _gs(g)] * inv[g] for g in range(N_GROUPS)], axis=1)
        scaled_nxt = jnp.concatenate([nxt[:, _gs(g)] * (1.0 / POOL_WINDOWS[g]) for g in range(N_GROUPS)], axis=1)
        sums = _window_sums(jnp.concatenate([scaled, scaled_nxt], axis=0), back=False)
        for g in range(N_GROUPS):
            dproj_ref[:, _gs(g)] = (sums[g][:tile] - dpl[:, _gs(g)]).astype(bf16)
        dxp_ref[...] = dproj_ref[:, :POOL_WIDTH]
        dproj_ref[:, POOL_WIDTH:] = dpre_ref[...]
        dx_ref[...] = ALPHA * dr1_ref[...] + _mm_nt(dproj_ref[...], w_ref[...])

    row = lambda i: (i, 0)
    return _call(
        body, name="bwd_in", grid=(tokens // tile,),
        in_specs=[pl.BlockSpec((tile, POOL_WIDTH), row),
                  pl.BlockSpec((HALO, POOL_WIDTH), lambda i: (jnp.minimum((i + 1) * hb, last_halo), 0)),
                  pl.BlockSpec((tile, 2 * SGU_WIDTH), row),
                  pl.BlockSpec((tile, D_MODEL), row),
                  pl.BlockSpec((D_MODEL, IN_COLS), lambda i: (0, 0))],
        out_specs=[pl.BlockSpec((tile, D_MODEL), row), pl.BlockSpec((tile, POOL_WIDTH), row)],
        out_shape=[jax.ShapeDtypeStruct((tokens, D_MODEL), f32), jax.ShapeDtypeStruct((tokens, POOL_WIDTH), bf16)],
        scratch_shapes=[pltpu.VMEM((tile, IN_COLS), bf16)],
        args=(dpool, dpool, dpre_b, dr1, w_in_b))[0]


def _wgrad(a, b, col_tile, tile, name, jobs=()):
    tokens, m = a.shape
    n = b.shape[1]

    def body(a_ref, b_ref, o_ref):
        @pl.when(pl.program_id(1) == 0)
        def _():
            o_ref[...] = jnp.zeros_like(o_ref)

        o_ref[...] += _mm_tn(a_ref[...].astype(bf16), b_ref[...].astype(bf16))

    (out,), got = _call(
        body, name=name, grid=(n // col_tile, tokens // tile),
        in_specs=[pl.BlockSpec((tile, m), lambda j, k: (k, 0)),
                  pl.BlockSpec((tile, col_tile), lambda j, k: (k, j))],
        out_specs=[pl.BlockSpec((m, col_tile), lambda j, k: (0, j))],
        out_shape=[jax.ShapeDtypeStruct((m, n), f32)],
        args=(a, b), jobs=jobs)
    return out, got


def _wgrad_in(x_b, dxp_b, dpre_b, tile, jobs):
    tokens = x_b.shape[0]

    def body(x_ref, dxp_ref, dpre_ref, o_ref):
        @pl.when(pl.program_id(0) == 0)
        def _():
            o_ref[...] = jnp.zeros_like(o_ref)

        xt = x_ref[...]
        o_ref[:, :POOL_WIDTH] += _mm_tn(xt, dxp_ref[...])
        o_ref[:, POOL_WIDTH:] += _mm_tn(xt, dpre_ref[...])

    row = lambda k: (k, 0)
    (out,), got = _call(
        body, name="wgrad_in", grid=(tokens // tile,),
        in_specs=[pl.BlockSpec((tile, D_MODEL), row), pl.BlockSpec((tile, POOL_WIDTH), row),
                  pl.BlockSpec((tile, 2 * SGU_WIDTH), row)],
        out_specs=[pl.BlockSpec((D_MODEL, IN_COLS), lambda k: (0, 0))],
        out_shape=[jax.ShapeDtypeStruct((D_MODEL, IN_COLS), f32)],
        args=(x_b, dxp_b, dpre_b), jobs=jobs)
    return out, got


def _add_halves(name, g, got, place):
    cut = CUTS[name]
    br, bc = cut.block_shape
    wire = _wire_dtype(name)

    def body(place_ref, g_ref, got_ref, o_ref, wire_ref):
        s = g_ref[...] + got_ref[...]
        wire_ref[...] = s.astype(wire)

        @pl.when(pl.program_id(0) == place_ref[1])
        def _():
            o_ref[...] = s

    blocks = pl.BlockSpec((None, br, bc), lambda j, place_ref: (j, 0, 0))
    return pl.pallas_call(
        body, name="reduce_add_halves_" + name,
        grid_spec=pltpu.PrefetchScalarGridSpec(
            num_scalar_prefetch=1, grid=(N_CHIPS,),
            in_specs=[pl.BlockSpec((br, bc), lambda j, place_ref: cut.block_index(j, place_ref[0])), blocks],
            out_specs=[pl.BlockSpec((br, bc), lambda j, place_ref: (0, 0)), blocks]),
        out_shape=[jax.ShapeDtypeStruct((br, bc), f32), jax.ShapeDtypeStruct((N_CHIPS, br, bc), wire)],
        compiler_params=_params(),
    )(place, g, got)


def _reduce_tail(late, early):
    late_names, early_names = tuple(late), tuple(early)
    names = early_names + late_names
    nl, ne, n = len(late_names), len(early_names), len(names)
    cuts = [CUTS[name] for name in names]
    is_big = [name in BIG for name in names]
    share_base, share_sem = 7 * nl, []
    for i in range(n):
        share_sem.append(share_base)
        share_base += 1 if is_big[i] else 7
    n_in = nl + 2 * ne

    def body(*refs):
        g_refs = refs[:nl]
        sums_refs, got_refs = refs[nl:n_in:2], refs[nl + 1:n_in:2]
        out_refs = refs[n_in:n_in + n]
        send_sems, recv_sems, local_sems = refs[n_in + n:n_in + n + 3]
        vm = refs[n_in + n + 3:]
        own, recv_a, wire, recv_b = vm[0:4 * nl:4], vm[1:4 * nl:4], vm[2:4 * nl:4], vm[3:4 * nl:4]
        acc, gotv = vm[4 * nl::2], vm[4 * nl + 1::2]
        x, y, c, me, others, other_ids = _place()
        sibling = (x, y, 1 - c)

        def mine(i, half):
            return cuts[i].half_of_shard(out_refs[i], half) if is_big[i] else cuts[i].block(out_refs[i], me, half)

        def reduced(i):
            return acc[i] if i < ne else own[i - ne].at[me]

        sent, stores = [], []

        def share(i):
            store = pltpu.make_async_copy(reduced(i), mine(i, c), local_sems.at[4 * nl + 2 * ne + i])
            store.start()
            stores.append(store)
            to = [sibling] if is_big[i] else [sibling] + [(*chip, c) for chip in others]
            for which, device in enumerate(to):
                cp = _remote(reduced(i), mine(i, c), send_sems, recv_sems, share_sem[i] + which, device)
                cp.start()
                sent.append(cp)

        early_loads = []
        for e in range(ne):
            early_loads.append(pltpu.make_async_copy(sums_refs[e], acc[e], local_sems.at[4 * nl + 2 * e]))
            early_loads.append(pltpu.make_async_copy(got_refs[e], gotv[e], local_sems.at[4 * nl + 2 * e + 1]))
        late_loads = [pltpu.make_async_copy(cuts[ne + l].block(g_refs[l], j, c), own[l].at[j], local_sems.at[4 * l + j])
                      for l in range(nl) for j in range(N_CHIPS)]
        halves = [_remote(cuts[ne + l].block(g_refs[l], j, 1 - c), recv_a[l].at[j], send_sems, recv_sems, 4 * l + j, sibling)
                  for l in range(nl) for j in range(N_CHIPS)]
        for cp in early_loads + late_loads + halves:
            cp.start()

        for cp in early_loads:
            cp.wait()
        for e in range(ne):
            acc[e][...] = ((acc[e][...] + gotv[e][0].astype(f32)) + gotv[e][1].astype(f32)) + gotv[e][2].astype(f32)
            share(e)

        for cp in late_loads:
            cp.wait()
        for cp in halves:
            cp.wait_recv()
        for l in range(nl):
            for j in range(N_CHIPS):
                s = own[l][j] + recv_a[l][j]
                own[l][j] = s
                wire[l][j] = s.astype(wire[l].dtype)
        chips = [_remote(wire[l].at[other_ids[k]], recv_b[l].at[k], send_sems, recv_sems, 4 * nl + 3 * l + k, (*chip, c))
                 for l in range(nl) for k, chip in enumerate(others)]
        for cp in chips:
            cp.start()
        for cp in chips:
            cp.wait_recv()
        for l in range(nl):
            mine_l = own[l].at[me]
            mine_l[...] = ((mine_l[...] + recv_b[l][0].astype(f32)) + recv_b[l][1].astype(f32)) + recv_b[l][2].astype(f32)
            share(ne + l)

        for i in range(n):
            if not is_big[i]:
                for k, chip in enumerate(others):
                    landed = cuts[i].block(out_refs[i], other_ids[k], c)
                    _remote(landed, landed, send_sems, recv_sems, share_sem[i] + 1 + k, (*chip, c)).wait_recv()
                    cp = _remote(landed, landed, send_sems, recv_sems, share_sem[i] + 4 + k, sibling)
                    cp.start()
                    sent.append(cp)
        for i in range(n):
            theirs = mine(i, 1 - c)
            _remote(theirs, theirs, send_sems, recv_sems, share_sem[i], sibling).wait_recv()
            if not is_big[i]:
                for k in range(3):
                    passed = cuts[i].block(out_refs[i], other_ids[k], 1 - c)
                    _remote(passed, passed, send_sems, recv_sems, share_sem[i] + 4 + k, sibling).wait_recv()
        for cp in halves + chips + sent:
            cp.wait_send()
        for cp in stores:
            cp.wait()

    scratch = [pltpu.SemaphoreType.DMA((share_base,)), pltpu.SemaphoreType.DMA((share_base,)),
               pltpu.SemaphoreType.DMA((4 * nl + 2 * ne + n,))]
    for name in late_names:
        block = CUTS[name].block_shape
        scratch += [pltpu.VMEM((N_CHIPS, *block), f32), pltpu.VMEM((N_CHIPS, *block), f32),
                    pltpu.VMEM((N_CHIPS, *block), _wire_dtype(name)), pltpu.VMEM((3, *block), _wire_dtype(name))]
    for name in early_names:
        block = CUTS[name].block_shape
        scratch += [pltpu.VMEM(block, f32), pltpu.VMEM((3, *block), _wire_dtype(name))]
    args = [late[name] for name in late_names] + [a for name in early_names for a in early[name]]
    outs = pl.pallas_call(
        body, name="reduce_tail",
        in_specs=[ANY] * n_in, out_specs=[ANY] * n,
        out_shape=[jax.ShapeDtypeStruct(CUTS[name].shard_shape if name in BIG else (CUTS[name].rows, CUTS[name].cols), f32)
                   for name in names],
        scratch_shapes=scratch, compiler_params=_params(),
    )(*args)
    return dict(zip(names, outs))


def _adamw_refs(w_ref, g_ref, m_ref, v_ref, d_ref, nm_ref, nv_ref):
    g = g_ref[...]
    nm = ADAM_B1 * m_ref[...] + (1.0 - ADAM_B1) * g
    nv = ADAM_B2 * v_ref[...] + (1.0 - ADAM_B2) * jnp.square(g)
    m_hat = nm / (1.0 - ADAM_B1 ** ADAM_STEP)
    v_hat = nv / (1.0 - ADAM_B2 ** ADAM_STEP)
    d_ref[...] = -ADAM_LR * (m_hat / (jnp.sqrt(v_hat) + ADAM_EPS) + ADAM_WD * w_ref[...])
    nm_ref[...] = nm
    nv_ref[...] = nv


def _adamw_small(ws, gs, ms, vs):
    n = len(ws)

    def body(*refs):
        for i in range(n):
            _adamw_refs(*[refs[k * n + i] for k in range(7)])

    whole = pl.BlockSpec(memory_space=pltpu.VMEM)
    outs = pl.pallas_call(
        body, name="adamw_small",
        in_specs=[whole] * (4 * n), out_specs=[whole] * (3 * n),
        out_shape=[jax.ShapeDtypeStruct(w.shape, f32) for w in ws] * 3,
        compiler_params=_params(),
    )(*ws, *gs, *ms, *vs)
    return outs[:n], outs[n:2 * n], outs[2 * n:]


def _adamw(name, w, g, m, v):
    rows, cols = w.shape
    rt = rows // 4

    def body(w_ref, g_ref, m_ref, v_ref, d_ref, nm_ref, nv_ref):
        _adamw_refs(w_ref, g_ref, m_ref, v_ref, d_ref, nm_ref, nv_ref)

    spec = pl.BlockSpec((rt, cols), lambda i: (i, 0))
    shape = jax.ShapeDtypeStruct((rows, cols), f32)
    return pl.pallas_call(
        body, name="adamw_" + name, grid=(rows // rt,),
        in_specs=[spec] * 4, out_specs=[spec] * 3, out_shape=[shape] * 3,
        compiler_params=_params(),
    )(w, g, m, v)


VEC_NAMES = ("pool_scale", "sgu_ln_g", "sgu_ln_b", "sgu_b", "ln1_g", "ln1_b", "ln2_g", "ln2_b")
WEIGHT_ORDER = ("w_in", "pool_w", "pool_scale", "sgu_ln_g", "sgu_ln_b", "sgu_w", "sgu_b", "w_out", "ln1_g", "ln1_b",
                "w_gate_up", "w_down", "ln2_g", "ln2_b")


def _pack_vecs(parts, extra=None):
    rows = [parts[name].reshape(-1, GROUP) for name in VEC_NAMES]
    if extra is not None:
        rows.append(extra.reshape(-1, GROUP))
    used = sum(r.shape[0] for r in rows)
    return jnp.concatenate(rows + [jnp.zeros((VEC_ROWS - used, GROUP), f32)], axis=0)


def _unpack_vecs(packed, shapes):
    out, at = {}, 0
    for name in VEC_NAMES:
        rows = math.prod(shapes[name]) // GROUP
        out[name] = packed[at:at + rows].reshape(shapes[name])
        at += rows
    return out, packed[at:]


def kernel(x, w_in, pool_w, pool_scale, sgu_ln_g, sgu_ln_b, sgu_w, sgu_b, w_out, ln1_g, ln1_b, w_gate_up, w_down, ln2_g, ln2_b, loss_target, m_w_in, m_pool_w, m_pool_scale, m_sgu_ln_g, m_sgu_ln_b, m_sgu_w, m_sgu_b, m_w_out, m_ln1_g, m_ln1_b, m_w_gate_up, m_w_down, m_ln2_g, m_ln2_b, v_w_in, v_pool_w, v_pool_scale, v_sgu_ln_g, v_sgu_ln_b, v_sgu_w, v_sgu_b, v_w_out, v_ln1_g, v_ln1_b, v_w_gate_up, v_w_down, v_ln2_g, v_ln2_b):
    given = dict(locals())
    batch, seq, _ = x.shape
    tokens = batch * seq
    tile = min(TOKEN_TILE, seq)
    ffn_bwd_tile = min(FFN_BWD_TILE, seq)
    wtile = min(WGRAD_TILE, tokens)
    shapes = {name: given[name].shape for name in WEIGHT_ORDER}

    x2 = x.reshape(tokens, D_MODEL)
    target = loss_target.reshape(tokens, D_MODEL)
    small = (pool_w[0], pool_scale[0][None], sgu_ln_g[0][None], sgu_ln_b[0][None], sgu_w[0],
             jnp.broadcast_to(sgu_b[0][:, :, None], (N_GROUPS, GROUP, GROUP)))
    g1, b1, g2, b2 = ln1_g[0][None], ln1_b[0][None], ln2_g[0][None], ln2_b[0][None]
    shard_b = {name: given[name][0].astype(bf16) for name in BIG}
    shard_b["w_gate_up_top"] = shard_b["w_gate_up"][:TOP_ROWS]
    shard_b["w_gate_up_bottom"] = shard_b["w_gate_up"][TOP_ROWS:]
    place = jnp.stack([lax.axis_index("c"), 2 * lax.axis_index("x") + lax.axis_index("y")]).astype(jnp.int32)

    def gather(*names):
        return [_GatherJob({name: shard_b[name] for name in names})]

    def halves_summed(name, grad, got):
        return _add_halves(name, grad, got, place)

    (x_b,), (got,) = _cast_x(x2, 2 * tile, gather("w_in"))
    w_in_b = got["w_in"]
    (xp_all, pre_b), (got,) = _fwd_proj(x_b, w_in_b, tile, gather("w_out", "w_gate_up_top"))
    w_out_b, w_top = got["w_out"], got["w_gate_up_top"]
    (mix_b, r1, h_b), (got,) = _fwd_mix(xp_all, pre_b, x2, small, w_out_b, g1, b1, tile, seq, gather("w_gate_up_bottom"))
    w_bottom = got["w_gate_up_bottom"]
    (gu_b, a_b), (got,) = _fwd_gate_up(h_b, w_top, w_bottom, tile, gather("w_down"))
    w_dn_b = got["w_down"]
    dr2, dr2_b, stats2 = _fwd_down_loss(a_b, w_dn_b, r1, target, g1, b1, g2, b2, tile)

    early = {}
    g_down, _ = _wgrad(a_b, dr2_b, D_MODEL, wtile, "wgrad_down")
    (dgu_b,), (got,) = _bwd_gate_up(dr2_b, gu_b, w_dn_b, ffn_bwd_tile, [_SwapHalvesJob({"w_down": g_down})])
    sums_down = halves_summed("w_down", g_down, got["w_down"])
    g_gu, (got,) = _wgrad(h_b, dgu_b, D_FF, wtile, "wgrad_gate_up", [_SwapChipsJob({"w_down": sums_down[1]})])
    early["w_down"] = (sums_down[0], got["w_down"])
    (dr1, dr1_b, stats1), (got,) = _bwd_ffn_in(dgu_b, w_top, w_bottom, dr2, r1, g1, b1, tile,
                                               [_SwapHalvesJob({"w_gate_up": g_gu})])
    sums_gu = halves_summed("w_gate_up", g_gu, got["w_gate_up"])
    g_out, _ = _wgrad(mix_b, dr1_b, D_MODEL, wtile, "wgrad_out")
    (dpool, dpre_b, g_pool_w, g_sgu_w, g_sgu_b, vecs), (got_gu, got_out) = _bwd_mix(
        dr1_b, xp_all, pre_b, small, w_out_b, tile, seq,
        [_SwapChipsJob({"w_gate_up": sums_gu[1]}), _SwapHalvesJob({"w_out": g_out})])
    early["w_gate_up"] = (sums_gu[0], got_gu["w_gate_up"])
    sums_out = halves_summed("w_out", g_out, got_out["w_out"])
    grad_x, dxp_b = _bwd_in(dpool, dpre_b, dr1, w_in_b, tile, seq)
    g_in, (got,) = _wgrad_in(x_b, dxp_b, dpre_b, wtile, [_SwapChipsJob({"w_out": sums_out[1]})])
    early["w_out"] = (sums_out[0], got["w_out"])

    late = {
        "w_in": g_in,
        "pool_w": g_pool_w.reshape(SQUARE_ROWS, GROUP),
        "sgu_w": g_sgu_w.reshape(SQUARE_ROWS, GROUP),
        "vecs": _pack_vecs({"pool_scale": vecs[0], "sgu_ln_g": vecs[1], "sgu_ln_b": vecs[2], "sgu_b": g_sgu_b[:, :, 0],
                            "ln1_g": stats1[0], "ln1_b": stats1[1], "ln2_g": stats2[0], "ln2_b": stats2[1]},
                           extra=stats2[2]),
    }
    shared = _reduce_tail(late, early)

    grad, delta, new_m, new_v = {}, {}, {}, {}
    for name in BIG:
        grad[name] = shared[name][None]
        d, nm, nv = _adamw(name, given[name][0], shared[name], given["m_" + name][0], given["v_" + name][0])
        delta[name], new_m[name], new_v[name] = d[None], nm[None], nv[None]
    vec_grads, after = _unpack_vecs(shared["vecs"], shapes)
    grad.update(vec_grads)
    for name in ("pool_w", "sgu_w"):
        grad[name] = shared[name].reshape(shapes[name])
    small_names = ("pool_w", "sgu_w") + VEC_NAMES
    state = {pre: [given[pre + name] for name in small_names] for pre in ("", "m_", "v_")}
    ds, nms, nvs = _adamw_small(state[""], [grad[name] for name in small_names], state["m_"], state["v_"])
    delta.update(zip(small_names, ds))
    new_m.update(zip(small_names, nms))
    new_v.update(zip(small_names, nvs))

    sq_err = after[:LOSS_ROWS]
    loss = jnp.sum(sq_err) * (0.5 / D_MODEL)
    return (loss, grad_x.reshape(x.shape), *[grad[name] for name in WEIGHT_ORDER],
            *[delta[name] for name in WEIGHT_ORDER], *[new_m[name] for name in WEIGHT_ORDER],
            *[new_v[name] for name in WEIGHT_ORDER])
```

```python
import math

import jax
import jax.numpy as jnp
from jax import lax
from jax.experimental import pallas as pl
from jax.experimental.pallas import tpu as pltpu

f32 = jnp.float32
bf16 = jnp.bfloat16
MESH = pl.DeviceIdType.MESH

D_MODEL = 1024
POOL_WIDTH = 512
SGU_WIDTH = 512
IN_COLS = POOL_WIDTH + 2 * SGU_WIDTH
D_FF = 2816
POOL_WINDOWS = (2, 4, 8, 16)
GROUP = 128
N_GROUPS = 4
HALO = 16
LN_EPS = 1e-5
ALPHA = float(2.0 ** 0.25)
N_CHIPS = 4

ADAM_LR = 0.001
ADAM_B1 = 0.9
ADAM_B2 = 0.999
ADAM_EPS = 1e-08
ADAM_WD = 0.01
ADAM_STEP = 10

TOKEN_TILE = 512
FFN_BWD_TILE = 512
FF_CHUNK = 256
ROW_SUB = 256
WGRAD_TILE = 1024
TOP_ROWS = 256
V7X_VMEM_LIMIT = 56 * 1024 * 1024

SQUARE_ROWS = N_GROUPS * GROUP
VEC_ROWS = 64
LOSS_ROWS = D_MODEL // GROUP


def _params(**kw):
    return pltpu.CompilerParams(vmem_limit_bytes=V7X_VMEM_LIMIT, **kw)


def _mm(a, b):
    return jnp.dot(a, b, preferred_element_type=f32)


def _mm_nt(a, b):
    return lax.dot_general(a, b, (((1,), (1,)), ((), ())), preferred_element_type=f32)


def _mm_tn(a, b):
    return lax.dot_general(a, b, (((0,), (0,)), ((), ())), preferred_element_type=f32)


def _ln_fwd(r, g, b):
    mu = jnp.mean(r, axis=-1, keepdims=True)
    xc = r - mu
    var = jnp.mean(xc * xc, axis=-1, keepdims=True)
    rstd = lax.rsqrt(var + LN_EPS)
    xhat = xc * rstd
    return xhat * g + b, xhat, rstd


def _ln_bwd(dout, xhat, rstd, g):
    dxhat = dout * g
    m1 = jnp.mean(dxhat, axis=-1, keepdims=True)
    m2 = jnp.mean(dxhat * xhat, axis=-1, keepdims=True)
    return rstd * (dxhat - m1 - xhat * m2)


def _col_sum(a):
    return jnp.sum(a, axis=0, keepdims=True)


def _gelu_parts(z):
    cdf = 0.5 * (1.0 + lax.erf(z * (1.0 / math.sqrt(2.0))))
    pdf = jnp.exp(-0.5 * z * z) * (1.0 / math.sqrt(2.0 * math.pi))
    return cdf, pdf


def _inv_counts(seq_tile, rows):
    pos = seq_tile * rows + lax.broadcasted_iota(jnp.int32, (rows, GROUP), 0) + 1
    return [1.0 / jnp.minimum(pos, w).astype(f32) for w in POOL_WINDOWS]


def _window_sums(e, back):
    n = e.shape[0]

    def shifted(a, s):
        return pltpu.roll(a, s if back else n - s, 0)

    s2 = e + shifted(e, 1)
    s4 = s2[:, GROUP:] + shifted(s2[:, GROUP:], 2)
    s8 = s4[:, GROUP:] + shifted(s4[:, GROUP:], 4)
    s16 = s8[:, GROUP:] + shifted(s8[:, GROUP:], 8)
    return [s2[:, :GROUP], s4[:, :GROUP], s8[:, :GROUP], s16]


def _pooled_groups(xp, halo, inv):
    sums = _window_sums(jnp.concatenate([halo, xp], axis=0), back=True)
    return [sums[g][HALO:] * inv[g] - xp[:, g * GROUP:(g + 1) * GROUP] for g in range(N_GROUPS)]


def _tril_mask():
    r = lax.broadcasted_iota(jnp.int32, (GROUP, GROUP), 0)
    c = lax.broadcasted_iota(jnp.int32, (GROUP, GROUP), 1)
    return (r >= c).astype(f32)


def _gs(g):
    return slice(g * GROUP, (g + 1) * GROUP)


class _Cut:
    def __init__(self, rows, cols, by_cols):
        self.rows, self.cols, self.by_cols = rows, cols, by_cols
        if by_cols:
            self.block_shape = (rows // 2, cols // N_CHIPS)
            self.shard_shape = (rows, cols // N_CHIPS)
        else:
            self.block_shape = (rows // (2 * N_CHIPS), cols)
            self.shard_shape = (rows // N_CHIPS, cols)

    def block(self, ref, chip, half):
        br, bc = self.block_shape
        if self.by_cols:
            return ref.at[pl.ds(pl.multiple_of(half * br, 16), br), pl.ds(pl.multiple_of(chip * bc, 128), bc)]
        return ref.at[pl.ds(pl.multiple_of((2 * chip + half) * br, 8), br), :]

    def shard(self, ref, chip):
        sr, sc = self.shard_shape
        if self.by_cols:
            return ref.at[:, pl.ds(pl.multiple_of(chip * sc, 128), sc)]
        return ref.at[pl.ds(pl.multiple_of(chip * sr, 16), sr), :]

    def half_of_shard(self, ref, half):
        br = self.block_shape[0]
        return ref.at[pl.ds(pl.multiple_of(half * br, 8), br), :]

    def block_index(self, chip, half):
        return (half, chip) if self.by_cols else (2 * chip + half, 0)


CUTS = {
    "w_in": _Cut(D_MODEL, IN_COLS, True),
    "w_out": _Cut(D_MODEL, D_MODEL, False),
    "w_gate_up": _Cut(D_MODEL, 2 * D_FF, True),
    "w_down": _Cut(D_FF, D_MODEL, False),
    "w_gate_up_top": _Cut(TOP_ROWS, 2 * D_FF, True),
    "w_gate_up_bottom": _Cut(D_MODEL - TOP_ROWS, 2 * D_FF, True),
    "pool_w": _Cut(SQUARE_ROWS, GROUP, False),
    "sgu_w": _Cut(SQUARE_ROWS, GROUP, False),
    "vecs": _Cut(VEC_ROWS, GROUP, False),
}
BIG = ("w_in", "w_out", "w_gate_up", "w_down")
ANY = pl.BlockSpec(memory_space=pl.ANY)


def _wire_dtype(name):
    return bf16 if name in BIG else f32


def _place():
    x, y, c = lax.axis_index("x"), lax.axis_index("y"), lax.axis_index("c")
    others = [(1 - x, y), (x, 1 - y), (1 - x, 1 - y)]
    return x, y, c, 2 * x + y, others, [2 * ox + oy for ox, oy in others]


def _remote(src, dst, send_sems, recv_sems, k, to):
    return pltpu.make_async_remote_copy(src_ref=src, dst_ref=dst, send_sem=send_sems.at[k], recv_sem=recv_sems.at[k],
                                        device_id=to, device_id_type=MESH)


class _GatherJob:
    def __init__(self, shards):
        self.names = tuple(shards)
        self.arrays = tuple(shards.values())
        n = len(self.names)
        self.out_shapes = [jax.ShapeDtypeStruct((CUTS[name].rows, CUTS[name].cols), bf16) for name in self.names]
        self.scratch_shapes = ([pltpu.SemaphoreType.DMA((6 * n,)), pltpu.SemaphoreType.DMA((6 * n,)),
                                pltpu.SemaphoreType.DMA((2 * n,))]
                               + [pltpu.VMEM(CUTS[name].shard_shape, bf16) for name in self.names])

    def bind(self, shard_refs, full_refs, scratch):
        self.shards, self.full = shard_refs, full_refs
        self.send_sems, self.recv_sems, self.local_sems = scratch[:3]
        self.stages = scratch[3:]
        return self

    def _sends(self):
        _, _, c, me, others, _ = _place()
        return [_remote(CUTS[name].half_of_shard(self.shards[w], c), CUTS[name].block(self.full[w], me, c),
                        self.send_sems, self.recv_sems, 3 * w + k, (*chip, c))
                for w, name in enumerate(self.names) for k, chip in enumerate(others)]

    def _relays(self, half):
        x, y, c, _, _, other_ids = _place()
        n = len(self.names)
        return [_remote(CUTS[name].block(self.full[w], other_ids[k], half),
                        CUTS[name].block(self.full[w], other_ids[k], half),
                        self.send_sems, self.recv_sems, 3 * n + 3 * w + k, (x, y, 1 - c))
                for w, name in enumerate(self.names) for k in range(3)]

    def _stores(self):
        me = _place()[3]
        return [pltpu.make_async_copy(self.stages[w], CUTS[name].shard(self.full[w], me), self.local_sems.at[2 * w + 1])
                for w, name in enumerate(self.names)]

    def start(self):
        loads = [pltpu.make_async_copy(self.shards[w], self.stages[w], self.local_sems.at[2 * w])
                 for w in range(len(self.names))]
        for cp in loads:
            cp.start()
        for cp in self._sends():
            cp.start()
        for load, store in zip(loads, self._stores()):
            load.wait()
            store.start()

    def relay(self):
        _, _, c, _, others, other_ids = _place()
        relays = self._relays(c)
        for w, name in enumerate(self.names):
            for k, chip in enumerate(others):
                landed = CUTS[name].block(self.full[w], other_ids[k], c)
                _remote(landed, landed, self.send_sems, self.recv_sems, 3 * w + k, (*chip, c)).wait_recv()
                relays[3 * w + k].start()

    def finish(self):
        c = _place()[2]
        for cp in self._relays(1 - c):
            cp.wait_recv()
        for cp in self._sends() + self._relays(c):
            cp.wait_send()
        for cp in self._stores():
            cp.wait()


class _SwapHalvesJob:
    def __init__(self, grads):
        self.names = tuple(grads)
        self.arrays = tuple(grads.values())
        n = len(self.names)
        self.out_shapes = [jax.ShapeDtypeStruct((N_CHIPS, *CUTS[name].block_shape), f32) for name in self.names]
        self.scratch_shapes = [pltpu.SemaphoreType.DMA((N_CHIPS * n,)), pltpu.SemaphoreType.DMA((N_CHIPS * n,))]

    def bind(self, g_refs, got_refs, scratch):
        self.g_refs, self.got_refs = g_refs, got_refs
        self.send_sems, self.recv_sems = scratch
        return self

    def _copies(self):
        x, y, c, _, _, _ = _place()
        return [_remote(CUTS[name].block(self.g_refs[a], j, 1 - c), self.got_refs[a].at[j], self.send_sems,
                        self.recv_sems, N_CHIPS * a + j, (x, y, 1 - c))
                for a, name in enumerate(self.names) for j in range(N_CHIPS)]

    def start(self):
        for cp in self._copies():
            cp.start()

    def finish(self):
        for cp in self._copies():
            cp.wait()


class _SwapChipsJob:
    def __init__(self, partials):
        self.names = tuple(partials)
        self.arrays = tuple(partials.values())
        n = len(self.names)
        self.out_shapes = [jax.ShapeDtypeStruct((3, *CUTS[name].block_shape), _wire_dtype(name)) for name in self.names]
        self.scratch_shapes = [pltpu.SemaphoreType.DMA((3 * n,)), pltpu.SemaphoreType.DMA((3 * n,))]

    def bind(self, p_refs, got_refs, scratch):
        self.p_refs, self.got_refs = p_refs, got_refs
        self.send_sems, self.recv_sems = scratch
        return self

    def _copies(self):
        _, _, c, _, others, other_ids = _place()
        return [_remote(self.p_refs[a].at[other_ids[k]], self.got_refs[a].at[k], self.send_sems, self.recv_sems,
                        3 * a + k, (*chip, c))
                for a in range(len(self.names)) for k, chip in enumerate(others)]

    def start(self):
        for cp in self._copies():
            cp.start()

    def finish(self):
        for cp in self._copies():
            cp.wait()


def _call(body, *, name, grid, in_specs, out_specs, out_shape, args, scratch_shapes=(), jobs=()):
    n_in, n_out, n_scr = len(in_specs), len(out_specs), len(scratch_shapes)
    j_in = [len(j.arrays) for j in jobs]
    j_out = [len(j.out_shapes) for j in jobs]
    j_scr = [len(j.scratch_shapes) for j in jobs]

    def wrapped(*refs):
        refs = list(refs)

        def take(k):
            head = refs[:k]
            del refs[:k]
            return head

        ins, jins = take(n_in), [take(k) for k in j_in]
        outs, jouts = take(n_out), [take(k) for k in j_out]
        scr, jscr = take(n_scr), [take(k) for k in j_scr]
        bound = [j.bind(a, b, c) for j, a, b, c in zip(jobs, jins, jouts, jscr)]
        relaying = [b for b in bound if hasattr(b, "relay")]
        if not bound:
            body(*ins, *outs, *scr)
            return
        first = _all([pl.program_id(d) == 0 for d in range(len(grid))])
        last = _all([pl.program_id(d) == grid[d] - 1 for d in range(len(grid))])

        @pl.when(first)
        def _():
            for b in bound:
                b.start()

        if relaying:
            @pl.when(_all([pl.program_id(d) == (3 * grid[d]) // 4 for d in range(len(grid))]))
            def _():
                for b in relaying:
                    b.relay()

        body(*ins, *outs, *scr)

        @pl.when(last)
        def _():
            for b in bound:
                b.finish()

    results = pl.pallas_call(
        wrapped, name=name, grid=grid,
        in_specs=list(in_specs) + [ANY] * sum(j_in), out_specs=list(out_specs) + [ANY] * sum(j_out),
        out_shape=list(out_shape) + [s for j in jobs for s in j.out_shapes],
        scratch_shapes=list(scratch_shapes) + [s for j in jobs for s in j.scratch_shapes],
        compiler_params=_params(),
    )(*args, *[a for j in jobs for a in j.arrays])
    results = list(results)
    own, rest = results[:n_out], results[n_out:]
    per_job = []
    for j, k in zip(jobs, j_out):
        per_job.append(dict(zip(j.names, rest[:k])))
        rest = rest[k:]
    return own, per_job


def _all(conds):
    out = conds[0]
    for c in conds[1:]:
        out = jnp.logical_and(out, c)
    return out


def _cast_x(x2, tile, jobs):
    tokens = x2.shape[0]

    def body(x_ref, xb_ref):
        xb_ref[...] = x_ref[...].astype(bf16)

    row = lambda i: (i, 0)
    return _call(
        body, name="cast_x", grid=(tokens // tile,),
        in_specs=[pl.BlockSpec((tile, D_MODEL), row)], out_specs=[pl.BlockSpec((tile, D_MODEL), row)],
        out_shape=[jax.ShapeDtypeStruct((tokens, D_MODEL), bf16)], args=(x2,), jobs=jobs)


def _fwd_proj(x_b, w_in_b, tile, jobs):
    tokens = x_b.shape[0]

    def body(x_ref, w_ref, xp_ref, pre_ref):
        proj = _mm(x_ref[...], w_ref[...])
        xp_ref[...] = proj[:, :POOL_WIDTH]
        pre_ref[...] = proj[:, POOL_WIDTH:].astype(bf16)

    row = lambda i: (i, 0)
    return _call(
        body, name="fwd_proj", grid=(tokens // tile,),
        in_specs=[pl.BlockSpec((tile, D_MODEL), row), pl.BlockSpec((D_MODEL, IN_COLS), lambda i: (0, 0))],
        out_specs=[pl.BlockSpec((tile, POOL_WIDTH), row), pl.BlockSpec((tile, 2 * SGU_WIDTH), row)],
        out_shape=[jax.ShapeDtypeStruct((tokens, POOL_WIDTH), f32), jax.ShapeDtypeStruct((tokens, 2 * SGU_WIDTH), bf16)],
        args=(x_b, w_in_b), jobs=jobs)


def _small_specs():
    return [pl.BlockSpec((N_GROUPS, GROUP, GROUP), lambda i: (0, 0, 0)),
            pl.BlockSpec((1, POOL_WIDTH), lambda i: (0, 0)),
            pl.BlockSpec((1, SGU_WIDTH), lambda i: (0, 0)),
            pl.BlockSpec((1, SGU_WIDTH), lambda i: (0, 0)),
            pl.BlockSpec((N_GROUPS, GROUP, GROUP), lambda i: (0, 0, 0)),
            pl.BlockSpec((N_GROUPS, GROUP, GROUP), lambda i: (0, 0, 0))]


def _fwd_mix(xp_all, pre_b, x2, small, w_out_b, ln1_g, ln1_b, tile, seq, jobs):
    tokens = x2.shape[0]
    tps = seq // tile
    hb = tile // HALO

    def body(xp_ref, halo_ref, pre_ref, x_ref, pw_ref, ps_ref, lg_ref, lb_ref, sw_ref, sb_ref, wout_ref, g1_ref, b1_ref,
             mix_ref, r1_ref, h_ref):
        seq_tile = pl.program_id(0) % tps
        xp = xp_ref[...]
        halo = jnp.where(seq_tile == 0, 0.0, halo_ref[...])
        pooled = _pooled_groups(xp, halo, _inv_counts(seq_tile, tile))
        for g in range(N_GROUPS):
            po = _mm(pooled[g].astype(bf16), pw_ref[g].astype(bf16)) * ps_ref[:, _gs(g)]
            mix_ref[:, _gs(g)] = po.astype(bf16)

        pre = pre_ref[...].astype(f32)
        cdf, _ = _gelu_parts(pre)
        zg = pre * cdf
        u = zg[:, :SGU_WIDTH]
        vln, _, _ = _ln_fwd(zg[:, SGU_WIDTH:], lg_ref[...], lb_ref[...])
        vb = vln.astype(bf16)
        mask = _tril_mask()
        for h in range(N_GROUPS):
            wm = (sw_ref[h] * mask).astype(bf16)
            bias = sb_ref[h]
            for c in range(tile // GROUP):
                rows = slice(c * GROUP, (c + 1) * GROUP)
                mixed = _mm(wm, vb[rows, _gs(h)]) + bias
                mix_ref[rows, POOL_WIDTH + h * GROUP:POOL_WIDTH + (h + 1) * GROUP] = (u[rows, _gs(h)] * mixed).astype(bf16)

        r1 = ALPHA * x_ref[...] + _mm(mix_ref[...], wout_ref[...])
        r1_ref[...] = r1
        h1, _, _ = _ln_fwd(r1, g1_ref[...], b1_ref[...])
        h_ref[...] = h1.astype(bf16)

    row = lambda i: (i, 0)
    vec = pl.BlockSpec((1, D_MODEL), lambda i: (0, 0))
    return _call(
        body, name="fwd_mix", grid=(tokens // tile,),
        in_specs=[pl.BlockSpec((tile, POOL_WIDTH), row),
                  pl.BlockSpec((HALO, POOL_WIDTH), lambda i: (jnp.maximum(i * hb - 1, 0), 0)),
                  pl.BlockSpec((tile, 2 * SGU_WIDTH), row),
                  pl.BlockSpec((tile, D_MODEL), row)] + _small_specs()
                 + [pl.BlockSpec((D_MODEL, D_MODEL), lambda i: (0, 0)), vec, vec],
        out_specs=[pl.BlockSpec((tile, D_MODEL), row)] * 3,
        out_shape=[jax.ShapeDtypeStruct((tokens, D_MODEL), bf16),
                   jax.ShapeDtypeStruct((tokens, D_MODEL), f32),
                   jax.ShapeDtypeStruct((tokens, D_MODEL), bf16)],
        args=(xp_all, xp_all, pre_b, x2, *small, w_out_b, ln1_g, ln1_b), jobs=jobs)


def _fwd_gate_up(h_b, w_top, w_bottom, tile, jobs):
    tokens = h_b.shape[0]
    def body(h_ref, wt_ref, wb_ref, gu_ref, a_ref):
        ht, hb = h_ref[:, :TOP_ROWS], h_ref[:, TOP_ROWS:]
        for c in range(D_FF // FF_CHUNK):
            gcols = slice(c * FF_CHUNK, (c + 1) * FF_CHUNK)
            ucols = slice(D_FF + c * FF_CHUNK, D_FF + (c + 1) * FF_CHUNK)
            gate = _mm(ht, wt_ref[:, gcols]) + _mm(hb, wb_ref[:, gcols])
            up = _mm(ht, wt_ref[:, ucols]) + _mm(hb, wb_ref[:, ucols])
            sg = jax.nn.sigmoid(gate)
            silu = gate * sg
            gu_ref[:, gcols] = (up * (sg + silu * (1.0 - sg))).astype(bf16)
            gu_ref[:, ucols] = silu.astype(bf16)
            a_ref[:, gcols] = (silu * up).astype(bf16)

    return _call(
        body, name="fwd_gate_up", grid=(tokens // tile,),
        in_specs=[pl.BlockSpec((tile, D_MODEL), lambda i: (i, 0)),
                  pl.BlockSpec((TOP_ROWS, 2 * D_FF), lambda i: (0, 0), pipeline_mode=pl.Buffered(1)),
                  pl.BlockSpec((D_MODEL - TOP_ROWS, 2 * D_FF), lambda i: (0, 0), pipeline_mode=pl.Buffered(1))],
        out_specs=[pl.BlockSpec((tile, 2 * D_FF), lambda i: (i, 0)),
                   pl.BlockSpec((tile, D_FF), lambda i: (i, 0))],
        out_shape=[jax.ShapeDtypeStruct((tokens, 2 * D_FF), bf16),
                   jax.ShapeDtypeStruct((tokens, D_FF), bf16)],
        args=(h_b, w_top, w_bottom), jobs=jobs)


def _fwd_down_loss(a_b, w_dn_b, r1, target, ln1_g, ln1_b, ln2_g, ln2_b, tile):
    tokens = a_b.shape[0]

    def body(a_ref, w_ref, r1_ref, t_ref, g1_ref, b1_ref, g2_ref, b2_ref, dr2_ref, dr2b_ref, st_ref):
        @pl.when(pl.program_id(0) == 0)
        def _():
            st_ref[...] = jnp.zeros_like(st_ref)

        sub = min(ROW_SUB, tile)
        for s in range(tile // sub):
            rows = slice(s * sub, (s + 1) * sub)
            h1, _, _ = _ln_fwd(r1_ref[rows, :], g1_ref[...], b1_ref[...])
            r2 = ALPHA * h1 + _mm(a_ref[rows, :], w_ref[...])
            y, xhat, rstd = _ln_fwd(r2, g2_ref[...], b2_ref[...])
            diff = y - t_ref[rows, :]
            dy = diff * (1.0 / D_MODEL)
            st_ref[0:1, :] += _col_sum(dy * xhat)
            st_ref[1:2, :] += _col_sum(dy)
            st_ref[2:3, :] += _col_sum(diff * diff)
            dr2 = _ln_bwd(dy, xhat, rstd, g2_ref[...])
            dr2_ref[rows, :] = dr2
            dr2b_ref[rows, :] = dr2.astype(bf16)

    row = lambda i: (i, 0)
    vec = pl.BlockSpec((1, D_MODEL), lambda i: (0, 0))
    return _call(
        body, name="fwd_down_loss", grid=(tokens // tile,),
        in_specs=[pl.BlockSpec((tile, D_FF), row), pl.BlockSpec((D_FF, D_MODEL), lambda i: (0, 0)),
                  pl.BlockSpec((tile, D_MODEL), row), pl.BlockSpec((tile, D_MODEL), row), vec, vec, vec, vec],
        out_specs=[pl.BlockSpec((tile, D_MODEL), row), pl.BlockSpec((tile, D_MODEL), row),
                   pl.BlockSpec((8, D_MODEL), lambda i: (0, 0))],
        out_shape=[jax.ShapeDtypeStruct((tokens, D_MODEL), f32), jax.ShapeDtypeStruct((tokens, D_MODEL), bf16),
                   jax.ShapeDtypeStruct((8, D_MODEL), f32)],
        args=(a_b, w_dn_b, r1, target, ln1_g, ln1_b, ln2_g, ln2_b))[0]


def _bwd_gate_up(dr2, gu_b, w_dn_b, tile, jobs):
    tokens = dr2.shape[0]

    def body(d_ref, gu_ref, w_ref, dgu_ref):
        d = d_ref[...].astype(bf16)
        for c in range(D_FF // FF_CHUNK):
            gcols = slice(c * FF_CHUNK, (c + 1) * FF_CHUNK)
            ucols = slice(D_FF + c * FF_CHUNK, D_FF + (c + 1) * FF_CHUNK)
            da = _mm_nt(d, w_ref[gcols, :])
            dgu_ref[:, gcols] = (da * gu_ref[:, gcols].astype(f32)).astype(bf16)
            dgu_ref[:, ucols] = (da * gu_ref[:, ucols].astype(f32)).astype(bf16)

    return _call(
        body, name="bwd_gate_up", grid=(tokens // tile,),
        in_specs=[pl.BlockSpec((tile, D_MODEL), lambda i: (i, 0)),
                  pl.BlockSpec((tile, 2 * D_FF), lambda i: (i, 0)),
                  pl.BlockSpec((D_FF, D_MODEL), lambda i: (0, 0))],
        out_specs=[pl.BlockSpec((tile, 2 * D_FF), lambda i: (i, 0))],
        out_shape=[jax.ShapeDtypeStruct((tokens, 2 * D_FF), bf16)],
        args=(dr2, gu_b, w_dn_b), jobs=jobs)


def _bwd_ffn_in(dgu_b, w_top, w_bottom, dr2, r1, ln1_g, ln1_b, tile, jobs):
    tokens = dr2.shape[0]

    def body(dgu_ref, wt_ref, wb_ref, d_ref, r1_ref, g1_ref, b1_ref, dr1_ref, dr1b_ref, st_ref):
        @pl.when(pl.program_id(0) == 0)
        def _():
            st_ref[...] = jnp.zeros_like(st_ref)

        dgu = dgu_ref[...]
        dh = ALPHA * d_ref[...] + jnp.concatenate([_mm_nt(dgu, wt_ref[...]), _mm_nt(dgu, wb_ref[...])], axis=1)
        _, xhat, rstd = _ln_fwd(r1_ref[...], g1_ref[...], b1_ref[...])
        st_ref[0:1, :] += _col_sum(dh * xhat)
        st_ref[1:2, :] += _col_sum(dh)
        dr1 = _ln_bwd(dh, xhat, rstd, g1_ref[...])
        dr1_ref[...] = dr1
        dr1b_ref[...] = dr1.astype(bf16)

    row = lambda i: (i, 0)
    vec = pl.BlockSpec((1, D_MODEL), lambda i: (0, 0))
    return _call(
        body, name="bwd_ffn_in", grid=(tokens // tile,),
        in_specs=[pl.BlockSpec((tile, 2 * D_FF), row),
                  pl.BlockSpec((TOP_ROWS, 2 * D_FF), lambda i: (0, 0), pipeline_mode=pl.Buffered(1)),
                  pl.BlockSpec((D_MODEL - TOP_ROWS, 2 * D_FF), lambda i: (0, 0), pipeline_mode=pl.Buffered(1)),
                  pl.BlockSpec((tile, D_MODEL), row), pl.BlockSpec((tile, D_MODEL), row), vec, vec],
        out_specs=[pl.BlockSpec((tile, D_MODEL), row), pl.BlockSpec((tile, D_MODEL), row),
                   pl.BlockSpec((8, D_MODEL), lambda i: (0, 0))],
        out_shape=[jax.ShapeDtypeStruct((tokens, D_MODEL), f32), jax.ShapeDtypeStruct((tokens, D_MODEL), bf16),
                   jax.ShapeDtypeStruct((8, D_MODEL), f32)],
        args=(dgu_b, w_top, w_bottom, dr2, r1, ln1_g, ln1_b), jobs=jobs)


def _bwd_mix(dr1, xp_all, pre_b, small, w_out_b, tile, seq, jobs):
    tokens = dr1.shape[0]
    tps = seq // tile
    hb = tile // HALO
    steps = tokens // tile

    def body(dr1_ref, xp_ref, halo_ref, pre_ref, wout_ref, pw_ref, ps_ref, lg_ref, lb_ref, sw_ref, sb_ref,
             dpool_ref, dpre_ref, gpw_ref, gsw_ref, gsb_ref, vec_ref, du_ref, dvln_ref):
        step = pl.program_id(0)
        seq_tile = step % tps

        @pl.when(step == 0)
        def _():
            gpw_ref[...] = jnp.zeros_like(gpw_ref)
            gsw_ref[...] = jnp.zeros_like(gsw_ref)
            gsb_ref[...] = jnp.zeros_like(gsb_ref)
            vec_ref[...] = jnp.zeros_like(vec_ref)

        dmix = _mm_nt(dr1_ref[...].astype(bf16), wout_ref[...])

        xp = xp_ref[...]
        halo = jnp.where(seq_tile == 0, 0.0, halo_ref[...])
        pooled = _pooled_groups(xp, halo, _inv_counts(seq_tile, tile))
        for g in range(N_GROUPS):
            pb = pooled[g].astype(bf16)
            pwb = pw_ref[g].astype(bf16)
            dpo = dmix[:, _gs(g)]
            vec_ref[0:1, _gs(g)] += _col_sum(dpo * _mm(pb, pwb))
            dpo_b = (dpo * ps_ref[:, _gs(g)]).astype(bf16)
            gpw_ref[g] += _mm_tn(pb, dpo_b)
            dpool_ref[:, _gs(g)] = _mm_nt(dpo_b, pwb)

        pre = pre_ref[...].astype(f32)
        cdf, pdf = _gelu_parts(pre)
        zg = pre * cdf
        u = zg[:, :SGU_WIDTH]
        vln, vhat, rstd = _ln_fwd(zg[:, SGU_WIDTH:], lg_ref[...], lb_ref[...])
        vb = vln.astype(bf16)
        mask = _tril_mask()
        for h in range(N_GROUPS):
            wm = (sw_ref[h] * mask).astype(bf16)
            bias = sb_ref[h]
            gsw = jnp.zeros((GROUP, GROUP), f32)
            gsb = jnp.zeros((GROUP, GROUP), f32)
            for c in range(tile // GROUP):
                rows = slice(c * GROUP, (c + 1) * GROUP)
                v_ch = vb[rows, _gs(h)]
                d = dmix[rows, POOL_WIDTH + h * GROUP:POOL_WIDTH + (h + 1) * GROUP]
                du_ref[rows, _gs(h)] = d * (_mm(wm, v_ch) + bias)
                dmixed = d * u[rows, _gs(h)]
                gsb += dmixed
                dmixed_b = dmixed.astype(bf16)
                gsw += _mm_nt(dmixed_b, v_ch)
                dvln_ref[rows, _gs(h)] = _mm_tn(wm, dmixed_b)
            gsw_ref[h] += gsw * mask
            gsb_ref[h] += gsb

        dvln = dvln_ref[...]
        vec_ref[1:2, :] += _col_sum(dvln * vhat)
        vec_ref[2:3, :] += _col_sum(dvln)
        dgelu = cdf + pre * pdf
        dpre_ref[:, :SGU_WIDTH] = (du_ref[...] * dgelu[:, :SGU_WIDTH]).astype(bf16)
        dpre_ref[:, SGU_WIDTH:] = (_ln_bwd(dvln, vhat, rstd, lg_ref[...]) * dgelu[:, SGU_WIDTH:]).astype(bf16)

        @pl.when(step == steps - 1)
        def _():
            for h in range(N_GROUPS):
                gsb_ref[h] = jnp.broadcast_to(jnp.sum(gsb_ref[h], axis=1, keepdims=True), (GROUP, GROUP))

    row = lambda i: (i, 0)
    sq = jax.ShapeDtypeStruct((N_GROUPS, GROUP, GROUP), f32)
    sq_spec = pl.BlockSpec((N_GROUPS, GROUP, GROUP), lambda i: (0, 0, 0))
    return _call(
        body, name="bwd_mix", grid=(steps,),
        in_specs=[pl.BlockSpec((tile, D_MODEL), row), pl.BlockSpec((tile, POOL_WIDTH), row),
                  pl.BlockSpec((HALO, POOL_WIDTH), lambda i: (jnp.maximum(i * hb - 1, 0), 0)),
                  pl.BlockSpec((tile, 2 * SGU_WIDTH), row),
                  pl.BlockSpec((D_MODEL, D_MODEL), lambda i: (0, 0))] + _small_specs(),
        out_specs=[pl.BlockSpec((tile, POOL_WIDTH), row), pl.BlockSpec((tile, 2 * SGU_WIDTH), row),
                   sq_spec, sq_spec, sq_spec, pl.BlockSpec((8, POOL_WIDTH), lambda i: (0, 0))],
        out_shape=[jax.ShapeDtypeStruct((tokens, POOL_WIDTH), f32), jax.ShapeDtypeStruct((tokens, 2 * SGU_WIDTH), bf16),
                   sq, sq, sq, jax.ShapeDtypeStruct((8, POOL_WIDTH), f32)],
        scratch_shapes=[pltpu.VMEM((tile, SGU_WIDTH), f32), pltpu.VMEM((tile, SGU_WIDTH), f32)],
        args=(dr1, xp_all, xp_all, pre_b, w_out_b, *small), jobs=jobs)


def _bwd_in(dpool, dpre_b, dr1, w_in_b, tile, seq):
    tokens = dr1.shape[0]
    tps = seq // tile
    hb = tile // HALO
    last_halo = tokens // HALO - 1

    def body(dpool_ref, nxt_ref, dpre_ref, dr1_ref, w_ref, dx_ref, dxp_ref, dproj_ref):
        seq_tile = pl.program_id(0) % tps
        inv = _inv_counts(seq_tile, tile)
        dpl = dpool_ref[...]
        nxt = jnp.where(seq_tile == tps - 1, 0.0, nxt_ref[...])
        scaled = jnp.concatenate([dpl[:, _gs(g)] * inv[g] for g in range(N_GROUPS)], axis=1)
        scaled_nxt = jnp.concatenate([nxt[:, _gs(g)] * (1.0 / POOL_WINDOWS[g]) for g in range(N_GROUPS)], axis=1)
        sums = _window_sums(jnp.concatenate([scaled, scaled_nxt], axis=0), back=False)
        for g in range(N_GROUPS):
            dproj_ref[:, _gs(g)] = (sums[g][:tile] - dpl[:, _gs(g)]).astype(bf16)
        dxp_ref[...] = dproj_ref[:, :POOL_WIDTH]
        dproj_ref[:, POOL_WIDTH:] = dpre_ref[...]
        dx_ref[...] = ALPHA * dr1_ref[...] + _mm_nt(dproj_ref[...], w_ref[...])

    row = lambda i: (i, 0)
    return _call(
        body, name="bwd_in", grid=(tokens // tile,),
        in_specs=[pl.BlockSpec((tile, POOL_WIDTH), row),
                  pl.BlockSpec((HALO, POOL_WIDTH), lambda i: (jnp.minimum((i + 1) * hb, last_halo), 0)),
                  pl.BlockSpec((tile, 2 * SGU_WIDTH), row),
                  pl.BlockSpec((tile, D_MODEL), row),
                  pl.BlockSpec((D_MODEL, IN_COLS), lambda i: (0, 0))],
        out_specs=[pl.BlockSpec((tile, D_MODEL), row), pl.BlockSpec((tile, POOL_WIDTH), row)],
        out_shape=[jax.ShapeDtypeStruct((tokens, D_MODEL), f32), jax.ShapeDtypeStruct((tokens, POOL_WIDTH), bf16)],
        scratch_shapes=[pltpu.VMEM((tile, IN_COLS), bf16)],
        args=(dpool, dpool, dpre_b, dr1, w_in_b))[0]


def _wgrad(a, b, col_tile, tile, name, jobs=()):
    tokens, m = a.shape
    n = b.shape[1]

    def body(a_ref, b_ref, o_ref):
        @pl.when(pl.program_id(1) == 0)
        def _():
            o_ref[...] = jnp.zeros_like(o_ref)

        o_ref[...] += _mm_tn(a_ref[...].astype(bf16), b_ref[...].astype(bf16))

    (out,), got = _call(
        body, name=name, grid=(n // col_tile, tokens // tile),
        in_specs=[pl.BlockSpec((tile, m), lambda j, k: (k, 0)),
                  pl.BlockSpec((tile, col_tile), lambda j, k: (k, j))],
        out_specs=[pl.BlockSpec((m, col_tile), lambda j, k: (0, j))],
        out_shape=[jax.ShapeDtypeStruct((m, n), f32)],
        args=(a, b), jobs=jobs)
    return out, got


def _wgrad_in(x_b, dxp_b, dpre_b, tile, jobs):
    tokens = x_b.shape[0]

    def body(x_ref, dxp_ref, dpre_ref, o_ref):
        @pl.when(pl.program_id(0) == 0)
        def _():
            o_ref[...] = jnp.zeros_like(o_ref)

        xt = x_ref[...]
        o_ref[:, :POOL_WIDTH] += _mm_tn(xt, dxp_ref[...])
        o_ref[:, POOL_WIDTH:] += _mm_tn(xt, dpre_ref[...])

    row = lambda k: (k, 0)
    (out,), got = _call(
        body, name="wgrad_in", grid=(tokens // tile,),
        in_specs=[pl.BlockSpec((tile, D_MODEL), row), pl.BlockSpec((tile, POOL_WIDTH), row),
                  pl.BlockSpec((tile, 2 * SGU_WIDTH), row)],
        out_specs=[pl.BlockSpec((D_MODEL, IN_COLS), lambda k: (0, 0))],
        out_shape=[jax.ShapeDtypeStruct((D_MODEL, IN_COLS), f32)],
        args=(x_b, dxp_b, dpre_b), jobs=jobs)
    return out, got


def _add_halves(name, g, got, place):
    cut = CUTS[name]
    br, bc = cut.block_shape
    wire = _wire_dtype(name)

    def body(place_ref, g_ref, got_ref, o_ref, wire_ref):
        s = g_ref[...] + got_ref[...]
        wire_ref[...] = s.astype(wire)

        @pl.when(pl.program_id(0) == place_ref[1])
        def _():
            o_ref[...] = s

    blocks = pl.BlockSpec((None, br, bc), lambda j, place_ref: (j, 0, 0))
    return pl.pallas_call(
        body, name="reduce_add_halves_" + name,
        grid_spec=pltpu.PrefetchScalarGridSpec(
            num_scalar_prefetch=1, grid=(N_CHIPS,),
            in_specs=[pl.BlockSpec((br, bc), lambda j, place_ref: cut.block_index(j, place_ref[0])), blocks],
            out_specs=[pl.BlockSpec((br, bc), lambda j, place_ref: (0, 0)), blocks]),
        out_shape=[jax.ShapeDtypeStruct((br, bc), f32), jax.ShapeDtypeStruct((N_CHIPS, br, bc), wire)],
        compiler_params=_params(),
    )(place, g, got)


def _reduce_tail(late, early):
    late_names, early_names = tuple(late), tuple(early)
    names = early_names + late_names
    nl, ne, n = len(late_names), len(early_names), len(names)
    cuts = [CUTS[name] for name in names]
    is_big = [name in BIG for name in names]
    share_base, share_sem = 7 * nl, []
    for i in range(n):
        share_sem.append(share_base)
        share_base += 1 if is_big[i] else 7
    n_in = nl + 2 * ne

    def body(*refs):
        g_refs = refs[:nl]
        sums_refs, got_refs = refs[nl:n_in:2], refs[nl + 1:n_in:2]
        out_refs = refs[n_in:n_in + n]
        send_sems, recv_sems, local_sems = refs[n_in + n:n_in + n + 3]
        vm = refs[n_in + n + 3:]
        own, recv_a, wire, recv_b = vm[0:4 * nl:4], vm[1:4 * nl:4], vm[2:4 * nl:4], vm[3:4 * nl:4]
        acc, gotv = vm[4 * nl::2], vm[4 * nl + 1::2]
        x, y, c, me, others, other_ids = _place()
        sibling = (x, y, 1 - c)

        def mine(i, half):
            return cuts[i].half_of_shard(out_refs[i], half) if is_big[i] else cuts[i].block(out_refs[i], me, half)

        def reduced(i):
            return acc[i] if i < ne else own[i - ne].at[me]

        sent, stores = [], []

        def share(i):
            store = pltpu.make_async_copy(reduced(i), mine(i, c), local_sems.at[4 * nl + 2 * ne + i])
            store.start()
            stores.append(store)
            to = [sibling] if is_big[i] else [sibling] + [(*chip, c) for chip in others]
            for which, device in enumerate(to):
                cp = _remote(reduced(i), mine(i, c), send_sems, recv_sems, share_sem[i] + which, device)
                cp.start()
                sent.append(cp)

        early_loads = []
        for e in range(ne):
            early_loads.append(pltpu.make_async_copy(sums_refs[e], acc[e], local_sems.at[4 * nl + 2 * e]))
            early_loads.append(pltpu.make_async_copy(got_refs[e], gotv[e], local_sems.at[4 * nl + 2 * e + 1]))
        late_loads = [pltpu.make_async_copy(cuts[ne + l].block(g_refs[l], j, c), own[l].at[j], local_sems.at[4 * l + j])
                      for l in range(nl) for j in range(N_CHIPS)]
        halves = [_remote(cuts[ne + l].block(g_refs[l], j, 1 - c), recv_a[l].at[j], send_sems, recv_sems, 4 * l + j, sibling)
                  for l in range(nl) for j in range(N_CHIPS)]
        for cp in early_loads + late_loads + halves:
            cp.start()

        for cp in early_loads:
            cp.wait()
        for e in range(ne):
            acc[e][...] = ((acc[e][...] + gotv[e][0].astype(f32)) + gotv[e][1].astype(f32)) + gotv[e][2].astype(f32)
            share(e)

        for cp in late_loads:
            cp.wait()
        for cp in halves:
            cp.wait_recv()
        for l in range(nl):
            for j in range(N_CHIPS):
                s = own[l][j] + recv_a[l][j]
                own[l][j] = s
                wire[l][j] = s.astype(wire[l].dtype)
        chips = [_remote(wire[l].at[other_ids[k]], recv_b[l].at[k], send_sems, recv_sems, 4 * nl + 3 * l + k, (*chip, c))
                 for l in range(nl) for k, chip in enumerate(others)]
        for cp in chips:
            cp.start()
        for cp in chips:
            cp.wait_recv()
        for l in range(nl):
            mine_l = own[l].at[me]
            mine_l[...] = ((mine_l[...] + recv_b[l][0].astype(f32)) + recv_b[l][1].astype(f32)) + recv_b[l][2].astype(f32)
            share(ne + l)

        for i in range(n):
            if not is_big[i]:
                for k, chip in enumerate(others):
                    landed = cuts[i].block(out_refs[i], other_ids[k], c)
                    _remote(landed, landed, send_sems, recv_sems, share_sem[i] + 1 + k, (*chip, c)).wait_recv()
                    cp = _remote(landed, landed, send_sems, recv_sems, share_sem[i] + 4 + k, sibling)
                    cp.start()
                    sent.append(cp)
        for i in range(n):
            theirs = mine(i, 1 - c)
            _remote(theirs, theirs, send_sems, recv_sems, share_sem[i], sibling).wait_recv()
            if not is_big[i]:
                for k in range(3):
                    passed = cuts[i].block(out_refs[i], other_ids[k], 1 - c)
                    _remote(passed, passed, send_sems, recv_sems, share_sem[i] + 4 + k, sibling).wait_recv()
        for cp in halves + chips + sent:
            cp.wait_send()
        for cp in stores:
            cp.wait()

    scratch = [pltpu.SemaphoreType.DMA((share_base,)), pltpu.SemaphoreType.DMA((share_base,)),
               pltpu.SemaphoreType.DMA((4 * nl + 2 * ne + n,))]
    for name in late_names:
        block = CUTS[name].block_shape
        scratch += [pltpu.VMEM((N_CHIPS, *block), f32), pltpu.VMEM((N_CHIPS, *block), f32),
                    pltpu.VMEM((N_CHIPS, *block), _wire_dtype(name)), pltpu.VMEM((3, *block), _wire_dtype(name))]
    for name in early_names:
        block = CUTS[name].block_shape
        scratch += [pltpu.VMEM(block, f32), pltpu.VMEM((3, *block), _wire_dtype(name))]
    args = [late[name] for name in late_names] + [a for name in early_names for a in early[name]]
    outs = pl.pallas_call(
        body, name="reduce_tail",
        in_specs=[ANY] * n_in, out_specs=[ANY] * n,
        out_shape=[jax.ShapeDtypeStruct(CUTS[name].shard_shape if name in BIG else (CUTS[name].rows, CUTS[name].cols), f32)
                   for name in names],
        scratch_shapes=scratch, compiler_params=_params(),
    )(*args)
    return dict(zip(names, outs))


def _adamw_refs(w_ref, g_ref, m_ref, v_ref, d_ref, nm_ref, nv_ref):
    g = g_ref[...]
    nm = ADAM_B1 * m_ref[...] + (1.0 - ADAM_B1) * g
    nv = ADAM_B2 * v_ref[...] + (1.0 - ADAM_B2) * jnp.square(g)
    m_hat = nm / (1.0 - ADAM_B1 ** ADAM_STEP)
    v_hat = nv / (1.0 - ADAM_B2 ** ADAM_STEP)
    d_ref[...] = -ADAM_LR * (m_hat / (jnp.sqrt(v_hat) + ADAM_EPS) + ADAM_WD * w_ref[...])
    nm_ref[...] = nm
    nv_ref[...] = nv


def _adamw_small(ws, gs, ms, vs):
    n = len(ws)

    def body(*refs):
        for i in range(n):
            _adamw_refs(*[refs[k * n + i] for k in range(7)])

    whole = pl.BlockSpec(memory_space=pltpu.VMEM)
    outs = pl.pallas_call(
        body, name="adamw_small",
        in_specs=[whole] * (4 * n), out_specs=[whole] * (3 * n),
        out_shape=[jax.ShapeDtypeStruct(w.shape, f32) for w in ws] * 3,
        compiler_params=_params(),
    )(*ws, *gs, *ms, *vs)
    return outs[:n], outs[n:2 * n], outs[2 * n:]


def _adamw(name, w, g, m, v):
    rows, cols = w.shape
    rt = rows // 4

    def body(w_ref, g_ref, m_ref, v_ref, d_ref, nm_ref, nv_ref):
        _adamw_refs(w_ref, g_ref, m_ref, v_ref, d_ref, nm_ref, nv_ref)

    spec = pl.BlockSpec((rt, cols), lambda i: (i, 0))
    shape = jax.ShapeDtypeStruct((rows, cols), f32)
    return pl.pallas_call(
        body, name="adamw_" + name, grid=(rows // rt,),
        in_specs=[spec] * 4, out_specs=[spec] * 3, out_shape=[shape] * 3,
        compiler_params=_params(),
    )(w, g, m, v)


VEC_NAMES = ("pool_scale", "sgu_ln_g", "sgu_ln_b", "sgu_b", "ln1_g", "ln1_b", "ln2_g", "ln2_b")
WEIGHT_ORDER = ("w_in", "pool_w", "pool_scale", "sgu_ln_g", "sgu_ln_b", "sgu_w", "sgu_b", "w_out", "ln1_g", "ln1_b",
                "w_gate_up", "w_down", "ln2_g", "ln2_b")


def _pack_vecs(parts, extra=None):
    rows = [parts[name].reshape(-1, GROUP) for name in VEC_NAMES]
    if extra is not None:
        rows.append(extra.reshape(-1, GROUP))
    used = sum(r.shape[0] for r in rows)
    return jnp.concatenate(rows + [jnp.zeros((VEC_ROWS - used, GROUP), f32)], axis=0)


def _unpack_vecs(packed, shapes):
    out, at = {}, 0
    for name in VEC_NAMES:
        rows = math.prod(shapes[name]) // GROUP
        out[name] = packed[at:at + rows].reshape(shapes[name])
        at += rows
    return out, packed[at:]


def kernel(x, w_in, pool_w, pool_scale, sgu_ln_g, sgu_ln_b, sgu_w, sgu_b, w_out, ln1_g, ln1_b, w_gate_up, w_down, ln2_g, ln2_b, loss_target, m_w_in, m_pool_w, m_pool_scale, m_sgu_ln_g, m_sgu_ln_b, m_sgu_w, m_sgu_b, m_w_out, m_ln1_g, m_ln1_b, m_w_gate_up, m_w_down, m_ln2_g, m_ln2_b, v_w_in, v_pool_w, v_pool_scale, v_sgu_ln_g, v_sgu_ln_b, v_sgu_w, v_sgu_b, v_w_out, v_ln1_g, v_ln1_b, v_w_gate_up, v_w_down, v_ln2_g, v_ln2_b):
    given = dict(locals())
    batch, seq, _ = x.shape
    tokens = batch * seq
    tile = min(TOKEN_TILE, seq)
    ffn_bwd_tile = min(FFN_BWD_TILE, seq)
    wtile = min(WGRAD_TILE, tokens)
    shapes = {name: given[name].shape for name in WEIGHT_ORDER}

    x2 = x.reshape(tokens, D_MODEL)
    target = loss_target.reshape(tokens, D_MODEL)
    small = (pool_w[0], pool_scale[0][None], sgu_ln_g[0][None], sgu_ln_b[0][None], sgu_w[0],
             jnp.broadcast_to(sgu_b[0][:, :, None], (N_GROUPS, GROUP, GROUP)))
    g1, b1, g2, b2 = ln1_g[0][None], ln1_b[0][None], ln2_g[0][None], ln2_b[0][None]
    shard_b = {name: given[name][0].astype(bf16) for name in BIG}
    shard_b["w_gate_up_top"] = shard_b["w_gate_up"][:TOP_ROWS]
    shard_b["w_gate_up_bottom"] = shard_b["w_gate_up"][TOP_ROWS:]
    place = jnp.stack([lax.axis_index("c"), 2 * lax.axis_index("x") + lax.axis_index("y")]).astype(jnp.int32)

    def gather(*names):
        return [_GatherJob({name: shard_b[name] for name in names})]

    def halves_summed(name, grad, got):
        return _add_halves(name, grad, got, place)

    (x_b,), (got,) = _cast_x(x2, 2 * tile, gather("w_in"))
    w_in_b = got["w_in"]
    (xp_all, pre_b), (got,) = _fwd_proj(x_b, w_in_b, tile, gather("w_out", "w_gate_up_top"))
    w_out_b, w_top = got["w_out"], got["w_gate_up_top"]
    (mix_b, r1, h_b), (got,) = _fwd_mix(xp_all, pre_b, x2, small, w_out_b, g1, b1, tile, seq, gather("w_gate_up_bottom"))
    w_bottom = got["w_gate_up_bottom"]
    (gu_b, a_b), (got,) = _fwd_gate_up(h_b, w_top, w_bottom, tile, gather("w_down"))
    w_dn_b = got["w_down"]
    dr2, dr2_b, stats2 = _fwd_down_loss(a_b, w_dn_b, r1, target, g1, b1, g2, b2, tile)

    early = {}
    g_down, _ = _wgrad(a_b, dr2_b, D_MODEL, wtile, "wgrad_down")
    (dgu_b,), (got,) = _bwd_gate_up(dr2_b, gu_b, w_dn_b, ffn_bwd_tile, [_SwapHalvesJob({"w_down": g_down})])
    sums_down = halves_summed("w_down", g_down, got["w_down"])
    g_gu, (got,) = _wgrad(h_b, dgu_b, D_FF, wtile, "wgrad_gate_up", [_SwapChipsJob({"w_down": sums_down[1]})])
    early["w_down"] = (sums_down[0], got["w_down"])
    (dr1, dr1_b, stats1), (got,) = _bwd_ffn_in(dgu_b, w_top, w_bottom, dr2, r1, g1, b1, tile,
                                               [_SwapHalvesJob({"w_gate_up": g_gu})])
    sums_gu = halves_summed("w_gate_up", g_gu, got["w_gate_up"])
    g_out, _ = _wgrad(mix_b, dr1_b, D_MODEL, min(2 * wtile, tokens), "wgrad_out")
    (dpool, dpre_b, g_pool_w, g_sgu_w, g_sgu_b, vecs), (got_gu, got_out) = _bwd_mix(
        dr1_b, xp_all, pre_b, small, w_out_b, tile, seq,
        [_SwapChipsJob({"w_gate_up": sums_gu[1]}), _SwapHalvesJob({"w_out": g_out})])
    early["w_gate_up"] = (sums_gu[0], got_gu["w_gate_up"])
    sums_out = halves_summed("w_out", g_out, got_out["w_out"])
    grad_x, dxp_b = _bwd_in(dpool, dpre_b, dr1, w_in_b, tile, seq)
    g_in, (got,) = _wgrad_in(x_b, dxp_b, dpre_b, min(2 * wtile, tokens), [_SwapChipsJob({"w_out": sums_out[1]})])
    early["w_out"] = (sums_out[0], got["w_out"])

    late = {
        "w_in": g_in,
        "pool_w": g_pool_w.reshape(SQUARE_ROWS, GROUP),
        "sgu_w": g_sgu_w.reshape(SQUARE_ROWS, GROUP),
        "vecs": _pack_vecs({"pool_scale": vecs[0], "sgu_ln_g": vecs[1], "sgu_ln_b": vecs[2], "sgu_b": g_sgu_b[:, :, 0],
                            "ln1_g": stats1[0], "ln1_b": stats1[1], "ln2_g": stats2[0], "ln2_b": stats2[1]},
                           extra=stats2[2]),
    }
    shared = _reduce_tail(late, early)

    grad, delta, new_m, new_v = {}, {}, {}, {}
    for name in BIG:
        grad[name] = shared[name][None]
        d, nm, nv = _adamw(name, given[name][0], shared[name], given["m_" + name][0], given["v_" + name][0])
        delta[name], new_m[name], new_v[name] = d[None], nm[None], nv[None]
    vec_grads, after = _unpack_vecs(shared["vecs"], shapes)
    grad.update(vec_grads)
    for name in ("pool_w", "sgu_w"):
        grad[name] = shared[name].reshape(shapes[name])
    small_names = ("pool_w", "sgu_w") + VEC_NAMES
    state = {pre: [given[pre + name] for name in small_names] for pre in ("", "m_", "v_")}
    ds, nms, nvs = _adamw_small(state[""], [grad[name] for name in small_names], state["m_"], state["v_"])
    delta.update(zip(small_names, ds))
    new_m.update(zip(small_names, nms))
    new_v.update(zip(small_names, nvs))

    sq_err = after[:LOSS_ROWS]
    loss = jnp.sum(sq_err) * (0.5 / D_MODEL)
    return (loss, grad_x.reshape(x.shape), *[grad[name] for name in WEIGHT_ORDER],
            *[delta[name] for name in WEIGHT_ORDER], *[new_m[name] for name in WEIGHT_ORDER],
            *[new_v[name] for name in WEIGHT_ORDER])
```

```python
import math

import jax
import jax.numpy as jnp
from jax import lax
from jax.experimental import pallas as pl
from jax.experimental.pallas import tpu as pltpu

f32 = jnp.float32
bf16 = jnp.bfloat16
MESH = pl.DeviceIdType.MESH

D_MODEL = 1024
POOL_WIDTH = 512
SGU_WIDTH = 512
IN_COLS = POOL_WIDTH + 2 * SGU_WIDTH
D_FF = 2816
POOL_WINDOWS = (2, 4, 8, 16)
GROUP = 128
N_GROUPS = 4
HALO = 16
LN_EPS = 1e-5
ALPHA = float(2.0 ** 0.25)
N_CHIPS = 4

ADAM_LR = 0.001
ADAM_B1 = 0.9
ADAM_B2 = 0.999
ADAM_EPS = 1e-08
ADAM_WD = 0.01
ADAM_STEP = 10

TOKEN_TILE = 512
FFN_BWD_TILE = 512
FF_CHUNK = 256
ROW_SUB = 256
WGRAD_TILE = 1024
TOP_ROWS = 256
V7X_VMEM_LIMIT = 56 * 1024 * 1024

SQUARE_ROWS = N_GROUPS * GROUP
VEC_ROWS = 64
LOSS_ROWS = D_MODEL // GROUP


def _params(**kw):
    return pltpu.CompilerParams(vmem_limit_bytes=V7X_VMEM_LIMIT, **kw)


def _mm(a, b):
    return jnp.dot(a, b, preferred_element_type=f32)


def _mm_nt(a, b):
    return lax.dot_general(a, b, (((1,), (1,)), ((), ())), preferred_element_type=f32)


def _mm_tn(a, b):
    return lax.dot_general(a, b, (((0,), (0,)), ((), ())), preferred_element_type=f32)


def _ln_fwd(r, g, b):
    mu = jnp.mean(r, axis=-1, keepdims=True)
    xc = r - mu
    var = jnp.mean(xc * xc, axis=-1, keepdims=True)
    rstd = lax.rsqrt(var + LN_EPS)
    xhat = xc * rstd
    return xhat * g + b, xhat, rstd


def _ln_bwd(dout, xhat, rstd, g):
    dxhat = dout * g
    m1 = jnp.mean(dxhat, axis=-1, keepdims=True)
    m2 = jnp.mean(dxhat * xhat, axis=-1, keepdims=True)
    return rstd * (dxhat - m1 - xhat * m2)


def _col_sum(a):
    return jnp.sum(a, axis=0, keepdims=True)


def _gelu_parts(z):
    cdf = 0.5 * (1.0 + lax.erf(z * (1.0 / math.sqrt(2.0))))
    pdf = jnp.exp(-0.5 * z * z) * (1.0 / math.sqrt(2.0 * math.pi))
    return cdf, pdf


def _inv_counts(seq_tile, rows):
    pos = seq_tile * rows + lax.broadcasted_iota(jnp.int32, (rows, GROUP), 0) + 1
    return [1.0 / jnp.minimum(pos, w).astype(f32) for w in POOL_WINDOWS]


def _window_sums(e, back):
    n = e.shape[0]

    def shifted(a, s):
        return pltpu.roll(a, s if back else n - s, 0)

    s2 = e + shifted(e, 1)
    s4 = s2[:, GROUP:] + shifted(s2[:, GROUP:], 2)
    s8 = s4[:, GROUP:] + shifted(s4[:, GROUP:], 4)
    s16 = s8[:, GROUP:] + shifted(s8[:, GROUP:], 8)
    return [s2[:, :GROUP], s4[:, :GROUP], s8[:, :GROUP], s16]


def _pooled_groups(xp, halo, inv):
    sums = _window_sums(jnp.concatenate([halo, xp], axis=0), back=True)
    return [sums[g][HALO:] * inv[g] - xp[:, g * GROUP:(g + 1) * GROUP] for g in range(N_GROUPS)]


def _tril_mask():
    r = lax.broadcasted_iota(jnp.int32, (GROUP, GROUP), 0)
    c = lax.broadcasted_iota(jnp.int32, (GROUP, GROUP), 1)
    return (r >= c).astype(f32)


def _gs(g):
    return slice(g * GROUP, (g + 1) * GROUP)


class _Cut:
    def __init__(self, rows, cols, by_cols):
        self.rows, self.cols, self.by_cols = rows, cols, by_cols
        if by_cols:
            self.block_shape = (rows // 2, cols // N_CHIPS)
            self.shard_shape = (rows, cols // N_CHIPS)
        else:
            self.block_shape = (rows // (2 * N_CHIPS), cols)
            self.shard_shape = (rows // N_CHIPS, cols)

    def block(self, ref, chip, half):
        br, bc = self.block_shape
        if self.by_cols:
            return ref.at[pl.ds(pl.multiple_of(half * br, 16), br), pl.ds(pl.multiple_of(chip * bc, 128), bc)]
        return ref.at[pl.ds(pl.multiple_of((2 * chip + half) * br, 8), br), :]

    def shard(self, ref, chip):
        sr, sc = self.shard_shape
        if self.by_cols:
            return ref.at[:, pl.ds(pl.multiple_of(chip * sc, 128), sc)]
        return ref.at[pl.ds(pl.multiple_of(chip * sr, 16), sr), :]

    def half_of_shard(self, ref, half):
        br = self.block_shape[0]
        return ref.at[pl.ds(pl.multiple_of(half * br, 8), br), :]

    def block_index(self, chip, half):
        return (half, chip) if self.by_cols else (2 * chip + half, 0)


CUTS = {
    "w_in": _Cut(D_MODEL, IN_COLS, True),
    "w_out": _Cut(D_MODEL, D_MODEL, False),
    "w_gate_up": _Cut(D_MODEL, 2 * D_FF, True),
    "w_down": _Cut(D_FF, D_MODEL, False),
    "w_gate_up_top": _Cut(TOP_ROWS, 2 * D_FF, True),
    "w_gate_up_bottom": _Cut(D_MODEL - TOP_ROWS, 2 * D_FF, True),
    "pool_w": _Cut(SQUARE_ROWS, GROUP, False),
    "sgu_w": _Cut(SQUARE_ROWS, GROUP, False),
    "vecs": _Cut(VEC_ROWS, GROUP, False),
}
BIG = ("w_in", "w_out", "w_gate_up", "w_down")
ANY = pl.BlockSpec(memory_space=pl.ANY)


def _wire_dtype(name):
    return bf16 if name in BIG else f32


def _place():
    x, y, c = lax.axis_index("x"), lax.axis_index("y"), lax.axis_index("c")
    others = [(1 - x, y), (x, 1 - y), (1 - x, 1 - y)]
    return x, y, c, 2 * x + y, others, [2 * ox + oy for ox, oy in others]


def _remote(src, dst, send_sems, recv_sems, k, to):
    return pltpu.make_async_remote_copy(src_ref=src, dst_ref=dst, send_sem=send_sems.at[k], recv_sem=recv_sems.at[k],
                                        device_id=to, device_id_type=MESH)


class _GatherJob:
    def __init__(self, shards, first_row=None):
        self.names = tuple(shards)
        self.arrays = tuple(shards.values())
        self.first_row = first_row or {}
        n = len(self.names)
        self.out_shapes = [jax.ShapeDtypeStruct((CUTS[name].rows, CUTS[name].cols), bf16) for name in self.names]
        self.scratch_shapes = ([pltpu.SemaphoreType.DMA((6 * n,)), pltpu.SemaphoreType.DMA((6 * n,)),
                                pltpu.SemaphoreType.DMA((2 * n,))]
                               + [pltpu.VMEM(CUTS[name].shard_shape, bf16) for name in self.names])

    def bind(self, shard_refs, full_refs, scratch):
        self.shards = [ref.at[pl.ds(self.first_row[name], CUTS[name].rows), :] if name in self.first_row else ref
                       for name, ref in zip(self.names, shard_refs)]
        self.full = full_refs
        self.send_sems, self.recv_sems, self.local_sems = scratch[:3]
        self.stages = scratch[3:]
        return self

    def _sends(self):
        _, _, c, me, others, _ = _place()
        return [_remote(CUTS[name].half_of_shard(self.shards[w], c), CUTS[name].block(self.full[w], me, c),
                        self.send_sems, self.recv_sems, 3 * w + k, (*chip, c))
                for w, name in enumerate(self.names) for k, chip in enumerate(others)]

    def _relays(self, half):
        x, y, c, _, _, other_ids = _place()
        n = len(self.names)
        return [_remote(CUTS[name].block(self.full[w], other_ids[k], half),
                        CUTS[name].block(self.full[w], other_ids[k], half),
                        self.send_sems, self.recv_sems, 3 * n + 3 * w + k, (x, y, 1 - c))
                for w, name in enumerate(self.names) for k in range(3)]

    def _stores(self):
        me = _place()[3]
        return [pltpu.make_async_copy(self.stages[w], CUTS[name].shard(self.full[w], me), self.local_sems.at[2 * w + 1])
                for w, name in enumerate(self.names)]

    def start(self):
        loads = [pltpu.make_async_copy(self.shards[w], self.stages[w], self.local_sems.at[2 * w])
                 for w in range(len(self.names))]
        for cp in loads:
            cp.start()
        for cp in self._sends():
            cp.start()
        for load, store in zip(loads, self._stores()):
            load.wait()
            store.start()

    def relay(self):
        _, _, c, _, others, other_ids = _place()
        relays = self._relays(c)
        for w, name in enumerate(self.names):
            for k, chip in enumerate(others):
                landed = CUTS[name].block(self.full[w], other_ids[k], c)
                _remote(landed, landed, self.send_sems, self.recv_sems, 3 * w + k, (*chip, c)).wait_recv()
                relays[3 * w + k].start()

    def finish(self):
        c = _place()[2]
        for cp in self._relays(1 - c):
            cp.wait_recv()
        for cp in self._sends() + self._relays(c):
            cp.wait_send()
        for cp in self._stores():
            cp.wait()


class _SwapHalvesJob:
    def __init__(self, grads):
        self.names = tuple(grads)
        self.arrays = tuple(grads.values())
        n = len(self.names)
        self.out_shapes = [jax.ShapeDtypeStruct((N_CHIPS, *CUTS[name].block_shape), f32) for name in self.names]
        self.scratch_shapes = [pltpu.SemaphoreType.DMA((N_CHIPS * n,)), pltpu.SemaphoreType.DMA((N_CHIPS * n,))]

    def bind(self, g_refs, got_refs, scratch):
        self.g_refs, self.got_refs = g_refs, got_refs
        self.send_sems, self.recv_sems = scratch
        return self

    def _copies(self):
        x, y, c, _, _, _ = _place()
        return [_remote(CUTS[name].block(self.g_refs[a], j, 1 - c), self.got_refs[a].at[j], self.send_sems,
                        self.recv_sems, N_CHIPS * a + j, (x, y, 1 - c))
                for a, name in enumerate(self.names) for j in range(N_CHIPS)]

    def start(self):
        for cp in self._copies():
            cp.start()

    def finish(self):
        for cp in self._copies():
            cp.wait()


class _SwapChipsJob:
    def __init__(self, partials):
        self.names = tuple(partials)
        self.arrays = tuple(partials.values())
        n = len(self.names)
        self.out_shapes = [jax.ShapeDtypeStruct((3, *CUTS[name].block_shape), _wire_dtype(name)) for name in self.names]
        self.scratch_shapes = [pltpu.SemaphoreType.DMA((3 * n,)), pltpu.SemaphoreType.DMA((3 * n,))]

    def bind(self, p_refs, got_refs, scratch):
        self.p_refs, self.got_refs = p_refs, got_refs
        self.send_sems, self.recv_sems = scratch
        return self

    def _copies(self):
        _, _, c, _, others, other_ids = _place()
        return [_remote(self.p_refs[a].at[other_ids[k]], self.got_refs[a].at[k], self.send_sems, self.recv_sems,
                        3 * a + k, (*chip, c))
                for a in range(len(self.names)) for k, chip in enumerate(others)]

    def start(self):
        for cp in self._copies():
            cp.start()

    def finish(self):
        for cp in self._copies():
            cp.wait()


def _call(body, *, name, grid, in_specs, out_specs, out_shape, args, scratch_shapes=(), jobs=()):
    n_in, n_out, n_scr = len(in_specs), len(out_specs), len(scratch_shapes)
    j_in = [len(j.arrays) for j in jobs]
    j_out = [len(j.out_shapes) for j in jobs]
    j_scr = [len(j.scratch_shapes) for j in jobs]

    def wrapped(*refs):
        refs = list(refs)

        def take(k):
            head = refs[:k]
            del refs[:k]
            return head

        ins, jins = take(n_in), [take(k) for k in j_in]
        outs, jouts = take(n_out), [take(k) for k in j_out]
        scr, jscr = take(n_scr), [take(k) for k in j_scr]
        bound = [j.bind(a, b, c) for j, a, b, c in zip(jobs, jins, jouts, jscr)]
        relaying = [b for b in bound if hasattr(b, "relay")]
        if not bound:
            body(*ins, *outs, *scr)
            return
        first = _all([pl.program_id(d) == 0 for d in range(len(grid))])
        last = _all([pl.program_id(d) == grid[d] - 1 for d in range(len(grid))])

        @pl.when(first)
        def _():
            for b in bound:
                b.start()

        if relaying:
            @pl.when(_all([pl.program_id(d) == (3 * grid[d]) // 4 for d in range(len(grid))]))
            def _():
                for b in relaying:
                    b.relay()

        body(*ins, *outs, *scr)

        @pl.when(last)
        def _():
            for b in bound:
                b.finish()

    results = pl.pallas_call(
        wrapped, name=name, grid=grid,
        in_specs=list(in_specs) + [ANY] * sum(j_in), out_specs=list(out_specs) + [ANY] * sum(j_out),
        out_shape=list(out_shape) + [s for j in jobs for s in j.out_shapes],
        scratch_shapes=list(scratch_shapes) + [s for j in jobs for s in j.scratch_shapes],
        compiler_params=_params(),
    )(*args, *[a for j in jobs for a in j.arrays])
    results = list(results)
    own, rest = results[:n_out], results[n_out:]
    per_job = []
    for j, k in zip(jobs, j_out):
        per_job.append(dict(zip(j.names, rest[:k])))
        rest = rest[k:]
    return own, per_job


def _all(conds):
    out = conds[0]
    for c in conds[1:]:
        out = jnp.logical_and(out, c)
    return out


def _cast_x(x2, tile, jobs):
    tokens = x2.shape[0]

    def body(x_ref, xb_ref):
        xb_ref[...] = x_ref[...].astype(bf16)

    row = lambda i: (i, 0)
    return _call(
        body, name="cast_x", grid=(tokens // tile,),
        in_specs=[pl.BlockSpec((tile, D_MODEL), row)], out_specs=[pl.BlockSpec((tile, D_MODEL), row)],
        out_shape=[jax.ShapeDtypeStruct((tokens, D_MODEL), bf16)], args=(x2,), jobs=jobs)


def _fwd_proj(x_b, w_in_b, tile, jobs):
    tokens = x_b.shape[0]

    def body(x_ref, w_ref, xp_ref, pre_ref):
        proj = _mm(x_ref[...], w_ref[...])
        xp_ref[...] = proj[:, :POOL_WIDTH]
        pre_ref[...] = proj[:, POOL_WIDTH:].astype(bf16)

    row = lambda i: (i, 0)
    return _call(
        body, name="fwd_proj", grid=(tokens // tile,),
        in_specs=[pl.BlockSpec((tile, D_MODEL), row), pl.BlockSpec((D_MODEL, IN_COLS), lambda i: (0, 0))],
        out_specs=[pl.BlockSpec((tile, POOL_WIDTH), row), pl.BlockSpec((tile, 2 * SGU_WIDTH), row)],
        out_shape=[jax.ShapeDtypeStruct((tokens, POOL_WIDTH), f32), jax.ShapeDtypeStruct((tokens, 2 * SGU_WIDTH), bf16)],
        args=(x_b, w_in_b), jobs=jobs)


def _small_specs():
    return [pl.BlockSpec((N_GROUPS, GROUP, GROUP), lambda i: (0, 0, 0)),
            pl.BlockSpec((1, POOL_WIDTH), lambda i: (0, 0)),
            pl.BlockSpec((1, SGU_WIDTH), lambda i: (0, 0)),
            pl.BlockSpec((1, SGU_WIDTH), lambda i: (0, 0)),
            pl.BlockSpec((N_GROUPS, GROUP, GROUP), lambda i: (0, 0, 0)),
            pl.BlockSpec((N_GROUPS, GROUP, GROUP), lambda i: (0, 0, 0))]


def _fwd_mix(xp_all, pre_b, x2, small, w_out_b, ln1_g, ln1_b, tile, seq, jobs):
    tokens = x2.shape[0]
    tps = seq // tile
    hb = tile // HALO

    def body(xp_ref, halo_ref, pre_ref, x_ref, pw_ref, ps_ref, lg_ref, lb_ref, sw_ref, sb_ref, wout_ref, g1_ref, b1_ref,
             mix_ref, r1_ref, h_ref):
        seq_tile = pl.program_id(0) % tps
        xp = xp_ref[...]
        halo = jnp.where(seq_tile == 0, 0.0, halo_ref[...])
        pooled = _pooled_groups(xp, halo, _inv_counts(seq_tile, tile))
        for g in range(N_GROUPS):
            po = _mm(pooled[g].astype(bf16), pw_ref[g].astype(bf16)) * ps_ref[:, _gs(g)]
            mix_ref[:, _gs(g)] = po.astype(bf16)

        pre = pre_ref[...].astype(f32)
        cdf, _ = _gelu_parts(pre)
        zg = pre * cdf
        u = zg[:, :SGU_WIDTH]
        vln, _, _ = _ln_fwd(zg[:, SGU_WIDTH:], lg_ref[...], lb_ref[...])
        vb = vln.astype(bf16)
        mask = _tril_mask()
        for h in range(N_GROUPS):
            wm = (sw_ref[h] * mask).astype(bf16)
            bias = sb_ref[h]
            for c in range(tile // GROUP):
                rows = slice(c * GROUP, (c + 1) * GROUP)
                mixed = _mm(wm, vb[rows, _gs(h)]) + bias
                mix_ref[rows, POOL_WIDTH + h * GROUP:POOL_WIDTH + (h + 1) * GROUP] = (u[rows, _gs(h)] * mixed).astype(bf16)

        r1 = ALPHA * x_ref[...] + _mm(mix_ref[...], wout_ref[...])
        r1_ref[...] = r1
        h1, _, _ = _ln_fwd(r1, g1_ref[...], b1_ref[...])
        h_ref[...] = h1.astype(bf16)

    row = lambda i: (i, 0)
    vec = pl.BlockSpec((1, D_MODEL), lambda i: (0, 0))
    return _call(
        body, name="fwd_mix", grid=(tokens // tile,),
        in_specs=[pl.BlockSpec((tile, POOL_WIDTH), row),
                  pl.BlockSpec((HALO, POOL_WIDTH), lambda i: (jnp.maximum(i * hb - 1, 0), 0)),
                  pl.BlockSpec((tile, 2 * SGU_WIDTH), row),
                  pl.BlockSpec((tile, D_MODEL), row)] + _small_specs()
                 + [pl.BlockSpec((D_MODEL, D_MODEL), lambda i: (0, 0)), vec, vec],
        out_specs=[pl.BlockSpec((tile, D_MODEL), row)] * 3,
        out_shape=[jax.ShapeDtypeStruct((tokens, D_MODEL), bf16),
                   jax.ShapeDtypeStruct((tokens, D_MODEL), f32),
                   jax.ShapeDtypeStruct((tokens, D_MODEL), bf16)],
        args=(xp_all, xp_all, pre_b, x2, *small, w_out_b, ln1_g, ln1_b), jobs=jobs)


def _fwd_gate_up(h_b, w_top, w_bottom, tile, jobs):
    tokens = h_b.shape[0]
    def body(h_ref, wt_ref, wb_ref, gu_ref, a_ref):
        ht, hb = h_ref[:, :TOP_ROWS], h_ref[:, TOP_ROWS:]
        for c in range(D_FF // FF_CHUNK):
            gcols = slice(c * FF_CHUNK, (c + 1) * FF_CHUNK)
            ucols = slice(D_FF + c * FF_CHUNK, D_FF + (c + 1) * FF_CHUNK)
            gate = _mm(ht, wt_ref[:, gcols]) + _mm(hb, wb_ref[:, gcols])
            up = _mm(ht, wt_ref[:, ucols]) + _mm(hb, wb_ref[:, ucols])
            sg = jax.nn.sigmoid(gate)
            silu = gate * sg
            gu_ref[:, gcols] = (up * (sg + silu * (1.0 - sg))).astype(bf16)
            gu_ref[:, ucols] = silu.astype(bf16)
            a_ref[:, gcols] = (silu * up).astype(bf16)

    return _call(
        body, name="fwd_gate_up", grid=(tokens // tile,),
        in_specs=[pl.BlockSpec((tile, D_MODEL), lambda i: (i, 0)),
                  pl.BlockSpec((TOP_ROWS, 2 * D_FF), lambda i: (0, 0), pipeline_mode=pl.Buffered(1)),
                  pl.BlockSpec((D_MODEL - TOP_ROWS, 2 * D_FF), lambda i: (0, 0), pipeline_mode=pl.Buffered(1))],
        out_specs=[pl.BlockSpec((tile, 2 * D_FF), lambda i: (i, 0)),
                   pl.BlockSpec((tile, D_FF), lambda i: (i, 0))],
        out_shape=[jax.ShapeDtypeStruct((tokens, 2 * D_FF), bf16),
                   jax.ShapeDtypeStruct((tokens, D_FF), bf16)],
        args=(h_b, w_top, w_bottom), jobs=jobs)


def _fwd_down_loss(a_b, w_dn_b, r1, target, ln1_g, ln1_b, ln2_g, ln2_b, tile):
    tokens = a_b.shape[0]

    def body(a_ref, w_ref, r1_ref, t_ref, g1_ref, b1_ref, g2_ref, b2_ref, dr2_ref, dr2b_ref, st_ref):
        @pl.when(pl.program_id(0) == 0)
        def _():
            st_ref[...] = jnp.zeros_like(st_ref)

        sub = min(ROW_SUB, tile)
        for s in range(tile // sub):
            rows = slice(s * sub, (s + 1) * sub)
            h1, _, _ = _ln_fwd(r1_ref[rows, :], g1_ref[...], b1_ref[...])
            r2 = ALPHA * h1 + _mm(a_ref[rows, :], w_ref[...])
            y, xhat, rstd = _ln_fwd(r2, g2_ref[...], b2_ref[...])
            diff = y - t_ref[rows, :]
            dy = diff * (1.0 / D_MODEL)
            st_ref[0:1, :] += _col_sum(dy * xhat)
            st_ref[1:2, :] += _col_sum(dy)
            st_ref[2:3, :] += _col_sum(diff * diff)
            dr2 = _ln_bwd(dy, xhat, rstd, g2_ref[...])
            dr2_ref[rows, :] = dr2
            dr2b_ref[rows, :] = dr2.astype(bf16)

    row = lambda i: (i, 0)
    vec = pl.BlockSpec((1, D_MODEL), lambda i: (0, 0))
    return _call(
        body, name="fwd_down_loss", grid=(tokens // tile,),
        in_specs=[pl.BlockSpec((tile, D_FF), row), pl.BlockSpec((D_FF, D_MODEL), lambda i: (0, 0)),
                  pl.BlockSpec((tile, D_MODEL), row), pl.BlockSpec((tile, D_MODEL), row), vec, vec, vec, vec],
        out_specs=[pl.BlockSpec((tile, D_MODEL), row), pl.BlockSpec((tile, D_MODEL), row),
                   pl.BlockSpec((8, D_MODEL), lambda i: (0, 0))],
        out_shape=[jax.ShapeDtypeStruct((tokens, D_MODEL), f32), jax.ShapeDtypeStruct((tokens, D_MODEL), bf16),
                   jax.ShapeDtypeStruct((8, D_MODEL), f32)],
        args=(a_b, w_dn_b, r1, target, ln1_g, ln1_b, ln2_g, ln2_b))[0]


def _bwd_gate_up(dr2, gu_b, w_dn_b, tile, jobs):
    tokens = dr2.shape[0]

    def body(d_ref, gu_ref, w_ref, dgu_ref):
        d = d_ref[...].astype(bf16)
        for c in range(D_FF // FF_CHUNK):
            gcols = slice(c * FF_CHUNK, (c + 1) * FF_CHUNK)
            ucols = slice(D_FF + c * FF_CHUNK, D_FF + (c + 1) * FF_CHUNK)
            da = _mm_nt(d, w_ref[gcols, :])
            dgu_ref[:, gcols] = (da * gu_ref[:, gcols].astype(f32)).astype(bf16)
            dgu_ref[:, ucols] = (da * gu_ref[:, ucols].astype(f32)).astype(bf16)

    return _call(
        body, name="bwd_gate_up", grid=(tokens // tile,),
        in_specs=[pl.BlockSpec((tile, D_MODEL), lambda i: (i, 0)),
                  pl.BlockSpec((tile, 2 * D_FF), lambda i: (i, 0)),
                  pl.BlockSpec((D_FF, D_MODEL), lambda i: (0, 0))],
        out_specs=[pl.BlockSpec((tile, 2 * D_FF), lambda i: (i, 0))],
        out_shape=[jax.ShapeDtypeStruct((tokens, 2 * D_FF), bf16)],
        args=(dr2, gu_b, w_dn_b), jobs=jobs)


def _bwd_ffn_in(dgu_b, w_top, w_bottom, dr2, r1, ln1_g, ln1_b, tile, jobs):
    tokens = dr2.shape[0]

    def body(dgu_ref, wt_ref, wb_ref, d_ref, r1_ref, g1_ref, b1_ref, dr1_ref, dr1b_ref, st_ref):
        @pl.when(pl.program_id(0) == 0)
        def _():
            st_ref[...] = jnp.zeros_like(st_ref)

        dgu = dgu_ref[...]
        dh = ALPHA * d_ref[...] + jnp.concatenate([_mm_nt(dgu, wt_ref[...]), _mm_nt(dgu, wb_ref[...])], axis=1)
        _, xhat, rstd = _ln_fwd(r1_ref[...], g1_ref[...], b1_ref[...])
        st_ref[0:1, :] += _col_sum(dh * xhat)
        st_ref[1:2, :] += _col_sum(dh)
        dr1 = _ln_bwd(dh, xhat, rstd, g1_ref[...])
        dr1_ref[...] = dr1
        dr1b_ref[...] = dr1.astype(bf16)

    row = lambda i: (i, 0)
    vec = pl.BlockSpec((1, D_MODEL), lambda i: (0, 0))
    return _call(
        body, name="bwd_ffn_in", grid=(tokens // tile,),
        in_specs=[pl.BlockSpec((tile, 2 * D_FF), row),
                  pl.BlockSpec((TOP_ROWS, 2 * D_FF), lambda i: (0, 0), pipeline_mode=pl.Buffered(1)),
                  pl.BlockSpec((D_MODEL - TOP_ROWS, 2 * D_FF), lambda i: (0, 0), pipeline_mode=pl.Buffered(1)),
                  pl.BlockSpec((tile, D_MODEL), row), pl.BlockSpec((tile, D_MODEL), row), vec, vec],
        out_specs=[pl.BlockSpec((tile, D_MODEL), row), pl.BlockSpec((tile, D_MODEL), row),
                   pl.BlockSpec((8, D_MODEL), lambda i: (0, 0))],
        out_shape=[jax.ShapeDtypeStruct((tokens, D_MODEL), f32), jax.ShapeDtypeStruct((tokens, D_MODEL), bf16),
                   jax.ShapeDtypeStruct((8, D_MODEL), f32)],
        args=(dgu_b, w_top, w_bottom, dr2, r1, ln1_g, ln1_b), jobs=jobs)


def _bwd_mix(dr1, xp_all, pre_b, small, w_out_b, tile, seq, jobs):
    tokens = dr1.shape[0]
    tps = seq // tile
    hb = tile // HALO
    steps = tokens // tile

    def body(dr1_ref, xp_ref, halo_ref, pre_ref, wout_ref, pw_ref, ps_ref, lg_ref, lb_ref, sw_ref, sb_ref,
             dpool_ref, dpre_ref, gpw_ref, gsw_ref, gsb_ref, vec_ref, du_ref, dvln_ref):
        step = pl.program_id(0)
        seq_tile = step % tps

        @pl.when(step == 0)
        def _():
            gpw_ref[...] = jnp.zeros_like(gpw_ref)
            gsw_ref[...] = jnp.zeros_like(gsw_ref)
            gsb_ref[...] = jnp.zeros_like(gsb_ref)
            vec_ref[...] = jnp.zeros_like(vec_ref)

        dmix = _mm_nt(dr1_ref[...].astype(bf16), wout_ref[...])

        xp = xp_ref[...]
        halo = jnp.where(seq_tile == 0, 0.0, halo_ref[...])
        pooled = _pooled_groups(xp, halo, _inv_counts(seq_tile, tile))
        for g in range(N_GROUPS):
            pb = pooled[g].astype(bf16)
            pwb = pw_ref[g].astype(bf16)
            dpo = dmix[:, _gs(g)]
            vec_ref[0:1, _gs(g)] += _col_sum(dpo * _mm(pb, pwb))
            dpo_b = (dpo * ps_ref[:, _gs(g)]).astype(bf16)
            gpw_ref[g] += _mm_tn(pb, dpo_b)
            dpool_ref[:, _gs(g)] = _mm_nt(dpo_b, pwb)

        pre = pre_ref[...].astype(f32)
        cdf, pdf = _gelu_parts(pre)
        zg = pre * cdf
        u = zg[:, :SGU_WIDTH]
        vln, vhat, rstd = _ln_fwd(zg[:, SGU_WIDTH:], lg_ref[...], lb_ref[...])
        vb = vln.astype(bf16)
        mask = _tril_mask()
        for h in range(N_GROUPS):
            wm = (sw_ref[h] * mask).astype(bf16)
            bias = sb_ref[h]
            gsw = jnp.zeros((GROUP, GROUP), f32)
            gsb = jnp.zeros((GROUP, GROUP), f32)
            for c in range(tile // GROUP):
                rows = slice(c * GROUP, (c + 1) * GROUP)
                v_ch = vb[rows, _gs(h)]
                d = dmix[rows, POOL_WIDTH + h * GROUP:POOL_WIDTH + (h + 1) * GROUP]
                du_ref[rows, _gs(h)] = d * (_mm(wm, v_ch) + bias)
                dmixed = d * u[rows, _gs(h)]
                gsb += dmixed
                dmixed_b = dmixed.astype(bf16)
                gsw += _mm_nt(dmixed_b, v_ch)
                dvln_ref[rows, _gs(h)] = _mm_tn(wm, dmixed_b)
            gsw_ref[h] += gsw * mask
            gsb_ref[h] += gsb

        dvln = dvln_ref[...]
        vec_ref[1:2, :] += _col_sum(dvln * vhat)
        vec_ref[2:3, :] += _col_sum(dvln)
        dgelu = cdf + pre * pdf
        dpre_ref[:, :SGU_WIDTH] = (du_ref[...] * dgelu[:, :SGU_WIDTH]).astype(bf16)
        dpre_ref[:, SGU_WIDTH:] = (_ln_bwd(dvln, vhat, rstd, lg_ref[...]) * dgelu[:, SGU_WIDTH:]).astype(bf16)

        @pl.when(step == steps - 1)
        def _():
            for h in range(N_GROUPS):
                gsb_ref[h] = jnp.broadcast_to(jnp.sum(gsb_ref[h], axis=1, keepdims=True), (GROUP, GROUP))

    row = lambda i: (i, 0)
    sq = jax.ShapeDtypeStruct((N_GROUPS, GROUP, GROUP), f32)
    sq_spec = pl.BlockSpec((N_GROUPS, GROUP, GROUP), lambda i: (0, 0, 0))
    return _call(
        body, name="bwd_mix", grid=(steps,),
        in_specs=[pl.BlockSpec((tile, D_MODEL), row), pl.BlockSpec((tile, POOL_WIDTH), row),
                  pl.BlockSpec((HALO, POOL_WIDTH), lambda i: (jnp.maximum(i * hb - 1, 0), 0)),
                  pl.BlockSpec((tile, 2 * SGU_WIDTH), row),
                  pl.BlockSpec((D_MODEL, D_MODEL), lambda i: (0, 0))] + _small_specs(),
        out_specs=[pl.BlockSpec((tile, POOL_WIDTH), row), pl.BlockSpec((tile, 2 * SGU_WIDTH), row),
                   sq_spec, sq_spec, sq_spec, pl.BlockSpec((8, POOL_WIDTH), lambda i: (0, 0))],
        out_shape=[jax.ShapeDtypeStruct((tokens, POOL_WIDTH), f32), jax.ShapeDtypeStruct((tokens, 2 * SGU_WIDTH), bf16),
                   sq, sq, sq, jax.ShapeDtypeStruct((8, POOL_WIDTH), f32)],
        scratch_shapes=[pltpu.VMEM((tile, SGU_WIDTH), f32), pltpu.VMEM((tile, SGU_WIDTH), f32)],
        args=(dr1, xp_all, xp_all, pre_b, w_out_b, *small), jobs=jobs)


def _bwd_in(dpool, dpre_b, dr1, w_in_b, tile, seq):
    tokens = dr1.shape[0]
    tps = seq // tile
    hb = tile // HALO
    last_halo = tokens // HALO - 1

    def body(dpool_ref, nxt_ref, dpre_ref, dr1_ref, w_ref, dx_ref, dxp_ref, dproj_ref):
        seq_tile = pl.program_id(0) % tps
        inv = _inv_counts(seq_tile, tile)
        dpl = dpool_ref[...]
        nxt = jnp.where(seq_tile == tps - 1, 0.0, nxt_ref[...])
        scaled = jnp.concatenate([dpl[:, _gs(g)] * inv[g] for g in range(N_GROUPS)], axis=1)
        scaled_nxt = jnp.concatenate([nxt[:, _gs(g)] * (1.0 / POOL_WINDOWS[g]) for g in range(N_GROUPS)], axis=1)
        sums = _window_sums(jnp.concatenate([scaled, scaled_nxt], axis=0), back=False)
        for g in range(N_GROUPS):
            dproj_ref[:, _gs(g)] = (sums[g][:tile] - dpl[:, _gs(g)]).astype(bf16)
        dxp_ref[...] = dproj_ref[:, :POOL_WIDTH]
        dproj_ref[:, POOL_WIDTH:] = dpre_ref[...]
        dx_ref[...] = ALPHA * dr1_ref[...] + _mm_nt(dproj_ref[...], w_ref[...])

    row = lambda i: (i, 0)
    return _call(
        body, name="bwd_in", grid=(tokens // tile,),
        in_specs=[pl.BlockSpec((tile, POOL_WIDTH), row),
                  pl.BlockSpec((HALO, POOL_WIDTH), lambda i: (jnp.minimum((i + 1) * hb, last_halo), 0)),
                  pl.BlockSpec((tile, 2 * SGU_WIDTH), row),
                  pl.BlockSpec((tile, D_MODEL), row),
                  pl.BlockSpec((D_MODEL, IN_COLS), lambda i: (0, 0))],
        out_specs=[pl.BlockSpec((tile, D_MODEL), row), pl.BlockSpec((tile, POOL_WIDTH), row)],
        out_shape=[jax.ShapeDtypeStruct((tokens, D_MODEL), f32), jax.ShapeDtypeStruct((tokens, POOL_WIDTH), bf16)],
        scratch_shapes=[pltpu.VMEM((tile, IN_COLS), bf16)],
        args=(dpool, dpool, dpre_b, dr1, w_in_b))[0]


def _wgrad(a, b, col_tile, tile, name, jobs=()):
    tokens, m = a.shape
    n = b.shape[1]

    def body(a_ref, b_ref, o_ref):
        @pl.when(pl.program_id(1) == 0)
        def _():
            o_ref[...] = jnp.zeros_like(o_ref)

        o_ref[...] += _mm_tn(a_ref[...].astype(bf16), b_ref[...].astype(bf16))

    (out,), got = _call(
        body, name=name, grid=(n // col_tile, tokens // tile),
        in_specs=[pl.BlockSpec((tile, m), lambda j, k: (k, 0)),
                  pl.BlockSpec((tile, col_tile), lambda j, k: (k, j))],
        out_specs=[pl.BlockSpec((m, col_tile), lambda j, k: (0, j))],
        out_shape=[jax.ShapeDtypeStruct((m, n), f32)],
        args=(a, b), jobs=jobs)
    return out, got


def _wgrad_in(x_b, dxp_b, dpre_b, tile, jobs):
    tokens = x_b.shape[0]

    def body(x_ref, dxp_ref, dpre_ref, o_ref):
        @pl.when(pl.program_id(0) == 0)
        def _():
            o_ref[...] = jnp.zeros_like(o_ref)

        xt = x_ref[...]
        o_ref[:, :POOL_WIDTH] += _mm_tn(xt, dxp_ref[...])
        o_ref[:, POOL_WIDTH:] += _mm_tn(xt, dpre_ref[...])

    row = lambda k: (k, 0)
    (out,), got = _call(
        body, name="wgrad_in", grid=(tokens // tile,),
        in_specs=[pl.BlockSpec((tile, D_MODEL), row), pl.BlockSpec((tile, POOL_WIDTH), row),
                  pl.BlockSpec((tile, 2 * SGU_WIDTH), row)],
        out_specs=[pl.BlockSpec((D_MODEL, IN_COLS), lambda k: (0, 0))],
        out_shape=[jax.ShapeDtypeStruct((D_MODEL, IN_COLS), f32)],
        args=(x_b, dxp_b, dpre_b), jobs=jobs)
    return out, got


def _add_halves(name, g, got, place):
    cut = CUTS[name]
    br, bc = cut.block_shape
    wire = _wire_dtype(name)

    def body(place_ref, g_ref, got_ref, o_ref, wire_ref):
        s = g_ref[...] + got_ref[...]
        wire_ref[...] = s.astype(wire)

        @pl.when(pl.program_id(0) == place_ref[1])
        def _():
            o_ref[...] = s

    blocks = pl.BlockSpec((None, br, bc), lambda j, place_ref: (j, 0, 0))
    return pl.pallas_call(
        body, name="reduce_add_halves_" + name,
        grid_spec=pltpu.PrefetchScalarGridSpec(
            num_scalar_prefetch=1, grid=(N_CHIPS,),
            in_specs=[pl.BlockSpec((br, bc), lambda j, place_ref: cut.block_index(j, place_ref[0])), blocks],
            out_specs=[pl.BlockSpec((br, bc), lambda j, place_ref: (0, 0)), blocks]),
        out_shape=[jax.ShapeDtypeStruct((br, bc), f32), jax.ShapeDtypeStruct((N_CHIPS, br, bc), wire)],
        compiler_params=_params(),
    )(place, g, got)


def _reduce_tail(late, early):
    late_names, early_names = tuple(late), tuple(early)
    names = early_names + late_names
    nl, ne, n = len(late_names), len(early_names), len(names)
    cuts = [CUTS[name] for name in names]
    is_big = [name in BIG for name in names]
    share_base, share_sem = 7 * nl, []
    for i in range(n):
        share_sem.append(share_base)
        share_base += 1 if is_big[i] else 7
    n_in = nl + 2 * ne

    def body(*refs):
        g_refs = refs[:nl]
        sums_refs, got_refs = refs[nl:n_in:2], refs[nl + 1:n_in:2]
        out_refs = refs[n_in:n_in + n]
        send_sems, recv_sems, local_sems = refs[n_in + n:n_in + n + 3]
        vm = refs[n_in + n + 3:]
        own, recv_a, wire, recv_b = vm[0:4 * nl:4], vm[1:4 * nl:4], vm[2:4 * nl:4], vm[3:4 * nl:4]
        acc, gotv = vm[4 * nl::2], vm[4 * nl + 1::2]
        x, y, c, me, others, other_ids = _place()
        sibling = (x, y, 1 - c)

        def mine(i, half):
            return cuts[i].half_of_shard(out_refs[i], half) if is_big[i] else cuts[i].block(out_refs[i], me, half)

        def reduced(i):
            return acc[i] if i < ne else own[i - ne].at[me]

        sent, stores = [], []

        def share(i):
            store = pltpu.make_async_copy(reduced(i), mine(i, c), local_sems.at[4 * nl + 2 * ne + i])
            store.start()
            stores.append(store)
            to = [sibling] if is_big[i] else [sibling] + [(*chip, c) for chip in others]
            for which, device in enumerate(to):
                cp = _remote(reduced(i), mine(i, c), send_sems, recv_sems, share_sem[i] + which, device)
                cp.start()
                sent.append(cp)

        early_loads = []
        for e in range(ne):
            early_loads.append(pltpu.make_async_copy(sums_refs[e], acc[e], local_sems.at[4 * nl + 2 * e]))
            early_loads.append(pltpu.make_async_copy(got_refs[e], gotv[e], local_sems.at[4 * nl + 2 * e + 1]))
        late_loads = [pltpu.make_async_copy(cuts[ne + l].block(g_refs[l], j, c), own[l].at[j], local_sems.at[4 * l + j])
                      for l in range(nl) for j in range(N_CHIPS)]
        halves = [_remote(cuts[ne + l].block(g_refs[l], j, 1 - c), recv_a[l].at[j], send_sems, recv_sems, 4 * l + j, sibling)
                  for l in range(nl) for j in range(N_CHIPS)]
        for cp in early_loads + late_loads + halves:
            cp.start()

        for cp in early_loads:
            cp.wait()
        for e in range(ne):
            acc[e][...] = ((acc[e][...] + gotv[e][0].astype(f32)) + gotv[e][1].astype(f32)) + gotv[e][2].astype(f32)
            share(e)

        for cp in late_loads:
            cp.wait()
        for cp in halves:
            cp.wait_recv()
        for l in range(nl):
            for j in range(N_CHIPS):
                s = own[l][j] + recv_a[l][j]
                own[l][j] = s
                wire[l][j] = s.astype(wire[l].dtype)
        chips = [_remote(wire[l].at[other_ids[k]], recv_b[l].at[k], send_sems, recv_sems, 4 * nl + 3 * l + k, (*chip, c))
                 for l in range(nl) for k, chip in enumerate(others)]
        for cp in chips:
            cp.start()
        for cp in chips:
            cp.wait_recv()
        for l in range(nl):
            mine_l = own[l].at[me]
            mine_l[...] = ((mine_l[...] + recv_b[l][0].astype(f32)) + recv_b[l][1].astype(f32)) + recv_b[l][2].astype(f32)
            share(ne + l)

        for i in range(n):
            if not is_big[i]:
                for k, chip in enumerate(others):
                    landed = cuts[i].block(out_refs[i], other_ids[k], c)
                    _remote(landed, landed, send_sems, recv_sems, share_sem[i] + 1 + k, (*chip, c)).wait_recv()
                    cp = _remote(landed, landed, send_sems, recv_sems, share_sem[i] + 4 + k, sibling)
                    cp.start()
                    sent.append(cp)
        for i in range(n):
            theirs = mine(i, 1 - c)
            _remote(theirs, theirs, send_sems, recv_sems, share_sem[i], sibling).wait_recv()
            if not is_big[i]:
                for k in range(3):
                    passed = cuts[i].block(out_refs[i], other_ids[k], 1 - c)
                    _remote(passed, passed, send_sems, recv_sems, share_sem[i] + 4 + k, sibling).wait_recv()
        for cp in halves + chips + sent:
            cp.wait_send()
        for cp in stores:
            cp.wait()

    scratch = [pltpu.SemaphoreType.DMA((share_base,)), pltpu.SemaphoreType.DMA((share_base,)),
               pltpu.SemaphoreType.DMA((4 * nl + 2 * ne + n,))]
    for name in late_names:
        block = CUTS[name].block_shape
        scratch += [pltpu.VMEM((N_CHIPS, *block), f32), pltpu.VMEM((N_CHIPS, *block), f32),
                    pltpu.VMEM((N_CHIPS, *block), _wire_dtype(name)), pltpu.VMEM((3, *block), _wire_dtype(name))]
    for name in early_names:
        block = CUTS[name].block_shape
        scratch += [pltpu.VMEM(block, f32), pltpu.VMEM((3, *block), _wire_dtype(name))]
    args = [late[name] for name in late_names] + [a for name in early_names for a in early[name]]
    outs = pl.pallas_call(
        body, name="reduce_tail",
        in_specs=[ANY] * n_in, out_specs=[ANY] * n,
        out_shape=[jax.ShapeDtypeStruct(CUTS[name].shard_shape if name in BIG else (CUTS[name].rows, CUTS[name].cols), f32)
                   for name in names],
        scratch_shapes=scratch, compiler_params=_params(),
    )(*args)
    return dict(zip(names, outs))


def _adamw_refs(w_ref, g_ref, m_ref, v_ref, d_ref, nm_ref, nv_ref):
    g = g_ref[...]
    nm = ADAM_B1 * m_ref[...] + (1.0 - ADAM_B1) * g
    nv = ADAM_B2 * v_ref[...] + (1.0 - ADAM_B2) * jnp.square(g)
    m_hat = nm / (1.0 - ADAM_B1 ** ADAM_STEP)
    v_hat = nv / (1.0 - ADAM_B2 ** ADAM_STEP)
    d_ref[...] = -ADAM_LR * (m_hat / (jnp.sqrt(v_hat) + ADAM_EPS) + ADAM_WD * w_ref[...])
    nm_ref[...] = nm
    nv_ref[...] = nv


def _adamw_small(ws, gs, ms, vs):
    n = len(ws)

    def body(*refs):
        for i in range(n):
            _adamw_refs(*[refs[k * n + i] for k in range(7)])

    whole = pl.BlockSpec(memory_space=pltpu.VMEM)
    outs = pl.pallas_call(
        body, name="adamw_small",
        in_specs=[whole] * (4 * n), out_specs=[whole] * (3 * n),
        out_shape=[jax.ShapeDtypeStruct(w.shape, f32) for w in ws] * 3,
        compiler_params=_params(),
    )(*ws, *gs, *ms, *vs)
    return outs[:n], outs[n:2 * n], outs[2 * n:]


def _adamw(name, w, g, m, v):
    rows, cols = w.shape
    rt = rows // 4

    def body(w_ref, g_ref, m_ref, v_ref, d_ref, nm_ref, nv_ref):
        _adamw_refs(w_ref, g_ref, m_ref, v_ref, d_ref, nm_ref, nv_ref)

    spec = pl.BlockSpec((rt, cols), lambda i: (i, 0))
    shape = jax.ShapeDtypeStruct((rows, cols), f32)
    return pl.pallas_call(
        body, name="adamw_" + name, grid=(rows // rt,),
        in_specs=[spec] * 4, out_specs=[spec] * 3, out_shape=[shape] * 3,
        compiler_params=_params(),
    )(w, g, m, v)


VEC_NAMES = ("pool_scale", "sgu_ln_g", "sgu_ln_b", "sgu_b", "ln1_g", "ln1_b", "ln2_g", "ln2_b")
WEIGHT_ORDER = ("w_in", "pool_w", "pool_scale", "sgu_ln_g", "sgu_ln_b", "sgu_w", "sgu_b", "w_out", "ln1_g", "ln1_b",
                "w_gate_up", "w_down", "ln2_g", "ln2_b")


def _pack_vecs(parts, extra=None):
    rows = [parts[name].reshape(-1, GROUP) for name in VEC_NAMES]
    if extra is not None:
        rows.append(extra.reshape(-1, GROUP))
    used = sum(r.shape[0] for r in rows)
    return jnp.concatenate(rows + [jnp.zeros((VEC_ROWS - used, GROUP), f32)], axis=0)


def _unpack_vecs(packed, shapes):
    out, at = {}, 0
    for name in VEC_NAMES:
        rows = math.prod(shapes[name]) // GROUP
        out[name] = packed[at:at + rows].reshape(shapes[name])
        at += rows
    return out, packed[at:]


def kernel(x, w_in, pool_w, pool_scale, sgu_ln_g, sgu_ln_b, sgu_w, sgu_b, w_out, ln1_g, ln1_b, w_gate_up, w_down, ln2_g, ln2_b, loss_target, m_w_in, m_pool_w, m_pool_scale, m_sgu_ln_g, m_sgu_ln_b, m_sgu_w, m_sgu_b, m_w_out, m_ln1_g, m_ln1_b, m_w_gate_up, m_w_down, m_ln2_g, m_ln2_b, v_w_in, v_pool_w, v_pool_scale, v_sgu_ln_g, v_sgu_ln_b, v_sgu_w, v_sgu_b, v_w_out, v_ln1_g, v_ln1_b, v_w_gate_up, v_w_down, v_ln2_g, v_ln2_b):
    given = dict(locals())
    batch, seq, _ = x.shape
    tokens = batch * seq
    tile = min(TOKEN_TILE, seq)
    ffn_bwd_tile = min(FFN_BWD_TILE, seq)
    wtile = min(WGRAD_TILE, tokens)
    shapes = {name: given[name].shape for name in WEIGHT_ORDER}

    x2 = x.reshape(tokens, D_MODEL)
    target = loss_target.reshape(tokens, D_MODEL)
    small = (pool_w[0], pool_scale[0][None], sgu_ln_g[0][None], sgu_ln_b[0][None], sgu_w[0],
             jnp.broadcast_to(sgu_b[0][:, :, None], (N_GROUPS, GROUP, GROUP)))
    g1, b1, g2, b2 = ln1_g[0][None], ln1_b[0][None], ln2_g[0][None], ln2_b[0][None]
    shard_b = {name: given[name][0].astype(bf16) for name in BIG}
    shard_b["w_gate_up_top"] = shard_b["w_gate_up_bottom"] = shard_b["w_gate_up"]
    first_row = {"w_gate_up_top": 0, "w_gate_up_bottom": TOP_ROWS}
    place = jnp.stack([lax.axis_index("c"), 2 * lax.axis_index("x") + lax.axis_index("y")]).astype(jnp.int32)

    def gather(*names):
        return [_GatherJob({name: shard_b[name] for name in names},
                           {name: first_row[name] for name in names if name in first_row})]

    def halves_summed(name, grad, got):
        return _add_halves(name, grad, got, place)

    (x_b,), (got,) = _cast_x(x2, 2 * tile, gather("w_in"))
    w_in_b = got["w_in"]
    (xp_all, pre_b), (got,) = _fwd_proj(x_b, w_in_b, tile, gather("w_out", "w_gate_up_top"))
    w_out_b, w_top = got["w_out"], got["w_gate_up_top"]
    (mix_b, r1, h_b), (got,) = _fwd_mix(xp_all, pre_b, x2, small, w_out_b, g1, b1, tile, seq, gather("w_gate_up_bottom"))
    w_bottom = got["w_gate_up_bottom"]
    (gu_b, a_b), (got,) = _fwd_gate_up(h_b, w_top, w_bottom, tile, gather("w_down"))
    w_dn_b = got["w_down"]
    dr2, dr2_b, stats2 = _fwd_down_loss(a_b, w_dn_b, r1, target, g1, b1, g2, b2, tile)

    early = {}
    g_down, _ = _wgrad(a_b, dr2_b, D_MODEL, wtile, "wgrad_down")
    (dgu_b,), (got,) = _bwd_gate_up(dr2_b, gu_b, w_dn_b, ffn_bwd_tile, [_SwapHalvesJob({"w_down": g_down})])
    sums_down = halves_summed("w_down", g_down, got["w_down"])
    g_gu, (got,) = _wgrad(h_b, dgu_b, D_FF, wtile, "wgrad_gate_up", [_SwapChipsJob({"w_down": sums_down[1]})])
    early["w_down"] = (sums_down[0], got["w_down"])
    (dr1, dr1_b, stats1), (got,) = _bwd_ffn_in(dgu_b, w_top, w_bottom, dr2, r1, g1, b1, tile,
                                               [_SwapHalvesJob({"w_gate_up": g_gu})])
    sums_gu = halves_summed("w_gate_up", g_gu, got["w_gate_up"])
    g_out, _ = _wgrad(mix_b, dr1_b, D_MODEL, wtile, "wgrad_out")
    (dpool, dpre_b, g_pool_w, g_sgu_w, g_sgu_b, vecs), (got_gu, got_out) = _bwd_mix(
        dr1_b, xp_all, pre_b, small, w_out_b, tile, seq,
        [_SwapChipsJob({"w_gate_up": sums_gu[1]}), _SwapHalvesJob({"w_out": g_out})])
    early["w_gate_up"] = (sums_gu[0], got_gu["w_gate_up"])
    sums_out = halves_summed("w_out", g_out, got_out["w_out"])
    grad_x, dxp_b = _bwd_in(dpool, dpre_b, dr1, w_in_b, tile, seq)
    g_in, (got,) = _wgrad_in(x_b, dxp_b, dpre_b, wtile, [_SwapChipsJob({"w_out": sums_out[1]})])
    early["w_out"] = (sums_out[0], got["w_out"])

    late = {
        "w_in": g_in,
        "pool_w": g_pool_w.reshape(SQUARE_ROWS, GROUP),
        "sgu_w": g_sgu_w.reshape(SQUARE_ROWS, GROUP),
        "vecs": _pack_vecs({"pool_scale": vecs[0], "sgu_ln_g": vecs[1], "sgu_ln_b": vecs[2], "sgu_b": g_sgu_b[:, :, 0],
                            "ln1_g": stats1[0], "ln1_b": stats1[1], "ln2_g": stats2[0], "ln2_b": stats2[1]},
                           extra=stats2[2]),
    }
    shared = _reduce_tail(late, early)

    grad, delta, new_m, new_v = {}, {}, {}, {}
    for name in BIG:
        grad[name] = shared[name][None]
        d, nm, nv = _adamw(name, given[name][0], shared[name], given["m_" + name][0], given["v_" + name][0])
        delta[name], new_m[name], new_v[name] = d[None], nm[None], nv[None]
    vec_grads, after = _unpack_vecs(shared["vecs"], shapes)
    grad.update(vec_grads)
    for name in ("pool_w", "sgu_w"):
        grad[name] = shared[name].reshape(shapes[name])
    small_names = ("pool_w", "sgu_w") + VEC_NAMES
    state = {pre: [given[pre + name] for name in small_names] for pre in ("", "m_", "v_")}
    ds, nms, nvs = _adamw_small(state[""], [grad[name] for name in small_names], state["m_"], state["v_"])
    delta.update(zip(small_names, ds))
    new_m.update(zip(small_names, nms))
    new_v.update(zip(small_names, nvs))

    sq_err = after[:LOSS_ROWS]
    loss = jnp.sum(sq_err) * (0.5 / D_MODEL)
    return (loss, grad_x.reshape(x.shape), *[grad[name] for name in WEIGHT_ORDER],
            *[delta[name] for name in WEIGHT_ORDER], *[new_m[name] for name in WEIGHT_ORDER],
            *[new_v[name] for name in WEIGHT_ORDER])
```
